```python
import math, functools
import jax, jax.numpy as jnp
from jax import lax
import numpy as np

D_MODEL = 2048
BATCH = 2
SEQ = 4096
DEPTH = 4
DEC_BATCH = 8
DEC_SEQ = 4
PAST_LEN = 16384
PAGE_SIZE = 128

N_HEADS = 16
HEAD_DIM = 128
N_KV_HEADS = 4
ATTN_WIDTH = N_HEADS * HEAD_DIM
KV_WIDTH = N_KV_HEADS * HEAD_DIM
IDX_HEADS = 16
IDX_DIM = 64
TOPK_MAX = 256
Q_BLOCK = 128
S5_WIDTH = D_MODEL // 2
S5_GROUP = 16
S5_GROUPS = S5_WIDTH // S5_GROUP
S5_STATE = 64
DT_MIN = 1e-3
DT_MAX = 1e-1
ROPE_THETA = 10000.0
EPS = 1e-6
IN_SIZES = (ATTN_WIDTH, KV_WIDTH, KV_WIDTH, IDX_HEADS * IDX_DIM, IDX_DIM, IDX_HEADS,
            ATTN_WIDTH, S5_WIDTH, S5_WIDTH, D_MODEL, D_MODEL)
IN_WIDTH = sum(IN_SIZES)

kernel_name = 'gated_s5_dsa_hybrid_step'


def _split_points():
    pts, acc = [], 0
    for s in IN_SIZES[:-1]:
        acc += s
        pts.append(acc)
    return pts


def rms_norm(x, gain):
    x32 = x.astype(jnp.float32)
    y = x32 * lax.rsqrt(jnp.mean(x32 * x32, axis=-1, keepdims=True) + EPS)
    return (y * gain.astype(jnp.float32)).astype(x.dtype)


def rope(x, pos):
    d = x.shape[-1]
    half = d // 2
    inv = ROPE_THETA ** (-jnp.arange(half, dtype=jnp.float32) / half)
    ang = pos.astype(jnp.float32)[:, None] * inv[None, :]
    cos = jnp.cos(ang)[:, None, :]
    sin = jnp.sin(ang)[:, None, :]
    x32 = x.astype(jnp.float32)
    x1, x2 = x32[..., :half], x32[..., half:]
    return jnp.concatenate([x1 * cos - x2 * sin, x2 * cos + x1 * sin], axis=-1).astype(x.dtype)


def gather_rows(rows, idx):
    return jax.vmap(lambda r, i: r[i])(rows, idx)


def indexer_topk(qi, wi, ki_all, qpos, k_top):
    L = ki_all.shape[1]
    logits = jnp.einsum('bthd,bsd->bths', qi.astype(jnp.float32), ki_all.astype(jnp.float32)) * (IDX_DIM ** -0.5)
    score = jnp.einsum('bths,bth->bts', jax.nn.relu(logits), wi.astype(jnp.float32)) * (IDX_HEADS ** -0.5)
    kpos = jnp.arange(L, dtype=jnp.int32)
    visible = kpos[None, None, :] <= qpos[None, :, None]
    score = jnp.where(visible, score, -jnp.inf)
    _, idx = lax.top_k(score, k_top)
    valid = idx <= qpos[None, :, None]
    return idx, valid


def sparse_attend(q, k_sel, v_sel, valid):
    B, T = q.shape[:2]
    qg = q.reshape(B, T, N_KV_HEADS, N_HEADS // N_KV_HEADS, HEAD_DIM).astype(jnp.float32)
    s = jnp.einsum('btgrd,btkgd->btgrk', qg, k_sel.astype(jnp.float32)) * (HEAD_DIM ** -0.5)
    s = jnp.where(valid[:, :, None, None, :], s, -jnp.inf)
    p = jax.nn.softmax(s, axis=-1)
    o = jnp.einsum('btgrk,btkgd->btgrd', p, v_sel.astype(jnp.float32))
    return o.reshape(B, T, ATTN_WIDTH).astype(q.dtype)


def attend_prompt(q, k, v, qi, ki, wi, pos):
    B, L = q.shape[:2]
    nblk = L // Q_BLOCK
    k_top = min(TOPK_MAX, L // 4)

    def to_blocks(a):
        return jnp.moveaxis(a.reshape((B, nblk, Q_BLOCK) + a.shape[2:]), 1, 0)

    def block(args):
        qb, qib, wib, pb = args
        idx, valid = indexer_topk(qib, wib, ki, pb, k_top)
        return sparse_attend(qb, gather_rows(k, idx), gather_rows(v, idx), valid)

    out = lax.map(block, (to_blocks(q), to_blocks(qi), to_blocks(wi), pos.reshape(nblk, Q_BLOCK)))
    return jnp.moveaxis(out, 0, 1).reshape(B, L, ATTN_WIDTH)


def attend_sample(q, k, v, qi, ki, wi, pos, cache_k, cache_v, cache_kidx, page_table, layer):
    B, T = q.shape[:2]
    past = page_table.shape[1] * PAGE_SIZE
    L = past + T
    k_top = min(TOPK_MAX, L // 4)
    ki_past = cache_kidx[layer, page_table].reshape(B, past, IDX_DIM)
    ki_all = jnp.concatenate([ki_past.astype(ki.dtype), ki], axis=1)
    idx, valid = indexer_topk(qi, wi, ki_all, pos, k_top)
    is_past = idx < past
    pidx = jnp.minimum(idx, past - 1)
    phys = gather_rows(page_table, pidx // PAGE_SIZE)
    off = pidx % PAGE_SIZE
    nidx = jnp.clip(idx - past, 0, T - 1)

    def select(pool, new):
        from_pool = pool[layer, phys, off].astype(new.dtype)
        return jnp.where(is_past[..., None, None], from_pool, gather_rows(new, nidx))

    return sparse_attend(q, select(cache_k, k), select(cache_v, v), valid)


def _s5_combine(e1, e2):
    a1r, a1i, b1r, b1i = e1
    a2r, a2i, b2r, b2i = e2
    return (a1r * a2r - a1i * a2i,
            a1r * a2i + a1i * a2r,
            a2r * b1r - a2i * b1i + b2r,
            a2r * b1i + a2i * b1r + b2i)


def s5_branch(u, h0_re, h0_im, prm):
    B, T = u.shape[:2]
    ug = u.reshape(B, T, S5_GROUPS, S5_GROUP).astype(jnp.float32)
    lr = prm['lam_re'].astype(jnp.float32)
    li = prm['lam_im'].astype(jnp.float32)
    dt = jnp.exp(prm['log_dt'].astype(jnp.float32))[:, None]
    mag = jnp.exp(lr * dt)
    ar, ai = mag * jnp.cos(li * dt), mag * jnp.sin(li * dt)
    den = lr * lr + li * li
    nr = ar - 1.0
    cr = (nr * lr + ai * li) / den
    ci = (ai * lr - nr * li) / den
    b_re = prm['b_re'].astype(jnp.float32)
    b_im = prm['b_im'].astype(jnp.float32)
    bb_re = cr[..., None] * b_re - ci[..., None] * b_im
    bb_im = cr[..., None] * b_im + ci[..., None] * b_re
    bu_re = jnp.einsum('gph,btgh->btgp', bb_re, ug)
    bu_im = jnp.einsum('gph,btgh->btgp', bb_im, ug)
    h0r = h0_re.astype(jnp.float32)
    h0i = h0_im.astype(jnp.float32)
    bu_re = bu_re.at[:, 0].add(ar * h0r - ai * h0i)
    bu_im = bu_im.at[:, 0].add(ar * h0i + ai * h0r)
    a_re = jnp.broadcast_to(ar, bu_re.shape)
    a_im = jnp.broadcast_to(ai, bu_im.shape)
    _, _, xr, xi = lax.associative_scan(_s5_combine, (a_re, a_im, bu_re, bu_im), axis=1)
    c_re = prm['c_re'].astype(jnp.float32)
    c_im = prm['c_im'].astype(jnp.float32)
    y = (jnp.einsum('ghp,btgp->btgh', c_re, xr) - jnp.einsum('ghp,btgp->btgh', c_im, xi)
         + prm['d'].astype(jnp.float32).reshape(S5_GROUPS, S5_GROUP) * ug)
    y = jax.nn.gelu(y.reshape(B, T, S5_WIDTH))
    y = y * jax.nn.sigmoid(y @ prm['w_glu'].astype(jnp.float32))
    return y.astype(u.dtype), xr[:, -1], xi[:, -1]


def layer(x, pos, prm, attend, h0_re, h0_im):
    B, T = x.shape[:2]
    h = rms_norm(x, prm['norm'])
    proj = h @ prm['w_in']
    q, k, v, qi, ki, wi, z_a, u, z_s, g_a, g_s = jnp.split(proj, _split_points(), axis=-1)
    q = rope(rms_norm(q.reshape(B, T, N_HEADS, HEAD_DIM), prm['q_norm']), pos)
    k = rope(rms_norm(k.reshape(B, T, N_KV_HEADS, HEAD_DIM), prm['k_norm']), pos)
    v = v.reshape(B, T, N_KV_HEADS, HEAD_DIM)
    qi = rope(qi.reshape(B, T, IDX_HEADS, IDX_DIM), pos)
    ki = rope(ki[:, :, None, :], pos)[:, :, 0, :]
    attn = attend(q, k, v, qi, ki, wi, pos)
    s5y, s_re, s_im = s5_branch(u, h0_re, h0_im, prm)
    o_a = (attn * jax.nn.silu(z_a)) @ prm['w_br_attn']
    o_s = (s5y * jax.nn.silu(z_s)) @ prm['w_br_s5']
    merged = jax.nn.sigmoid(g_a) * o_a + jax.nn.sigmoid(g_s) * o_s
    x = x + merged @ prm['w_out']
    return x, k, v, ki, s_re, s_im


def setup_inputs(seed: int = 0) -> dict:
    key = jax.random.key(seed)
    ks = jax.random.split(key, 24)
    n_pages = PAST_LEN // PAGE_SIZE
    n_phys = (DEC_BATCH * n_pages * 5) // 4
    nrm = jax.random.normal
    f32 = jnp.float32
    return {
        'x_prompt': nrm(ks[0], (BATCH, SEQ, D_MODEL), f32),
        'x_sample': nrm(ks[1], (DEC_BATCH, DEC_SEQ, D_MODEL), f32),
        'cache_k': nrm(ks[2], (DEPTH, n_phys, PAGE_SIZE, N_KV_HEADS, HEAD_DIM), f32),
        'cache_v': nrm(ks[3], (DEPTH, n_phys, PAGE_SIZE, N_KV_HEADS, HEAD_DIM), f32),
        'cache_kidx': nrm(ks[4], (DEPTH, n_phys, PAGE_SIZE, IDX_DIM), f32),
        'state_s5_re': 0.5 * nrm(ks[5], (DEPTH, DEC_BATCH, S5_GROUPS, S5_STATE), f32),
        'state_s5_im': 0.5 * nrm(ks[6], (DEPTH, DEC_BATCH, S5_GROUPS, S5_STATE), f32),
        'page_table': jax.random.permutation(ks[7], n_phys)[:DEC_BATCH * n_pages].reshape(DEC_BATCH, n_pages).astype(jnp.int32),
        'norm_gain': 1.0 + 0.02 * nrm(ks[8], (DEPTH, D_MODEL), f32),
        'w_in': nrm(ks[9], (DEPTH, D_MODEL, IN_WIDTH), f32) * D_MODEL ** -0.5,
        'q_norm_gain': 1.0 + 0.02 * nrm(ks[10], (DEPTH, HEAD_DIM), f32),
        'k_norm_gain': 1.0 + 0.02 * nrm(ks[11], (DEPTH, HEAD_DIM), f32),
        's5_lam_re': -0.5 + 0.01 * nrm(ks[12], (DEPTH, S5_GROUPS, S5_STATE), f32),
        's5_lam_im': math.pi * jnp.arange(S5_STATE, dtype=f32) + 0.01 * nrm(ks[13], (DEPTH, S5_GROUPS, S5_STATE), f32),
        's5_log_dt': jax.random.uniform(ks[14], (DEPTH, S5_GROUPS), f32, math.log(DT_MIN), math.log(DT_MAX)),
        's5_b_re': nrm(ks[15], (DEPTH, S5_GROUPS, S5_STATE, S5_GROUP), f32) * (2 * S5_GROUP) ** -0.5,
        's5_b_im': nrm(ks[16], (DEPTH, S5_GROUPS, S5_STATE, S5_GROUP), f32) * (2 * S5_GROUP) ** -0.5,
        's5_c_re': 0.5 * nrm(ks[17], (DEPTH, S5_GROUPS, S5_GROUP, S5_STATE), f32),
        's5_c_im': 0.5 * nrm(ks[18], (DEPTH, S5_GROUPS, S5_GROUP, S5_STATE), f32),
        's5_d': nrm(ks[19], (DEPTH, S5_WIDTH), f32),
        'w_glu': nrm(ks[20], (DEPTH, S5_WIDTH, S5_WIDTH), f32) * S5_WIDTH ** -0.5,
        'w_br_attn': nrm(ks[21], (DEPTH, ATTN_WIDTH, D_MODEL), f32) * ATTN_WIDTH ** -0.5,
        'w_br_s5': nrm(ks[22], (DEPTH, S5_WIDTH, D_MODEL), f32) * S5_WIDTH ** -0.5,
        'w_out': nrm(ks[23], (DEPTH, D_MODEL, D_MODEL), f32) * D_MODEL ** -0.5,
    }


def reference(x_prompt, x_sample, cache_k, cache_v, cache_kidx, state_s5_re, state_s5_im, page_table,
              norm_gain, w_in, q_norm_gain, k_norm_gain, s5_lam_re, s5_lam_im, s5_log_dt,
              s5_b_re, s5_b_im, s5_c_re, s5_c_im, s5_d, w_glu, w_br_attn, w_br_s5, w_out):
    b_p, t_p = x_prompt.shape[:2]
    t_s = x_sample.shape[1]
    past = page_table.shape[1] * PAGE_SIZE
    pos_p = jnp.arange(t_p, dtype=jnp.int32)
    pos_s = past + jnp.arange(t_s, dtype=jnp.int32)
    h0_re = jnp.zeros((b_p, S5_GROUPS, S5_STATE), jnp.float32)
    h0_im = jnp.zeros((b_p, S5_GROUPS, S5_STATE), jnp.float32)
    xp, xs = x_prompt, x_sample
    outs_p, outs_s = [], []
    for l in range(DEPTH):
        prm = {'norm': norm_gain[l], 'w_in': w_in[l], 'q_norm': q_norm_gain[l], 'k_norm': k_norm_gain[l],
               'lam_re': s5_lam_re[l], 'lam_im': s5_lam_im[l], 'log_dt': s5_log_dt[l],
               'b_re': s5_b_re[l], 'b_im': s5_b_im[l], 'c_re': s5_c_re[l], 'c_im': s5_c_im[l],
               'd': s5_d[l], 'w_glu': w_glu[l], 'w_br_attn': w_br_attn[l], 'w_br_s5': w_br_s5[l],
               'w_out': w_out[l]}
        xp, k_p, v_p, ki_p, sr_p, si_p = layer(xp, pos_p, prm, attend_prompt, h0_re, h0_im)
        attend_s = functools.partial(attend_sample, cache_k=cache_k, cache_v=cache_v, cache_kidx=cache_kidx,
                                     page_table=page_table, layer=l)
        xs, k_s, v_s, ki_s, sr_s, si_s = layer(xs, pos_s, prm, attend_s, state_s5_re[l], state_s5_im[l])
        outs_p.append((k_p, v_p, ki_p, sr_p, si_p))
        outs_s.append((k_s, v_s, ki_s, sr_s, si_s))
    k_prompt, v_prompt, kidx_prompt, s5_re_prompt, s5_im_prompt = [jnp.stack(a) for a in zip(*outs_p)]
    k_sample, v_sample, kidx_sample, s5_re_sample, s5_im_sample = [jnp.stack(a) for a in zip(*outs_s)]
    return (xp, xs, k_prompt, v_prompt, kidx_prompt, s5_re_prompt, s5_im_prompt,
            k_sample, v_sample, kidx_sample, s5_re_sample, s5_im_sample)
```

```python
import functools
import math

import jax
import jax.numpy as jnp
from jax import lax
from jax.experimental import pallas as pl
from jax.experimental.pallas import tpu as pltpu

D_MODEL = 2048
PAGE_SIZE = 128
N_HEADS = 16
HEAD_DIM = 128
N_KV_HEADS = 4
HEADS_PER_KV = N_HEADS // N_KV_HEADS
ATTN_WIDTH = N_HEADS * HEAD_DIM
KV_WIDTH = N_KV_HEADS * HEAD_DIM
IDX_HEADS = 16
IDX_DIM = 64
TOPK_MAX = 256
S5_WIDTH = D_MODEL // 2
S5_GROUP = 16
S5_GROUPS = S5_WIDTH // S5_GROUP
S5_STATE = 64
ROPE_THETA = 10000.0
EPS = 1e-6
IN_SIZES = (ATTN_WIDTH, KV_WIDTH, KV_WIDTH, IDX_HEADS * IDX_DIM, IDX_DIM, IDX_HEADS,
            ATTN_WIDTH, S5_WIDTH, S5_WIDTH, D_MODEL, D_MODEL)

LANES = 128
SUBLANES = 8
VMEM_LIMIT_BYTES = 56 * 1024 * 1024

S5_SG = S5_WIDTH // LANES
S5_SG_GROUPS = S5_GROUPS // S5_SG
S5_SG_STATE = S5_SG_GROUPS * S5_STATE

SAMPLE_ROWS = 16

INT_MIN = -2 ** 31
MASK_BIAS = -1e30

f32 = jnp.float32
bf16 = jnp.bfloat16
i32 = jnp.int32


def _cparams(sem):
    return pltpu.CompilerParams(dimension_semantics=sem, vmem_limit_bytes=VMEM_LIMIT_BYTES)


def _row_tile(m, cap):
    return m if m <= cap else cap


def _norm_body(x_ref, g_ref, o_ref):
    x = x_ref[...]
    ms = jnp.mean(x * x, axis=-1, keepdims=True)
    o_ref[...] = (x * lax.rsqrt(ms + EPS) * g_ref[...]).astype(o_ref.dtype)


def _rmsnorm(x, gain):
    m, d = x.shape
    tm = _row_tile(m, 512)
    return pl.pallas_call(
        _norm_body,
        grid=(m // tm,),
        in_specs=[pl.BlockSpec((tm, d), lambda i: (i, 0)),
                  pl.BlockSpec((1, d), lambda i: (0, 0))],
        out_specs=pl.BlockSpec((tm, d), lambda i: (i, 0)),
        out_shape=jax.ShapeDtypeStruct((m, d), bf16),
        compiler_params=_cparams(("parallel",)),
        name="rmsnorm",
    )(x, gain.reshape(1, d))


def _head_norm_rope(x, gain, cos, sin):
    ms = jnp.mean(x * x, axis=-1, keepdims=True)
    y = x * lax.rsqrt(ms + EPS) * gain
    return y * cos + pltpu.roll(y, HEAD_DIM // 2, 1) * sin


def _q_body(h_ref, w_ref, g_ref, cos_ref, sin_ref, o_ref, *, heads):
    acc = jnp.dot(h_ref[...], w_ref[...], preferred_element_type=f32)
    cos = cos_ref[...]
    sin = sin_ref[...]
    g = g_ref[...]
    scale = HEAD_DIM ** -0.5
    for hh in range(heads):
        sl = slice(hh * HEAD_DIM, (hh + 1) * HEAD_DIM)
        o_ref[:, sl] = (_head_norm_rope(acc[:, sl], g, cos, sin) * scale).astype(o_ref.dtype)


def _proj_q(h, w_q, gain, cos, sin):
    m, d = h.shape
    tm = _row_tile(m, 512)
    tn = 4 * HEAD_DIM
    return pl.pallas_call(
        functools.partial(_q_body, heads=tn // HEAD_DIM),
        grid=(m // tm, ATTN_WIDTH // tn),
        in_specs=[pl.BlockSpec((tm, d), lambda i, j: (i, 0)),
                  pl.BlockSpec((d, tn), lambda i, j: (0, j)),
                  pl.BlockSpec((1, HEAD_DIM), lambda i, j: (0, 0)),
                  pl.BlockSpec((tm, HEAD_DIM), lambda i, j: (i, 0)),
                  pl.BlockSpec((tm, HEAD_DIM), lambda i, j: (i, 0))],
        out_specs=pl.BlockSpec((tm, tn), lambda i, j: (i, j)),
        out_shape=jax.ShapeDtypeStruct((m, ATTN_WIDTH), bf16),
        compiler_params=_cparams(("parallel", "parallel")),
        name="proj_q",
    )(h, w_q, gain.reshape(1, HEAD_DIM), cos, sin)


def _kv_body(h_ref, w_ref, g_ref, cos_ref, sin_ref, k_ref, v_ref, kb_ref, vb_ref):
    acc = jnp.dot(h_ref[...], w_ref[...], preferred_element_type=f32)
    cos = cos_ref[...]
    sin = sin_ref[...]
    g = g_ref[...]
    for hh in range(N_KV_HEADS):
        sl = slice(hh * HEAD_DIM, (hh + 1) * HEAD_DIM)
        kh = _head_norm_rope(acc[:, sl], g, cos, sin)
        k_ref[:, sl] = kh
        kb_ref[:, sl] = kh.astype(bf16)
    v = acc[:, KV_WIDTH:]
    v_ref[...] = v
    vb_ref[...] = v.astype(bf16)


def _proj_kv(h, w_kv, gain, cos, sin):
    m, d = h.shape
    tm = _row_tile(m, 512)
    row = lambda i: (i, 0)
    fix = lambda i: (0, 0)
    return pl.pallas_call(
        _kv_body,
        grid=(m // tm,),
        in_specs=[pl.BlockSpec((tm, d), row),
                  pl.BlockSpec((d, 2 * KV_WIDTH), fix),
                  pl.BlockSpec((1, HEAD_DIM), fix),
                  pl.BlockSpec((tm, HEAD_DIM), row),
                  pl.BlockSpec((tm, HEAD_DIM), row)],
        out_specs=[pl.BlockSpec((tm, KV_WIDTH), row)] * 4,
        out_shape=[jax.ShapeDtypeStruct((m, KV_WIDTH), f32),
                   jax.ShapeDtypeStruct((m, KV_WIDTH), f32),
                   jax.ShapeDtypeStruct((m, KV_WIDTH), bf16),
                   jax.ShapeDtypeStruct((m, KV_WIDTH), bf16)],
        compiler_params=_cparams(("parallel",)),
        name="proj_kv",
    )(h, w_kv, gain.reshape(1, HEAD_DIM), cos, sin)


def _qit_body(wt_ref, h_ref, cos_ref, sin_ref, qit_ref, wt_out_ref):
    acc = lax.dot_general(wt_ref[...], h_ref[...], (((1,), (1,)), ((), ())),
                          preferred_element_type=f32)
    c = cos_ref[...]
    s = sin_ref[...]
    half = IDX_DIM // 2
    scale = IDX_DIM ** -0.5
    for hh in range(IDX_HEADS):
        r0 = hh * IDX_DIM
        x1 = acc[r0:r0 + half]
        x2 = acc[r0 + half:r0 + IDX_DIM]
        qit_ref[r0:r0 + half, :] = ((x1 * c - x2 * s) * scale).astype(qit_ref.dtype)
        qit_ref[r0 + half:r0 + IDX_DIM, :] = ((x2 * c + x1 * s) * scale).astype(qit_ref.dtype)
    n_qi = IDX_HEADS * IDX_DIM
    wt_out_ref[...] = acc[n_qi:n_qi + IDX_HEADS] * (IDX_HEADS ** -0.5)


def _proj_qit(h, wt_qi, cos_t, sin_t):
    m, d = h.shape
    tm = _row_tile(m, 512)
    n_rows = wt_qi.shape[0]
    n_qi = IDX_HEADS * IDX_DIM
    return pl.pallas_call(
        _qit_body,
        grid=(m // tm,),
        in_specs=[pl.BlockSpec((n_rows, d), lambda i: (0, 0)),
                  pl.BlockSpec((tm, d), lambda i: (i, 0)),
                  pl.BlockSpec((IDX_DIM // 2, tm), lambda i: (0, i)),
                  pl.BlockSpec((IDX_DIM // 2, tm), lambda i: (0, i))],
        out_specs=[pl.BlockSpec((n_qi, tm), lambda i: (0, i)),
                   pl.BlockSpec((IDX_HEADS, tm), lambda i: (0, i))],
        out_shape=[jax.ShapeDtypeStruct((n_qi, m), bf16),
                   jax.ShapeDtypeStruct((IDX_HEADS, m), f32)],
        compiler_params=_cparams(("parallel",)),
        name="proj_qi_t",
    )(wt_qi, h, cos_t, sin_t)


def _ki_body(h_ref, w_ref, cos_ref, sina_ref, sinb_ref, ki_ref, kib_ref):
    x = jnp.dot(h_ref[...], w_ref[...], preferred_element_type=f32)
    half = IDX_DIM // 2
    r = (x * cos_ref[...] + pltpu.roll(x, LANES - half, 1) * sina_ref[...]
         + pltpu.roll(x, half, 1) * sinb_ref[...])
    ki = r[:, :IDX_DIM]
    ki_ref[...] = ki
    kib_ref[...] = ki.astype(bf16)


def _proj_ki(h, w_ki, cos_k, sin_a, sin_b):
    m, d = h.shape
    tm = _row_tile(m, 512)
    row = lambda i: (i, 0)
    return pl.pallas_call(
        _ki_body,
        grid=(m // tm,),
        in_specs=[pl.BlockSpec((tm, d), row),
                  pl.BlockSpec((d, LANES), lambda i: (0, 0)),
                  pl.BlockSpec((tm, LANES), row),
                  pl.BlockSpec((tm, LANES), row),
                  pl.BlockSpec((tm, LANES), row)],
        out_specs=[pl.BlockSpec((tm, IDX_DIM), row)] * 2,
        out_shape=[jax.ShapeDtypeStruct((m, IDX_DIM), f32),
                   jax.ShapeDtypeStruct((m, IDX_DIM), bf16)],
        compiler_params=_cparams(("parallel",)),
        name="proj_ki",
    )(h, w_ki, cos_k, sin_a, sin_b)


def _plain_body(h_ref, w_ref, o_ref):
    o_ref[...] = jnp.dot(h_ref[...], w_ref[...], preferred_element_type=f32).astype(o_ref.dtype)


def _proj_plain(h, w, out_dtype):
    m, d = h.shape
    n = w.shape[1]
    tm = _row_tile(m, 512)
    tn = 512
    return pl.pallas_call(
        _plain_body,
        grid=(m // tm, n // tn),
        in_specs=[pl.BlockSpec((tm, d), lambda i, j: (i, 0)),
                  pl.BlockSpec((d, tn), lambda i, j: (0, j))],
        out_specs=pl.BlockSpec((tm, tn), lambda i, j: (i, j)),
        out_shape=jax.ShapeDtypeStruct((m, n), out_dtype),
        compiler_params=_cparams(("parallel", "parallel")),
        name="proj_u",
    )(h, w)


def _sigmoid(x):
    return 1.0 / (1.0 + jnp.exp(-x))


def _zg_body(h_ref, w_ref, o_ref, *, silu_tiles):
    acc = jnp.dot(h_ref[...], w_ref[...], preferred_element_type=f32)
    sg = _sigmoid(acc)
    is_silu = pl.program_id(1) < silu_tiles
    o_ref[...] = jnp.where(is_silu, acc * sg, sg).astype(o_ref.dtype)


def _proj_zg(h, w_zg):
    m, d = h.shape
    n = w_zg.shape[1]
    tm = _row_tile(m, 512)
    tn = 512
    return pl.pallas_call(
        functools.partial(_zg_body, silu_tiles=(ATTN_WIDTH + S5_WIDTH) // tn),
        grid=(m // tm, n // tn),
        in_specs=[pl.BlockSpec((tm, d), lambda i, j: (i, 0)),
                  pl.BlockSpec((d, tn), lambda i, j: (0, j))],
        out_specs=pl.BlockSpec((tm, tn), lambda i, j: (i, j)),
        out_shape=jax.ShapeDtypeStruct((m, n), bf16),
        compiler_params=_cparams(("parallel", "parallel")),
        name="proj_zg",
    )(h, w_zg)


def _sortable_key(score):
    b = pltpu.bitcast(score, i32)
    return b ^ ((b >> 31) & jnp.int32(0x7FFFFFFF))


def _kth_largest(count_ge, shape, k):
    zero = jnp.zeros(shape, i32)
    prefix = jnp.where(count_ge(zero) >= k, zero, jnp.full(shape, INT_MIN, i32))

    def bit_body(bi, prefix):
        cand = prefix | (jnp.int32(1) << (30 - bi))
        return jnp.where(count_ge(cand) >= k, cand, prefix)

    thr = lax.fori_loop(0, 31, bit_body, prefix)
    return jnp.maximum(thr, jnp.int32(INT_MIN + 1))


def _attn_prompt_body(q_ref, qit_ref, wt_ref, z_ref, k_ref, v_ref, ki_ref, o_ref,
                      key_scr, acc_scr, m_scr, l_scr, *, tq, k_top):
    i = pl.program_id(1)
    n_chunks = i + 1
    half = tq // 2

    def score_keys(c, diagonal):
        for sub in range(2):
            r0 = pl.multiple_of(c * tq + sub * half, half)
            kic = ki_ref[pl.ds(r0, half), :]
            acc = jnp.zeros((half, tq), f32)
            for hh in range(IDX_HEADS):
                lg = jnp.dot(kic, qit_ref[hh * IDX_DIM:(hh + 1) * IDX_DIM, :],
                             preferred_element_type=f32)
                acc = acc + jnp.maximum(lg, 0.0) * wt_ref[hh:hh + 1, :]
            key = _sortable_key(acc)
            if diagonal:
                kpos = lax.broadcasted_iota(i32, (half, tq), 0) + sub * half
                qpos = lax.broadcasted_iota(i32, (half, tq), 1)
                key = jnp.where(kpos <= qpos, key, jnp.int32(INT_MIN))
            key_scr[pl.ds(r0, half), :] = key

    def full_chunk(c, carry):
        score_keys(c, False)
        return carry

    lax.fori_loop(0, i, full_chunk, 0)
    score_keys(i, True)

    def count_ge(cand):
        def body(c, cnt):
            kk = key_scr[pl.ds(pl.multiple_of(c * tq, tq), tq), :]
            hit = (kk >= cand).astype(i32)
            return cnt + jnp.sum(hit.reshape(tq // SUBLANES, SUBLANES, tq), axis=0)

        cnt = lax.fori_loop(0, n_chunks, body, jnp.zeros((SUBLANES, tq), i32))
        return jnp.sum(cnt, axis=0, keepdims=True)

    thr = _kth_largest(count_ge, (1, tq), k_top)

    m_scr[...] = jnp.full(m_scr.shape, -jnp.inf, f32)
    l_scr[...] = jnp.zeros(l_scr.shape, f32)
    acc_scr[...] = jnp.zeros(acc_scr.shape, f32)
    eye = (lax.broadcasted_iota(i32, (tq, tq), 0) == lax.broadcasted_iota(i32, (tq, tq), 1)).astype(bf16)

    def attend_chunk(c, carry):
        r0 = pl.multiple_of(c * tq, tq)
        sel_t = (key_scr[pl.ds(r0, tq), :] >= thr).astype(bf16)
        sel = lax.dot_general(eye, sel_t, (((1,), (1,)), ((), ())), preferred_element_type=f32)
        bias = (sel - 1.0) * (-MASK_BIAS)
        for g in range(N_KV_HEADS):
            gs = slice(g * HEAD_DIM, (g + 1) * HEAD_DIM)
            kc = k_ref[pl.ds(r0, tq), gs]
            vc = v_ref[pl.ds(r0, tq), gs]
            for hh in range(HEADS_PER_KV):
                h = g * HEADS_PER_KV + hh
                hs = slice(h * HEAD_DIM, (h + 1) * HEAD_DIM)
                s = lax.dot_general(q_ref[:, hs], kc, (((1,), (1,)), ((), ())),
                                    preferred_element_type=f32) + bias
                m_prev = m_scr[h]
                m_new = jnp.maximum(m_prev, jnp.max(s, axis=-1, keepdims=True))
                alpha = jnp.exp(m_prev - m_new)
                p = jnp.exp(s - m_new)
                l_scr[h] = alpha * l_scr[h] + jnp.sum(p, axis=-1, keepdims=True)
                acc_scr[:, hs] = alpha * acc_scr[:, hs] + jnp.dot(
                    p.astype(bf16), vc, preferred_element_type=f32)
                m_scr[h] = m_new
        return carry

    lax.fori_loop(0, n_chunks, attend_chunk, 0)

    for h in range(N_HEADS):
        hs = slice(h * HEAD_DIM, (h + 1) * HEAD_DIM)
        out = acc_scr[:, hs] / l_scr[h]
        o_ref[:, hs] = (out * z_ref[:, hs].astype(f32)).astype(o_ref.dtype)


def _attn_prompt(q, qit, wt, zg, kb, vb, kib, batch, seq):
    tq = 256 if seq % 256 == 0 else 128
    nq = seq // tq
    k_top = min(TOPK_MAX, seq // 4)
    assert k_top <= tq
    qrow = lambda b, i: (b * nq + i, 0)
    qcol = lambda b, i: (0, b * nq + i)
    per_b = lambda b, i: (b, 0)
    return pl.pallas_call(
        functools.partial(_attn_prompt_body, tq=tq, k_top=k_top),
        grid=(batch, nq),
        in_specs=[pl.BlockSpec((tq, ATTN_WIDTH), qrow),
                  pl.BlockSpec((IDX_HEADS * IDX_DIM, tq), qcol),
                  pl.BlockSpec((IDX_HEADS, tq), qcol),
                  pl.BlockSpec((tq, ATTN_WIDTH), qrow),
                  pl.BlockSpec((seq, KV_WIDTH), per_b),
                  pl.BlockSpec((seq, KV_WIDTH), per_b),
                  pl.BlockSpec((seq, IDX_DIM), per_b)],
        out_specs=pl.BlockSpec((tq, ATTN_WIDTH), qrow),
        out_shape=jax.ShapeDtypeStruct((batch * seq, ATTN_WIDTH), bf16),
        scratch_shapes=[pltpu.VMEM((seq, tq), i32),
                        pltpu.VMEM((tq, ATTN_WIDTH), f32),
                        pltpu.VMEM((N_HEADS, tq, 1), f32),
                        pltpu.VMEM((N_HEADS, tq, 1), f32)],
        compiler_params=_cparams(("parallel", "arbitrary")),
        name="attn_prompt",
    )(q, qit, wt, zg, kb, vb, kib)


def _idx_sample_body(pt_ref, qi_ref, w_ref, kid_ref, kinew_ref, key_ref, thr_ref, *, n_pages, n_new, k_top):
    p = pl.program_id(1)
    tok = SAMPLE_ROWS

    def page_keys(ki_page):
        lg = lax.dot_general(qi_ref[0], ki_page.astype(bf16), (((1,), (1,)), ((), ())),
                             preferred_element_type=f32)
        wgt = jnp.maximum(lg, 0.0) * w_ref[0]
        score = jnp.sum(wgt.reshape(IDX_HEADS, tok, PAGE_SIZE), axis=0)
        return _sortable_key(score)

    @pl.when(p < n_pages)
    def _():
        key_ref[0, p] = page_keys(kid_ref[...])

    @pl.when(p == n_pages)
    def _():
        key = page_keys(kinew_ref[0])
        kpos = lax.broadcasted_iota(i32, (tok, PAGE_SIZE), 1)
        qtok = lax.broadcasted_iota(i32, (tok, PAGE_SIZE), 0)
        key_ref[0, p] = jnp.where((kpos <= qtok) & (kpos < n_new), key, jnp.int32(INT_MIN))

        def count_ge(cand):
            def body(c, cnt):
                return cnt + (key_ref[0, c] >= cand).astype(i32)

            cnt = lax.fori_loop(0, n_pages + 1, body, jnp.zeros((tok, PAGE_SIZE), i32))
            return jnp.sum(cnt, axis=-1, keepdims=True)

        thr = _kth_largest(count_ge, (tok, 1), k_top)
        thr_ref[0] = jnp.broadcast_to(thr, (tok, PAGE_SIZE))


def _idx_sample(page_table, qi_rows, w_rows, cache_kidx, layer, ki_new_pages, n_new):
    nb, n_pages = page_table.shape
    tok = SAMPLE_ROWS
    k_top = min(TOPK_MAX, (n_pages * PAGE_SIZE + n_new) // 4)
    grid_spec = pltpu.PrefetchScalarGridSpec(
        num_scalar_prefetch=1,
        grid=(nb, n_pages + 1),
        in_specs=[pl.BlockSpec((1, IDX_HEADS * tok, IDX_DIM), lambda b, p, pt: (b, 0, 0)),
                  pl.BlockSpec((1, IDX_HEADS * tok, 1), lambda b, p, pt: (b, 0, 0)),
                  pl.BlockSpec((None, None, PAGE_SIZE, IDX_DIM),
                               lambda b, p, pt: (layer, pt[b, jnp.minimum(p, n_pages - 1)], 0, 0)),
                  pl.BlockSpec((1, PAGE_SIZE, IDX_DIM), lambda b, p, pt: (b, 0, 0))],
        out_specs=[pl.BlockSpec((1, n_pages + 1, tok, PAGE_SIZE), lambda b, p, pt: (b, 0, 0, 0)),
                   pl.BlockSpec((1, tok, PAGE_SIZE), lambda b, p, pt: (b, 0, 0))],
    )
    return pl.pallas_call(
        functools.partial(_idx_sample_body, n_pages=n_pages, n_new=n_new, k_top=k_top),
        grid_spec=grid_spec,
        out_shape=[jax.ShapeDtypeStruct((nb, n_pages + 1, tok, PAGE_SIZE), i32),
                   jax.ShapeDtypeStruct((nb, tok, PAGE_SIZE), i32)],
        compiler_params=_cparams(("parallel", "arbitrary")),
        name="idx_sample",
    )(page_table, qi_rows, w_rows, cache_kidx, ki_new_pages)


def _attn_sample_body(pt_ref, q_ref, key_ref, thr_ref, kc_ref, vc_ref, kn_ref, vn_ref, o_ref,
                      acc_scr, m_scr, l_scr, *, n_pages):
    p = pl.program_id(1)
    tok = SAMPLE_ROWS
    rows_g = HEADS_PER_KV * tok

    @pl.when(p == 0)
    def _():
        m_scr[...] = jnp.full(m_scr.shape, -jnp.inf, f32)
        l_scr[...] = jnp.zeros(l_scr.shape, f32)
        acc_scr[...] = jnp.zeros(acc_scr.shape, f32)

    sel = key_ref[0, 0] >= thr_ref[0]
    bias8 = jnp.where(sel, 0.0, MASK_BIAS).astype(f32)
    bias = jnp.concatenate([bias8] * HEADS_PER_KV, axis=0)

    def attend(k_page, v_page):
        for g in range(N_KV_HEADS):
            gs = slice(g * HEAD_DIM, (g + 1) * HEAD_DIM)
            rs = slice(g * rows_g, (g + 1) * rows_g)
            s = lax.dot_general(q_ref[0, rs, :], k_page[:, gs].astype(bf16), (((1,), (1,)), ((), ())),
                                preferred_element_type=f32) + bias
            m_prev = m_scr[rs, :]
            m_new = jnp.maximum(m_prev, jnp.max(s, axis=-1, keepdims=True))
            alpha = jnp.exp(m_prev - m_new)
            pr = jnp.exp(s - m_new)
            l_scr[rs, :] = alpha * l_scr[rs, :] + jnp.sum(pr, axis=-1, keepdims=True)
            acc_scr[rs, :] = alpha * acc_scr[rs, :] + jnp.dot(
                pr.astype(bf16), v_page[:, gs].astype(bf16), preferred_element_type=f32)
            m_scr[rs, :] = m_new

    @pl.when(p < n_pages)
    def _():
        attend(kc_ref[...], vc_ref[...])

    @pl.when(p == n_pages)
    def _():
        attend(kn_ref[0], vn_ref[0])
        o_ref[0] = acc_scr[...] / l_scr[...]


def _attn_sample(page_table, q_rows, keys, thr, cache_k, cache_v, layer, k_new_pages, v_new_pages):
    nb, n_pages = page_table.shape
    tok = SAMPLE_ROWS
    rows = N_HEADS * tok
    page = lambda b, p, pt: (layer, pt[b, jnp.minimum(p, n_pages - 1)], 0, 0)
    per_b = lambda b, p, pt: (b, 0, 0)
    grid_spec = pltpu.PrefetchScalarGridSpec(
        num_scalar_prefetch=1,
        grid=(nb, n_pages + 1),
        in_specs=[pl.BlockSpec((1, rows, HEAD_DIM), per_b),
                  pl.BlockSpec((1, 1, tok, PAGE_SIZE), lambda b, p, pt: (b, p, 0, 0)),
                  pl.BlockSpec((1, tok, PAGE_SIZE), per_b),
                  pl.BlockSpec((None, None, PAGE_SIZE, KV_WIDTH), page),
                  pl.BlockSpec((None, None, PAGE_SIZE, KV_WIDTH), page),
                  pl.BlockSpec((1, PAGE_SIZE, KV_WIDTH), per_b),
                  pl.BlockSpec((1, PAGE_SIZE, KV_WIDTH), per_b)],
        out_specs=pl.BlockSpec((1, rows, HEAD_DIM), per_b),
        scratch_shapes=[pltpu.VMEM((rows, HEAD_DIM), f32),
                        pltpu.VMEM((rows, 1), f32),
                        pltpu.VMEM((rows, 1), f32)],
    )
    return pl.pallas_call(
        functools.partial(_attn_sample_body, n_pages=n_pages),
        grid_spec=grid_spec,
        out_shape=jax.ShapeDtypeStruct((nb, rows, HEAD_DIM), f32),
        compiler_params=_cparams(("parallel", "arbitrary")),
        name="attn_sample",
    )(page_table, q_rows, keys, thr, cache_k, cache_v, k_new_pages, v_new_pages)


def _gate_mul_body(a_ref, z_ref, o_ref):
    o_ref[...] = (a_ref[...] * z_ref[...].astype(f32)).astype(o_ref.dtype)


def _gate_mul(a, zg):
    m, n = a.shape
    return pl.pallas_call(
        _gate_mul_body,
        grid=(1,),
        in_specs=[pl.BlockSpec((m, n), lambda i: (0, 0)),
                  pl.BlockSpec((m, n), lambda i: (0, 0))],
        out_specs=pl.BlockSpec((m, n), lambda i: (0, 0)),
        out_shape=jax.ShapeDtypeStruct((m, n), bf16),
        compiler_params=_cparams(("arbitrary",)),
        name="gate_mul",
    )(a, zg)


def _s5_param_body(lr_ref, li_ref, ldt_ref, bre_ref, bim_ref, ar_ref, ai_ref, bbre_ref, bbim_ref):
    lr = lr_ref[...]
    li = li_ref[...]
    dt = jnp.exp(ldt_ref[...])
    mag = jnp.exp(lr * dt)
    ar = mag * jnp.cos(li * dt)
    ai = mag * jnp.sin(li * dt)
    den = lr * lr + li * li
    nr = ar - 1.0
    cr = (nr * lr + ai * li) / den
    ci = (ai * lr - nr * li) / den
    ar_ref[...] = ar
    ai_ref[...] = ai
    for h in range(S5_GROUP):
        bbre_ref[h] = cr * bre_ref[h] - ci * bim_ref[h]
        bbim_ref[h] = cr * bim_ref[h] + ci * bre_ref[h]


def _s5_params(lam_re, lam_im, log_dt, b_re, b_im):
    gs = (S5_GROUPS, S5_STATE)
    full2 = lambda i: (0, 0)
    full3 = lambda i: (0, 0, 0)
    return pl.pallas_call(
        _s5_param_body,
        grid=(1,),
        in_specs=[pl.BlockSpec(gs, full2), pl.BlockSpec(gs, full2),
                  pl.BlockSpec((S5_GROUPS, 1), full2),
                  pl.BlockSpec((S5_GROUP,) + gs, full3), pl.BlockSpec((S5_GROUP,) + gs, full3)],
        out_specs=[pl.BlockSpec(gs, full2), pl.BlockSpec(gs, full2),
                   pl.BlockSpec((S5_GROUP,) + gs, full3), pl.BlockSpec((S5_GROUP,) + gs, full3)],
        out_shape=[jax.ShapeDtypeStruct(gs, f32), jax.ShapeDtypeStruct(gs, f32),
                   jax.ShapeDtypeStruct((S5_GROUP,) + gs, f32),
                   jax.ShapeDtypeStruct((S5_GROUP,) + gs, f32)],
        compiler_params=_cparams(("arbitrary",)),
        name="s5_params",
    )(lam_re, lam_im, log_dt.reshape(S5_GROUPS, 1), b_re, b_im)


def _gelu_tanh(x):
    return 0.5 * x * (1.0 + jnp.tanh(math.sqrt(2.0 / math.pi) * (x + 0.044715 * (x * x * x))))


def _s5_body(u_ref, zs_ref, h0r_ref, h0i_ref, ar_ref, ai_ref, wbu_ref, wcr_ref, wci_ref, d_ref, wglu_ref,
             o_ref, sr_ref, si_ref, xr_scr, xi_scr, y_scr, str_scr, sti_scr, *, tc, n_last):
    c = pl.program_id(1)

    @pl.when(c == 0)
    def _():
        str_scr[...] = h0r_ref[0]
        sti_scr[...] = h0i_ref[0]

    n_ct = S5_SG_STATE // LANES

    def gather_planes(scr, rows):
        return jnp.concatenate([scr[j, rows, :] for j in range(n_ct)], axis=1)

    def scatter_planes(scr, rows, val):
        for j in range(n_ct):
            scr[j, rows, :] = val[:, j * LANES:(j + 1) * LANES]

    for sg in range(S5_SG):
        rows = pl.ds(sg, tc, stride=S5_SG)
        bu = jnp.dot(u_ref[rows, :].astype(bf16), wbu_ref[sg], preferred_element_type=f32)
        scatter_planes(xr_scr, rows, bu[:, :S5_SG_STATE])
        scatter_planes(xi_scr, rows, bu[:, S5_SG_STATE:])

    ar = ar_ref[...]
    ai = ai_ref[...]

    def step(t, carry):
        xr, xi = carry
        r = pl.ds(pl.multiple_of(t * S5_SG, S5_SG), S5_SG)
        nr = ar * xr - ai * xi + gather_planes(xr_scr, r)
        ni = ar * xi + ai * xr + gather_planes(xi_scr, r)
        scatter_planes(xr_scr, r, nr)
        scatter_planes(xi_scr, r, ni)
        return nr, ni

    xr, xi = lax.fori_loop(0, tc, step, (str_scr[...], sti_scr[...]))
    str_scr[...] = xr
    sti_scr[...] = xi

    for sg in range(S5_SG):
        rows = pl.ds(sg, tc, stride=S5_SG)
        cs = slice(sg * LANES, (sg + 1) * LANES)
        y = (jnp.dot(gather_planes(xr_scr, rows).astype(bf16), wcr_ref[sg], preferred_element_type=f32)
             - jnp.dot(gather_planes(xi_scr, rows).astype(bf16), wci_ref[sg], preferred_element_type=f32)
             + d_ref[:, cs] * u_ref[rows, :])
        y_scr[:, cs] = _gelu_tanh(y)

    y = y_scr[...]
    gate = _sigmoid(jnp.dot(y.astype(bf16), wglu_ref[...], preferred_element_type=f32))
    o_ref[...] = (y * gate * zs_ref[...].astype(f32)).astype(o_ref.dtype)

    @pl.when(c == pl.num_programs(1) - 1)
    def _():
        last = slice((n_last - 1) * S5_SG, n_last * S5_SG)
        sr_ref[0] = gather_planes(xr_scr, last)
        si_ref[0] = gather_planes(xi_scr, last)


def _s5(u, zg, h0_re, h0_im, prm, nb, seq, n_real):
    tc = min(seq, 256)
    nch = seq // tc
    n_last = n_real - (nch - 1) * tc
    u8 = u.reshape(nb * seq * S5_SG, LANES)
    zs_col = ATTN_WIDTH // S5_WIDTH
    fix2 = lambda b, c: (0, 0)
    fix3 = lambda b, c: (0, 0, 0)
    st = lambda b, c: (b, 0, 0)
    out, s_re, s_im = pl.pallas_call(
        functools.partial(_s5_body, tc=tc, n_last=n_last),
        grid=(nb, nch),
        in_specs=[pl.BlockSpec((tc * S5_SG, LANES), lambda b, c: (b * nch + c, 0)),
                  pl.BlockSpec((tc, S5_WIDTH), lambda b, c: (b * nch + c, zs_col)),
                  pl.BlockSpec((1, S5_SG, S5_SG_STATE), st),
                  pl.BlockSpec((1, S5_SG, S5_SG_STATE), st),
                  pl.BlockSpec((S5_SG, S5_SG_STATE), fix2),
                  pl.BlockSpec((S5_SG, S5_SG_STATE), fix2),
                  pl.BlockSpec((S5_SG, LANES, 2 * S5_SG_STATE), fix3),
                  pl.BlockSpec((S5_SG, S5_SG_STATE, LANES), fix3),
                  pl.BlockSpec((S5_SG, S5_SG_STATE, LANES), fix3),
                  pl.BlockSpec((1, S5_WIDTH), fix2),
                  pl.BlockSpec((S5_WIDTH, S5_WIDTH), fix2)],
        out_specs=[pl.BlockSpec((tc, S5_WIDTH), lambda b, c: (b * nch + c, 0)),
                   pl.BlockSpec((1, S5_SG, S5_SG_STATE), st),
                   pl.BlockSpec((1, S5_SG, S5_SG_STATE), st)],
        out_shape=[jax.ShapeDtypeStruct((nb * seq, S5_WIDTH), bf16),
                   jax.ShapeDtypeStruct((nb, S5_SG, S5_SG_STATE), f32),
                   jax.ShapeDtypeStruct((nb, S5_SG, S5_SG_STATE), f32)],
        scratch_shapes=[pltpu.VMEM((S5_SG_STATE // LANES, tc * S5_SG, LANES), f32),
                        pltpu.VMEM((S5_SG_STATE // LANES, tc * S5_SG, LANES), f32),
                        pltpu.VMEM((tc, S5_WIDTH), f32),
                        pltpu.VMEM((S5_SG, S5_SG_STATE), f32),
                        pltpu.VMEM((S5_SG, S5_SG_STATE), f32)],
        compiler_params=_cparams(("parallel", "arbitrary")),
        name="s5_scan",
    )(u8, zg, h0_re.reshape(nb, S5_SG, S5_SG_STATE), h0_im.reshape(nb, S5_SG, S5_SG_STATE),
      prm["a_re"], prm["a_im"], prm["w_bu"], prm["w_c_re"], prm["w_c_im"], prm["d"], prm["w_glu"])
    return out, s_re.reshape(nb, S5_GROUPS, S5_STATE), s_im.reshape(nb, S5_GROUPS, S5_STATE)


def _merge_body(a_ref, s_ref, wa_ref, ws_ref, ga_ref, gs_ref, o_ref):
    o_a = jnp.dot(a_ref[...], wa_ref[...], preferred_element_type=f32)
    o_s = jnp.dot(s_ref[...], ws_ref[...], preferred_element_type=f32)
    o_ref[...] = (ga_ref[...].astype(f32) * o_a + gs_ref[...].astype(f32) * o_s).astype(o_ref.dtype)


def _merge(a_in, s_in, w_a, w_s, zg):
    m = a_in.shape[0]
    tm = _row_tile(m, 512)
    tn = 512
    ga0 = (ATTN_WIDTH + S5_WIDTH) // tn
    gs0 = (ATTN_WIDTH + S5_WIDTH + D_MODEL) // tn
    return pl.pallas_call(
        _merge_body,
        grid=(m // tm, D_MODEL // tn),
        in_specs=[pl.BlockSpec((tm, ATTN_WIDTH), lambda i, j: (i, 0)),
                  pl.BlockSpec((tm, S5_WIDTH), lambda i, j: (i, 0)),
                  pl.BlockSpec((ATTN_WIDTH, tn), lambda i, j: (0, j)),
                  pl.BlockSpec((S5_WIDTH, tn), lambda i, j: (0, j)),
                  pl.BlockSpec((tm, tn), lambda i, j: (i, ga0 + j)),
                  pl.BlockSpec((tm, tn), lambda i, j: (i, gs0 + j))],
        out_specs=pl.BlockSpec((tm, tn), lambda i, j: (i, j)),
        out_shape=jax.ShapeDtypeStruct((m, D_MODEL), bf16),
        compiler_params=_cparams(("parallel", "parallel")),
        name="merge",
    )(a_in, s_in, w_a, w_s, zg, zg)


def _out_body(x_ref, m_ref, w_ref, o_ref):
    o_ref[...] = x_ref[...] + jnp.dot(m_ref[...], w_ref[...], preferred_element_type=f32)


def _out_proj(x, merged, w_out):
    m = x.shape[0]
    tm = _row_tile(m, 512)
    tn = 512
    return pl.pallas_call(
        _out_body,
        grid=(m // tm, D_MODEL // tn),
        in_specs=[pl.BlockSpec((tm, tn), lambda i, j: (i, j)),
                  pl.BlockSpec((tm, D_MODEL), lambda i, j: (i, 0)),
                  pl.BlockSpec((D_MODEL, tn), lambda i, j: (0, j))],
        out_specs=pl.BlockSpec((tm, tn), lambda i, j: (i, j)),
        out_shape=jax.ShapeDtypeStruct((m, D_MODEL), f32),
        compiler_params=_cparams(("parallel", "parallel")),
        name="out_proj",
    )(x, merged, w_out)


def _rope_tables(pos):
    posf = pos.astype(f32)[:, None]
    half = HEAD_DIM // 2
    inv = ROPE_THETA ** (-jnp.arange(half, dtype=f32) / half)
    ang = posf * inv[None, :]
    cos, sin = jnp.cos(ang), jnp.sin(ang)
    cos_h = jnp.concatenate([cos, cos], axis=1)
    sin_h = jnp.concatenate([-sin, sin], axis=1)
    half_i = IDX_DIM // 2
    inv_i = ROPE_THETA ** (-jnp.arange(half_i, dtype=f32) / half_i)
    ang_i = posf * inv_i[None, :]
    cos_i, sin_i = jnp.cos(ang_i), jnp.sin(ang_i)
    zeros = jnp.zeros_like(cos_i)
    pad = jnp.zeros((pos.shape[0], LANES - IDX_DIM), f32)
    cos_k = jnp.concatenate([cos_i, cos_i, pad], axis=1)
    sin_a = jnp.concatenate([-sin_i, zeros, pad], axis=1)
    sin_b = jnp.concatenate([zeros, sin_i, pad], axis=1)
    return dict(cos_h=cos_h, sin_h=sin_h, cos_k=cos_k, sin_a=sin_a, sin_b=sin_b,
                cos_t=cos_i.T, sin_t=sin_i.T)


def _pack_layer_weights(w_in_l, w_glu_l, w_br_attn_l, w_br_s5_l, w_out_l):
    offs = [0]
    for s in IN_SIZES:
        offs.append(offs[-1] + s)
    seg = lambda k: w_in_l[:, offs[k]:offs[k + 1]]
    w_q = seg(0).astype(bf16)
    w_kv = jnp.concatenate([seg(1), seg(2)], axis=1).astype(bf16)
    wt_qi = jnp.concatenate([seg(3), seg(5)], axis=1).T.astype(bf16)
    w_ki = jnp.pad(seg(4), ((0, 0), (0, LANES - IDX_DIM))).astype(bf16)
    w_u = seg(7).astype(bf16)
    w_zg = jnp.concatenate([seg(6), seg(8), seg(9), seg(10)], axis=1).astype(bf16)
    return dict(w_q=w_q, w_kv=w_kv, wt_qi=wt_qi, w_ki=w_ki, w_u=w_u, w_zg=w_zg,
                w_glu=w_glu_l.astype(bf16), w_br_attn=w_br_attn_l.astype(bf16),
                w_br_s5=w_br_s5_l.astype(bf16), w_out=w_out_l.astype(bf16))


def _block_diag(blocks):
    sg, gg, r, c = blocks.shape
    eye = jnp.eye(gg, dtype=blocks.dtype)
    return jnp.einsum("sgrc,gk->sgrkc", blocks, eye).reshape(sg, gg * r, gg * c)


def _s5_layer_params(lam_re, lam_im, log_dt, b_re, b_im, c_re, c_im, d, w_glu_bf16):
    a_re, a_im, bb_re, bb_im = _s5_params(lam_re, lam_im, log_dt,
                                          jnp.transpose(b_re, (2, 0, 1)), jnp.transpose(b_im, (2, 0, 1)))
    def bu_blocks(bb):
        return jnp.transpose(bb, (1, 0, 2)).reshape(S5_SG, S5_SG_GROUPS, S5_GROUP, S5_STATE)
    w_bu = jnp.concatenate([_block_diag(bu_blocks(bb_re)), _block_diag(bu_blocks(bb_im))], axis=2)
    def c_blocks(cm):
        return jnp.transpose(cm, (0, 2, 1)).reshape(S5_SG, S5_SG_GROUPS, S5_STATE, S5_GROUP)
    return dict(a_re=a_re.reshape(S5_SG, S5_SG_STATE), a_im=a_im.reshape(S5_SG, S5_SG_STATE),
                w_bu=w_bu.astype(bf16),
                w_c_re=_block_diag(c_blocks(c_re)).astype(bf16),
                w_c_im=_block_diag(c_blocks(c_im)).astype(bf16),
                d=d.reshape(1, S5_WIDTH), w_glu=w_glu_bf16)


def _projections(x, gain, wts, q_gain, k_gain, tabs):
    h = _rmsnorm(x, gain)
    q = _proj_q(h, wts["w_q"], q_gain, tabs["cos_h"], tabs["sin_h"])
    k, v, kb, vb = _proj_kv(h, wts["w_kv"], k_gain, tabs["cos_h"], tabs["sin_h"])
    qit, wt = _proj_qit(h, wts["wt_qi"], tabs["cos_t"], tabs["sin_t"])
    ki, kib = _proj_ki(h, wts["w_ki"], tabs["cos_k"], tabs["sin_a"], tabs["sin_b"])
    u = _proj_plain(h, wts["w_u"], f32)
    zg = _proj_zg(h, wts["w_zg"])
    return dict(q=q, k=k, v=v, kb=kb, vb=vb, qit=qit, wt=wt, ki=ki, kib=kib, u=u, zg=zg)


def _finish(x, a_in, s_in, zg, wts):
    merged = _merge(a_in, s_in, wts["w_br_attn"], wts["w_br_s5"], zg)
    return _out_proj(x, merged, wts["w_out"])


def kernel(x_prompt, x_sample, cache_k, cache_v, cache_kidx, state_s5_re, state_s5_im, page_table, norm_gain, w_in, q_norm_gain, k_norm_gain, s5_lam_re, s5_lam_im, s5_log_dt, s5_b_re, s5_b_im, s5_c_re, s5_c_im, s5_d, w_glu, w_br_attn, w_br_s5, w_out):
    depth = w_in.shape[0]
    b_p, t_p = x_prompt.shape[:2]
    b_s, t_s = x_sample.shape[:2]
    tok = SAMPLE_ROWS
    assert t_s <= tok
    n_pages = page_table.shape[1]
    past = n_pages * PAGE_SIZE
    n_phys = cache_k.shape[1]

    tabs_p = _rope_tables(jnp.tile(jnp.arange(t_p, dtype=i32), b_p))
    tabs_s = _rope_tables(jnp.tile(past + jnp.arange(tok, dtype=i32), b_s))

    xp = x_prompt.reshape(b_p * t_p, D_MODEL)
    xs = jnp.pad(x_sample, ((0, 0), (0, tok - t_s), (0, 0))).reshape(b_s * tok, D_MODEL)

    cache_k4 = cache_k.reshape(depth, n_phys, PAGE_SIZE, KV_WIDTH)
    cache_v4 = cache_v.reshape(depth, n_phys, PAGE_SIZE, KV_WIDTH)
    zeros_state = jnp.zeros((b_p, S5_GROUPS, S5_STATE), f32)

    def new_page(a):
        w = a.shape[-1]
        return jnp.pad(a.reshape(b_s, tok, w), ((0, 0), (0, PAGE_SIZE - tok), (0, 0)))

    outs_p, outs_s = [], []
    for l in range(depth):
        wts = _pack_layer_weights(w_in[l], w_glu[l], w_br_attn[l], w_br_s5[l], w_out[l])
        s5p = _s5_layer_params(s5_lam_re[l], s5_lam_im[l], s5_log_dt[l], s5_b_re[l], s5_b_im[l],
                               s5_c_re[l], s5_c_im[l], s5_d[l], wts["w_glu"])

        pp = _projections(xp, norm_gain[l], wts, q_norm_gain[l], k_norm_gain[l], tabs_p)
        a_in = _attn_prompt(pp["q"], pp["qit"], pp["wt"], pp["zg"], pp["kb"], pp["vb"], pp["kib"], b_p, t_p)
        s_in, sr_p, si_p = _s5(pp["u"], pp["zg"], zeros_state, zeros_state, s5p, b_p, t_p, t_p)
        xp = _finish(xp, a_in, s_in, pp["zg"], wts)
        outs_p.append((pp["k"].reshape(b_p, t_p, N_KV_HEADS, HEAD_DIM),
                       pp["v"].reshape(b_p, t_p, N_KV_HEADS, HEAD_DIM),
                       pp["ki"].reshape(b_p, t_p, IDX_DIM), sr_p, si_p))

        ps = _projections(xs, norm_gain[l], wts, q_norm_gain[l], k_norm_gain[l], tabs_s)
        qi_rows = jnp.transpose(ps["qit"].reshape(IDX_HEADS, IDX_DIM, b_s, tok), (2, 0, 3, 1)
                                ).reshape(b_s, IDX_HEADS * tok, IDX_DIM)
        w_rows = jnp.transpose(ps["wt"].reshape(IDX_HEADS, b_s, tok), (1, 0, 2)
                               ).reshape(b_s, IDX_HEADS * tok, 1)
        keys, thr = _idx_sample(page_table, qi_rows, w_rows, cache_kidx, l, new_page(ps["ki"]), t_s)
        q_rows = jnp.transpose(ps["q"].reshape(b_s, tok, N_HEADS, HEAD_DIM), (0, 2, 1, 3)
                               ).reshape(b_s, N_HEADS * tok, HEAD_DIM)
        o_rows = _attn_sample(page_table, q_rows, keys, thr, cache_k4, cache_v4, l,
                              new_page(ps["k"]), new_page(ps["v"]))
        attn_s = jnp.transpose(o_rows.reshape(b_s, N_HEADS, tok, HEAD_DIM), (0, 2, 1, 3)
                               ).reshape(b_s * tok, ATTN_WIDTH)
        a_in_s = _gate_mul(attn_s, ps["zg"][:, :ATTN_WIDTH])
        s_in_s, sr_s, si_s = _s5(ps["u"], ps["zg"], state_s5_re[l], state_s5_im[l], s5p, b_s, tok, t_s)
        xs = _finish(xs, a_in_s, s_in_s, ps["zg"], wts)
        real = lambda a: a.reshape((b_s, tok) + a.shape[1:])[:, :t_s]
        outs_s.append((real(ps["k"]).reshape(b_s, t_s, N_KV_HEADS, HEAD_DIM),
                       real(ps["v"]).reshape(b_s, t_s, N_KV_HEADS, HEAD_DIM),
                       real(ps["ki"]), sr_s, si_s))

    k_prompt, v_prompt, kidx_prompt, s5_re_prompt, s5_im_prompt = [jnp.stack(a) for a in zip(*outs_p)]
    k_sample, v_sample, kidx_sample, s5_re_sample, s5_im_sample = [jnp.stack(a) for a in zip(*outs_s)]
    y_prompt = xp.reshape(b_p, t_p, D_MODEL)
    y_sample = xs.reshape(b_s, tok, D_MODEL)[:, :t_s]
    return (y_prompt, y_sample, k_prompt, v_prompt, kidx_prompt, s5_re_prompt, s5_im_prompt,
            k_sample, v_sample, kidx_sample, s5_re_sample, s5_im_sample)
```

```python
import functools
import math

import jax
import jax.numpy as jnp
from jax import lax
from jax.experimental import pallas as pl
from jax.experimental.pallas import tpu as pltpu

D_MODEL = 2048
PAGE_SIZE = 128
N_HEADS = 16
HEAD_DIM = 128
N_KV_HEADS = 4
HEADS_PER_KV = N_HEADS // N_KV_HEADS
ATTN_WIDTH = N_HEADS * HEAD_DIM
KV_WIDTH = N_KV_HEADS * HEAD_DIM
IDX_HEADS = 16
IDX_DIM = 64
TOPK_MAX = 256
S5_WIDTH = D_MODEL // 2
S5_GROUP = 16
S5_GROUPS = S5_WIDTH // S5_GROUP
S5_STATE = 64
ROPE_THETA = 10000.0
EPS = 1e-6
IN_SIZES = (ATTN_WIDTH, KV_WIDTH, KV_WIDTH, IDX_HEADS * IDX_DIM, IDX_DIM, IDX_HEADS,
            ATTN_WIDTH, S5_WIDTH, S5_WIDTH, D_MODEL, D_MODEL)

LANES = 128
SUBLANES = 8
VMEM_LIMIT_BYTES = 56 * 1024 * 1024

S5_SG = S5_WIDTH // LANES
S5_SG_GROUPS = S5_GROUPS // S5_SG
S5_SG_STATE = S5_SG_GROUPS * S5_STATE

SAMPLE_ROWS = 16

ATTN_Q_TILE = 256
PAGES_PER_STEP = 16
LOG2_E = math.log2(math.e)

INT_MIN = -2 ** 31
MASK_BIAS = -1e30

f32 = jnp.float32
bf16 = jnp.bfloat16
i32 = jnp.int32


def _cparams(sem):
    return pltpu.CompilerParams(dimension_semantics=sem, vmem_limit_bytes=VMEM_LIMIT_BYTES)


def _row_tile(m, cap):
    return m if m <= cap else cap


def _norm_body(x_ref, g_ref, o_ref):
    x = x_ref[...]
    ms = jnp.mean(x * x, axis=-1, keepdims=True)
    o_ref[...] = (x * lax.rsqrt(ms + EPS) * g_ref[...]).astype(o_ref.dtype)


def _rmsnorm(x, gain):
    m, d = x.shape
    tm = _row_tile(m, 512)
    return pl.pallas_call(
        _norm_body,
        grid=(m // tm,),
        in_specs=[pl.BlockSpec((tm, d), lambda i: (i, 0)),
                  pl.BlockSpec((1, d), lambda i: (0, 0))],
        out_specs=pl.BlockSpec((tm, d), lambda i: (i, 0)),
        out_shape=jax.ShapeDtypeStruct((m, d), bf16),
        compiler_params=_cparams(("parallel",)),
        name="rmsnorm",
    )(x, gain.reshape(1, d))


def _head_norm_rope(x, gain, cos, sin):
    ms = jnp.mean(x * x, axis=-1, keepdims=True)
    y = x * lax.rsqrt(ms + EPS) * gain
    return y * cos + pltpu.roll(y, HEAD_DIM // 2, 1) * sin


def _qt_body(wt_ref, h_ref, g_ref, cos_ref, sin_ref, o_ref, *, heads):
    acc = lax.dot_general(wt_ref[...], h_ref[...], (((1,), (1,)), ((), ())),
                          preferred_element_type=f32)
    c = cos_ref[...]
    s = sin_ref[...]
    g = g_ref[...]
    half = HEAD_DIM // 2
    scale = HEAD_DIM ** -0.5 * LOG2_E
    for hh in range(heads):
        r0 = hh * HEAD_DIM
        x = acc[r0:r0 + HEAD_DIM]
        ms = jnp.mean(x * x, axis=0, keepdims=True)
        y = x * lax.rsqrt(ms + EPS) * g
        x1 = y[:half]
        x2 = y[half:]
        o_ref[r0:r0 + half, :] = ((x1 * c - x2 * s) * scale).astype(o_ref.dtype)
        o_ref[r0 + half:r0 + HEAD_DIM, :] = ((x2 * c + x1 * s) * scale).astype(o_ref.dtype)


def _proj_qt(h, wt_q, gain, cos_ht, sin_ht):
    m, d = h.shape
    tm = _row_tile(m, 512)
    tn = 4 * HEAD_DIM
    return pl.pallas_call(
        functools.partial(_qt_body, heads=tn // HEAD_DIM),
        grid=(m // tm, ATTN_WIDTH // tn),
        in_specs=[pl.BlockSpec((tn, d), lambda i, j: (j, 0)),
                  pl.BlockSpec((tm, d), lambda i, j: (i, 0)),
                  pl.BlockSpec((HEAD_DIM, 1), lambda i, j: (0, 0)),
                  pl.BlockSpec((HEAD_DIM // 2, tm), lambda i, j: (0, i)),
                  pl.BlockSpec((HEAD_DIM // 2, tm), lambda i, j: (0, i))],
        out_specs=pl.BlockSpec((tn, tm), lambda i, j: (j, i)),
        out_shape=jax.ShapeDtypeStruct((ATTN_WIDTH, m), bf16),
        compiler_params=_cparams(("parallel", "parallel")),
        name="proj_q_t",
    )(wt_q, h, gain.reshape(HEAD_DIM, 1), cos_ht, sin_ht)


def _kv_body(h_ref, w_ref, wvt_ref, g_ref, cos_ref, sin_ref, k_ref, v_ref, kb_ref, vt_ref, *, chunk):
    h = h_ref[...]
    acc = jnp.dot(h, w_ref[...], preferred_element_type=f32)
    cos = cos_ref[...]
    sin = sin_ref[...]
    g = g_ref[...]
    for hh in range(N_KV_HEADS):
        sl = slice(hh * HEAD_DIM, (hh + 1) * HEAD_DIM)
        kh = _head_norm_rope(acc[:, sl], g, cos, sin)
        k_ref[:, sl] = kh
        kb_ref[:, sl] = kh.astype(bf16)
    v_ref[...] = acc[:, KV_WIDTH:]
    vt = lax.dot_general(wvt_ref[...], h, (((1,), (1,)), ((), ())), preferred_element_type=f32)
    for cc in range(vt_ref.shape[0]):
        vt_ref[cc] = vt[:, cc * chunk:(cc + 1) * chunk].astype(bf16)


def _proj_kv(h, w_kv, wt_v, gain, cos, sin):
    m, d = h.shape
    tm = _row_tile(m, 512)
    chunk = min(tm, ATTN_Q_TILE)
    row = lambda i: (i, 0)
    fix = lambda i: (0, 0)
    return pl.pallas_call(
        functools.partial(_kv_body, chunk=chunk),
        grid=(m // tm,),
        in_specs=[pl.BlockSpec((tm, d), row),
                  pl.BlockSpec((d, 2 * KV_WIDTH), fix),
                  pl.BlockSpec((KV_WIDTH, d), fix),
                  pl.BlockSpec((1, HEAD_DIM), fix),
                  pl.BlockSpec((tm, HEAD_DIM), row),
                  pl.BlockSpec((tm, HEAD_DIM), row)],
        out_specs=[pl.BlockSpec((tm, KV_WIDTH), row)] * 3
                  + [pl.BlockSpec((tm // chunk, KV_WIDTH, chunk), lambda i: (i, 0, 0))],
        out_shape=[jax.ShapeDtypeStruct((m, KV_WIDTH), f32),
                   jax.ShapeDtypeStruct((m, KV_WIDTH), f32),
                   jax.ShapeDtypeStruct((m, KV_WIDTH), bf16),
                   jax.ShapeDtypeStruct((m // chunk, KV_WIDTH, chunk), bf16)],
        compiler_params=_cparams(("parallel",)),
        name="proj_kv",
    )(h, w_kv, wt_v, gain.reshape(1, HEAD_DIM), cos, sin)


def _qit_body(wt_ref, h_ref, cos_ref, sin_ref, qit_ref, wt_out_ref):
    acc = lax.dot_general(wt_ref[...], h_ref[...], (((1,), (1,)), ((), ())),
                          preferred_element_type=f32)
    c = cos_ref[...]
    s = sin_ref[...]
    half = IDX_DIM // 2
    scale = IDX_DIM ** -0.5
    for hh in range(IDX_HEADS):
        r0 = hh * IDX_DIM
        x1 = acc[r0:r0 + half]
        x2 = acc[r0 + half:r0 + IDX_DIM]
        qit_ref[r0:r0 + half, :] = ((x1 * c - x2 * s) * scale).astype(qit_ref.dtype)
        qit_ref[r0 + half:r0 + IDX_DIM, :] = ((x2 * c + x1 * s) * scale).astype(qit_ref.dtype)
    n_qi = IDX_HEADS * IDX_DIM
    wt_out_ref[...] = acc[n_qi:n_qi + IDX_HEADS] * (IDX_HEADS ** -0.5)


def _proj_qit(h, wt_qi, cos_t, sin_t):
    m, d = h.shape
    tm = _row_tile(m, 512)
    n_rows = wt_qi.shape[0]
    n_qi = IDX_HEADS * IDX_DIM
    return pl.pallas_call(
        _qit_body,
        grid=(m // tm,),
        in_specs=[pl.BlockSpec((n_rows, d), lambda i: (0, 0)),
                  pl.BlockSpec((tm, d), lambda i: (i, 0)),
                  pl.BlockSpec((IDX_DIM // 2, tm), lambda i: (0, i)),
                  pl.BlockSpec((IDX_DIM // 2, tm), lambda i: (0, i))],
        out_specs=[pl.BlockSpec((n_qi, tm), lambda i: (0, i)),
                   pl.BlockSpec((IDX_HEADS, tm), lambda i: (0, i))],
        out_shape=[jax.ShapeDtypeStruct((n_qi, m), bf16),
                   jax.ShapeDtypeStruct((IDX_HEADS, m), f32)],
        compiler_params=_cparams(("parallel",)),
        name="proj_qi_t",
    )(wt_qi, h, cos_t, sin_t)


def _ki_body(h_ref, w_ref, cos_ref, sina_ref, sinb_ref, ki_ref, kib_ref):
    x = jnp.dot(h_ref[...], w_ref[...], preferred_element_type=f32)
    half = IDX_DIM // 2
    r = (x * cos_ref[...] + pltpu.roll(x, LANES - half, 1) * sina_ref[...]
         + pltpu.roll(x, half, 1) * sinb_ref[...])
    ki = r[:, :IDX_DIM]
    ki_ref[...] = ki
    kib_ref[...] = ki.astype(bf16)


def _proj_ki(h, w_ki, cos_k, sin_a, sin_b):
    m, d = h.shape
    tm = _row_tile(m, 512)
    row = lambda i: (i, 0)
    return pl.pallas_call(
        _ki_body,
        grid=(m // tm,),
        in_specs=[pl.BlockSpec((tm, d), row),
                  pl.BlockSpec((d, LANES), lambda i: (0, 0)),
                  pl.BlockSpec((tm, LANES), row),
                  pl.BlockSpec((tm, LANES), row),
                  pl.BlockSpec((tm, LANES), row)],
        out_specs=[pl.BlockSpec((tm, IDX_DIM), row)] * 2,
        out_shape=[jax.ShapeDtypeStruct((m, IDX_DIM), f32),
                   jax.ShapeDtypeStruct((m, IDX_DIM), bf16)],
        compiler_params=_cparams(("parallel",)),
        name="proj_ki",
    )(h, w_ki, cos_k, sin_a, sin_b)


def _plain_body(h_ref, w_ref, o_ref):
    o_ref[...] = jnp.dot(h_ref[...], w_ref[...], preferred_element_type=f32).astype(o_ref.dtype)


def _proj_plain(h, w, out_dtype):
    m, d = h.shape
    n = w.shape[1]
    tm = _row_tile(m, 512)
    tn = 512
    return pl.pallas_call(
        _plain_body,
        grid=(m // tm, n // tn),
        in_specs=[pl.BlockSpec((tm, d), lambda i, j: (i, 0)),
                  pl.BlockSpec((d, tn), lambda i, j: (0, j))],
        out_specs=pl.BlockSpec((tm, tn), lambda i, j: (i, j)),
        out_shape=jax.ShapeDtypeStruct((m, n), out_dtype),
        compiler_params=_cparams(("parallel", "parallel")),
        name="proj_u",
    )(h, w)


def _sigmoid(x):
    return 1.0 / (1.0 + jnp.exp(-x))


def _zg_body(h_ref, w_ref, o_ref, *, silu_tiles):
    acc = jnp.dot(h_ref[...], w_ref[...], preferred_element_type=f32)
    sg = _sigmoid(acc)
    is_silu = pl.program_id(1) < silu_tiles
    o_ref[...] = jnp.where(is_silu, acc * sg, sg).astype(o_ref.dtype)


def _proj_zg(h, w_zg):
    m, d = h.shape
    n = w_zg.shape[1]
    tm = _row_tile(m, 512)
    tn = 512
    return pl.pallas_call(
        functools.partial(_zg_body, silu_tiles=(ATTN_WIDTH + S5_WIDTH) // tn),
        grid=(m // tm, n // tn),
        in_specs=[pl.BlockSpec((tm, d), lambda i, j: (i, 0)),
                  pl.BlockSpec((d, tn), lambda i, j: (0, j))],
        out_specs=pl.BlockSpec((tm, tn), lambda i, j: (i, j)),
        out_shape=jax.ShapeDtypeStruct((m, n), bf16),
        compiler_params=_cparams(("parallel", "parallel")),
        name="proj_zg",
    )(h, w_zg)


def _sortable_key(score):
    b = pltpu.bitcast(score, i32)
    return b ^ ((b >> 31) & jnp.int32(0x7FFFFFFF))


def _kth_largest(count_ge, shape, k):
    zero = jnp.zeros(shape, i32)
    lowest = jnp.full(shape, INT_MIN, i32)
    c_zero = count_ge(zero)
    nonneg = c_zero >= k
    prefix = jnp.where(nonneg, zero, lowest)
    c_prefix = jnp.where(nonneg, c_zero, count_ge(lowest))

    def unsettled(state):
        bi, _, c_prefix = state
        return (bi < 31) & (jnp.max(jnp.abs(c_prefix - k)) > 0)

    def bit_body(state):
        bi, prefix, c_prefix = state
        cand = prefix | (jnp.int32(1) << (30 - bi))
        c_cand = count_ge(cand)
        take = c_cand >= k
        return bi + 1, jnp.where(take, cand, prefix), jnp.where(take, c_cand, c_prefix)

    _, thr, _ = lax.while_loop(unsettled, bit_body, (jnp.int32(0), prefix, c_prefix))
    return jnp.maximum(thr, jnp.int32(INT_MIN + 1))


def _sublane_allreduce(x, op):
    for shift in (4, 2, 1):
        x = op(x, pltpu.roll(x, shift, 0))
    return x


def _attn_prompt_body(qt_ref, qit_ref, wt_ref, z_ref, k_ref, vt_ref, ki_ref, o_ref,
                      key_scr, acc_scr, m_scr, l_scr, alpha_scr, s_scr, p_scr, *, tq, k_top):
    i = pl.program_id(1)
    n_chunks = i + 1
    half = tq // 2

    def score_keys(c, diagonal):
        for sub in range(2):
            r0 = pl.multiple_of(c * tq + sub * half, half)
            kic = ki_ref[pl.ds(r0, half), :]
            acc = jnp.zeros((half, tq), f32)
            for hh in range(IDX_HEADS):
                lg = jnp.dot(kic, qit_ref[hh * IDX_DIM:(hh + 1) * IDX_DIM, :],
                             preferred_element_type=f32)
                acc = acc + jnp.maximum(lg, 0.0) * wt_ref[hh:hh + 1, :]
            key = _sortable_key(acc)
            if diagonal:
                kpos = lax.broadcasted_iota(i32, (half, tq), 0) + sub * half
                qpos = lax.broadcasted_iota(i32, (half, tq), 1)
                key = jnp.where(kpos <= qpos, key, jnp.int32(INT_MIN))
            key_scr[pl.ds(r0, half), :] = key

    def full_chunk(c, carry):
        score_keys(c, False)
        return carry

    lax.fori_loop(0, i, full_chunk, 0)
    score_keys(i, True)

    def count_ge(cand):
        def body(c, cnt):
            kk = key_scr[pl.ds(pl.multiple_of(c * tq, tq), tq), :]
            hit = (kk >= cand).astype(i32)
            return cnt + jnp.sum(hit.reshape(tq // SUBLANES, SUBLANES, tq), axis=0)

        cnt = lax.fori_loop(0, n_chunks, body, jnp.zeros((SUBLANES, tq), i32))
        return jnp.sum(cnt, axis=0, keepdims=True)

    thr = _kth_largest(count_ge, (1, tq), k_top)

    m_scr[...] = jnp.full(m_scr.shape, -jnp.inf, f32)
    l_scr[...] = jnp.zeros(l_scr.shape, f32)
    acc_scr[...] = jnp.zeros(acc_scr.shape, f32)
    n_sub = tq // SUBLANES
    d_sub = HEAD_DIM // SUBLANES

    def attend_chunk(c, carry):
        r0 = pl.multiple_of(c * tq, tq)
        sel = key_scr[pl.ds(r0, tq), :] >= thr
        for g in range(N_KV_HEADS):
            kc = k_ref[pl.ds(r0, tq), g * HEAD_DIM:(g + 1) * HEAD_DIM]
            for hh in range(HEADS_PER_KV):
                h = g * HEADS_PER_KV + hh
                s = jnp.dot(kc, qt_ref[h * HEAD_DIM:(h + 1) * HEAD_DIM, :], preferred_element_type=f32)
                s = jnp.where(sel, s, MASK_BIAS)
                s_scr[h] = s
                m_prev = m_scr[h]
                m_new = jnp.maximum(m_prev, _sublane_allreduce(
                    jnp.max(s.reshape(n_sub, SUBLANES, tq), axis=0), jnp.maximum))
                alpha_scr[h] = jnp.exp2(m_prev - m_new)
                m_scr[h] = m_new
        for h in range(N_HEADS):
            p3 = jnp.exp2(s_scr[h].reshape(n_sub, SUBLANES, tq) - m_scr[h][None])
            l_scr[h] = alpha_scr[h] * l_scr[h] + _sublane_allreduce(jnp.sum(p3, axis=0), jnp.add)
            p_scr[h] = p3.reshape(tq, tq).astype(bf16)
        for g in range(N_KV_HEADS):
            vtc = vt_ref[c, g * HEAD_DIM:(g + 1) * HEAD_DIM, :]
            for hh in range(HEADS_PER_KV):
                h = g * HEADS_PER_KV + hh
                hs = slice(h * HEAD_DIM, (h + 1) * HEAD_DIM)
                pv = jnp.dot(vtc, p_scr[h], preferred_element_type=f32)
                acc = acc_scr[hs, :].reshape(d_sub, SUBLANES, tq) * alpha_scr[h][None]
                acc_scr[hs, :] = acc.reshape(HEAD_DIM, tq) + pv
        return carry

    lax.fori_loop(0, n_chunks, attend_chunk, 0)

    for h in range(N_HEADS):
        hs = slice(h * HEAD_DIM, (h + 1) * HEAD_DIM)
        out_t = (acc_scr[hs, :].reshape(d_sub, SUBLANES, tq) / l_scr[h][None]).reshape(HEAD_DIM, tq)
        o_ref[:, hs] = (out_t.T * z_ref[:, hs].astype(f32)).astype(o_ref.dtype)


def _attn_prompt(qt, qit, wt, zg, kb, vt, kib, batch, seq):
    tq = ATTN_Q_TILE
    assert seq % tq == 0
    nq = seq // tq
    k_top = min(TOPK_MAX, seq // 4)
    assert k_top <= tq
    qrow = lambda b, i: (b * nq + i, 0)
    qcol = lambda b, i: (0, b * nq + i)
    per_b = lambda b, i: (b, 0)
    return pl.pallas_call(
        functools.partial(_attn_prompt_body, tq=tq, k_top=k_top),
        grid=(batch, nq),
        in_specs=[pl.BlockSpec((ATTN_WIDTH, tq), qcol),
                  pl.BlockSpec((IDX_HEADS * IDX_DIM, tq), qcol),
                  pl.BlockSpec((IDX_HEADS, tq), qcol),
                  pl.BlockSpec((tq, ATTN_WIDTH), qrow),
                  pl.BlockSpec((seq, KV_WIDTH), per_b),
                  pl.BlockSpec((nq, KV_WIDTH, tq), lambda b, i: (b, 0, 0)),
                  pl.BlockSpec((seq, IDX_DIM), per_b)],
        out_specs=pl.BlockSpec((tq, ATTN_WIDTH), qrow),
        out_shape=jax.ShapeDtypeStruct((batch * seq, ATTN_WIDTH), bf16),
        scratch_shapes=[pltpu.VMEM((seq, tq), i32),
                        pltpu.VMEM((ATTN_WIDTH, tq), f32),
                        pltpu.VMEM((N_HEADS, SUBLANES, tq), f32),
                        pltpu.VMEM((N_HEADS, SUBLANES, tq), f32),
                        pltpu.VMEM((N_HEADS, SUBLANES, tq), f32),
                        pltpu.VMEM((N_HEADS, tq, tq), f32),
                        pltpu.VMEM((N_HEADS, tq, tq), bf16)],
        compiler_params=_cparams(("parallel", "arbitrary")),
        name="attn_prompt",
    )(qt, qit, wt, zg, kb, vt, kib)


def _idx_sample_body(pt_ref, qi_ref, w_ref, *rest, n_pages, n_new, k_top, ppb):
    kid_refs = rest[:ppb]
    kinew_ref, key_ref, thr_ref = rest[ppb:]
    p = pl.program_id(1)
    tok = SAMPLE_ROWS

    def page_keys(ki_page):
        lg = lax.dot_general(qi_ref[0], ki_page.astype(bf16), (((1,), (1,)), ((), ())),
                             preferred_element_type=f32)
        wgt = jnp.maximum(lg, 0.0) * w_ref[0]
        score = jnp.sum(wgt.reshape(IDX_HEADS, tok, PAGE_SIZE), axis=0)
        return _sortable_key(score)

    for j in range(ppb):
        key_ref[0, p * ppb + j] = page_keys(kid_refs[j][...])

    @pl.when(p == pl.num_programs(1) - 1)
    def _():
        key = page_keys(kinew_ref[0])
        kpos = lax.broadcasted_iota(i32, (tok, PAGE_SIZE), 1)
        qtok = lax.broadcasted_iota(i32, (tok, PAGE_SIZE), 0)
        key_ref[0, n_pages] = jnp.where((kpos <= qtok) & (kpos < n_new), key, jnp.int32(INT_MIN))

        def count_ge(cand):
            def body(c, cnt):
                return cnt + (key_ref[0, c] >= cand).astype(i32)

            cnt = lax.fori_loop(0, n_pages + 1, body, jnp.zeros((tok, PAGE_SIZE), i32))
            return jnp.sum(cnt, axis=-1, keepdims=True)

        thr = _kth_largest(count_ge, (tok, 1), k_top)
        thr_ref[0] = jnp.broadcast_to(thr, (tok, PAGE_SIZE))


def _pages_per_step(n_pages):
    ppb = min(PAGES_PER_STEP, n_pages)
    assert n_pages % ppb == 0
    return ppb


def _page_spec(block, layer, ppb, j):
    return pl.BlockSpec(block, lambda b, p, pt: (layer, pt[b, p * ppb + j], 0, 0))


def _idx_sample(page_table, qi_rows, w_rows, cache_kidx, layer, ki_new_pages, n_new):
    nb, n_pages = page_table.shape
    tok = SAMPLE_ROWS
    ppb = _pages_per_step(n_pages)
    k_top = min(TOPK_MAX, (n_pages * PAGE_SIZE + n_new) // 4)
    per_b = lambda b, p, pt: (b, 0, 0)
    grid_spec = pltpu.PrefetchScalarGridSpec(
        num_scalar_prefetch=1,
        grid=(nb, n_pages // ppb),
        in_specs=[pl.BlockSpec((1, IDX_HEADS * tok, IDX_DIM), per_b),
                  pl.BlockSpec((1, IDX_HEADS * tok, 1), per_b)]
                 + [_page_spec((None, None, PAGE_SIZE, IDX_DIM), layer, ppb, j) for j in range(ppb)]
                 + [pl.BlockSpec((1, PAGE_SIZE, IDX_DIM), per_b)],
        out_specs=[pl.BlockSpec((1, n_pages + 1, tok, PAGE_SIZE), lambda b, p, pt: (b, 0, 0, 0)),
                   pl.BlockSpec((1, tok, PAGE_SIZE), per_b)],
    )
    return pl.pallas_call(
        functools.partial(_idx_sample_body, n_pages=n_pages, n_new=n_new, k_top=k_top, ppb=ppb),
        grid_spec=grid_spec,
        out_shape=[jax.ShapeDtypeStruct((nb, n_pages + 1, tok, PAGE_SIZE), i32),
                   jax.ShapeDtypeStruct((nb, tok, PAGE_SIZE), i32)],
        compiler_params=_cparams(("parallel", "arbitrary")),
        name="idx_sample",
    )(page_table, qi_rows, w_rows, *([cache_kidx] * ppb), ki_new_pages)


def _attn_sample_body(pt_ref, q_ref, key_ref, keyn_ref, thr_ref, *rest, ppb):
    k_refs = rest[:ppb]
    v_refs = rest[ppb:2 * ppb]
    kn_ref, vn_ref, o_ref, acc_scr, m_scr, l_scr = rest[2 * ppb:]
    p = pl.program_id(1)
    tok = SAMPLE_ROWS
    rows_g = HEADS_PER_KV * tok
    thr = thr_ref[0]

    @pl.when(p == 0)
    def _():
        m_scr[...] = jnp.full(m_scr.shape, -jnp.inf, f32)
        l_scr[...] = jnp.zeros(l_scr.shape, f32)
        acc_scr[...] = jnp.zeros(acc_scr.shape, f32)

    def attend(bias_tok, k_of_group, v_of_group):
        bias = jnp.concatenate([bias_tok] * HEADS_PER_KV, axis=0)
        for g in range(N_KV_HEADS):
            rs = slice(g * rows_g, (g + 1) * rows_g)
            s = lax.dot_general(q_ref[0, rs, :], k_of_group(g), (((1,), (1,)), ((), ())),
                                preferred_element_type=f32) + bias
            m_prev = m_scr[rs, :]
            m_new = jnp.maximum(m_prev, jnp.max(s, axis=-1, keepdims=True))
            alpha = jnp.exp2(m_prev - m_new)
            pr = jnp.exp2(s - m_new)
            l_scr[rs, :] = alpha * l_scr[rs, :] + jnp.sum(pr, axis=-1, keepdims=True)
            acc_scr[rs, :] = alpha * acc_scr[rs, :] + jnp.dot(
                pr.astype(bf16), v_of_group(g), preferred_element_type=f32)
            m_scr[rs, :] = m_new

    def mask_bias(keys):
        return jnp.where(keys >= thr, 0.0, MASK_BIAS).astype(f32)

    def cached(refs):
        def of_group(g):
            rows = pl.ds(g, PAGE_SIZE, stride=N_KV_HEADS)
            return jnp.concatenate([r[rows, :].astype(bf16) for r in refs], axis=0)
        return of_group

    attend(jnp.concatenate([mask_bias(key_ref[0, j]) for j in range(ppb)], axis=1),
           cached(k_refs), cached(v_refs))

    @pl.when(p == pl.num_programs(1) - 1)
    def _():
        def fresh(ref):
            return lambda g: ref[0, :, g * HEAD_DIM:(g + 1) * HEAD_DIM].astype(bf16)

        attend(mask_bias(keyn_ref[0, 0]), fresh(kn_ref), fresh(vn_ref))
        o_ref[0] = acc_scr[...] / l_scr[...]


def _attn_sample(page_table, q_rows, keys, thr, cache_k, cache_v, layer, k_new_pages, v_new_pages):
    nb, n_pages = page_table.shape
    tok = SAMPLE_ROWS
    rows = N_HEADS * tok
    ppb = _pages_per_step(n_pages)
    per_b = lambda b, p, pt: (b, 0, 0)
    page_block = (None, None, PAGE_SIZE * N_KV_HEADS, HEAD_DIM)
    grid_spec = pltpu.PrefetchScalarGridSpec(
        num_scalar_prefetch=1,
        grid=(nb, n_pages // ppb),
        in_specs=[pl.BlockSpec((1, rows, HEAD_DIM), per_b),
                  pl.BlockSpec((1, ppb, tok, PAGE_SIZE), lambda b, p, pt: (b, p, 0, 0)),
                  pl.BlockSpec((1, 1, tok, PAGE_SIZE), lambda b, p, pt: (b, n_pages, 0, 0)),
                  pl.BlockSpec((1, tok, PAGE_SIZE), per_b)]
                 + [_page_spec(page_block, layer, ppb, j) for j in range(ppb)]
                 + [_page_spec(page_block, layer, ppb, j) for j in range(ppb)]
                 + [pl.BlockSpec((1, PAGE_SIZE, KV_WIDTH), per_b),
                    pl.BlockSpec((1, PAGE_SIZE, KV_WIDTH), per_b)],
        out_specs=pl.BlockSpec((1, rows, HEAD_DIM), per_b),
        scratch_shapes=[pltpu.VMEM((rows, HEAD_DIM), f32),
                        pltpu.VMEM((rows, 1), f32),
                        pltpu.VMEM((rows, 1), f32)],
    )
    return pl.pallas_call(
        functools.partial(_attn_sample_body, ppb=ppb),
        grid_spec=grid_spec,
        out_shape=jax.ShapeDtypeStruct((nb, rows, HEAD_DIM), f32),
        compiler_params=_cparams(("parallel", "arbitrary")),
        name="attn_sample",
    )(page_table, q_rows, keys, keys, thr, *([cache_k] * ppb), *([cache_v] * ppb),
      k_new_pages, v_new_pages)


def _gate_mul_body(a_ref, z_ref, o_ref):
    o_ref[...] = (a_ref[...] * z_ref[...].astype(f32)).astype(o_ref.dtype)


def _gate_mul(a, zg):
    m, n = a.shape
    return pl.pallas_call(
        _gate_mul_body,
        grid=(1,),
        in_specs=[pl.BlockSpec((m, n), lambda i: (0, 0)),
                  pl.BlockSpec((m, n), lambda i: (0, 0))],
        out_specs=pl.BlockSpec((m, n), lambda i: (0, 0)),
        out_shape=jax.ShapeDtypeStruct((m, n), bf16),
        compiler_params=_cparams(("arbitrary",)),
        name="gate_mul",
    )(a, zg)


def _s5_param_body(lr_ref, li_ref, ldt_ref, bre_ref, bim_ref, ar_ref, ai_ref, bbre_ref, bbim_ref):
    lr = lr_ref[...]
    li = li_ref[...]
    dt = jnp.exp(ldt_ref[...])
    mag = jnp.exp(lr * dt)
    ar = mag * jnp.cos(li * dt)
    ai = mag * jnp.sin(li * dt)
    den = lr * lr + li * li
    nr = ar - 1.0
    cr = (nr * lr + ai * li) / den
    ci = (ai * lr - nr * li) / den
    ar_ref[...] = ar
    ai_ref[...] = ai
    for h in range(S5_GROUP):
        bbre_ref[h] = cr * bre_ref[h] - ci * bim_ref[h]
        bbim_ref[h] = cr * bim_ref[h] + ci * bre_ref[h]


def _s5_params(lam_re, lam_im, log_dt, b_re, b_im):
    gs = (S5_GROUPS, S5_STATE)
    full2 = lambda i: (0, 0)
    full3 = lambda i: (0, 0, 0)
    return pl.pallas_call(
        _s5_param_body,
        grid=(1,),
        in_specs=[pl.BlockSpec(gs, full2), pl.BlockSpec(gs, full2),
                  pl.BlockSpec((S5_GROUPS, 1), full2),
                  pl.BlockSpec((S5_GROUP,) + gs, full3), pl.BlockSpec((S5_GROUP,) + gs, full3)],
        out_specs=[pl.BlockSpec(gs, full2), pl.BlockSpec(gs, full2),
                   pl.BlockSpec((S5_GROUP,) + gs, full3), pl.BlockSpec((S5_GROUP,) + gs, full3)],
        out_shape=[jax.ShapeDtypeStruct(gs, f32), jax.ShapeDtypeStruct(gs, f32),
                   jax.ShapeDtypeStruct((S5_GROUP,) + gs, f32),
                   jax.ShapeDtypeStruct((S5_GROUP,) + gs, f32)],
        compiler_params=_cparams(("arbitrary",)),
        name="s5_params",
    )(lam_re, lam_im, log_dt.reshape(S5_GROUPS, 1), b_re, b_im)


def _gelu_tanh(x):
    return 0.5 * x * (1.0 + jnp.tanh(math.sqrt(2.0 / math.pi) * (x + 0.044715 * (x * x * x))))


def _s5_body(u_ref, zs_ref, h0r_ref, h0i_ref, ar_ref, ai_ref, wbu_ref, wcr_ref, wci_ref, d_ref, wglu_ref,
             o_ref, sr_ref, si_ref, xr_scr, xi_scr, y_scr, str_scr, sti_scr, *, tc, n_last):
    c = pl.program_id(1)

    @pl.when(c == 0)
    def _():
        str_scr[...] = h0r_ref[0]
        sti_scr[...] = h0i_ref[0]

    n_ct = S5_SG_STATE // LANES

    def gather_planes(scr, rows):
        return jnp.concatenate([scr[j, rows, :] for j in range(n_ct)], axis=1)

    def scatter_planes(scr, rows, val):
        for j in range(n_ct):
            scr[j, rows, :] = val[:, j * LANES:(j + 1) * LANES]

    for sg in range(S5_SG):
        rows = pl.ds(sg, tc, stride=S5_SG)
        bu = jnp.dot(u_ref[rows, :].astype(bf16), wbu_ref[sg], preferred_element_type=f32)
        scatter_planes(xr_scr, rows, bu[:, :S5_SG_STATE])
        scatter_planes(xi_scr, rows, bu[:, S5_SG_STATE:])

    ar = ar_ref[...]
    ai = ai_ref[...]

    def step(t, carry):
        xr, xi = carry
        r = pl.ds(pl.multiple_of(t * S5_SG, S5_SG), S5_SG)
        nr = ar * xr - ai * xi + gather_planes(xr_scr, r)
        ni = ar * xi + ai * xr + gather_planes(xi_scr, r)
        scatter_planes(xr_scr, r, nr)
        scatter_planes(xi_scr, r, ni)
        return nr, ni

    xr, xi = lax.fori_loop(0, tc, step, (str_scr[...], sti_scr[...]))
    str_scr[...] = xr
    sti_scr[...] = xi

    for sg in range(S5_SG):
        rows = pl.ds(sg, tc, stride=S5_SG)
        cs = slice(sg * LANES, (sg + 1) * LANES)
        y = (jnp.dot(gather_planes(xr_scr, rows).astype(bf16), wcr_ref[sg], preferred_element_type=f32)
             - jnp.dot(gather_planes(xi_scr, rows).astype(bf16), wci_ref[sg], preferred_element_type=f32)
             + d_ref[:, cs] * u_ref[rows, :])
        y_scr[:, cs] = _gelu_tanh(y)

    y = y_scr[...]
    gate = _sigmoid(jnp.dot(y.astype(bf16), wglu_ref[...], preferred_element_type=f32))
    o_ref[...] = (y * gate * zs_ref[...].astype(f32)).astype(o_ref.dtype)

    @pl.when(c == pl.num_programs(1) - 1)
    def _():
        last = slice((n_last - 1) * S5_SG, n_last * S5_SG)
        sr_ref[0] = gather_planes(xr_scr, last)
        si_ref[0] = gather_planes(xi_scr, last)


def _s5(u, zg, h0_re, h0_im, prm, nb, seq, n_real):
    tc = min(seq, 256)
    nch = seq // tc
    n_last = n_real - (nch - 1) * tc
    u8 = u.reshape(nb * seq * S5_SG, LANES)
    zs_col = ATTN_WIDTH // S5_WIDTH
    fix2 = lambda b, c: (0, 0)
    fix3 = lambda b, c: (0, 0, 0)
    st = lambda b, c: (b, 0, 0)
    out, s_re, s_im = pl.pallas_call(
        functools.partial(_s5_body, tc=tc, n_last=n_last),
        grid=(nb, nch),
        in_specs=[pl.BlockSpec((tc * S5_SG, LANES), lambda b, c: (b * nch + c, 0)),
                  pl.BlockSpec((tc, S5_WIDTH), lambda b, c: (b * nch + c, zs_col)),
                  pl.BlockSpec((1, S5_SG, S5_SG_STATE), st),
                  pl.BlockSpec((1, S5_SG, S5_SG_STATE), st),
                  pl.BlockSpec((S5_SG, S5_SG_STATE), fix2),
                  pl.BlockSpec((S5_SG, S5_SG_STATE), fix2),
                  pl.BlockSpec((S5_SG, LANES, 2 * S5_SG_STATE), fix3),
                  pl.BlockSpec((S5_SG, S5_SG_STATE, LANES), fix3),
                  pl.BlockSpec((S5_SG, S5_SG_STATE, LANES), fix3),
                  pl.BlockSpec((1, S5_WIDTH), fix2),
                  pl.BlockSpec((S5_WIDTH, S5_WIDTH), fix2)],
        out_specs=[pl.BlockSpec((tc, S5_WIDTH), lambda b, c: (b * nch + c, 0)),
                   pl.BlockSpec((1, S5_SG, S5_SG_STATE), st),
                   pl.BlockSpec((1, S5_SG, S5_SG_STATE), st)],
        out_shape=[jax.ShapeDtypeStruct((nb * seq, S5_WIDTH), bf16),
                   jax.ShapeDtypeStruct((nb, S5_SG, S5_SG_STATE), f32),
                   jax.ShapeDtypeStruct((nb, S5_SG, S5_SG_STATE), f32)],
        scratch_shapes=[pltpu.VMEM((S5_SG_STATE // LANES, tc * S5_SG, LANES), f32),
                        pltpu.VMEM((S5_SG_STATE // LANES, tc * S5_SG, LANES), f32),
                        pltpu.VMEM((tc, S5_WIDTH), f32),
                        pltpu.VMEM((S5_SG, S5_SG_STATE), f32),
                        pltpu.VMEM((S5_SG, S5_SG_STATE), f32)],
        compiler_params=_cparams(("parallel", "arbitrary")),
        name="s5_scan",
    )(u8, zg, h0_re.reshape(nb, S5_SG, S5_SG_STATE), h0_im.reshape(nb, S5_SG, S5_SG_STATE),
      prm["a_re"], prm["a_im"], prm["w_bu"], prm["w_c_re"], prm["w_c_im"], prm["d"], prm["w_glu"])
    return out, s_re.reshape(nb, S5_GROUPS, S5_STATE), s_im.reshape(nb, S5_GROUPS, S5_STATE)


def _merge_body(a_ref, s_ref, wa_ref, ws_ref, ga_ref, gs_ref, o_ref):
    o_a = jnp.dot(a_ref[...], wa_ref[...], preferred_element_type=f32)
    o_s = jnp.dot(s_ref[...], ws_ref[...], preferred_element_type=f32)
    o_ref[...] = (ga_ref[...].astype(f32) * o_a + gs_ref[...].astype(f32) * o_s).astype(o_ref.dtype)


def _merge(a_in, s_in, w_a, w_s, zg):
    m = a_in.shape[0]
    tm = _row_tile(m, 512)
    tn = 512
    ga0 = (ATTN_WIDTH + S5_WIDTH) // tn
    gs0 = (ATTN_WIDTH + S5_WIDTH + D_MODEL) // tn
    return pl.pallas_call(
        _merge_body,
        grid=(m // tm, D_MODEL // tn),
        in_specs=[pl.BlockSpec((tm, ATTN_WIDTH), lambda i, j: (i, 0)),
                  pl.BlockSpec((tm, S5_WIDTH), lambda i, j: (i, 0)),
                  pl.BlockSpec((ATTN_WIDTH, tn), lambda i, j: (0, j)),
                  pl.BlockSpec((S5_WIDTH, tn), lambda i, j: (0, j)),
                  pl.BlockSpec((tm, tn), lambda i, j: (i, ga0 + j)),
                  pl.BlockSpec((tm, tn), lambda i, j: (i, gs0 + j))],
        out_specs=pl.BlockSpec((tm, tn), lambda i, j: (i, j)),
        out_shape=jax.ShapeDtypeStruct((m, D_MODEL), bf16),
        compiler_params=_cparams(("parallel", "parallel")),
        name="merge",
    )(a_in, s_in, w_a, w_s, zg, zg)


def _out_body(x_ref, m_ref, w_ref, o_ref):
    o_ref[...] = x_ref[...] + jnp.dot(m_ref[...], w_ref[...], preferred_element_type=f32)


def _out_proj(x, merged, w_out):
    m = x.shape[0]
    tm = _row_tile(m, 512)
    tn = 512
    return pl.pallas_call(
        _out_body,
        grid=(m // tm, D_MODEL // tn),
        in_specs=[pl.BlockSpec((tm, tn), lambda i, j: (i, j)),
                  pl.BlockSpec((tm, D_MODEL), lambda i, j: (i, 0)),
                  pl.BlockSpec((D_MODEL, tn), lambda i, j: (0, j))],
        out_specs=pl.BlockSpec((tm, tn), lambda i, j: (i, j)),
        out_shape=jax.ShapeDtypeStruct((m, D_MODEL), f32),
        compiler_params=_cparams(("parallel", "parallel")),
        name="out_proj",
    )(x, merged, w_out)


def _rope_tables(pos):
    posf = pos.astype(f32)[:, None]
    half = HEAD_DIM // 2
    inv = ROPE_THETA ** (-jnp.arange(half, dtype=f32) / half)
    ang = posf * inv[None, :]
    cos, sin = jnp.cos(ang), jnp.sin(ang)
    cos_h = jnp.concatenate([cos, cos], axis=1)
    sin_h = jnp.concatenate([-sin, sin], axis=1)
    half_i = IDX_DIM // 2
    inv_i = ROPE_THETA ** (-jnp.arange(half_i, dtype=f32) / half_i)
    ang_i = posf * inv_i[None, :]
    cos_i, sin_i = jnp.cos(ang_i), jnp.sin(ang_i)
    zeros = jnp.zeros_like(cos_i)
    pad = jnp.zeros((pos.shape[0], LANES - IDX_DIM), f32)
    cos_k = jnp.concatenate([cos_i, cos_i, pad], axis=1)
    sin_a = jnp.concatenate([-sin_i, zeros, pad], axis=1)
    sin_b = jnp.concatenate([zeros, sin_i, pad], axis=1)
    return dict(cos_h=cos_h, sin_h=sin_h, cos_k=cos_k, sin_a=sin_a, sin_b=sin_b,
                cos_ht=cos.T, sin_ht=sin.T, cos_t=cos_i.T, sin_t=sin_i.T)


def _pack_layer_weights(w_in_l, w_glu_l, w_br_attn_l, w_br_s5_l, w_out_l):
    offs = [0]
    for s in IN_SIZES:
        offs.append(offs[-1] + s)
    seg = lambda k: w_in_l[:, offs[k]:offs[k + 1]]
    wt_q = seg(0).T.astype(bf16)
    wt_v = seg(2).T.astype(bf16)
    w_kv = jnp.concatenate([seg(1), seg(2)], axis=1).astype(bf16)
    wt_qi = jnp.concatenate([seg(3), seg(5)], axis=1).T.astype(bf16)
    w_ki = jnp.pad(seg(4), ((0, 0), (0, LANES - IDX_DIM))).astype(bf16)
    w_u = seg(7).astype(bf16)
    w_zg = jnp.concatenate([seg(6), seg(8), seg(9), seg(10)], axis=1).astype(bf16)
    return dict(wt_q=wt_q, wt_v=wt_v, w_kv=w_kv, wt_qi=wt_qi, w_ki=w_ki, w_u=w_u, w_zg=w_zg,
                w_glu=w_glu_l.astype(bf16), w_br_attn=w_br_attn_l.astype(bf16),
                w_br_s5=w_br_s5_l.astype(bf16), w_out=w_out_l.astype(bf16))


def _block_diag(blocks):
    sg, gg, r, c = blocks.shape
    eye = jnp.eye(gg, dtype=blocks.dtype)
    return jnp.einsum("sgrc,gk->sgrkc", blocks, eye).reshape(sg, gg * r, gg * c)


def _s5_layer_params(lam_re, lam_im, log_dt, b_re, b_im, c_re, c_im, d, w_glu_bf16):
    a_re, a_im, bb_re, bb_im = _s5_params(lam_re, lam_im, log_dt,
                                          jnp.transpose(b_re, (2, 0, 1)), jnp.transpose(b_im, (2, 0, 1)))
    def bu_blocks(bb):
        return jnp.transpose(bb, (1, 0, 2)).reshape(S5_SG, S5_SG_GROUPS, S5_GROUP, S5_STATE)
    w_bu = jnp.concatenate([_block_diag(bu_blocks(bb_re)), _block_diag(bu_blocks(bb_im))], axis=2)
    def c_blocks(cm):
        return jnp.transpose(cm, (0, 2, 1)).reshape(S5_SG, S5_SG_GROUPS, S5_STATE, S5_GROUP)
    return dict(a_re=a_re.reshape(S5_SG, S5_SG_STATE), a_im=a_im.reshape(S5_SG, S5_SG_STATE),
                w_bu=w_bu.astype(bf16),
                w_c_re=_block_diag(c_blocks(c_re)).astype(bf16),
                w_c_im=_block_diag(c_blocks(c_im)).astype(bf16),
                d=d.reshape(1, S5_WIDTH), w_glu=w_glu_bf16)


def _projections(x, gain, wts, q_gain, k_gain, tabs):
    h = _rmsnorm(x, gain)
    qt = _proj_qt(h, wts["wt_q"], q_gain, tabs["cos_ht"], tabs["sin_ht"])
    k, v, kb, vt = _proj_kv(h, wts["w_kv"], wts["wt_v"], k_gain, tabs["cos_h"], tabs["sin_h"])
    qit, wt = _proj_qit(h, wts["wt_qi"], tabs["cos_t"], tabs["sin_t"])
    ki, kib = _proj_ki(h, wts["w_ki"], tabs["cos_k"], tabs["sin_a"], tabs["sin_b"])
    u = _proj_plain(h, wts["w_u"], f32)
    zg = _proj_zg(h, wts["w_zg"])
    return dict(qt=qt, k=k, v=v, kb=kb, vt=vt, qit=qit, wt=wt, ki=ki, kib=kib, u=u, zg=zg)


def _finish(x, a_in, s_in, zg, wts):
    merged = _merge(a_in, s_in, wts["w_br_attn"], wts["w_br_s5"], zg)
    return _out_proj(x, merged, wts["w_out"])


def kernel(x_prompt, x_sample, cache_k, cache_v, cache_kidx, state_s5_re, state_s5_im, page_table, norm_gain, w_in, q_norm_gain, k_norm_gain, s5_lam_re, s5_lam_im, s5_log_dt, s5_b_re, s5_b_im, s5_c_re, s5_c_im, s5_d, w_glu, w_br_attn, w_br_s5, w_out):
    depth = w_in.shape[0]
    b_p, t_p = x_prompt.shape[:2]
    b_s, t_s = x_sample.shape[:2]
    tok = SAMPLE_ROWS
    assert t_s <= tok
    n_pages = page_table.shape[1]
    past = n_pages * PAGE_SIZE
    n_phys = cache_k.shape[1]

    tabs_p = _rope_tables(jnp.tile(jnp.arange(t_p, dtype=i32), b_p))
    tabs_s = _rope_tables(jnp.tile(past + jnp.arange(tok, dtype=i32), b_s))

    xp = x_prompt.reshape(b_p * t_p, D_MODEL)
    xs = jnp.pad(x_sample, ((0, 0), (0, tok - t_s), (0, 0))).reshape(b_s * tok, D_MODEL)

    cache_k4 = cache_k.reshape(depth, n_phys, PAGE_SIZE * N_KV_HEADS, HEAD_DIM)
    cache_v4 = cache_v.reshape(depth, n_phys, PAGE_SIZE * N_KV_HEADS, HEAD_DIM)
    zeros_state = jnp.zeros((b_p, S5_GROUPS, S5_STATE), f32)

    def new_page(a):
        w = a.shape[-1]
        return jnp.pad(a.reshape(b_s, tok, w), ((0, 0), (0, PAGE_SIZE - tok), (0, 0)))

    outs_p, outs_s = [], []
    for l in range(depth):
        wts = _pack_layer_weights(w_in[l], w_glu[l], w_br_attn[l], w_br_s5[l], w_out[l])
        s5p = _s5_layer_params(s5_lam_re[l], s5_lam_im[l], s5_log_dt[l], s5_b_re[l], s5_b_im[l],
                               s5_c_re[l], s5_c_im[l], s5_d[l], wts["w_glu"])

        pp = _projections(xp, norm_gain[l], wts, q_norm_gain[l], k_norm_gain[l], tabs_p)
        a_in = _attn_prompt(pp["qt"], pp["qit"], pp["wt"], pp["zg"], pp["kb"], pp["vt"], pp["kib"], b_p, t_p)
        s_in, sr_p, si_p = _s5(pp["u"], pp["zg"], zeros_state, zeros_state, s5p, b_p, t_p, t_p)
        xp = _finish(xp, a_in, s_in, pp["zg"], wts)
        outs_p.append((pp["k"].reshape(b_p, t_p, N_KV_HEADS, HEAD_DIM),
                       pp["v"].reshape(b_p, t_p, N_KV_HEADS, HEAD_DIM),
                       pp["ki"].reshape(b_p, t_p, IDX_DIM), sr_p, si_p))

        ps = _projections(xs, norm_gain[l], wts, q_norm_gain[l], k_norm_gain[l], tabs_s)
        qi_rows = jnp.transpose(ps["qit"].reshape(IDX_HEADS, IDX_DIM, b_s, tok), (2, 0, 3, 1)
                                ).reshape(b_s, IDX_HEADS * tok, IDX_DIM)
        w_rows = jnp.transpose(ps["wt"].reshape(IDX_HEADS, b_s, tok), (1, 0, 2)
                               ).reshape(b_s, IDX_HEADS * tok, 1)
        keys, thr = _idx_sample(page_table, qi_rows, w_rows, cache_kidx, l, new_page(ps["ki"]), t_s)
        q_rows = jnp.transpose(ps["qt"].reshape(N_HEADS, HEAD_DIM, b_s, tok), (2, 0, 3, 1)
                               ).reshape(b_s, N_HEADS * tok, HEAD_DIM)
        o_rows = _attn_sample(page_table, q_rows, keys, thr, cache_k4, cache_v4, l,
                              new_page(ps["k"]), new_page(ps["v"]))
        attn_s = jnp.transpose(o_rows.reshape(b_s, N_HEADS, tok, HEAD_DIM), (0, 2, 1, 3)
                               ).reshape(b_s * tok, ATTN_WIDTH)
        a_in_s = _gate_mul(attn_s, ps["zg"][:, :ATTN_WIDTH])
        s_in_s, sr_s, si_s = _s5(ps["u"], ps["zg"], state_s5_re[l], state_s5_im[l], s5p, b_s, tok, t_s)
        xs = _finish(xs, a_in_s, s_in_s, ps["zg"], wts)
        real = lambda a: a.reshape((b_s, tok) + a.shape[1:])[:, :t_s]
        outs_s.append((real(ps["k"]).reshape(b_s, t_s, N_KV_HEADS, HEAD_DIM),
                       real(ps["v"]).reshape(b_s, t_s, N_KV_HEADS, HEAD_DIM),
                       real(ps["ki"]), sr_s, si_s))

    k_prompt, v_prompt, kidx_prompt, s5_re_prompt, s5_im_prompt = [jnp.stack(a) for a in zip(*outs_p)]
    k_sample, v_sample, kidx_sample, s5_re_sample, s5_im_sample = [jnp.stack(a) for a in zip(*outs_s)]
    y_prompt = xp.reshape(b_p, t_p, D_MODEL)
    y_sample = xs.reshape(b_s, tok, D_MODEL)[:, :t_s]
    return (y_prompt, y_sample, k_prompt, v_prompt, kidx_prompt, s5_re_prompt, s5_im_prompt,
            k_sample, v_sample, kidx_sample, s5_re_sample, s5_im_sample)
```

```python
import functools
import math

import jax
import jax.numpy as jnp
from jax import lax
from jax.experimental import pallas as pl
from jax.experimental.pallas import tpu as pltpu

D_MODEL = 2048
PAGE_SIZE = 128
N_HEADS = 16
HEAD_DIM = 128
N_KV_HEADS = 4
HEADS_PER_KV = N_HEADS // N_KV_HEADS
ATTN_WIDTH = N_HEADS * HEAD_DIM
KV_WIDTH = N_KV_HEADS * HEAD_DIM
IDX_HEADS = 16
IDX_DIM = 64
TOPK_MAX = 256
S5_WIDTH = D_MODEL // 2
S5_GROUP = 16
S5_GROUPS = S5_WIDTH // S5_GROUP
S5_STATE = 64
ROPE_THETA = 10000.0
EPS = 1e-6
IN_SIZES = (ATTN_WIDTH, KV_WIDTH, KV_WIDTH, IDX_HEADS * IDX_DIM, IDX_DIM, IDX_HEADS,
            ATTN_WIDTH, S5_WIDTH, S5_WIDTH, D_MODEL, D_MODEL)

LANES = 128
SUBLANES = 8
VMEM_LIMIT_BYTES = 56 * 1024 * 1024

S5_SG = S5_WIDTH // LANES
S5_SG_GROUPS = S5_GROUPS // S5_SG
S5_SG_STATE = S5_SG_GROUPS * S5_STATE

SAMPLE_ROWS = 16

ATTN_Q_TILE = 256
ATTN_ROW_BLOCK = 64
PAGES_PER_STEP = 16
IDX_PAGES_PER_STEP = 32
LOG2_E = math.log2(math.e)

INT_MIN = -2 ** 31
MASK_BIAS = -1e30

f32 = jnp.float32
bf16 = jnp.bfloat16
i32 = jnp.int32


def _cparams(sem):
    return pltpu.CompilerParams(dimension_semantics=sem, vmem_limit_bytes=VMEM_LIMIT_BYTES)


def _row_tile(m, cap):
    return m if m <= cap else cap


def _norm_body(x_ref, g_ref, o_ref):
    x = x_ref[...]
    ms = jnp.mean(x * x, axis=-1, keepdims=True)
    o_ref[...] = (x * lax.rsqrt(ms + EPS) * g_ref[...]).astype(o_ref.dtype)


def _layer_spec(block, layer, index_map):
    return pl.BlockSpec((None,) + tuple(block), lambda *idx: (layer,) + tuple(index_map(*idx)))


def _rmsnorm(x, gains, layer):
    m, d = x.shape
    tm = _row_tile(m, 512)
    return pl.pallas_call(
        _norm_body,
        grid=(m // tm,),
        in_specs=[pl.BlockSpec((tm, d), lambda i: (i, 0)),
                  _layer_spec((1, d), layer, lambda i: (0, 0))],
        out_specs=pl.BlockSpec((tm, d), lambda i: (i, 0)),
        out_shape=jax.ShapeDtypeStruct((m, d), bf16),
        compiler_params=_cparams(("parallel",)),
        name="rmsnorm",
    )(x, gains)


def _head_norm_rope(x, gain, cos, sin):
    ms = jnp.mean(x * x, axis=-1, keepdims=True)
    y = x * lax.rsqrt(ms + EPS) * gain
    return y * cos + pltpu.roll(y, HEAD_DIM // 2, 1) * sin


def _qt_body(wt_ref, h_ref, g_ref, cos_ref, sin_ref, o_ref, *, heads):
    acc = lax.dot_general(wt_ref[...], h_ref[...], (((1,), (1,)), ((), ())),
                          preferred_element_type=f32)
    c = cos_ref[...]
    s = sin_ref[...]
    g = g_ref[...]
    half = HEAD_DIM // 2
    scale = HEAD_DIM ** -0.5 * LOG2_E
    for hh in range(heads):
        r0 = hh * HEAD_DIM
        x = acc[r0:r0 + HEAD_DIM]
        ms = jnp.mean(x * x, axis=0, keepdims=True)
        y = x * lax.rsqrt(ms + EPS) * g
        x1 = y[:half]
        x2 = y[half:]
        o_ref[r0:r0 + half, :] = ((x1 * c - x2 * s) * scale).astype(o_ref.dtype)
        o_ref[r0 + half:r0 + HEAD_DIM, :] = ((x2 * c + x1 * s) * scale).astype(o_ref.dtype)


def _proj_qt(h, wt_q, gain, cos_ht, sin_ht, layer):
    m, d = h.shape
    tm = _row_tile(m, 512)
    tn = 4 * HEAD_DIM
    return pl.pallas_call(
        functools.partial(_qt_body, heads=tn // HEAD_DIM),
        grid=(m // tm, ATTN_WIDTH // tn),
        in_specs=[_layer_spec((tn, d), layer, lambda i, j: (j, 0)),
                  pl.BlockSpec((tm, d), lambda i, j: (i, 0)),
                  _layer_spec((HEAD_DIM, 1), layer, lambda i, j: (0, 0)),
                  pl.BlockSpec((HEAD_DIM // 2, tm), lambda i, j: (0, i)),
                  pl.BlockSpec((HEAD_DIM // 2, tm), lambda i, j: (0, i))],
        out_specs=pl.BlockSpec((tn, tm), lambda i, j: (j, i)),
        out_shape=jax.ShapeDtypeStruct((ATTN_WIDTH, m), bf16),
        compiler_params=_cparams(("parallel", "parallel")),
        name="proj_q_t",
    )(wt_q, h, gain, cos_ht, sin_ht)


def _kv_body(h_ref, w_ref, wvt_ref, g_ref, cos_ref, sin_ref, k_ref, v_ref, kb_ref, vt_ref, *, chunk):
    h = h_ref[...]
    acc = jnp.dot(h, w_ref[...], preferred_element_type=f32)
    cos = cos_ref[...]
    sin = sin_ref[...]
    g = g_ref[...]
    tm = h.shape[0]
    for hh in range(N_KV_HEADS):
        sl = slice(hh * HEAD_DIM, (hh + 1) * HEAD_DIM)
        kh = _head_norm_rope(acc[:, sl], g, cos, sin)
        rows = pl.ds(hh, tm, stride=N_KV_HEADS)
        k_ref[rows, :] = kh
        v_ref[rows, :] = acc[:, KV_WIDTH + hh * HEAD_DIM:KV_WIDTH + (hh + 1) * HEAD_DIM]
        kb_ref[:, sl] = kh.astype(bf16)
    vt = lax.dot_general(wvt_ref[...], h, (((1,), (1,)), ((), ())), preferred_element_type=f32)
    for cc in range(vt_ref.shape[0]):
        vt_ref[cc] = vt[:, cc * chunk:(cc + 1) * chunk].astype(bf16)


def _proj_kv(h, w_kv, wt_v, gain, cos, sin, layer):
    m, d = h.shape
    tm = _row_tile(m, 512)
    chunk = min(tm, ATTN_Q_TILE)
    row = lambda i: (i, 0)
    fix = lambda i: (0, 0)
    return pl.pallas_call(
        functools.partial(_kv_body, chunk=chunk),
        grid=(m // tm,),
        in_specs=[pl.BlockSpec((tm, d), row),
                  _layer_spec((d, 2 * KV_WIDTH), layer, fix),
                  _layer_spec((KV_WIDTH, d), layer, fix),
                  _layer_spec((1, HEAD_DIM), layer, fix),
                  pl.BlockSpec((tm, HEAD_DIM), row),
                  pl.BlockSpec((tm, HEAD_DIM), row)],
        out_specs=[pl.BlockSpec((tm * N_KV_HEADS, HEAD_DIM), row),
                   pl.BlockSpec((tm * N_KV_HEADS, HEAD_DIM), row),
                   pl.BlockSpec((tm, KV_WIDTH), row),
                   pl.BlockSpec((tm // chunk, KV_WIDTH, chunk), lambda i: (i, 0, 0))],
        out_shape=[jax.ShapeDtypeStruct((m * N_KV_HEADS, HEAD_DIM), f32),
                   jax.ShapeDtypeStruct((m * N_KV_HEADS, HEAD_DIM), f32),
                   jax.ShapeDtypeStruct((m, KV_WIDTH), bf16),
                   jax.ShapeDtypeStruct((m // chunk, KV_WIDTH, chunk), bf16)],
        compiler_params=_cparams(("parallel",)),
        name="proj_kv",
    )(h, w_kv, wt_v, gain, cos, sin)


def _qit_body(wt_ref, h_ref, cos_ref, sin_ref, qit_ref, wt_out_ref):
    acc = lax.dot_general(wt_ref[...], h_ref[...], (((1,), (1,)), ((), ())),
                          preferred_element_type=f32)
    c = cos_ref[...]
    s = sin_ref[...]
    half = IDX_DIM // 2
    scale = IDX_DIM ** -0.5
    for hh in range(IDX_HEADS):
        r0 = hh * IDX_DIM
        x1 = acc[r0:r0 + half]
        x2 = acc[r0 + half:r0 + IDX_DIM]
        qit_ref[r0:r0 + half, :] = ((x1 * c - x2 * s) * scale).astype(qit_ref.dtype)
        qit_ref[r0 + half:r0 + IDX_DIM, :] = ((x2 * c + x1 * s) * scale).astype(qit_ref.dtype)
    n_qi = IDX_HEADS * IDX_DIM
    wt_out_ref[...] = acc[n_qi:n_qi + IDX_HEADS] * (IDX_HEADS ** -0.5)


def _proj_qit(h, wt_qi, cos_t, sin_t, layer):
    m, d = h.shape
    tm = _row_tile(m, 512)
    n_rows = wt_qi.shape[1]
    n_qi = IDX_HEADS * IDX_DIM
    return pl.pallas_call(
        _qit_body,
        grid=(m // tm,),
        in_specs=[_layer_spec((n_rows, d), layer, lambda i: (0, 0)),
                  pl.BlockSpec((tm, d), lambda i: (i, 0)),
                  pl.BlockSpec((IDX_DIM // 2, tm), lambda i: (0, i)),
                  pl.BlockSpec((IDX_DIM // 2, tm), lambda i: (0, i))],
        out_specs=[pl.BlockSpec((n_qi, tm), lambda i: (0, i)),
                   pl.BlockSpec((IDX_HEADS, tm), lambda i: (0, i))],
        out_shape=[jax.ShapeDtypeStruct((n_qi, m), bf16),
                   jax.ShapeDtypeStruct((IDX_HEADS, m), f32)],
        compiler_params=_cparams(("parallel",)),
        name="proj_qi_t",
    )(wt_qi, h, cos_t, sin_t)


def _ki_body(h_ref, w_ref, cos_ref, sina_ref, sinb_ref, ki_ref, kib_ref):
    x = jnp.dot(h_ref[...], w_ref[...], preferred_element_type=f32)
    half = IDX_DIM // 2
    r = (x * cos_ref[...] + pltpu.roll(x, LANES - half, 1) * sina_ref[...]
         + pltpu.roll(x, half, 1) * sinb_ref[...])
    ki = r[:, :IDX_DIM]
    ki_ref[...] = ki
    kib_ref[...] = ki.astype(bf16)


def _proj_ki(h, w_ki, cos_k, sin_a, sin_b, layer):
    m, d = h.shape
    tm = _row_tile(m, 512)
    row = lambda i: (i, 0)
    return pl.pallas_call(
        _ki_body,
        grid=(m // tm,),
        in_specs=[pl.BlockSpec((tm, d), row),
                  _layer_spec((d, LANES), layer, lambda i: (0, 0)),
                  pl.BlockSpec((tm, LANES), row),
                  pl.BlockSpec((tm, LANES), row),
                  pl.BlockSpec((tm, LANES), row)],
        out_specs=[pl.BlockSpec((tm, IDX_DIM), row)] * 2,
        out_shape=[jax.ShapeDtypeStruct((m, IDX_DIM), f32),
                   jax.ShapeDtypeStruct((m, IDX_DIM), bf16)],
        compiler_params=_cparams(("parallel",)),
        name="proj_ki",
    )(h, w_ki, cos_k, sin_a, sin_b)


def _plain_body(h_ref, w_ref, o_ref):
    o_ref[...] = jnp.dot(h_ref[...], w_ref[...], preferred_element_type=f32).astype(o_ref.dtype)


def _proj_plain(h, w, out_dtype, layer):
    m, d = h.shape
    n = w.shape[2]
    tm = _row_tile(m, 512)
    tn = 512
    return pl.pallas_call(
        _plain_body,
        grid=(m // tm, n // tn),
        in_specs=[pl.BlockSpec((tm, d), lambda i, j: (i, 0)),
                  _layer_spec((d, tn), layer, lambda i, j: (0, j))],
        out_specs=pl.BlockSpec((tm, tn), lambda i, j: (i, j)),
        out_shape=jax.ShapeDtypeStruct((m, n), out_dtype),
        compiler_params=_cparams(("parallel", "parallel")),
        name="proj_u",
    )(h, w)


def _sigmoid(x):
    return 1.0 / (1.0 + jnp.exp(-x))


def _zg_body(h_ref, w_ref, o_ref, *, silu_tiles):
    acc = jnp.dot(h_ref[...], w_ref[...], preferred_element_type=f32)
    sg = _sigmoid(acc)
    is_silu = pl.program_id(1) < silu_tiles
    o_ref[...] = jnp.where(is_silu, acc * sg, sg).astype(o_ref.dtype)


def _proj_zg(h, w_zg, layer):
    m, d = h.shape
    n = w_zg.shape[2]
    tm = _row_tile(m, 512)
    tn = 512
    return pl.pallas_call(
        functools.partial(_zg_body, silu_tiles=(ATTN_WIDTH + S5_WIDTH) // tn),
        grid=(m // tm, n // tn),
        in_specs=[pl.BlockSpec((tm, d), lambda i, j: (i, 0)),
                  _layer_spec((d, tn), layer, lambda i, j: (0, j))],
        out_specs=pl.BlockSpec((tm, tn), lambda i, j: (i, j)),
        out_shape=jax.ShapeDtypeStruct((m, n), bf16),
        compiler_params=_cparams(("parallel", "parallel")),
        name="proj_zg",
    )(h, w_zg)


def _sortable_key(score):
    b = pltpu.bitcast(score, i32)
    return b ^ ((b >> 31) & jnp.int32(0x7FFFFFFF))


def _kth_largest(count_ge, shape, k):
    zero = jnp.zeros(shape, i32)
    lowest = jnp.full(shape, INT_MIN, i32)
    c_zero = count_ge(zero)
    nonneg = c_zero >= k
    prefix = jnp.where(nonneg, zero, lowest)
    c_prefix = jnp.where(nonneg, c_zero, count_ge(lowest))

    def unsettled(state):
        bi, _, c_prefix = state
        return (bi < 31) & (jnp.max(jnp.abs(c_prefix - k)) > 0)

    def bit_body(state):
        bi, prefix, c_prefix = state
        cand = prefix | (jnp.int32(1) << (30 - bi))
        c_cand = count_ge(cand)
        take = c_cand >= k
        return bi + 1, jnp.where(take, cand, prefix), jnp.where(take, c_cand, c_prefix)

    _, thr, _ = lax.while_loop(unsettled, bit_body, (jnp.int32(0), prefix, c_prefix))
    return jnp.maximum(thr, jnp.int32(INT_MIN + 1))


def _sublane_allreduce(x, op):
    for shift in (4, 2, 1):
        x = op(x, pltpu.roll(x, shift, 0))
    return x


def _attn_prompt_body(qt_ref, qit_ref, wt_ref, z_ref, k_ref, vt_ref, ki_ref, o_ref,
                      key_scr, acc_scr, m_scr, l_scr, alpha_scr, s_scr, p_scr, bias_scr, *, tq, k_top):
    i = pl.program_id(1)
    n_chunks = i + 1
    half = tq // 2

    def score_keys(c, diagonal):
        for sub in range(2):
            r0 = pl.multiple_of(c * tq + sub * half, half)
            kic = ki_ref[pl.ds(r0, half), :]
            acc = jnp.zeros((half, tq), f32)
            for hh in range(IDX_HEADS):
                lg = jnp.dot(kic, qit_ref[hh * IDX_DIM:(hh + 1) * IDX_DIM, :],
                             preferred_element_type=f32)
                acc = acc + jnp.maximum(lg, 0.0) * wt_ref[hh:hh + 1, :]
            key = _sortable_key(acc)
            if diagonal:
                kpos = lax.broadcasted_iota(i32, (half, tq), 0) + sub * half
                qpos = lax.broadcasted_iota(i32, (half, tq), 1)
                key = jnp.where(kpos <= qpos, key, jnp.int32(INT_MIN))
            key_scr[pl.ds(r0, half), :] = key

    def full_chunk(c, carry):
        score_keys(c, False)
        return carry

    lax.fori_loop(0, i, full_chunk, 0)
    score_keys(i, True)

    def count_ge(cand):
        def body(c, cnt):
            kk = key_scr[pl.ds(pl.multiple_of(c * tq, tq), tq), :]
            hit = (kk >= cand).astype(i32)
            return cnt + jnp.sum(hit.reshape(tq // SUBLANES, SUBLANES, tq), axis=0)

        cnt = lax.fori_loop(0, n_chunks, body, jnp.zeros((SUBLANES, tq), i32))
        return jnp.sum(cnt, axis=0, keepdims=True)

    thr = _kth_largest(count_ge, (1, tq), k_top)

    m_scr[...] = jnp.full(m_scr.shape, -jnp.inf, f32)
    l_scr[...] = jnp.zeros(l_scr.shape, f32)
    acc_scr[...] = jnp.zeros(acc_scr.shape, f32)
    n_sub = tq // SUBLANES
    d_sub = HEAD_DIM // SUBLANES

    def attend_chunk(c, carry):
        r0 = pl.multiple_of(c * tq, tq)
        bias_scr[...] = jnp.where(key_scr[pl.ds(r0, tq), :] >= thr, 0.0, MASK_BIAS)
        for g in range(N_KV_HEADS):
            kc = k_ref[pl.ds(r0, tq), g * HEAD_DIM:(g + 1) * HEAD_DIM]
            for hh in range(HEADS_PER_KV):
                h = g * HEADS_PER_KV + hh
                s = jnp.dot(kc, qt_ref[h * HEAD_DIM:(h + 1) * HEAD_DIM, :], preferred_element_type=f32)
                s_scr[h] = s + bias_scr[...]
        blocks = [slice(j * ATTN_ROW_BLOCK, (j + 1) * ATTN_ROW_BLOCK) for j in range(tq // ATTN_ROW_BLOCK)]
        blk_sub = ATTN_ROW_BLOCK // SUBLANES
        for h in range(N_HEADS):
            mx = m_scr[h]
            for rows in blocks:
                mx = jnp.maximum(mx, jnp.max(s_scr[h, rows, :].reshape(blk_sub, SUBLANES, tq), axis=0))
            m_new = _sublane_allreduce(mx, jnp.maximum)
            alpha_scr[h] = jnp.exp2(m_scr[h] - m_new)
            m_scr[h] = m_new
        for h in range(N_HEADS):
            m_new = m_scr[h]
            psum = jnp.zeros((SUBLANES, tq), f32)
            for rows in blocks:
                p3 = jnp.exp2(s_scr[h, rows, :].reshape(blk_sub, SUBLANES, tq) - m_new[None])
                psum = psum + jnp.sum(p3, axis=0)
                p_scr[h, rows, :] = p3.reshape(ATTN_ROW_BLOCK, tq).astype(bf16)
            l_scr[h] = alpha_scr[h] * l_scr[h] + _sublane_allreduce(psum, jnp.add)
        for g in range(N_KV_HEADS):
            vtc = vt_ref[c, g * HEAD_DIM:(g + 1) * HEAD_DIM, :]
            for hh in range(HEADS_PER_KV):
                h = g * HEADS_PER_KV + hh
                hs = slice(h * HEAD_DIM, (h + 1) * HEAD_DIM)
                pv = jnp.dot(vtc, p_scr[h], preferred_element_type=f32)
                acc = acc_scr[hs, :].reshape(d_sub, SUBLANES, tq) * alpha_scr[h][None]
                acc_scr[hs, :] = acc.reshape(HEAD_DIM, tq) + pv
        return carry

    lax.fori_loop(0, n_chunks, attend_chunk, 0)

    for h in range(N_HEADS):
        hs = slice(h * HEAD_DIM, (h + 1) * HEAD_DIM)
        out_t = (acc_scr[hs, :].reshape(d_sub, SUBLANES, tq) / l_scr[h][None]).reshape(HEAD_DIM, tq)
        o_ref[:, hs] = (out_t.T * z_ref[:, hs].astype(f32)).astype(o_ref.dtype)


def _attn_prompt(qt, qit, wt, zg, kb, vt, kib, batch, seq):
    tq = ATTN_Q_TILE
    assert seq % tq == 0
    nq = seq // tq
    k_top = min(TOPK_MAX, seq // 4)
    assert k_top <= tq
    qrow = lambda b, i: (b * nq + i, 0)
    qcol = lambda b, i: (0, b * nq + i)
    per_b = lambda b, i: (b, 0)
    return pl.pallas_call(
        functools.partial(_attn_prompt_body, tq=tq, k_top=k_top),
        grid=(batch, nq),
        in_specs=[pl.BlockSpec((ATTN_WIDTH, tq), qcol),
                  pl.BlockSpec((IDX_HEADS * IDX_DIM, tq), qcol),
                  pl.BlockSpec((IDX_HEADS, tq), qcol),
                  pl.BlockSpec((tq, ATTN_WIDTH), qrow),
                  pl.BlockSpec((seq, KV_WIDTH), per_b),
                  pl.BlockSpec((nq, KV_WIDTH, tq), lambda b, i: (b, 0, 0)),
                  pl.BlockSpec((seq, IDX_DIM), per_b)],
        out_specs=pl.BlockSpec((tq, ATTN_WIDTH), qrow),
        out_shape=jax.ShapeDtypeStruct((batch * seq, ATTN_WIDTH), bf16),
        scratch_shapes=[pltpu.VMEM((seq, tq), i32),
                        pltpu.VMEM((ATTN_WIDTH, tq), f32),
                        pltpu.VMEM((N_HEADS, SUBLANES, tq), f32),
                        pltpu.VMEM((N_HEADS, SUBLANES, tq), f32),
                        pltpu.VMEM((N_HEADS, SUBLANES, tq), f32),
                        pltpu.VMEM((N_HEADS, tq, tq), f32),
                        pltpu.VMEM((N_HEADS, tq, tq), bf16),
                        pltpu.VMEM((tq, tq), f32)],
        compiler_params=_cparams(("parallel", "arbitrary")),
        name="attn_prompt",
    )(qt, qit, wt, zg, kb, vt, kib)


def _idx_sample_body(pt_ref, qi_ref, w_ref, *rest, n_pages, n_new, k_top, ppb):
    kid_refs = rest[:ppb]
    kinew_ref, key_ref, thr_ref = rest[ppb:]
    p = pl.program_id(1)
    tok = SAMPLE_ROWS

    def page_keys(ki_page_t):
        lg = jnp.dot(qi_ref[0], ki_page_t.astype(bf16), preferred_element_type=f32)
        wgt = jnp.maximum(lg, 0.0) * w_ref[0]
        score = jnp.sum(wgt.reshape(IDX_HEADS, tok, PAGE_SIZE), axis=0)
        return _sortable_key(score)

    for j in range(ppb):
        key_ref[0, p * ppb + j] = page_keys(kid_refs[j][...])

    @pl.when(p == pl.num_programs(1) - 1)
    def _():
        key = page_keys(kinew_ref[0])
        kpos = lax.broadcasted_iota(i32, (tok, PAGE_SIZE), 1)
        qtok = lax.broadcasted_iota(i32, (tok, PAGE_SIZE), 0)
        key_ref[0, n_pages] = jnp.where((kpos <= qtok) & (kpos < n_new), key, jnp.int32(INT_MIN))

        def count_ge(cand):
            def body(c, cnt):
                return cnt + (key_ref[0, c] >= cand).astype(i32)

            cnt = lax.fori_loop(0, n_pages + 1, body, jnp.zeros((tok, PAGE_SIZE), i32))
            return jnp.sum(cnt, axis=-1, keepdims=True)

        thr = _kth_largest(count_ge, (tok, 1), k_top)
        thr_ref[0] = jnp.broadcast_to(thr, (tok, PAGE_SIZE))


def _pages_per_step(n_pages, cap=PAGES_PER_STEP):
    ppb = min(cap, n_pages)
    assert n_pages % ppb == 0
    return ppb


def _page_spec(block, layer, ppb, j):
    return pl.BlockSpec(block, lambda b, p, pt: (layer, pt[b, p * ppb + j], 0, 0))


def _idx_sample(page_table, qi_rows, w_rows, cache_kidx, layer, ki_new_pages, n_new):
    nb, n_pages = page_table.shape
    tok = SAMPLE_ROWS
    ppb = _pages_per_step(n_pages, IDX_PAGES_PER_STEP)
    k_top = min(TOPK_MAX, (n_pages * PAGE_SIZE + n_new) // 4)
    per_b = lambda b, p, pt: (b, 0, 0)
    grid_spec = pltpu.PrefetchScalarGridSpec(
        num_scalar_prefetch=1,
        grid=(nb, n_pages // ppb),
        in_specs=[pl.BlockSpec((1, IDX_HEADS * tok, IDX_DIM), per_b),
                  pl.BlockSpec((1, IDX_HEADS * tok, 1), per_b)]
                 + [_page_spec((None, None, IDX_DIM, PAGE_SIZE), layer, ppb, j) for j in range(ppb)]
                 + [pl.BlockSpec((1, IDX_DIM, PAGE_SIZE), per_b)],
        out_specs=[pl.BlockSpec((1, n_pages + 1, tok, PAGE_SIZE), lambda b, p, pt: (b, 0, 0, 0)),
                   pl.BlockSpec((1, tok, PAGE_SIZE), per_b)],
    )
    return pl.pallas_call(
        functools.partial(_idx_sample_body, n_pages=n_pages, n_new=n_new, k_top=k_top, ppb=ppb),
        grid_spec=grid_spec,
        out_shape=[jax.ShapeDtypeStruct((nb, n_pages + 1, tok, PAGE_SIZE), i32),
                   jax.ShapeDtypeStruct((nb, tok, PAGE_SIZE), i32)],
        compiler_params=_cparams(("parallel", "arbitrary")),
        name="idx_sample",
    )(page_table, qi_rows, w_rows, *([cache_kidx] * ppb), ki_new_pages)


def _attn_sample_body(pt_ref, q_ref, key_ref, keyn_ref, thr_ref, *rest, ppb):
    k_refs = rest[:ppb]
    v_refs = rest[ppb:2 * ppb]
    kn_ref, vn_ref, o_ref, acc_scr, m_scr, l_scr = rest[2 * ppb:]
    p = pl.program_id(1)
    tok = SAMPLE_ROWS
    rows_g = HEADS_PER_KV * tok
    thr = thr_ref[0]

    @pl.when(p == 0)
    def _():
        m_scr[...] = jnp.full(m_scr.shape, -jnp.inf, f32)
        l_scr[...] = jnp.zeros(l_scr.shape, f32)
        acc_scr[...] = jnp.zeros(acc_scr.shape, f32)

    def attend(bias_tok, k_of_group, v_of_group):
        bias = jnp.concatenate([bias_tok] * HEADS_PER_KV, axis=0)
        for g in range(N_KV_HEADS):
            rs = slice(g * rows_g, (g + 1) * rows_g)
            s = lax.dot_general(q_ref[0, rs, :], k_of_group(g), (((1,), (1,)), ((), ())),
                                preferred_element_type=f32) + bias
            m_prev = m_scr[rs, :]
            m_new = jnp.maximum(m_prev, jnp.max(s, axis=-1, keepdims=True))
            alpha = jnp.exp2(m_prev - m_new)
            pr = jnp.exp2(s - m_new)
            l_scr[rs, :] = alpha * l_scr[rs, :] + jnp.sum(pr, axis=-1, keepdims=True)
            acc_scr[rs, :] = alpha * acc_scr[rs, :] + jnp.dot(
                pr.astype(bf16), v_of_group(g), preferred_element_type=f32)
            m_scr[rs, :] = m_new

    def mask_bias(keys):
        return jnp.where(keys >= thr, 0.0, MASK_BIAS).astype(f32)

    def cached(refs):
        def of_group(g):
            rows = pl.ds(g, PAGE_SIZE, stride=N_KV_HEADS)
            return jnp.concatenate([r[rows, :].astype(bf16) for r in refs], axis=0)
        return of_group

    attend(jnp.concatenate([mask_bias(key_ref[0, j]) for j in range(ppb)], axis=1),
           cached(k_refs), cached(v_refs))

    @pl.when(p == pl.num_programs(1) - 1)
    def _():
        def fresh(ref):
            return lambda g: ref[0, :, g * HEAD_DIM:(g + 1) * HEAD_DIM].astype(bf16)

        attend(mask_bias(keyn_ref[0, 0]), fresh(kn_ref), fresh(vn_ref))
        o_ref[0] = acc_scr[...] / l_scr[...]


def _attn_sample(page_table, q_rows, keys, thr, cache_k, cache_v, layer, k_new_pages, v_new_pages):
    nb, n_pages = page_table.shape
    tok = SAMPLE_ROWS
    rows = N_HEADS * tok
    ppb = _pages_per_step(n_pages)
    per_b = lambda b, p, pt: (b, 0, 0)
    page_block = (None, None, PAGE_SIZE * N_KV_HEADS, HEAD_DIM)
    grid_spec = pltpu.PrefetchScalarGridSpec(
        num_scalar_prefetch=1,
        grid=(nb, n_pages // ppb),
        in_specs=[pl.BlockSpec((1, rows, HEAD_DIM), per_b),
                  pl.BlockSpec((1, ppb, tok, PAGE_SIZE), lambda b, p, pt: (b, p, 0, 0)),
                  pl.BlockSpec((1, 1, tok, PAGE_SIZE), lambda b, p, pt: (b, n_pages, 0, 0)),
                  pl.BlockSpec((1, tok, PAGE_SIZE), per_b)]
                 + [_page_spec(page_block, layer, ppb, j) for j in range(ppb)]
                 + [_page_spec(page_block, layer, ppb, j) for j in range(ppb)]
                 + [pl.BlockSpec((1, PAGE_SIZE, KV_WIDTH), per_b),
                    pl.BlockSpec((1, PAGE_SIZE, KV_WIDTH), per_b)],
        out_specs=pl.BlockSpec((1, rows, HEAD_DIM), per_b),
        scratch_shapes=[pltpu.VMEM((rows, HEAD_DIM), f32),
                        pltpu.VMEM((rows, 1), f32),
                        pltpu.VMEM((rows, 1), f32)],
    )
    return pl.pallas_call(
        functools.partial(_attn_sample_body, ppb=ppb),
        grid_spec=grid_spec,
        out_shape=jax.ShapeDtypeStruct((nb, rows, HEAD_DIM), f32),
        compiler_params=_cparams(("parallel", "arbitrary")),
        name="attn_sample",
    )(page_table, q_rows, keys, keys, thr, *([cache_k] * ppb), *([cache_v] * ppb),
      k_new_pages, v_new_pages)


def _gate_mul_body(a_ref, z_ref, o_ref):
    o_ref[...] = (a_ref[...] * z_ref[...].astype(f32)).astype(o_ref.dtype)


def _gate_mul(a, zg):
    m, n = a.shape
    return pl.pallas_call(
        _gate_mul_body,
        grid=(1,),
        in_specs=[pl.BlockSpec((m, n), lambda i: (0, 0)),
                  pl.BlockSpec((m, n), lambda i: (0, 0))],
        out_specs=pl.BlockSpec((m, n), lambda i: (0, 0)),
        out_shape=jax.ShapeDtypeStruct((m, n), bf16),
        compiler_params=_cparams(("arbitrary",)),
        name="gate_mul",
    )(a, zg)


def _s5_param_body(lr_ref, li_ref, ldt_ref, bre_ref, bim_ref, ar_ref, ai_ref, bbre_ref, bbim_ref):
    lr = lr_ref[...]
    li = li_ref[...]
    dt = jnp.exp(ldt_ref[...])
    mag = jnp.exp(lr * dt)
    ar = mag * jnp.cos(li * dt)
    ai = mag * jnp.sin(li * dt)
    den = lr * lr + li * li
    nr = ar - 1.0
    cr = (nr * lr + ai * li) / den
    ci = (ai * lr - nr * li) / den
    ar_ref[...] = ar
    ai_ref[...] = ai
    for h in range(S5_GROUP):
        bbre_ref[h] = cr * bre_ref[h] - ci * bim_ref[h]
        bbim_ref[h] = cr * bim_ref[h] + ci * bre_ref[h]


def _s5_params(lam_re, lam_im, log_dt, b_re, b_im):
    depth = lam_re.shape[0]
    gs = (None, S5_GROUPS, S5_STATE)
    bs = (None, S5_GROUP, S5_GROUPS, S5_STATE)
    at3 = lambda l: (l, 0, 0)
    at4 = lambda l: (l, 0, 0, 0)
    return pl.pallas_call(
        _s5_param_body,
        grid=(depth,),
        in_specs=[pl.BlockSpec(gs, at3), pl.BlockSpec(gs, at3),
                  pl.BlockSpec((None, S5_GROUPS, 1), at3),
                  pl.BlockSpec(bs, at4), pl.BlockSpec(bs, at4)],
        out_specs=[pl.BlockSpec(gs, at3), pl.BlockSpec(gs, at3),
                   pl.BlockSpec(bs, at4), pl.BlockSpec(bs, at4)],
        out_shape=[jax.ShapeDtypeStruct((depth, S5_GROUPS, S5_STATE), f32),
                   jax.ShapeDtypeStruct((depth, S5_GROUPS, S5_STATE), f32),
                   jax.ShapeDtypeStruct((depth, S5_GROUP, S5_GROUPS, S5_STATE), f32),
                   jax.ShapeDtypeStruct((depth, S5_GROUP, S5_GROUPS, S5_STATE), f32)],
        compiler_params=_cparams(("arbitrary",)),
        name="s5_params",
    )(lam_re, lam_im, log_dt.reshape(depth, S5_GROUPS, 1), b_re, b_im)


def _gelu_tanh(x):
    return 0.5 * x * (1.0 + jnp.tanh(math.sqrt(2.0 / math.pi) * (x + 0.044715 * (x * x * x))))


def _s5_body(u_ref, zs_ref, h0r_ref, h0i_ref, ar_ref, ai_ref, wbu_ref, wcr_ref, wci_ref, d_ref, wglu_ref,
             o_ref, sr_ref, si_ref, xr_scr, xi_scr, y_scr, str_scr, sti_scr, *, tc, n_last):
    c = pl.program_id(1)

    @pl.when(c == 0)
    def _():
        str_scr[...] = h0r_ref[0]
        sti_scr[...] = h0i_ref[0]

    n_ct = S5_SG_STATE // LANES

    def gather_planes(scr, rows):
        return jnp.concatenate([scr[j, rows, :] for j in range(n_ct)], axis=1)

    def scatter_planes(scr, rows, val):
        for j in range(n_ct):
            scr[j, rows, :] = val[:, j * LANES:(j + 1) * LANES]

    for sg in range(S5_SG):
        rows = pl.ds(sg, tc, stride=S5_SG)
        u_sg = u_ref[:, sg * LANES:(sg + 1) * LANES]
        bu = jnp.dot(u_sg.astype(bf16), wbu_ref[sg], preferred_element_type=f32)
        scatter_planes(xr_scr, rows, bu[:, :S5_SG_STATE])
        scatter_planes(xi_scr, rows, bu[:, S5_SG_STATE:])

    ar = ar_ref[...]
    ai = ai_ref[...]

    def step(t, carry):
        xr, xi = carry
        r = pl.ds(pl.multiple_of(t * S5_SG, S5_SG), S5_SG)
        nr = ar * xr - ai * xi + gather_planes(xr_scr, r)
        ni = ar * xi + ai * xr + gather_planes(xi_scr, r)
        scatter_planes(xr_scr, r, nr)
        scatter_planes(xi_scr, r, ni)
        return nr, ni

    xr, xi = lax.fori_loop(0, tc, step, (str_scr[...], sti_scr[...]))
    str_scr[...] = xr
    sti_scr[...] = xi

    for sg in range(S5_SG):
        rows = pl.ds(sg, tc, stride=S5_SG)
        cs = slice(sg * LANES, (sg + 1) * LANES)
        y = (jnp.dot(gather_planes(xr_scr, rows).astype(bf16), wcr_ref[sg], preferred_element_type=f32)
             - jnp.dot(gather_planes(xi_scr, rows).astype(bf16), wci_ref[sg], preferred_element_type=f32)
             + d_ref[:, cs] * u_ref[:, cs])
        y_scr[:, cs] = _gelu_tanh(y)

    y = y_scr[...]
    gate = _sigmoid(jnp.dot(y.astype(bf16), wglu_ref[...], preferred_element_type=f32))
    o_ref[...] = (y * gate * zs_ref[...].astype(f32)).astype(o_ref.dtype)

    @pl.when(c == pl.num_programs(1) - 1)
    def _():
        last = slice((n_last - 1) * S5_SG, n_last * S5_SG)
        sr_ref[0] = gather_planes(xr_scr, last)
        si_ref[0] = gather_planes(xi_scr, last)


def _s5(u, zg, h0_re, h0_im, prm, layer, nb, seq, n_real):
    tc = min(seq, 256)
    nch = seq // tc
    n_last = n_real - (nch - 1) * tc
    zs_col = ATTN_WIDTH // S5_WIDTH
    fix2 = lambda b, c: (0, 0)
    fix3 = lambda b, c: (0, 0, 0)
    st = lambda b, c: (b, 0, 0)
    out, s_re, s_im = pl.pallas_call(
        functools.partial(_s5_body, tc=tc, n_last=n_last),
        grid=(nb, nch),
        in_specs=[pl.BlockSpec((tc, S5_WIDTH), lambda b, c: (b * nch + c, 0)),
                  pl.BlockSpec((tc, S5_WIDTH), lambda b, c: (b * nch + c, zs_col)),
                  pl.BlockSpec((1, S5_SG, S5_SG_STATE), st),
                  pl.BlockSpec((1, S5_SG, S5_SG_STATE), st),
                  _layer_spec((S5_SG, S5_SG_STATE), layer, fix2),
                  _layer_spec((S5_SG, S5_SG_STATE), layer, fix2),
                  _layer_spec((S5_SG, LANES, 2 * S5_SG_STATE), layer, fix3),
                  _layer_spec((S5_SG, S5_SG_STATE, LANES), layer, fix3),
                  _layer_spec((S5_SG, S5_SG_STATE, LANES), layer, fix3),
                  _layer_spec((1, S5_WIDTH), layer, fix2),
                  _layer_spec((S5_WIDTH, S5_WIDTH), layer, fix2)],
        out_specs=[pl.BlockSpec((tc, S5_WIDTH), lambda b, c: (b * nch + c, 0)),
                   pl.BlockSpec((1, S5_SG, S5_SG_STATE), st),
                   pl.BlockSpec((1, S5_SG, S5_SG_STATE), st)],
        out_shape=[jax.ShapeDtypeStruct((nb * seq, S5_WIDTH), bf16),
                   jax.ShapeDtypeStruct((nb, S5_SG, S5_SG_STATE), f32),
                   jax.ShapeDtypeStruct((nb, S5_SG, S5_SG_STATE), f32)],
        scratch_shapes=[pltpu.VMEM((S5_SG_STATE // LANES, tc * S5_SG, LANES), f32),
                        pltpu.VMEM((S5_SG_STATE // LANES, tc * S5_SG, LANES), f32),
                        pltpu.VMEM((tc, S5_WIDTH), f32),
                        pltpu.VMEM((S5_SG, S5_SG_STATE), f32),
                        pltpu.VMEM((S5_SG, S5_SG_STATE), f32)],
        compiler_params=_cparams(("parallel", "arbitrary")),
        name="s5_scan",
    )(u, zg, h0_re.reshape(nb, S5_SG, S5_SG_STATE), h0_im.reshape(nb, S5_SG, S5_SG_STATE),
      prm["a_re"], prm["a_im"], prm["w_bu"], prm["w_c_re"], prm["w_c_im"], prm["d"], prm["w_glu"])
    return out, s_re.reshape(nb, S5_GROUPS, S5_STATE), s_im.reshape(nb, S5_GROUPS, S5_STATE)


def _merge_body(a_ref, s_ref, wa_ref, ws_ref, ga_ref, gs_ref, o_ref):
    o_a = jnp.dot(a_ref[...], wa_ref[...], preferred_element_type=f32)
    o_s = jnp.dot(s_ref[...], ws_ref[...], preferred_element_type=f32)
    o_ref[...] = (ga_ref[...].astype(f32) * o_a + gs_ref[...].astype(f32) * o_s).astype(o_ref.dtype)


def _merge(a_in, s_in, w_a, w_s, zg, layer):
    m = a_in.shape[0]
    tm = _row_tile(m, 512)
    tn = 512
    ga0 = (ATTN_WIDTH + S5_WIDTH) // tn
    gs0 = (ATTN_WIDTH + S5_WIDTH + D_MODEL) // tn
    return pl.pallas_call(
        _merge_body,
        grid=(m // tm, D_MODEL // tn),
        in_specs=[pl.BlockSpec((tm, ATTN_WIDTH), lambda i, j: (i, 0)),
                  pl.BlockSpec((tm, S5_WIDTH), lambda i, j: (i, 0)),
                  _layer_spec((ATTN_WIDTH, tn), layer, lambda i, j: (0, j)),
                  _layer_spec((S5_WIDTH, tn), layer, lambda i, j: (0, j)),
                  pl.BlockSpec((tm, tn), lambda i, j: (i, ga0 + j)),
                  pl.BlockSpec((tm, tn), lambda i, j: (i, gs0 + j))],
        out_specs=pl.BlockSpec((tm, tn), lambda i, j: (i, j)),
        out_shape=jax.ShapeDtypeStruct((m, D_MODEL), bf16),
        compiler_params=_cparams(("parallel", "parallel")),
        name="merge",
    )(a_in, s_in, w_a, w_s, zg, zg)


def _out_body(x_ref, m_ref, w_ref, o_ref):
    o_ref[...] = x_ref[...] + jnp.dot(m_ref[...], w_ref[...], preferred_element_type=f32)


def _out_proj(x, merged, w_out, layer):
    m = x.shape[0]
    tm = _row_tile(m, 512)
    tn = 512
    return pl.pallas_call(
        _out_body,
        grid=(m // tm, D_MODEL // tn),
        in_specs=[pl.BlockSpec((tm, tn), lambda i, j: (i, j)),
                  pl.BlockSpec((tm, D_MODEL), lambda i, j: (i, 0)),
                  _layer_spec((D_MODEL, tn), layer, lambda i, j: (0, j))],
        out_specs=pl.BlockSpec((tm, tn), lambda i, j: (i, j)),
        out_shape=jax.ShapeDtypeStruct((m, D_MODEL), f32),
        compiler_params=_cparams(("parallel", "parallel")),
        name="out_proj",
    )(x, merged, w_out)


def _rope_tables(pos):
    posf = pos.astype(f32)[:, None]
    half = HEAD_DIM // 2
    inv = ROPE_THETA ** (-jnp.arange(half, dtype=f32) / half)
    ang = posf * inv[None, :]
    cos, sin = jnp.cos(ang), jnp.sin(ang)
    cos_h = jnp.concatenate([cos, cos], axis=1)
    sin_h = jnp.concatenate([-sin, sin], axis=1)
    half_i = IDX_DIM // 2
    inv_i = ROPE_THETA ** (-jnp.arange(half_i, dtype=f32) / half_i)
    ang_i = posf * inv_i[None, :]
    cos_i, sin_i = jnp.cos(ang_i), jnp.sin(ang_i)
    zeros = jnp.zeros_like(cos_i)
    pad = jnp.zeros((pos.shape[0], LANES - IDX_DIM), f32)
    cos_k = jnp.concatenate([cos_i, cos_i, pad], axis=1)
    sin_a = jnp.concatenate([-sin_i, zeros, pad], axis=1)
    sin_b = jnp.concatenate([zeros, sin_i, pad], axis=1)
    return dict(cos_h=cos_h, sin_h=sin_h, cos_k=cos_k, sin_a=sin_a, sin_b=sin_b,
                cos_ht=cos.T, sin_ht=sin.T, cos_t=cos_i.T, sin_t=sin_i.T)


def _pack_weights(w_in, w_glu, w_br_attn, w_br_s5, w_out, norm_gain, q_norm_gain, k_norm_gain):
    depth = w_in.shape[0]
    offs = [0]
    for s in IN_SIZES:
        offs.append(offs[-1] + s)
    w_bf = w_in.astype(bf16)
    seg = lambda k: w_bf[:, :, offs[k]:offs[k + 1]]
    t = lambda a: jnp.swapaxes(a, 1, 2)
    return dict(
        wt_q=t(seg(0)), wt_v=t(seg(2)),
        w_kv=jnp.concatenate([seg(1), seg(2)], axis=2),
        wt_qi=t(jnp.concatenate([seg(3), seg(5)], axis=2)),
        w_ki=jnp.pad(seg(4), ((0, 0), (0, 0), (0, LANES - IDX_DIM))),
        w_u=seg(7),
        w_zg=jnp.concatenate([seg(6), seg(8), seg(9), seg(10)], axis=2),
        w_glu=w_glu.astype(bf16), w_br_attn=w_br_attn.astype(bf16),
        w_br_s5=w_br_s5.astype(bf16), w_out=w_out.astype(bf16),
        norm_gain=norm_gain.reshape(depth, 1, D_MODEL),
        q_gain=q_norm_gain.reshape(depth, HEAD_DIM, 1),
        k_gain=k_norm_gain.reshape(depth, 1, HEAD_DIM))


def _block_diag(blocks):
    depth, sg, gg, r, c = blocks.shape
    eye = jnp.eye(gg, dtype=blocks.dtype)
    return jnp.einsum("lsgrc,gk->lsgrkc", blocks, eye).reshape(depth, sg, gg * r, gg * c)


def _s5_stacked_params(lam_re, lam_im, log_dt, b_re, b_im, c_re, c_im, d, w_glu_bf16):
    depth = lam_re.shape[0]
    a_re, a_im, bb_re, bb_im = _s5_params(lam_re, lam_im, log_dt,
                                          jnp.transpose(b_re, (0, 3, 1, 2)), jnp.transpose(b_im, (0, 3, 1, 2)))
    def bu_blocks(bb):
        return jnp.transpose(bb, (0, 2, 1, 3)).reshape(depth, S5_SG, S5_SG_GROUPS, S5_GROUP, S5_STATE)
    w_bu = jnp.concatenate([_block_diag(bu_blocks(bb_re)), _block_diag(bu_blocks(bb_im))], axis=3)
    def c_blocks(cm):
        return jnp.transpose(cm, (0, 1, 3, 2)).reshape(depth, S5_SG, S5_SG_GROUPS, S5_STATE, S5_GROUP)
    return dict(a_re=a_re.reshape(depth, S5_SG, S5_SG_STATE), a_im=a_im.reshape(depth, S5_SG, S5_SG_STATE),
                w_bu=w_bu.astype(bf16),
                w_c_re=_block_diag(c_blocks(c_re)).astype(bf16),
                w_c_im=_block_diag(c_blocks(c_im)).astype(bf16),
                d=d.reshape(depth, 1, S5_WIDTH), w_glu=w_glu_bf16)


def _projections(x, wts, tabs, layer):
    h = _rmsnorm(x, wts["norm_gain"], layer)
    qt = _proj_qt(h, wts["wt_q"], wts["q_gain"], tabs["cos_ht"], tabs["sin_ht"], layer)
    k, v, kb, vt = _proj_kv(h, wts["w_kv"], wts["wt_v"], wts["k_gain"], tabs["cos_h"], tabs["sin_h"], layer)
    qit, wt = _proj_qit(h, wts["wt_qi"], tabs["cos_t"], tabs["sin_t"], layer)
    ki, kib = _proj_ki(h, wts["w_ki"], tabs["cos_k"], tabs["sin_a"], tabs["sin_b"], layer)
    u = _proj_plain(h, wts["w_u"], f32, layer)
    zg = _proj_zg(h, wts["w_zg"], layer)
    return dict(qt=qt, k=k, v=v, kb=kb, vt=vt, qit=qit, wt=wt, ki=ki, kib=kib, u=u, zg=zg)


def _finish(x, a_in, s_in, zg, wts, layer):
    merged = _merge(a_in, s_in, wts["w_br_attn"], wts["w_br_s5"], zg, layer)
    return _out_proj(x, merged, wts["w_out"], layer)


def kernel(x_prompt, x_sample, cache_k, cache_v, cache_kidx, state_s5_re, state_s5_im, page_table, norm_gain, w_in, q_norm_gain, k_norm_gain, s5_lam_re, s5_lam_im, s5_log_dt, s5_b_re, s5_b_im, s5_c_re, s5_c_im, s5_d, w_glu, w_br_attn, w_br_s5, w_out):
    depth = w_in.shape[0]
    b_p, t_p = x_prompt.shape[:2]
    b_s, t_s = x_sample.shape[:2]
    tok = SAMPLE_ROWS
    assert t_s <= tok
    n_pages = page_table.shape[1]
    past = n_pages * PAGE_SIZE
    n_phys = cache_k.shape[1]

    tabs_p = _rope_tables(jnp.tile(jnp.arange(t_p, dtype=i32), b_p))
    tabs_s = _rope_tables(jnp.tile(past + jnp.arange(tok, dtype=i32), b_s))

    xp = x_prompt.reshape(b_p * t_p, D_MODEL)
    xs = jnp.pad(x_sample, ((0, 0), (0, tok - t_s), (0, 0))).reshape(b_s * tok, D_MODEL)

    cache_k4 = cache_k.reshape(depth, n_phys, PAGE_SIZE * N_KV_HEADS, HEAD_DIM)
    cache_v4 = cache_v.reshape(depth, n_phys, PAGE_SIZE * N_KV_HEADS, HEAD_DIM)
    cache_kidx_t = jnp.swapaxes(cache_kidx, 2, 3)
    zeros_state = jnp.zeros((b_p, S5_GROUPS, S5_STATE), f32)

    def new_page(a):
        w = a.shape[-1]
        return jnp.pad(a.reshape(b_s, tok, w), ((0, 0), (0, PAGE_SIZE - tok), (0, 0)))

    wts = _pack_weights(w_in, w_glu, w_br_attn, w_br_s5, w_out, norm_gain, q_norm_gain, k_norm_gain)
    s5p = _s5_stacked_params(s5_lam_re, s5_lam_im, s5_log_dt, s5_b_re, s5_b_im, s5_c_re, s5_c_im, s5_d,
                             wts["w_glu"])

    outs_p, outs_s = [], []
    for l in range(depth):
        pp = _projections(xp, wts, tabs_p, l)
        a_in = _attn_prompt(pp["qt"], pp["qit"], pp["wt"], pp["zg"], pp["kb"], pp["vt"], pp["kib"], b_p, t_p)
        s_in, sr_p, si_p = _s5(pp["u"], pp["zg"], zeros_state, zeros_state, s5p, l, b_p, t_p, t_p)
        xp = _finish(xp, a_in, s_in, pp["zg"], wts, l)
        outs_p.append((pp["k"].reshape(b_p, t_p, N_KV_HEADS, HEAD_DIM),
                       pp["v"].reshape(b_p, t_p, N_KV_HEADS, HEAD_DIM),
                       pp["ki"].reshape(b_p, t_p, IDX_DIM), sr_p, si_p))

        ps = _projections(xs, wts, tabs_s, l)
        qi_rows = jnp.transpose(ps["qit"].reshape(IDX_HEADS, IDX_DIM, b_s, tok), (2, 0, 3, 1)
                                ).reshape(b_s, IDX_HEADS * tok, IDX_DIM)
        w_rows = jnp.transpose(ps["wt"].reshape(IDX_HEADS, b_s, tok), (1, 0, 2)
                               ).reshape(b_s, IDX_HEADS * tok, 1)
        keys, thr = _idx_sample(page_table, qi_rows, w_rows, cache_kidx_t, l,
                                jnp.swapaxes(new_page(ps["ki"]), 1, 2), t_s)
        q_rows = jnp.transpose(ps["qt"].reshape(N_HEADS, HEAD_DIM, b_s, tok), (2, 0, 3, 1)
                               ).reshape(b_s, N_HEADS * tok, HEAD_DIM)
        k_s = ps["k"].reshape(b_s * tok, KV_WIDTH)
        v_s = ps["v"].reshape(b_s * tok, KV_WIDTH)
        o_rows = _attn_sample(page_table, q_rows, keys, thr, cache_k4, cache_v4, l,
                              new_page(k_s), new_page(v_s))
        attn_s = jnp.transpose(o_rows.reshape(b_s, N_HEADS, tok, HEAD_DIM), (0, 2, 1, 3)
                               ).reshape(b_s * tok, ATTN_WIDTH)
        a_in_s = _gate_mul(attn_s, ps["zg"][:, :ATTN_WIDTH])
        s_in_s, sr_s, si_s = _s5(ps["u"], ps["zg"], state_s5_re[l], state_s5_im[l], s5p, l, b_s, tok, t_s)
        xs = _finish(xs, a_in_s, s_in_s, ps["zg"], wts, l)
        real = lambda a: a.reshape((b_s, tok) + a.shape[1:])[:, :t_s]
        outs_s.append((real(k_s).reshape(b_s, t_s, N_KV_HEADS, HEAD_DIM),
                       real(v_s).reshape(b_s, t_s, N_KV_HEADS, HEAD_DIM),
                       real(ps["ki"]), sr_s, si_s))

    k_prompt, v_prompt, kidx_prompt, s5_re_prompt, s5_im_prompt = [jnp.stack(a) for a in zip(*outs_p)]
    k_sample, v_sample, kidx_sample, s5_re_sample, s5_im_sample = [jnp.stack(a) for a in zip(*outs_s)]
    y_prompt = xp.reshape(b_p, t_p, D_MODEL)
    y_sample = xs.reshape(b_s, tok, D_MODEL)[:, :t_s]
    return (y_prompt, y_sample, k_prompt, v_prompt, kidx_prompt, s5_re_prompt, s5_im_prompt,
            k_sample, v_sample, kidx_sample, s5_re_sample, s5_im_sample)
```

```python
import functools
import math

import jax
import jax.numpy as jnp
from jax import lax
from jax.experimental import pallas as pl
from jax.experimental.pallas import tpu as pltpu

D_MODEL = 2048
PAGE_SIZE = 128
N_HEADS = 16
HEAD_DIM = 128
N_KV_HEADS = 4
HEADS_PER_KV = N_HEADS // N_KV_HEADS
ATTN_WIDTH = N_HEADS * HEAD_DIM
KV_WIDTH = N_KV_HEADS * HEAD_DIM
IDX_HEADS = 16
IDX_DIM = 64
TOPK_MAX = 256
S5_WIDTH = D_MODEL // 2
S5_GROUP = 16
S5_GROUPS = S5_WIDTH // S5_GROUP
S5_STATE = 64
ROPE_THETA = 10000.0
EPS = 1e-6
IN_SIZES = (ATTN_WIDTH, KV_WIDTH, KV_WIDTH, IDX_HEADS * IDX_DIM, IDX_DIM, IDX_HEADS,
            ATTN_WIDTH, S5_WIDTH, S5_WIDTH, D_MODEL, D_MODEL)

LANES = 128
SUBLANES = 8
VMEM_LIMIT_BYTES = 56 * 1024 * 1024
MM_ROW_TILE = 1024
MM_COL_TILE = 1024

S5_SG = S5_WIDTH // LANES
S5_SG_GROUPS = S5_GROUPS // S5_SG
S5_SG_STATE = S5_SG_GROUPS * S5_STATE

SAMPLE_ROWS = 16

ATTN_Q_TILE = 256
ATTN_ROW_BLOCK = 64
PAGES_PER_STEP = 16
IDX_PAGES_PER_STEP = 32
LOG2_E = math.log2(math.e)

INT_MIN = -2 ** 31
MASK_BIAS = -1e30

f32 = jnp.float32
bf16 = jnp.bfloat16
i32 = jnp.int32


def _cparams(sem):
    return pltpu.CompilerParams(dimension_semantics=sem, vmem_limit_bytes=VMEM_LIMIT_BYTES)


def _row_tile(m, cap):
    return m if m <= cap else cap


def _norm_body(x_ref, g_ref, o_ref):
    x = x_ref[...]
    ms = jnp.mean(x * x, axis=-1, keepdims=True)
    o_ref[...] = (x * lax.rsqrt(ms + EPS) * g_ref[...]).astype(o_ref.dtype)


def _layer_spec(block, layer, index_map):
    return pl.BlockSpec((None,) + tuple(block), lambda *idx: (layer,) + tuple(index_map(*idx)))


def _rmsnorm(x, gains, layer):
    m, d = x.shape
    tm = _row_tile(m, 512)
    return pl.pallas_call(
        _norm_body,
        grid=(m // tm,),
        in_specs=[pl.BlockSpec((tm, d), lambda i: (i, 0)),
                  _layer_spec((1, d), layer, lambda i: (0, 0))],
        out_specs=pl.BlockSpec((tm, d), lambda i: (i, 0)),
        out_shape=jax.ShapeDtypeStruct((m, d), bf16),
        compiler_params=_cparams(("parallel",)),
        name="rmsnorm",
    )(x, gains)


def _head_norm_rope(x, gain, cos, sin):
    ms = jnp.mean(x * x, axis=-1, keepdims=True)
    y = x * lax.rsqrt(ms + EPS) * gain
    return y * cos + pltpu.roll(y, HEAD_DIM // 2, 1) * sin


def _qt_body(wt_ref, h_ref, g_ref, cos_ref, sin_ref, o_ref, *, heads):
    acc = lax.dot_general(wt_ref[...], h_ref[...], (((1,), (1,)), ((), ())),
                          preferred_element_type=f32)
    c = cos_ref[...]
    s = sin_ref[...]
    g = g_ref[...]
    half = HEAD_DIM // 2
    scale = HEAD_DIM ** -0.5 * LOG2_E
    for hh in range(heads):
        r0 = hh * HEAD_DIM
        x = acc[r0:r0 + HEAD_DIM]
        ms = jnp.mean(x * x, axis=0, keepdims=True)
        y = x * lax.rsqrt(ms + EPS) * g
        x1 = y[:half]
        x2 = y[half:]
        o_ref[r0:r0 + half, :] = ((x1 * c - x2 * s) * scale).astype(o_ref.dtype)
        o_ref[r0 + half:r0 + HEAD_DIM, :] = ((x2 * c + x1 * s) * scale).astype(o_ref.dtype)


def _proj_qt(h, wt_q, gain, cos_ht, sin_ht, layer):
    m, d = h.shape
    tm = _row_tile(m, MM_ROW_TILE)
    tn = 4 * HEAD_DIM
    return pl.pallas_call(
        functools.partial(_qt_body, heads=tn // HEAD_DIM),
        grid=(m // tm, ATTN_WIDTH // tn),
        in_specs=[_layer_spec((tn, d), layer, lambda i, j: (j, 0)),
                  pl.BlockSpec((tm, d), lambda i, j: (i, 0)),
                  _layer_spec((HEAD_DIM, 1), layer, lambda i, j: (0, 0)),
                  pl.BlockSpec((HEAD_DIM // 2, tm), lambda i, j: (0, i)),
                  pl.BlockSpec((HEAD_DIM // 2, tm), lambda i, j: (0, i))],
        out_specs=pl.BlockSpec((tn, tm), lambda i, j: (j, i)),
        out_shape=jax.ShapeDtypeStruct((ATTN_WIDTH, m), bf16),
        compiler_params=_cparams(("parallel", "parallel")),
        name="proj_q_t",
    )(wt_q, h, gain, cos_ht, sin_ht)


def _kv_body(h_ref, w_ref, wvt_ref, g_ref, cos_ref, sin_ref, k_ref, v_ref, kb_ref, vt_ref, *, chunk):
    h = h_ref[...]
    acc = jnp.dot(h, w_ref[...], preferred_element_type=f32)
    cos = cos_ref[...]
    sin = sin_ref[...]
    g = g_ref[...]
    tm = h.shape[0]
    for hh in range(N_KV_HEADS):
        sl = slice(hh * HEAD_DIM, (hh + 1) * HEAD_DIM)
        kh = _head_norm_rope(acc[:, sl], g, cos, sin)
        rows = pl.ds(hh, tm, stride=N_KV_HEADS)
        k_ref[rows, :] = kh
        v_ref[rows, :] = acc[:, KV_WIDTH + hh * HEAD_DIM:KV_WIDTH + (hh + 1) * HEAD_DIM]
        kb_ref[:, sl] = kh.astype(bf16)
    vt = lax.dot_general(wvt_ref[...], h, (((1,), (1,)), ((), ())), preferred_element_type=f32)
    for cc in range(vt_ref.shape[0]):
        vt_ref[cc] = vt[:, cc * chunk:(cc + 1) * chunk].astype(bf16)


def _proj_kv(h, w_kv, wt_v, gain, cos, sin, layer):
    m, d = h.shape
    tm = _row_tile(m, 512)
    chunk = min(tm, ATTN_Q_TILE)
    row = lambda i: (i, 0)
    fix = lambda i: (0, 0)
    return pl.pallas_call(
        functools.partial(_kv_body, chunk=chunk),
        grid=(m // tm,),
        in_specs=[pl.BlockSpec((tm, d), row),
                  _layer_spec((d, 2 * KV_WIDTH), layer, fix),
                  _layer_spec((KV_WIDTH, d), layer, fix),
                  _layer_spec((1, HEAD_DIM), layer, fix),
                  pl.BlockSpec((tm, HEAD_DIM), row),
                  pl.BlockSpec((tm, HEAD_DIM), row)],
        out_specs=[pl.BlockSpec((tm * N_KV_HEADS, HEAD_DIM), row),
                   pl.BlockSpec((tm * N_KV_HEADS, HEAD_DIM), row),
                   pl.BlockSpec((tm, KV_WIDTH), row),
                   pl.BlockSpec((tm // chunk, KV_WIDTH, chunk), lambda i: (i, 0, 0))],
        out_shape=[jax.ShapeDtypeStruct((m * N_KV_HEADS, HEAD_DIM), f32),
                   jax.ShapeDtypeStruct((m * N_KV_HEADS, HEAD_DIM), f32),
                   jax.ShapeDtypeStruct((m, KV_WIDTH), bf16),
                   jax.ShapeDtypeStruct((m // chunk, KV_WIDTH, chunk), bf16)],
        compiler_params=_cparams(("parallel",)),
        name="proj_kv",
    )(h, w_kv, wt_v, gain, cos, sin)


def _qit_body(wt_ref, h_ref, cos_ref, sin_ref, qit_ref, wt_out_ref):
    acc = lax.dot_general(wt_ref[...], h_ref[...], (((1,), (1,)), ((), ())),
                          preferred_element_type=f32)
    c = cos_ref[...]
    s = sin_ref[...]
    half = IDX_DIM // 2
    scale = IDX_DIM ** -0.5
    for hh in range(IDX_HEADS):
        r0 = hh * IDX_DIM
        x1 = acc[r0:r0 + half]
        x2 = acc[r0 + half:r0 + IDX_DIM]
        qit_ref[r0:r0 + half, :] = ((x1 * c - x2 * s) * scale).astype(qit_ref.dtype)
        qit_ref[r0 + half:r0 + IDX_DIM, :] = ((x2 * c + x1 * s) * scale).astype(qit_ref.dtype)
    n_qi = IDX_HEADS * IDX_DIM
    wt_out_ref[...] = acc[n_qi:n_qi + IDX_HEADS] * (IDX_HEADS ** -0.5)


def _proj_qit(h, wt_qi, cos_t, sin_t, layer):
    m, d = h.shape
    tm = _row_tile(m, 512)
    n_rows = wt_qi.shape[1]
    n_qi = IDX_HEADS * IDX_DIM
    return pl.pallas_call(
        _qit_body,
        grid=(m // tm,),
        in_specs=[_layer_spec((n_rows, d), layer, lambda i: (0, 0)),
                  pl.BlockSpec((tm, d), lambda i: (i, 0)),
                  pl.BlockSpec((IDX_DIM // 2, tm), lambda i: (0, i)),
                  pl.BlockSpec((IDX_DIM // 2, tm), lambda i: (0, i))],
        out_specs=[pl.BlockSpec((n_qi, tm), lambda i: (0, i)),
                   pl.BlockSpec((IDX_HEADS, tm), lambda i: (0, i))],
        out_shape=[jax.ShapeDtypeStruct((n_qi, m), bf16),
                   jax.ShapeDtypeStruct((IDX_HEADS, m), f32)],
        compiler_params=_cparams(("parallel",)),
        name="proj_qi_t",
    )(wt_qi, h, cos_t, sin_t)


def _ki_body(h_ref, w_ref, cos_ref, sina_ref, sinb_ref, ki_ref, kib_ref):
    x = jnp.dot(h_ref[...], w_ref[...], preferred_element_type=f32)
    half = IDX_DIM // 2
    r = (x * cos_ref[...] + pltpu.roll(x, LANES - half, 1) * sina_ref[...]
         + pltpu.roll(x, half, 1) * sinb_ref[...])
    ki = r[:, :IDX_DIM]
    ki_ref[...] = ki
    kib_ref[...] = ki.astype(bf16)


def _proj_ki(h, w_ki, cos_k, sin_a, sin_b, layer):
    m, d = h.shape
    tm = _row_tile(m, 512)
    row = lambda i: (i, 0)
    return pl.pallas_call(
        _ki_body,
        grid=(m // tm,),
        in_specs=[pl.BlockSpec((tm, d), row),
                  _layer_spec((d, LANES), layer, lambda i: (0, 0)),
                  pl.BlockSpec((tm, LANES), row),
                  pl.BlockSpec((tm, LANES), row),
                  pl.BlockSpec((tm, LANES), row)],
        out_specs=[pl.BlockSpec((tm, IDX_DIM), row)] * 2,
        out_shape=[jax.ShapeDtypeStruct((m, IDX_DIM), f32),
                   jax.ShapeDtypeStruct((m, IDX_DIM), bf16)],
        compiler_params=_cparams(("parallel",)),
        name="proj_ki",
    )(h, w_ki, cos_k, sin_a, sin_b)


def _plain_body(h_ref, w_ref, o_ref):
    o_ref[...] = jnp.dot(h_ref[...], w_ref[...], preferred_element_type=f32).astype(o_ref.dtype)


def _proj_plain(h, w, out_dtype, layer):
    m, d = h.shape
    n = w.shape[2]
    tm = _row_tile(m, MM_ROW_TILE)
    tn = MM_COL_TILE
    return pl.pallas_call(
        _plain_body,
        grid=(m // tm, n // tn),
        in_specs=[pl.BlockSpec((tm, d), lambda i, j: (i, 0)),
                  _layer_spec((d, tn), layer, lambda i, j: (0, j))],
        out_specs=pl.BlockSpec((tm, tn), lambda i, j: (i, j)),
        out_shape=jax.ShapeDtypeStruct((m, n), out_dtype),
        compiler_params=_cparams(("parallel", "parallel")),
        name="proj_u",
    )(h, w)


def _sigmoid(x):
    return 1.0 / (1.0 + jnp.exp(-x))


def _zg_body(h_ref, w_ref, o_ref, *, silu_tiles):
    acc = jnp.dot(h_ref[...], w_ref[...], preferred_element_type=f32)
    sg = _sigmoid(acc)
    is_silu = pl.program_id(1) < silu_tiles
    o_ref[...] = jnp.where(is_silu, acc * sg, sg).astype(o_ref.dtype)


def _proj_zg(h, w_zg, layer):
    m, d = h.shape
    n = w_zg.shape[2]
    tm = _row_tile(m, MM_ROW_TILE)
    tn = MM_COL_TILE
    return pl.pallas_call(
        functools.partial(_zg_body, silu_tiles=(ATTN_WIDTH + S5_WIDTH) // tn),
        grid=(m // tm, n // tn),
        in_specs=[pl.BlockSpec((tm, d), lambda i, j: (i, 0)),
                  _layer_spec((d, tn), layer, lambda i, j: (0, j))],
        out_specs=pl.BlockSpec((tm, tn), lambda i, j: (i, j)),
        out_shape=jax.ShapeDtypeStruct((m, n), bf16),
        compiler_params=_cparams(("parallel", "parallel")),
        name="proj_zg",
    )(h, w_zg)


def _sortable_key(score):
    b = pltpu.bitcast(score, i32)
    return b ^ ((b >> 31) & jnp.int32(0x7FFFFFFF))


def _kth_largest(count_ge, shape, k):
    zero = jnp.zeros(shape, i32)
    prefix = jnp.where(count_ge(zero) >= k, zero, jnp.full(shape, INT_MIN, i32))

    def bit_body(bi, prefix):
        cand = prefix | (jnp.int32(1) << (30 - bi))
        return jnp.where(count_ge(cand) >= k, cand, prefix)

    thr = lax.fori_loop(0, 31, bit_body, prefix)
    return jnp.maximum(thr, jnp.int32(INT_MIN + 1))


def _sublane_allreduce(x, op):
    for shift in (4, 2, 1):
        x = op(x, pltpu.roll(x, shift, 0))
    return x


def _attn_prompt_body(qt_ref, qit_ref, wt_ref, z_ref, k_ref, vt_ref, ki_ref, o_ref,
                      key_scr, acc_scr, m_scr, l_scr, alpha_scr, s_scr, p_scr, bias_scr, *, tq, k_top):
    i = pl.program_id(1)
    n_chunks = i + 1
    half = tq // 2

    def score_keys(c, diagonal):
        for sub in range(2):
            r0 = pl.multiple_of(c * tq + sub * half, half)
            kic = ki_ref[pl.ds(r0, half), :]
            acc = jnp.zeros((half, tq), f32)
            for hh in range(IDX_HEADS):
                lg = jnp.dot(kic, qit_ref[hh * IDX_DIM:(hh + 1) * IDX_DIM, :],
                             preferred_element_type=f32)
                acc = acc + jnp.maximum(lg, 0.0) * wt_ref[hh:hh + 1, :]
            key = _sortable_key(acc)
            if diagonal:
                kpos = lax.broadcasted_iota(i32, (half, tq), 0) + sub * half
                qpos = lax.broadcasted_iota(i32, (half, tq), 1)
                key = jnp.where(kpos <= qpos, key, jnp.int32(INT_MIN))
            key_scr[pl.ds(r0, half), :] = key

    def full_chunk(c, carry):
        score_keys(c, False)
        return carry

    lax.fori_loop(0, i, full_chunk, 0)
    score_keys(i, True)

    def count_ge(cand):
        def body(c, cnt):
            kk = key_scr[pl.ds(pl.multiple_of(c * tq, tq), tq), :]
            hit = (kk >= cand).astype(i32)
            return cnt + jnp.sum(hit.reshape(tq // SUBLANES, SUBLANES, tq), axis=0)

        cnt = lax.fori_loop(0, n_chunks, body, jnp.zeros((SUBLANES, tq), i32))
        return jnp.sum(cnt, axis=0, keepdims=True)

    thr = _kth_largest(count_ge, (1, tq), k_top)

    m_scr[...] = jnp.full(m_scr.shape, -jnp.inf, f32)
    l_scr[...] = jnp.zeros(l_scr.shape, f32)
    acc_scr[...] = jnp.zeros(acc_scr.shape, f32)
    n_sub = tq // SUBLANES
    d_sub = HEAD_DIM // SUBLANES

    def attend_chunk(c, carry):
        r0 = pl.multiple_of(c * tq, tq)
        bias_scr[...] = jnp.where(key_scr[pl.ds(r0, tq), :] >= thr, 0.0, MASK_BIAS)
        for g in range(N_KV_HEADS):
            kc = k_ref[pl.ds(r0, tq), g * HEAD_DIM:(g + 1) * HEAD_DIM]
            for hh in range(HEADS_PER_KV):
                h = g * HEADS_PER_KV + hh
                s = jnp.dot(kc, qt_ref[h * HEAD_DIM:(h + 1) * HEAD_DIM, :], preferred_element_type=f32)
                s_scr[h] = s + bias_scr[...]
        blocks = [slice(j * ATTN_ROW_BLOCK, (j + 1) * ATTN_ROW_BLOCK) for j in range(tq // ATTN_ROW_BLOCK)]
        blk_sub = ATTN_ROW_BLOCK // SUBLANES
        for h in range(N_HEADS):
            mx = m_scr[h]
            for rows in blocks:
                mx = jnp.maximum(mx, jnp.max(s_scr[h, rows, :].reshape(blk_sub, SUBLANES, tq), axis=0))
            m_new = _sublane_allreduce(mx, jnp.maximum)
            alpha_scr[h] = jnp.exp2(m_scr[h] - m_new)
            m_scr[h] = m_new
        for h in range(N_HEADS):
            m_new = m_scr[h]
            psum = jnp.zeros((SUBLANES, tq), f32)
            for rows in blocks:
                p3 = jnp.exp2(s_scr[h, rows, :].reshape(blk_sub, SUBLANES, tq) - m_new[None])
                psum = psum + jnp.sum(p3, axis=0)
                p_scr[h, rows, :] = p3.reshape(ATTN_ROW_BLOCK, tq).astype(bf16)
            l_scr[h] = alpha_scr[h] * l_scr[h] + _sublane_allreduce(psum, jnp.add)
        for g in range(N_KV_HEADS):
            vtc = vt_ref[c, g * HEAD_DIM:(g + 1) * HEAD_DIM, :]
            for hh in range(HEADS_PER_KV):
                h = g * HEADS_PER_KV + hh
                hs = slice(h * HEAD_DIM, (h + 1) * HEAD_DIM)
                pv = jnp.dot(vtc, p_scr[h], preferred_element_type=f32)
                acc = acc_scr[hs, :].reshape(d_sub, SUBLANES, tq) * alpha_scr[h][None]
                acc_scr[hs, :] = acc.reshape(HEAD_DIM, tq) + pv
        return carry

    lax.fori_loop(0, n_chunks, attend_chunk, 0)

    for h in range(N_HEADS):
        hs = slice(h * HEAD_DIM, (h + 1) * HEAD_DIM)
        out_t = (acc_scr[hs, :].reshape(d_sub, SUBLANES, tq) / l_scr[h][None]).reshape(HEAD_DIM, tq)
        o_ref[:, hs] = (out_t.T * z_ref[:, hs].astype(f32)).astype(o_ref.dtype)


def _attn_prompt(qt, qit, wt, zg, kb, vt, kib, batch, seq):
    tq = ATTN_Q_TILE
    assert seq % tq == 0
    nq = seq // tq
    k_top = min(TOPK_MAX, seq // 4)
    assert k_top <= tq
    qrow = lambda b, i: (b * nq + i, 0)
    qcol = lambda b, i: (0, b * nq + i)
    per_b = lambda b, i: (b, 0)
    return pl.pallas_call(
        functools.partial(_attn_prompt_body, tq=tq, k_top=k_top),
        grid=(batch, nq),
        in_specs=[pl.BlockSpec((ATTN_WIDTH, tq), qcol),
                  pl.BlockSpec((IDX_HEADS * IDX_DIM, tq), qcol),
                  pl.BlockSpec((IDX_HEADS, tq), qcol),
                  pl.BlockSpec((tq, ATTN_WIDTH), qrow),
                  pl.BlockSpec((seq, KV_WIDTH), per_b),
                  pl.BlockSpec((nq, KV_WIDTH, tq), lambda b, i: (b, 0, 0)),
                  pl.BlockSpec((seq, IDX_DIM), per_b)],
        out_specs=pl.BlockSpec((tq, ATTN_WIDTH), qrow),
        out_shape=jax.ShapeDtypeStruct((batch * seq, ATTN_WIDTH), bf16),
        scratch_shapes=[pltpu.VMEM((seq, tq), i32),
                        pltpu.VMEM((ATTN_WIDTH, tq), f32),
                        pltpu.VMEM((N_HEADS, SUBLANES, tq), f32),
                        pltpu.VMEM((N_HEADS, SUBLANES, tq), f32),
                        pltpu.VMEM((N_HEADS, SUBLANES, tq), f32),
                        pltpu.VMEM((N_HEADS, tq, tq), f32),
                        pltpu.VMEM((N_HEADS, tq, tq), bf16),
                        pltpu.VMEM((tq, tq), f32)],
        compiler_params=_cparams(("parallel", "arbitrary")),
        name="attn_prompt",
    )(qt, qit, wt, zg, kb, vt, kib)


def _idx_sample_body(pt_ref, qi_ref, w_ref, *rest, n_pages, n_new, k_top, ppb):
    kid_refs = rest[:ppb]
    kinew_ref, key_ref, thr_ref = rest[ppb:]
    p = pl.program_id(1)
    tok = SAMPLE_ROWS

    def page_keys(ki_page_t):
        lg = jnp.dot(qi_ref[0], ki_page_t.astype(bf16), preferred_element_type=f32)
        wgt = jnp.maximum(lg, 0.0) * w_ref[0]
        score = jnp.sum(wgt.reshape(IDX_HEADS, tok, PAGE_SIZE), axis=0)
        return _sortable_key(score)

    for j in range(ppb):
        key_ref[0, p * ppb + j] = page_keys(kid_refs[j][...])

    @pl.when(p == pl.num_programs(1) - 1)
    def _():
        key = page_keys(kinew_ref[0])
        kpos = lax.broadcasted_iota(i32, (tok, PAGE_SIZE), 1)
        qtok = lax.broadcasted_iota(i32, (tok, PAGE_SIZE), 0)
        key_ref[0, n_pages] = jnp.where((kpos <= qtok) & (kpos < n_new), key, jnp.int32(INT_MIN))

        def count_ge(cand):
            def body(c, cnt):
                hit = (key_ref[0, pl.ds(c * ppb, ppb)] >= cand).astype(i32)
                return cnt + jnp.sum(hit, axis=0)

            cnt = lax.fori_loop(0, n_pages // ppb, body, jnp.zeros((tok, PAGE_SIZE), i32))
            cnt = cnt + (key_ref[0, n_pages] >= cand).astype(i32)
            return jnp.sum(cnt, axis=-1, keepdims=True)

        thr = _kth_largest(count_ge, (tok, 1), k_top)
        thr_ref[0] = jnp.broadcast_to(thr, (tok, PAGE_SIZE))


def _pages_per_step(n_pages, cap=PAGES_PER_STEP):
    ppb = min(cap, n_pages)
    assert n_pages % ppb == 0
    return ppb


def _page_spec(block, layer, ppb, j):
    return pl.BlockSpec(block, lambda b, p, pt: (layer, pt[b, p * ppb + j], 0, 0))


def _idx_sample(page_table, qi_rows, w_rows, cache_kidx, layer, ki_new_pages, n_new):
    nb, n_pages = page_table.shape
    tok = SAMPLE_ROWS
    ppb = _pages_per_step(n_pages, IDX_PAGES_PER_STEP)
    k_top = min(TOPK_MAX, (n_pages * PAGE_SIZE + n_new) // 4)
    per_b = lambda b, p, pt: (b, 0, 0)
    grid_spec = pltpu.PrefetchScalarGridSpec(
        num_scalar_prefetch=1,
        grid=(nb, n_pages // ppb),
        in_specs=[pl.BlockSpec((1, IDX_HEADS * tok, IDX_DIM), per_b),
                  pl.BlockSpec((1, IDX_HEADS * tok, 1), per_b)]
                 + [_page_spec((None, None, IDX_DIM, PAGE_SIZE), layer, ppb, j) for j in range(ppb)]
                 + [pl.BlockSpec((1, IDX_DIM, PAGE_SIZE), per_b)],
        out_specs=[pl.BlockSpec((1, n_pages + 1, tok, PAGE_SIZE), lambda b, p, pt: (b, 0, 0, 0)),
                   pl.BlockSpec((1, tok, PAGE_SIZE), per_b)],
    )
    return pl.pallas_call(
        functools.partial(_idx_sample_body, n_pages=n_pages, n_new=n_new, k_top=k_top, ppb=ppb),
        grid_spec=grid_spec,
        out_shape=[jax.ShapeDtypeStruct((nb, n_pages + 1, tok, PAGE_SIZE), i32),
                   jax.ShapeDtypeStruct((nb, tok, PAGE_SIZE), i32)],
        compiler_params=_cparams(("parallel", "arbitrary")),
        name="idx_sample",
    )(page_table, qi_rows, w_rows, *([cache_kidx] * ppb), ki_new_pages)


def _attn_sample_body(pt_ref, q_ref, key_ref, keyn_ref, thr_ref, *rest, ppb):
    k_refs = rest[:ppb]
    v_refs = rest[ppb:2 * ppb]
    kn_ref, vn_ref, o_ref, acc_scr, m_scr, l_scr = rest[2 * ppb:]
    p = pl.program_id(1)
    tok = SAMPLE_ROWS
    rows_g = HEADS_PER_KV * tok
    thr = thr_ref[0]

    @pl.when(p == 0)
    def _():
        m_scr[...] = jnp.full(m_scr.shape, -jnp.inf, f32)
        l_scr[...] = jnp.zeros(l_scr.shape, f32)
        acc_scr[...] = jnp.zeros(acc_scr.shape, f32)

    def attend(bias_tok, k_of_group, v_of_group):
        bias = jnp.concatenate([bias_tok] * HEADS_PER_KV, axis=0)
        for g in range(N_KV_HEADS):
            rs = slice(g * rows_g, (g + 1) * rows_g)
            s = lax.dot_general(q_ref[0, rs, :], k_of_group(g), (((1,), (1,)), ((), ())),
                                preferred_element_type=f32) + bias
            m_prev = m_scr[rs, :]
            m_new = jnp.maximum(m_prev, jnp.max(s, axis=-1, keepdims=True))
            alpha = jnp.exp2(m_prev - m_new)
            pr = jnp.exp2(s - m_new)
            l_scr[rs, :] = alpha * l_scr[rs, :] + jnp.sum(pr, axis=-1, keepdims=True)
            acc_scr[rs, :] = alpha * acc_scr[rs, :] + jnp.dot(
                pr.astype(bf16), v_of_group(g), preferred_element_type=f32)
            m_scr[rs, :] = m_new

    def mask_bias(keys):
        return jnp.where(keys >= thr, 0.0, MASK_BIAS).astype(f32)

    def cached(refs):
        def of_group(g):
            rows = pl.ds(g, PAGE_SIZE, stride=N_KV_HEADS)
            return jnp.concatenate([r[rows, :].astype(bf16) for r in refs], axis=0)
        return of_group

    attend(jnp.concatenate([mask_bias(key_ref[0, j]) for j in range(ppb)], axis=1),
           cached(k_refs), cached(v_refs))

    @pl.when(p == pl.num_programs(1) - 1)
    def _():
        def fresh(ref):
            return lambda g: ref[0, :, g * HEAD_DIM:(g + 1) * HEAD_DIM].astype(bf16)

        attend(mask_bias(keyn_ref[0, 0]), fresh(kn_ref), fresh(vn_ref))
        o_ref[0] = acc_scr[...] / l_scr[...]


def _attn_sample(page_table, q_rows, keys, thr, cache_k, cache_v, layer, k_new_pages, v_new_pages):
    nb, n_pages = page_table.shape
    tok = SAMPLE_ROWS
    rows = N_HEADS * tok
    ppb = _pages_per_step(n_pages)
    per_b = lambda b, p, pt: (b, 0, 0)
    page_block = (None, None, PAGE_SIZE * N_KV_HEADS, HEAD_DIM)
    grid_spec = pltpu.PrefetchScalarGridSpec(
        num_scalar_prefetch=1,
        grid=(nb, n_pages // ppb),
        in_specs=[pl.BlockSpec((1, rows, HEAD_DIM), per_b),
                  pl.BlockSpec((1, ppb, tok, PAGE_SIZE), lambda b, p, pt: (b, p, 0, 0)),
                  pl.BlockSpec((1, 1, tok, PAGE_SIZE), lambda b, p, pt: (b, n_pages, 0, 0)),
                  pl.BlockSpec((1, tok, PAGE_SIZE), per_b)]
                 + [_page_spec(page_block, layer, ppb, j) for j in range(ppb)]
                 + [_page_spec(page_block, layer, ppb, j) for j in range(ppb)]
                 + [pl.BlockSpec((1, PAGE_SIZE, KV_WIDTH), per_b),
                    pl.BlockSpec((1, PAGE_SIZE, KV_WIDTH), per_b)],
        out_specs=pl.BlockSpec((1, rows, HEAD_DIM), per_b),
        scratch_shapes=[pltpu.VMEM((rows, HEAD_DIM), f32),
                        pltpu.VMEM((rows, 1), f32),
                        pltpu.VMEM((rows, 1), f32)],
    )
    return pl.pallas_call(
        functools.partial(_attn_sample_body, ppb=ppb),
        grid_spec=grid_spec,
        out_shape=jax.ShapeDtypeStruct((nb, rows, HEAD_DIM), f32),
        compiler_params=_cparams(("parallel", "arbitrary")),
        name="attn_sample",
    )(page_table, q_rows, keys, keys, thr, *([cache_k] * ppb), *([cache_v] * ppb),
      k_new_pages, v_new_pages)


def _gate_mul_body(a_ref, z_ref, o_ref):
    o_ref[...] = (a_ref[...] * z_ref[...].astype(f32)).astype(o_ref.dtype)


def _gate_mul(a, zg):
    m, n = a.shape
    return pl.pallas_call(
        _gate_mul_body,
        grid=(1,),
        in_specs=[pl.BlockSpec((m, n), lambda i: (0, 0)),
                  pl.BlockSpec((m, n), lambda i: (0, 0))],
        out_specs=pl.BlockSpec((m, n), lambda i: (0, 0)),
        out_shape=jax.ShapeDtypeStruct((m, n), bf16),
        compiler_params=_cparams(("arbitrary",)),
        name="gate_mul",
    )(a, zg)


def _s5_param_body(lr_ref, li_ref, ldt_ref, bre_ref, bim_ref, ar_ref, ai_ref, bbre_ref, bbim_ref):
    lr = lr_ref[...]
    li = li_ref[...]
    dt = jnp.exp(ldt_ref[...])
    mag = jnp.exp(lr * dt)
    ar = mag * jnp.cos(li * dt)
    ai = mag * jnp.sin(li * dt)
    den = lr * lr + li * li
    nr = ar - 1.0
    cr = (nr * lr + ai * li) / den
    ci = (ai * lr - nr * li) / den
    ar_ref[...] = ar
    ai_ref[...] = ai
    for h in range(S5_GROUP):
        bbre_ref[h] = cr * bre_ref[h] - ci * bim_ref[h]
        bbim_ref[h] = cr * bim_ref[h] + ci * bre_ref[h]


def _s5_params(lam_re, lam_im, log_dt, b_re, b_im):
    depth = lam_re.shape[0]
    gs = (None, S5_GROUPS, S5_STATE)
    bs = (None, S5_GROUP, S5_GROUPS, S5_STATE)
    at3 = lambda l: (l, 0, 0)
    at4 = lambda l: (l, 0, 0, 0)
    return pl.pallas_call(
        _s5_param_body,
        grid=(depth,),
        in_specs=[pl.BlockSpec(gs, at3), pl.BlockSpec(gs, at3),
                  pl.BlockSpec((None, S5_GROUPS, 1), at3),
                  pl.BlockSpec(bs, at4), pl.BlockSpec(bs, at4)],
        out_specs=[pl.BlockSpec(gs, at3), pl.BlockSpec(gs, at3),
                   pl.BlockSpec(bs, at4), pl.BlockSpec(bs, at4)],
        out_shape=[jax.ShapeDtypeStruct((depth, S5_GROUPS, S5_STATE), f32),
                   jax.ShapeDtypeStruct((depth, S5_GROUPS, S5_STATE), f32),
                   jax.ShapeDtypeStruct((depth, S5_GROUP, S5_GROUPS, S5_STATE), f32),
                   jax.ShapeDtypeStruct((depth, S5_GROUP, S5_GROUPS, S5_STATE), f32)],
        compiler_params=_cparams(("arbitrary",)),
        name="s5_params",
    )(lam_re, lam_im, log_dt.reshape(depth, S5_GROUPS, 1), b_re, b_im)


def _gelu_tanh(x):
    return 0.5 * x * (1.0 + jnp.tanh(math.sqrt(2.0 / math.pi) * (x + 0.044715 * (x * x * x))))


def _s5_body(u_ref, zs_ref, h0r_ref, h0i_ref, ar_ref, ai_ref, wbu_ref, wcr_ref, wci_ref, d_ref, wglu_ref,
             o_ref, sr_ref, si_ref, xr_scr, xi_scr, y_scr, str_scr, sti_scr, *, tc, n_last):
    c = pl.program_id(1)

    @pl.when(c == 0)
    def _():
        str_scr[...] = h0r_ref[0]
        sti_scr[...] = h0i_ref[0]

    n_ct = S5_SG_STATE // LANES

    def gather_planes(scr, rows):
        return jnp.concatenate([scr[j, rows, :] for j in range(n_ct)], axis=1)

    def scatter_planes(scr, rows, val):
        for j in range(n_ct):
            scr[j, rows, :] = val[:, j * LANES:(j + 1) * LANES]

    for sg in range(S5_SG):
        rows = pl.ds(sg, tc, stride=S5_SG)
        u_sg = u_ref[:, sg * LANES:(sg + 1) * LANES]
        bu = jnp.dot(u_sg.astype(bf16), wbu_ref[sg], preferred_element_type=f32)
        scatter_planes(xr_scr, rows, bu[:, :S5_SG_STATE])
        scatter_planes(xi_scr, rows, bu[:, S5_SG_STATE:])

    ar = ar_ref[...]
    ai = ai_ref[...]

    def step(t, carry):
        xr, xi = carry
        r = pl.ds(pl.multiple_of(t * S5_SG, S5_SG), S5_SG)
        nr = ar * xr - ai * xi + gather_planes(xr_scr, r)
        ni = ar * xi + ai * xr + gather_planes(xi_scr, r)
        scatter_planes(xr_scr, r, nr)
        scatter_planes(xi_scr, r, ni)
        return nr, ni

    xr, xi = lax.fori_loop(0, tc, step, (str_scr[...], sti_scr[...]))
    str_scr[...] = xr
    sti_scr[...] = xi

    for sg in range(S5_SG):
        rows = pl.ds(sg, tc, stride=S5_SG)
        cs = slice(sg * LANES, (sg + 1) * LANES)
        y = (jnp.dot(gather_planes(xr_scr, rows).astype(bf16), wcr_ref[sg], preferred_element_type=f32)
             - jnp.dot(gather_planes(xi_scr, rows).astype(bf16), wci_ref[sg], preferred_element_type=f32)
             + d_ref[:, cs] * u_ref[:, cs])
        y_scr[:, cs] = _gelu_tanh(y)

    y = y_scr[...]
    gate = _sigmoid(jnp.dot(y.astype(bf16), wglu_ref[...], preferred_element_type=f32))
    o_ref[...] = (y * gate * zs_ref[...].astype(f32)).astype(o_ref.dtype)

    @pl.when(c == pl.num_programs(1) - 1)
    def _():
        last = slice((n_last - 1) * S5_SG, n_last * S5_SG)
        sr_ref[0] = gather_planes(xr_scr, last)
        si_ref[0] = gather_planes(xi_scr, last)


def _s5(u, zg, h0_re, h0_im, prm, layer, nb, seq, n_real):
    tc = min(seq, 256)
    nch = seq // tc
    n_last = n_real - (nch - 1) * tc
    zs_col = ATTN_WIDTH // S5_WIDTH
    fix2 = lambda b, c: (0, 0)
    fix3 = lambda b, c: (0, 0, 0)
    st = lambda b, c: (b, 0, 0)
    out, s_re, s_im = pl.pallas_call(
        functools.partial(_s5_body, tc=tc, n_last=n_last),
        grid=(nb, nch),
        in_specs=[pl.BlockSpec((tc, S5_WIDTH), lambda b, c: (b * nch + c, 0)),
                  pl.BlockSpec((tc, S5_WIDTH), lambda b, c: (b * nch + c, zs_col)),
                  pl.BlockSpec((1, S5_SG, S5_SG_STATE), st),
                  pl.BlockSpec((1, S5_SG, S5_SG_STATE), st),
                  _layer_spec((S5_SG, S5_SG_STATE), layer, fix2),
                  _layer_spec((S5_SG, S5_SG_STATE), layer, fix2),
                  _layer_spec((S5_SG, LANES, 2 * S5_SG_STATE), layer, fix3),
                  _layer_spec((S5_SG, S5_SG_STATE, LANES), layer, fix3),
                  _layer_spec((S5_SG, S5_SG_STATE, LANES), layer, fix3),
                  _layer_spec((1, S5_WIDTH), layer, fix2),
                  _layer_spec((S5_WIDTH, S5_WIDTH), layer, fix2)],
        out_specs=[pl.BlockSpec((tc, S5_WIDTH), lambda b, c: (b * nch + c, 0)),
                   pl.BlockSpec((1, S5_SG, S5_SG_STATE), st),
                   pl.BlockSpec((1, S5_SG, S5_SG_STATE), st)],
        out_shape=[jax.ShapeDtypeStruct((nb * seq, S5_WIDTH), bf16),
                   jax.ShapeDtypeStruct((nb, S5_SG, S5_SG_STATE), f32),
                   jax.ShapeDtypeStruct((nb, S5_SG, S5_SG_STATE), f32)],
        scratch_shapes=[pltpu.VMEM((S5_SG_STATE // LANES, tc * S5_SG, LANES), f32),
                        pltpu.VMEM((S5_SG_STATE // LANES, tc * S5_SG, LANES), f32),
                        pltpu.VMEM((tc, S5_WIDTH), f32),
                        pltpu.VMEM((S5_SG, S5_SG_STATE), f32),
                        pltpu.VMEM((S5_SG, S5_SG_STATE), f32)],
        compiler_params=_cparams(("parallel", "arbitrary")),
        name="s5_scan",
    )(u, zg, h0_re.reshape(nb, S5_SG, S5_SG_STATE), h0_im.reshape(nb, S5_SG, S5_SG_STATE),
      prm["a_re"], prm["a_im"], prm["w_bu"], prm["w_c_re"], prm["w_c_im"], prm["d"], prm["w_glu"])
    return out, s_re.reshape(nb, S5_GROUPS, S5_STATE), s_im.reshape(nb, S5_GROUPS, S5_STATE)


def _merge_body(a_ref, s_ref, wa_ref, ws_ref, ga_ref, gs_ref, o_ref):
    o_a = jnp.dot(a_ref[...], wa_ref[...], preferred_element_type=f32)
    o_s = jnp.dot(s_ref[...], ws_ref[...], preferred_element_type=f32)
    o_ref[...] = (ga_ref[...].astype(f32) * o_a + gs_ref[...].astype(f32) * o_s).astype(o_ref.dtype)


def _merge(a_in, s_in, w_a, w_s, zg, layer):
    m = a_in.shape[0]
    tm = _row_tile(m, MM_ROW_TILE)
    tn = MM_COL_TILE
    ga0 = (ATTN_WIDTH + S5_WIDTH) // tn
    gs0 = (ATTN_WIDTH + S5_WIDTH + D_MODEL) // tn
    return pl.pallas_call(
        _merge_body,
        grid=(m // tm, D_MODEL // tn),
        in_specs=[pl.BlockSpec((tm, ATTN_WIDTH), lambda i, j: (i, 0)),
                  pl.BlockSpec((tm, S5_WIDTH), lambda i, j: (i, 0)),
                  _layer_spec((ATTN_WIDTH, tn), layer, lambda i, j: (0, j)),
                  _layer_spec((S5_WIDTH, tn), layer, lambda i, j: (0, j)),
                  pl.BlockSpec((tm, tn), lambda i, j: (i, ga0 + j)),
                  pl.BlockSpec((tm, tn), lambda i, j: (i, gs0 + j))],
        out_specs=pl.BlockSpec((tm, tn), lambda i, j: (i, j)),
        out_shape=jax.ShapeDtypeStruct((m, D_MODEL), bf16),
        compiler_params=_cparams(("parallel", "parallel")),
        name="merge",
    )(a_in, s_in, w_a, w_s, zg, zg)


def _out_body(x_ref, m_ref, w_ref, o_ref):
    o_ref[...] = x_ref[...] + jnp.dot(m_ref[...], w_ref[...], preferred_element_type=f32)


def _out_proj(x, merged, w_out, layer):
    m = x.shape[0]
    tm = _row_tile(m, MM_ROW_TILE)
    tn = MM_COL_TILE
    return pl.pallas_call(
        _out_body,
        grid=(m // tm, D_MODEL // tn),
        in_specs=[pl.BlockSpec((tm, tn), lambda i, j: (i, j)),
                  pl.BlockSpec((tm, D_MODEL), lambda i, j: (i, 0)),
                  _layer_spec((D_MODEL, tn), layer, lambda i, j: (0, j))],
        out_specs=pl.BlockSpec((tm, tn), lambda i, j: (i, j)),
        out_shape=jax.ShapeDtypeStruct((m, D_MODEL), f32),
        compiler_params=_cparams(("parallel", "parallel")),
        name="out_proj",
    )(x, merged, w_out)


def _rope_tables(pos):
    posf = pos.astype(f32)[:, None]
    half = HEAD_DIM // 2
    inv = ROPE_THETA ** (-jnp.arange(half, dtype=f32) / half)
    ang = posf * inv[None, :]
    cos, sin = jnp.cos(ang), jnp.sin(ang)
    cos_h = jnp.concatenate([cos, cos], axis=1)
    sin_h = jnp.concatenate([-sin, sin], axis=1)
    half_i = IDX_DIM // 2
    inv_i = ROPE_THETA ** (-jnp.arange(half_i, dtype=f32) / half_i)
    ang_i = posf * inv_i[None, :]
    cos_i, sin_i = jnp.cos(ang_i), jnp.sin(ang_i)
    zeros = jnp.zeros_like(cos_i)
    pad = jnp.zeros((pos.shape[0], LANES - IDX_DIM), f32)
    cos_k = jnp.concatenate([cos_i, cos_i, pad], axis=1)
    sin_a = jnp.concatenate([-sin_i, zeros, pad], axis=1)
    sin_b = jnp.concatenate([zeros, sin_i, pad], axis=1)
    return dict(cos_h=cos_h, sin_h=sin_h, cos_k=cos_k, sin_a=sin_a, sin_b=sin_b,
                cos_ht=cos.T, sin_ht=sin.T, cos_t=cos_i.T, sin_t=sin_i.T)


def _pack_weights(w_in, w_glu, w_br_attn, w_br_s5, w_out, norm_gain, q_norm_gain, k_norm_gain):
    depth = w_in.shape[0]
    offs = [0]
    for s in IN_SIZES:
        offs.append(offs[-1] + s)
    w_bf = w_in.astype(bf16)
    seg = lambda k: w_bf[:, :, offs[k]:offs[k + 1]]
    t = lambda a: jnp.swapaxes(a, 1, 2)
    return dict(
        wt_q=t(seg(0)), wt_v=t(seg(2)),
        w_kv=jnp.concatenate([seg(1), seg(2)], axis=2),
        wt_qi=t(jnp.concatenate([seg(3), seg(5)], axis=2)),
        w_ki=jnp.pad(seg(4), ((0, 0), (0, 0), (0, LANES - IDX_DIM))),
        w_u=seg(7),
        w_zg=jnp.concatenate([seg(6), seg(8), seg(9), seg(10)], axis=2),
        w_glu=w_glu.astype(bf16), w_br_attn=w_br_attn.astype(bf16),
        w_br_s5=w_br_s5.astype(bf16), w_out=w_out.astype(bf16),
        norm_gain=norm_gain.reshape(depth, 1, D_MODEL),
        q_gain=q_norm_gain.reshape(depth, HEAD_DIM, 1),
        k_gain=k_norm_gain.reshape(depth, 1, HEAD_DIM))


def _block_diag(blocks):
    depth, sg, gg, r, c = blocks.shape
    eye = jnp.eye(gg, dtype=blocks.dtype)
    return jnp.einsum("lsgrc,gk->lsgrkc", blocks, eye).reshape(depth, sg, gg * r, gg * c)


def _s5_stacked_params(lam_re, lam_im, log_dt, b_re, b_im, c_re, c_im, d, w_glu_bf16):
    depth = lam_re.shape[0]
    a_re, a_im, bb_re, bb_im = _s5_params(lam_re, lam_im, log_dt,
                                          jnp.transpose(b_re, (0, 3, 1, 2)), jnp.transpose(b_im, (0, 3, 1, 2)))
    def bu_blocks(bb):
        return jnp.transpose(bb, (0, 2, 1, 3)).reshape(depth, S5_SG, S5_SG_GROUPS, S5_GROUP, S5_STATE)
    w_bu = jnp.concatenate([_block_diag(bu_blocks(bb_re)), _block_diag(bu_blocks(bb_im))], axis=3)
    def c_blocks(cm):
        return jnp.transpose(cm, (0, 1, 3, 2)).reshape(depth, S5_SG, S5_SG_GROUPS, S5_STATE, S5_GROUP)
    return dict(a_re=a_re.reshape(depth, S5_SG, S5_SG_STATE), a_im=a_im.reshape(depth, S5_SG, S5_SG_STATE),
                w_bu=w_bu.astype(bf16),
                w_c_re=_block_diag(c_blocks(c_re)).astype(bf16),
                w_c_im=_block_diag(c_blocks(c_im)).astype(bf16),
                d=d.reshape(depth, 1, S5_WIDTH), w_glu=w_glu_bf16)


def _projections(x, wts, tabs, layer):
    h = _rmsnorm(x, wts["norm_gain"], layer)
    qt = _proj_qt(h, wts["wt_q"], wts["q_gain"], tabs["cos_ht"], tabs["sin_ht"], layer)
    k, v, kb, vt = _proj_kv(h, wts["w_kv"], wts["wt_v"], wts["k_gain"], tabs["cos_h"], tabs["sin_h"], layer)
    qit, wt = _proj_qit(h, wts["wt_qi"], tabs["cos_t"], tabs["sin_t"], layer)
    ki, kib = _proj_ki(h, wts["w_ki"], tabs["cos_k"], tabs["sin_a"], tabs["sin_b"], layer)
    u = _proj_plain(h, wts["w_u"], f32, layer)
    zg = _proj_zg(h, wts["w_zg"], layer)
    return dict(qt=qt, k=k, v=v, kb=kb, vt=vt, qit=qit, wt=wt, ki=ki, kib=kib, u=u, zg=zg)


def _finish(x, a_in, s_in, zg, wts, layer):
    merged = _merge(a_in, s_in, wts["w_br_attn"], wts["w_br_s5"], zg, layer)
    return _out_proj(x, merged, wts["w_out"], layer)


def kernel(x_prompt, x_sample, cache_k, cache_v, cache_kidx, state_s5_re, state_s5_im, page_table, norm_gain, w_in, q_norm_gain, k_norm_gain, s5_lam_re, s5_lam_im, s5_log_dt, s5_b_re, s5_b_im, s5_c_re, s5_c_im, s5_d, w_glu, w_br_attn, w_br_s5, w_out):
    depth = w_in.shape[0]
    b_p, t_p = x_prompt.shape[:2]
    b_s, t_s = x_sample.shape[:2]
    tok = SAMPLE_ROWS
    assert t_s <= tok
    n_pages = page_table.shape[1]
    past = n_pages * PAGE_SIZE
    n_phys = cache_k.shape[1]

    tabs_p = _rope_tables(jnp.tile(jnp.arange(t_p, dtype=i32), b_p))
    tabs_s = _rope_tables(jnp.tile(past + jnp.arange(tok, dtype=i32), b_s))

    xp = x_prompt.reshape(b_p * t_p, D_MODEL)
    xs = jnp.pad(x_sample, ((0, 0), (0, tok - t_s), (0, 0))).reshape(b_s * tok, D_MODEL)

    cache_k4 = cache_k.reshape(depth, n_phys, PAGE_SIZE * N_KV_HEADS, HEAD_DIM)
    cache_v4 = cache_v.reshape(depth, n_phys, PAGE_SIZE * N_KV_HEADS, HEAD_DIM)
    cache_kidx_t = jnp.swapaxes(cache_kidx, 2, 3)
    zeros_state = jnp.zeros((b_p, S5_GROUPS, S5_STATE), f32)

    def new_page(a):
        w = a.shape[-1]
        return jnp.pad(a.reshape(b_s, tok, w), ((0, 0), (0, PAGE_SIZE - tok), (0, 0)))

    wts = _pack_weights(w_in, w_glu, w_br_attn, w_br_s5, w_out, norm_gain, q_norm_gain, k_norm_gain)
    s5p = _s5_stacked_params(s5_lam_re, s5_lam_im, s5_log_dt, s5_b_re, s5_b_im, s5_c_re, s5_c_im, s5_d,
                             wts["w_glu"])

    outs_p, outs_s = [], []
    for l in range(depth):
        pp = _projections(xp, wts, tabs_p, l)
        a_in = _attn_prompt(pp["qt"], pp["qit"], pp["wt"], pp["zg"], pp["kb"], pp["vt"], pp["kib"], b_p, t_p)
        s_in, sr_p, si_p = _s5(pp["u"], pp["zg"], zeros_state, zeros_state, s5p, l, b_p, t_p, t_p)
        xp = _finish(xp, a_in, s_in, pp["zg"], wts, l)
        outs_p.append((pp["k"].reshape(b_p, t_p, N_KV_HEADS, HEAD_DIM),
                       pp["v"].reshape(b_p, t_p, N_KV_HEADS, HEAD_DIM),
                       pp["ki"].reshape(b_p, t_p, IDX_DIM), sr_p, si_p))

        ps = _projections(xs, wts, tabs_s, l)
        qi_rows = jnp.transpose(ps["qit"].reshape(IDX_HEADS, IDX_DIM, b_s, tok), (2, 0, 3, 1)
                                ).reshape(b_s, IDX_HEADS * tok, IDX_DIM)
        w_rows = jnp.transpose(ps["wt"].reshape(IDX_HEADS, b_s, tok), (1, 0, 2)
                               ).reshape(b_s, IDX_HEADS * tok, 1)
        keys, thr = _idx_sample(page_table, qi_rows, w_rows, cache_kidx_t, l,
                                jnp.swapaxes(new_page(ps["ki"]), 1, 2), t_s)
        q_rows = jnp.transpose(ps["qt"].reshape(N_HEADS, HEAD_DIM, b_s, tok), (2, 0, 3, 1)
                               ).reshape(b_s, N_HEADS * tok, HEAD_DIM)
        k_s = ps["k"].reshape(b_s * tok, KV_WIDTH)
        v_s = ps["v"].reshape(b_s * tok, KV_WIDTH)
        o_rows = _attn_sample(page_table, q_rows, keys, thr, cache_k4, cache_v4, l,
                              new_page(k_s), new_page(v_s))
        attn_s = jnp.transpose(o_rows.reshape(b_s, N_HEADS, tok, HEAD_DIM), (0, 2, 1, 3)
                               ).reshape(b_s * tok, ATTN_WIDTH)
        a_in_s = _gate_mul(attn_s, ps["zg"][:, :ATTN_WIDTH])
        s_in_s, sr_s, si_s = _s5(ps["u"], ps["zg"], state_s5_re[l], state_s5_im[l], s5p, l, b_s, tok, t_s)
        xs = _finish(xs, a_in_s, s_in_s, ps["zg"], wts, l)
        real = lambda a: a.reshape((b_s, tok) + a.shape[1:])[:, :t_s]
        outs_s.append((real(k_s).reshape(b_s, t_s, N_KV_HEADS, HEAD_DIM),
                       real(v_s).reshape(b_s, t_s, N_KV_HEADS, HEAD_DIM),
                       real(ps["ki"]), sr_s, si_s))

    k_prompt, v_prompt, kidx_prompt, s5_re_prompt, s5_im_prompt = [jnp.stack(a) for a in zip(*outs_p)]
    k_sample, v_sample, kidx_sample, s5_re_sample, s5_im_sample = [jnp.stack(a) for a in zip(*outs_s)]
    y_prompt = xp.reshape(b_p, t_p, D_MODEL)
    y_sample = xs.reshape(b_s, tok, D_MODEL)[:, :t_s]
    return (y_prompt, y_sample, k_prompt, v_prompt, kidx_prompt, s5_re_prompt, s5_im_prompt,
            k_sample, v_sample, kidx_sample, s5_re_sample, s5_im_sample)
```

```python
import functools
import math

import jax
import jax.numpy as jnp
from jax import lax
from jax.experimental import pallas as pl
from jax.experimental.pallas import tpu as pltpu

D_MODEL = 2048
PAGE_SIZE = 128
N_HEADS = 16
HEAD_DIM = 128
N_KV_HEADS = 4
HEADS_PER_KV = N_HEADS // N_KV_HEADS
ATTN_WIDTH = N_HEADS * HEAD_DIM
KV_WIDTH = N_KV_HEADS * HEAD_DIM
IDX_HEADS = 16
IDX_DIM = 64
TOPK_MAX = 256
S5_WIDTH = D_MODEL // 2
S5_GROUP = 16
S5_GROUPS = S5_WIDTH // S5_GROUP
S5_STATE = 64
ROPE_THETA = 10000.0
EPS = 1e-6
IN_SIZES = (ATTN_WIDTH, KV_WIDTH, KV_WIDTH, IDX_HEADS * IDX_DIM, IDX_DIM, IDX_HEADS,
            ATTN_WIDTH, S5_WIDTH, S5_WIDTH, D_MODEL, D_MODEL)

LANES = 128
SUBLANES = 8
VMEM_LIMIT_BYTES = 56 * 1024 * 1024
MM_ROW_TILE = 1024
MM_COL_TILE = 1024

S5_SG = S5_WIDTH // LANES
S5_SG_GROUPS = S5_GROUPS // S5_SG
S5_SG_STATE = S5_SG_GROUPS * S5_STATE

SAMPLE_ROWS = 16

ATTN_Q_TILE = 256
ATTN_ROW_BLOCK = 64
PAGES_PER_STEP = 16
IDX_PAGES_PER_STEP = 32
LOG2_E = math.log2(math.e)

INT_MIN = -2 ** 31
MASK_BIAS = -1e30

f32 = jnp.float32
bf16 = jnp.bfloat16
i32 = jnp.int32


def _cparams(sem):
    return pltpu.CompilerParams(dimension_semantics=sem, vmem_limit_bytes=VMEM_LIMIT_BYTES)


def _row_tile(m, cap):
    return m if m <= cap else cap


def _norm_body(x_ref, g_ref, o_ref):
    x = x_ref[...]
    ms = jnp.mean(x * x, axis=-1, keepdims=True)
    o_ref[...] = (x * lax.rsqrt(ms + EPS) * g_ref[...]).astype(o_ref.dtype)


def _layer_spec(block, layer, index_map):
    return pl.BlockSpec((None,) + tuple(block), lambda *idx: (layer,) + tuple(index_map(*idx)))


def _rmsnorm(x, gains, layer):
    m, d = x.shape
    tm = _row_tile(m, 512)
    return pl.pallas_call(
        _norm_body,
        grid=(m // tm,),
        in_specs=[pl.BlockSpec((tm, d), lambda i: (i, 0)),
                  _layer_spec((1, d), layer, lambda i: (0, 0))],
        out_specs=pl.BlockSpec((tm, d), lambda i: (i, 0)),
        out_shape=jax.ShapeDtypeStruct((m, d), bf16),
        compiler_params=_cparams(("parallel",)),
        name="rmsnorm",
    )(x, gains)


def _head_norm_rope(x, gain, cos, sin):
    ms = jnp.mean(x * x, axis=-1, keepdims=True)
    y = x * lax.rsqrt(ms + EPS) * gain
    return y * cos + pltpu.roll(y, HEAD_DIM // 2, 1) * sin


def _qt_body(wt_ref, h_ref, g_ref, cos_ref, sin_ref, o_ref, *, heads):
    acc = lax.dot_general(wt_ref[...], h_ref[...], (((1,), (1,)), ((), ())),
                          preferred_element_type=f32)
    c = cos_ref[...]
    s = sin_ref[...]
    g = g_ref[...]
    half = HEAD_DIM // 2
    scale = HEAD_DIM ** -0.5 * LOG2_E
    for hh in range(heads):
        r0 = hh * HEAD_DIM
        x = acc[r0:r0 + HEAD_DIM]
        ms = jnp.mean(x * x, axis=0, keepdims=True)
        y = x * lax.rsqrt(ms + EPS) * g
        x1 = y[:half]
        x2 = y[half:]
        o_ref[r0:r0 + half, :] = ((x1 * c - x2 * s) * scale).astype(o_ref.dtype)
        o_ref[r0 + half:r0 + HEAD_DIM, :] = ((x2 * c + x1 * s) * scale).astype(o_ref.dtype)


def _proj_qt(h, wt_q, gain, cos_ht, sin_ht, layer):
    m, d = h.shape
    tm = _row_tile(m, MM_ROW_TILE)
    tn = 4 * HEAD_DIM
    return pl.pallas_call(
        functools.partial(_qt_body, heads=tn // HEAD_DIM),
        grid=(m // tm, ATTN_WIDTH // tn),
        in_specs=[_layer_spec((tn, d), layer, lambda i, j: (j, 0)),
                  pl.BlockSpec((tm, d), lambda i, j: (i, 0)),
                  _layer_spec((HEAD_DIM, 1), layer, lambda i, j: (0, 0)),
                  pl.BlockSpec((HEAD_DIM // 2, tm), lambda i, j: (0, i)),
                  pl.BlockSpec((HEAD_DIM // 2, tm), lambda i, j: (0, i))],
        out_specs=pl.BlockSpec((tn, tm), lambda i, j: (j, i)),
        out_shape=jax.ShapeDtypeStruct((ATTN_WIDTH, m), bf16),
        compiler_params=_cparams(("parallel", "parallel")),
        name="proj_q_t",
    )(wt_q, h, gain, cos_ht, sin_ht)


def _kv_body(h_ref, w_ref, wvt_ref, g_ref, cos_ref, sin_ref, k_ref, v_ref, kb_ref, vt_ref, *, chunk):
    h = h_ref[...]
    acc = jnp.dot(h, w_ref[...], preferred_element_type=f32)
    cos = cos_ref[...]
    sin = sin_ref[...]
    g = g_ref[...]
    tm = h.shape[0]
    for hh in range(N_KV_HEADS):
        sl = slice(hh * HEAD_DIM, (hh + 1) * HEAD_DIM)
        kh = _head_norm_rope(acc[:, sl], g, cos, sin)
        rows = pl.ds(hh, tm, stride=N_KV_HEADS)
        k_ref[rows, :] = kh
        v_ref[rows, :] = acc[:, KV_WIDTH + hh * HEAD_DIM:KV_WIDTH + (hh + 1) * HEAD_DIM]
        kb_ref[:, sl] = kh.astype(bf16)
    vt = lax.dot_general(wvt_ref[...], h, (((1,), (1,)), ((), ())), preferred_element_type=f32)
    for cc in range(vt_ref.shape[0]):
        vt_ref[cc] = vt[:, cc * chunk:(cc + 1) * chunk].astype(bf16)


def _proj_kv(h, w_kv, wt_v, gain, cos, sin, layer):
    m, d = h.shape
    tm = _row_tile(m, 512)
    chunk = min(tm, ATTN_Q_TILE)
    row = lambda i: (i, 0)
    fix = lambda i: (0, 0)
    return pl.pallas_call(
        functools.partial(_kv_body, chunk=chunk),
        grid=(m // tm,),
        in_specs=[pl.BlockSpec((tm, d), row),
                  _layer_spec((d, 2 * KV_WIDTH), layer, fix),
                  _layer_spec((KV_WIDTH, d), layer, fix),
                  _layer_spec((1, HEAD_DIM), layer, fix),
                  pl.BlockSpec((tm, HEAD_DIM), row),
                  pl.BlockSpec((tm, HEAD_DIM), row)],
        out_specs=[pl.BlockSpec((tm * N_KV_HEADS, HEAD_DIM), row),
                   pl.BlockSpec((tm * N_KV_HEADS, HEAD_DIM), row),
                   pl.BlockSpec((tm, KV_WIDTH), row),
                   pl.BlockSpec((tm // chunk, KV_WIDTH, chunk), lambda i: (i, 0, 0))],
        out_shape=[jax.ShapeDtypeStruct((m * N_KV_HEADS, HEAD_DIM), f32),
                   jax.ShapeDtypeStruct((m * N_KV_HEADS, HEAD_DIM), f32),
                   jax.ShapeDtypeStruct((m, KV_WIDTH), bf16),
                   jax.ShapeDtypeStruct((m // chunk, KV_WIDTH, chunk), bf16)],
        compiler_params=_cparams(("parallel",)),
        name="proj_kv",
    )(h, w_kv, wt_v, gain, cos, sin)


def _qit_body(wt_ref, h_ref, cos_ref, sin_ref, qit_ref, wt_out_ref):
    acc = lax.dot_general(wt_ref[...], h_ref[...], (((1,), (1,)), ((), ())),
                          preferred_element_type=f32)
    c = cos_ref[...]
    s = sin_ref[...]
    half = IDX_DIM // 2
    scale = IDX_DIM ** -0.5
    for hh in range(IDX_HEADS):
        r0 = hh * IDX_DIM
        x1 = acc[r0:r0 + half]
        x2 = acc[r0 + half:r0 + IDX_DIM]
        qit_ref[r0:r0 + half, :] = ((x1 * c - x2 * s) * scale).astype(qit_ref.dtype)
        qit_ref[r0 + half:r0 + IDX_DIM, :] = ((x2 * c + x1 * s) * scale).astype(qit_ref.dtype)
    n_qi = IDX_HEADS * IDX_DIM
    wt_out_ref[...] = acc[n_qi:n_qi + IDX_HEADS] * (IDX_HEADS ** -0.5)


def _proj_qit(h, wt_qi, cos_t, sin_t, layer):
    m, d = h.shape
    tm = _row_tile(m, 512)
    n_rows = wt_qi.shape[1]
    n_qi = IDX_HEADS * IDX_DIM
    return pl.pallas_call(
        _qit_body,
        grid=(m // tm,),
        in_specs=[_layer_spec((n_rows, d), layer, lambda i: (0, 0)),
                  pl.BlockSpec((tm, d), lambda i: (i, 0)),
                  pl.BlockSpec((IDX_DIM // 2, tm), lambda i: (0, i)),
                  pl.BlockSpec((IDX_DIM // 2, tm), lambda i: (0, i))],
        out_specs=[pl.BlockSpec((n_qi, tm), lambda i: (0, i)),
                   pl.BlockSpec((IDX_HEADS, tm), lambda i: (0, i))],
        out_shape=[jax.ShapeDtypeStruct((n_qi, m), bf16),
                   jax.ShapeDtypeStruct((IDX_HEADS, m), f32)],
        compiler_params=_cparams(("parallel",)),
        name="proj_qi_t",
    )(wt_qi, h, cos_t, sin_t)


def _ki_body(h_ref, w_ref, cos_ref, sina_ref, sinb_ref, ki_ref, kib_ref):
    x = jnp.dot(h_ref[...], w_ref[...], preferred_element_type=f32)
    half = IDX_DIM // 2
    r = (x * cos_ref[...] + pltpu.roll(x, LANES - half, 1) * sina_ref[...]
         + pltpu.roll(x, half, 1) * sinb_ref[...])
    ki = r[:, :IDX_DIM]
    ki_ref[...] = ki
    kib_ref[...] = ki.astype(bf16)


def _proj_ki(h, w_ki, cos_k, sin_a, sin_b, layer):
    m, d = h.shape
    tm = _row_tile(m, 512)
    row = lambda i: (i, 0)
    return pl.pallas_call(
        _ki_body,
        grid=(m // tm,),
        in_specs=[pl.BlockSpec((tm, d), row),
                  _layer_spec((d, LANES), layer, lambda i: (0, 0)),
                  pl.BlockSpec((tm, LANES), row),
                  pl.BlockSpec((tm, LANES), row),
                  pl.BlockSpec((tm, LANES), row)],
        out_specs=[pl.BlockSpec((tm, IDX_DIM), row)] * 2,
        out_shape=[jax.ShapeDtypeStruct((m, IDX_DIM), f32),
                   jax.ShapeDtypeStruct((m, IDX_DIM), bf16)],
        compiler_params=_cparams(("parallel",)),
        name="proj_ki",
    )(h, w_ki, cos_k, sin_a, sin_b)


def _plain_body(h_ref, w_ref, o_ref):
    o_ref[...] = jnp.dot(h_ref[...], w_ref[...], preferred_element_type=f32).astype(o_ref.dtype)


def _proj_plain(h, w, out_dtype, layer):
    m, d = h.shape
    n = w.shape[2]
    tm = _row_tile(m, MM_ROW_TILE)
    tn = MM_COL_TILE
    return pl.pallas_call(
        _plain_body,
        grid=(m // tm, n // tn),
        in_specs=[pl.BlockSpec((tm, d), lambda i, j: (i, 0)),
                  _layer_spec((d, tn), layer, lambda i, j: (0, j))],
        out_specs=pl.BlockSpec((tm, tn), lambda i, j: (i, j)),
        out_shape=jax.ShapeDtypeStruct((m, n), out_dtype),
        compiler_params=_cparams(("parallel", "parallel")),
        name="proj_u",
    )(h, w)


def _sigmoid(x):
    return 1.0 / (1.0 + jnp.exp(-x))


def _zg_body(h_ref, w_ref, o_ref, *, silu_tiles):
    acc = jnp.dot(h_ref[...], w_ref[...], preferred_element_type=f32)
    sg = _sigmoid(acc)
    is_silu = pl.program_id(1) < silu_tiles
    o_ref[...] = jnp.where(is_silu, acc * sg, sg).astype(o_ref.dtype)


def _proj_zg(h, w_zg, layer):
    m, d = h.shape
    n = w_zg.shape[2]
    tm = _row_tile(m, MM_ROW_TILE)
    tn = MM_COL_TILE
    return pl.pallas_call(
        functools.partial(_zg_body, silu_tiles=(ATTN_WIDTH + S5_WIDTH) // tn),
        grid=(m // tm, n // tn),
        in_specs=[pl.BlockSpec((tm, d), lambda i, j: (i, 0)),
                  _layer_spec((d, tn), layer, lambda i, j: (0, j))],
        out_specs=pl.BlockSpec((tm, tn), lambda i, j: (i, j)),
        out_shape=jax.ShapeDtypeStruct((m, n), bf16),
        compiler_params=_cparams(("parallel", "parallel")),
        name="proj_zg",
    )(h, w_zg)


def _sortable_key(score):
    b = pltpu.bitcast(score, i32)
    return b ^ ((b >> 31) & jnp.int32(0x7FFFFFFF))


def _kth_largest(count_ge, shape, k):
    zero = jnp.zeros(shape, i32)
    prefix = jnp.where(count_ge(zero) >= k, zero, jnp.full(shape, INT_MIN, i32))

    def bit_body(bi, prefix):
        cand = prefix | (jnp.int32(1) << (30 - bi))
        return jnp.where(count_ge(cand) >= k, cand, prefix)

    thr = lax.fori_loop(0, 31, bit_body, prefix)
    return jnp.maximum(thr, jnp.int32(INT_MIN + 1))


def _resolve_ties(thr, k, count_ge, count_tied_before, demote, n_pos_bits, active=None):
    excess = count_ge(thr) - k
    if active is not None:
        excess = jnp.where(active, excess, 0)

    @pl.when(jnp.max(excess) > 0)
    def _():
        need = k - count_ge(thr + 1)

        def bit_body(bi, pos):
            cand = pos | (jnp.int32(1) << (n_pos_bits - 1 - bi))
            return jnp.where(count_tied_before(thr, cand) < need, cand, pos)

        last = lax.fori_loop(0, n_pos_bits, bit_body, jnp.zeros(thr.shape, i32))
        demote(thr, last)


def _sublane_allreduce(x, op):
    for shift in (4, 2, 1):
        x = op(x, pltpu.roll(x, shift, 0))
    return x


def _attn_prompt_body(qt_ref, qit_ref, wt_ref, z_ref, k_ref, vt_ref, ki_ref, o_ref,
                      key_scr, acc_scr, m_scr, l_scr, alpha_scr, s_scr, p_scr, bias_scr,
                      *, tq, k_top, n_pos_bits):
    i = pl.program_id(1)
    n_chunks = i + 1
    half = tq // 2

    def score_keys(c, diagonal):
        for sub in range(2):
            r0 = pl.multiple_of(c * tq + sub * half, half)
            kic = ki_ref[pl.ds(r0, half), :]
            acc = jnp.zeros((half, tq), f32)
            for hh in range(IDX_HEADS):
                lg = jnp.dot(kic, qit_ref[hh * IDX_DIM:(hh + 1) * IDX_DIM, :],
                             preferred_element_type=f32)
                acc = acc + jnp.maximum(lg, 0.0) * wt_ref[hh:hh + 1, :]
            key = _sortable_key(acc)
            if diagonal:
                kpos = lax.broadcasted_iota(i32, (half, tq), 0) + sub * half
                qpos = lax.broadcasted_iota(i32, (half, tq), 1)
                key = jnp.where(kpos <= qpos, key, jnp.int32(INT_MIN))
            key_scr[pl.ds(r0, half), :] = key

    def full_chunk(c, carry):
        score_keys(c, False)
        return carry

    lax.fori_loop(0, i, full_chunk, 0)
    score_keys(i, True)

    def count_ge(cand):
        def body(c, cnt):
            kk = key_scr[pl.ds(pl.multiple_of(c * tq, tq), tq), :]
            hit = (kk >= cand).astype(i32)
            return cnt + jnp.sum(hit.reshape(tq // SUBLANES, SUBLANES, tq), axis=0)

        cnt = lax.fori_loop(0, n_chunks, body, jnp.zeros((SUBLANES, tq), i32))
        return jnp.sum(cnt, axis=0, keepdims=True)

    thr = _kth_largest(count_ge, (1, tq), k_top)

    def chunk_positions(c):
        return lax.broadcasted_iota(i32, (tq, tq), 0) + c * tq

    def count_tied_before(thr, pos_limit):
        def body(c, cnt):
            kk = key_scr[pl.ds(pl.multiple_of(c * tq, tq), tq), :]
            hit = ((kk == thr) & (chunk_positions(c) < pos_limit)).astype(i32)
            return cnt + jnp.sum(hit.reshape(tq // SUBLANES, SUBLANES, tq), axis=0)

        cnt = lax.fori_loop(0, n_chunks, body, jnp.zeros((SUBLANES, tq), i32))
        return jnp.sum(cnt, axis=0, keepdims=True)

    def demote(thr, last):
        def body(c, carry):
            rows = pl.ds(pl.multiple_of(c * tq, tq), tq)
            kk = key_scr[rows, :]
            key_scr[rows, :] = jnp.where((kk == thr) & (chunk_positions(c) > last), thr - 1, kk)
            return carry

        lax.fori_loop(0, n_chunks, body, 0)

    _resolve_ties(thr, k_top, count_ge, count_tied_before, demote, n_pos_bits)

    m_scr[...] = jnp.full(m_scr.shape, -jnp.inf, f32)
    l_scr[...] = jnp.zeros(l_scr.shape, f32)
    acc_scr[...] = jnp.zeros(acc_scr.shape, f32)
    n_sub = tq // SUBLANES
    d_sub = HEAD_DIM // SUBLANES

    def attend_chunk(c, carry):
        r0 = pl.multiple_of(c * tq, tq)
        bias_scr[...] = jnp.where(key_scr[pl.ds(r0, tq), :] >= thr, 0.0, MASK_BIAS)
        for g in range(N_KV_HEADS):
            kc = k_ref[pl.ds(r0, tq), g * HEAD_DIM:(g + 1) * HEAD_DIM]
            for hh in range(HEADS_PER_KV):
                h = g * HEADS_PER_KV + hh
                s = jnp.dot(kc, qt_ref[h * HEAD_DIM:(h + 1) * HEAD_DIM, :], preferred_element_type=f32)
                s_scr[h] = s + bias_scr[...]
        blocks = [slice(j * ATTN_ROW_BLOCK, (j + 1) * ATTN_ROW_BLOCK) for j in range(tq // ATTN_ROW_BLOCK)]
        blk_sub = ATTN_ROW_BLOCK // SUBLANES
        for h in range(N_HEADS):
            mx = m_scr[h]
            for rows in blocks:
                mx = jnp.maximum(mx, jnp.max(s_scr[h, rows, :].reshape(blk_sub, SUBLANES, tq), axis=0))
            m_new = _sublane_allreduce(mx, jnp.maximum)
            alpha_scr[h] = jnp.exp2(m_scr[h] - m_new)
            m_scr[h] = m_new
        for h in range(N_HEADS):
            m_new = m_scr[h]
            psum = jnp.zeros((SUBLANES, tq), f32)
            for rows in blocks:
                p3 = jnp.exp2(s_scr[h, rows, :].reshape(blk_sub, SUBLANES, tq) - m_new[None])
                psum = psum + jnp.sum(p3, axis=0)
                p_scr[h, rows, :] = p3.reshape(ATTN_ROW_BLOCK, tq).astype(bf16)
            l_scr[h] = alpha_scr[h] * l_scr[h] + _sublane_allreduce(psum, jnp.add)
        for g in range(N_KV_HEADS):
            vtc = vt_ref[c, g * HEAD_DIM:(g + 1) * HEAD_DIM, :]
            for hh in range(HEADS_PER_KV):
                h = g * HEADS_PER_KV + hh
                hs = slice(h * HEAD_DIM, (h + 1) * HEAD_DIM)
                pv = jnp.dot(vtc, p_scr[h], preferred_element_type=f32)
                acc = acc_scr[hs, :].reshape(d_sub, SUBLANES, tq) * alpha_scr[h][None]
                acc_scr[hs, :] = acc.reshape(HEAD_DIM, tq) + pv
        return carry

    lax.fori_loop(0, n_chunks, attend_chunk, 0)

    for h in range(N_HEADS):
        hs = slice(h * HEAD_DIM, (h + 1) * HEAD_DIM)
        out_t = (acc_scr[hs, :].reshape(d_sub, SUBLANES, tq) / l_scr[h][None]).reshape(HEAD_DIM, tq)
        o_ref[:, hs] = (out_t.T * z_ref[:, hs].astype(f32)).astype(o_ref.dtype)


def _attn_prompt(qt, qit, wt, zg, kb, vt, kib, batch, seq):
    tq = ATTN_Q_TILE
    assert seq % tq == 0
    nq = seq // tq
    k_top = min(TOPK_MAX, seq // 4)
    assert k_top <= tq
    qrow = lambda b, i: (b * nq + i, 0)
    qcol = lambda b, i: (0, b * nq + i)
    per_b = lambda b, i: (b, 0)
    return pl.pallas_call(
        functools.partial(_attn_prompt_body, tq=tq, k_top=k_top, n_pos_bits=(seq - 1).bit_length()),
        grid=(batch, nq),
        in_specs=[pl.BlockSpec((ATTN_WIDTH, tq), qcol),
                  pl.BlockSpec((IDX_HEADS * IDX_DIM, tq), qcol),
                  pl.BlockSpec((IDX_HEADS, tq), qcol),
                  pl.BlockSpec((tq, ATTN_WIDTH), qrow),
                  pl.BlockSpec((seq, KV_WIDTH), per_b),
                  pl.BlockSpec((nq, KV_WIDTH, tq), lambda b, i: (b, 0, 0)),
                  pl.BlockSpec((seq, IDX_DIM), per_b)],
        out_specs=pl.BlockSpec((tq, ATTN_WIDTH), qrow),
        out_shape=jax.ShapeDtypeStruct((batch * seq, ATTN_WIDTH), bf16),
        scratch_shapes=[pltpu.VMEM((seq, tq), i32),
                        pltpu.VMEM((ATTN_WIDTH, tq), f32),
                        pltpu.VMEM((N_HEADS, SUBLANES, tq), f32),
                        pltpu.VMEM((N_HEADS, SUBLANES, tq), f32),
                        pltpu.VMEM((N_HEADS, SUBLANES, tq), f32),
                        pltpu.VMEM((N_HEADS, tq, tq), f32),
                        pltpu.VMEM((N_HEADS, tq, tq), bf16),
                        pltpu.VMEM((tq, tq), f32)],
        compiler_params=_cparams(("parallel", "arbitrary")),
        name="attn_prompt",
    )(qt, qit, wt, zg, kb, vt, kib)


def _idx_sample_body(pt_ref, qi_ref, w_ref, *rest, n_pages, n_new, k_top, ppb):
    kid_refs = rest[:ppb]
    kinew_ref, key_ref, thr_ref = rest[ppb:]
    p = pl.program_id(1)
    tok = SAMPLE_ROWS

    def page_keys(ki_page_t):
        lg = jnp.dot(qi_ref[0], ki_page_t.astype(bf16), preferred_element_type=f32)
        wgt = jnp.maximum(lg, 0.0) * w_ref[0]
        score = jnp.sum(wgt.reshape(IDX_HEADS, tok, PAGE_SIZE), axis=0)
        return _sortable_key(score)

    for j in range(ppb):
        key_ref[0, p * ppb + j] = page_keys(kid_refs[j][...])

    @pl.when(p == pl.num_programs(1) - 1)
    def _():
        key = page_keys(kinew_ref[0])
        kpos = lax.broadcasted_iota(i32, (tok, PAGE_SIZE), 1)
        qtok = lax.broadcasted_iota(i32, (tok, PAGE_SIZE), 0)
        key_ref[0, n_pages] = jnp.where((kpos <= qtok) & (kpos < n_new), key, jnp.int32(INT_MIN))

        def count_ge(cand):
            def body(c, cnt):
                hit = (key_ref[0, pl.ds(c * ppb, ppb)] >= cand).astype(i32)
                return cnt + jnp.sum(hit, axis=0)

            cnt = lax.fori_loop(0, n_pages // ppb, body, jnp.zeros((tok, PAGE_SIZE), i32))
            cnt = cnt + (key_ref[0, n_pages] >= cand).astype(i32)
            return jnp.sum(cnt, axis=-1, keepdims=True)

        thr = _kth_largest(count_ge, (tok, 1), k_top)
        thr_ref[0] = jnp.broadcast_to(thr, (tok, PAGE_SIZE))

        def page_positions(c):
            return lax.broadcasted_iota(i32, (tok, PAGE_SIZE), 1) + c * PAGE_SIZE

        def count_tied_before(thr, pos_limit):
            def body(c, cnt):
                return cnt + ((key_ref[0, c] == thr) & (page_positions(c) < pos_limit)).astype(i32)

            cnt = lax.fori_loop(0, n_pages + 1, body, jnp.zeros((tok, PAGE_SIZE), i32))
            return jnp.sum(cnt, axis=-1, keepdims=True)

        def demote(thr, last):
            def body(c, carry):
                kk = key_ref[0, c]
                key_ref[0, c] = jnp.where((kk == thr) & (page_positions(c) > last), thr - 1, kk)
                return carry

            lax.fori_loop(0, n_pages + 1, body, 0)

        real_rows = lax.broadcasted_iota(i32, (tok, 1), 0) < n_new
        _resolve_ties(thr, k_top, count_ge, count_tied_before, demote,
                      ((n_pages + 1) * PAGE_SIZE - 1).bit_length(), active=real_rows)


def _pages_per_step(n_pages, cap=PAGES_PER_STEP):
    ppb = min(cap, n_pages)
    assert n_pages % ppb == 0
    return ppb


def _page_spec(block, layer, ppb, j):
    return pl.BlockSpec(block, lambda b, p, pt: (layer, pt[b, p * ppb + j], 0, 0))


def _idx_sample(page_table, qi_rows, w_rows, cache_kidx, layer, ki_new_pages, n_new):
    nb, n_pages = page_table.shape
    tok = SAMPLE_ROWS
    ppb = _pages_per_step(n_pages, IDX_PAGES_PER_STEP)
    k_top = min(TOPK_MAX, (n_pages * PAGE_SIZE + n_new) // 4)
    per_b = lambda b, p, pt: (b, 0, 0)
    grid_spec = pltpu.PrefetchScalarGridSpec(
        num_scalar_prefetch=1,
        grid=(nb, n_pages // ppb),
        in_specs=[pl.BlockSpec((1, IDX_HEADS * tok, IDX_DIM), per_b),
                  pl.BlockSpec((1, IDX_HEADS * tok, 1), per_b)]
                 + [_page_spec((None, None, IDX_DIM, PAGE_SIZE), layer, ppb, j) for j in range(ppb)]
                 + [pl.BlockSpec((1, IDX_DIM, PAGE_SIZE), per_b)],
        out_specs=[pl.BlockSpec((1, n_pages + 1, tok, PAGE_SIZE), lambda b, p, pt: (b, 0, 0, 0)),
                   pl.BlockSpec((1, tok, PAGE_SIZE), per_b)],
    )
    return pl.pallas_call(
        functools.partial(_idx_sample_body, n_pages=n_pages, n_new=n_new, k_top=k_top, ppb=ppb),
        grid_spec=grid_spec,
        out_shape=[jax.ShapeDtypeStruct((nb, n_pages + 1, tok, PAGE_SIZE), i32),
                   jax.ShapeDtypeStruct((nb, tok, PAGE_SIZE), i32)],
        compiler_params=_cparams(("parallel", "arbitrary")),
        name="idx_sample",
    )(page_table, qi_rows, w_rows, *([cache_kidx] * ppb), ki_new_pages)


def _attn_sample_body(pt_ref, q_ref, key_ref, keyn_ref, thr_ref, *rest, ppb):
    k_refs = rest[:ppb]
    v_refs = rest[ppb:2 * ppb]
    kn_ref, vn_ref, o_ref, acc_scr, m_scr, l_scr = rest[2 * ppb:]
    p = pl.program_id(1)
    tok = SAMPLE_ROWS
    rows_g = HEADS_PER_KV * tok
    thr = thr_ref[0]

    @pl.when(p == 0)
    def _():
        m_scr[...] = jnp.full(m_scr.shape, -jnp.inf, f32)
        l_scr[...] = jnp.zeros(l_scr.shape, f32)
        acc_scr[...] = jnp.zeros(acc_scr.shape, f32)

    def attend(bias_tok, k_of_group, v_of_group):
        bias = jnp.concatenate([bias_tok] * HEADS_PER_KV, axis=0)
        for g in range(N_KV_HEADS):
            rs = slice(g * rows_g, (g + 1) * rows_g)
            s = lax.dot_general(q_ref[0, rs, :], k_of_group(g), (((1,), (1,)), ((), ())),
                                preferred_element_type=f32) + bias
            m_prev = m_scr[rs, :]
            m_new = jnp.maximum(m_prev, jnp.max(s, axis=-1, keepdims=True))
            alpha = jnp.exp2(m_prev - m_new)
            pr = jnp.exp2(s - m_new)
            l_scr[rs, :] = alpha * l_scr[rs, :] + jnp.sum(pr, axis=-1, keepdims=True)
            acc_scr[rs, :] = alpha * acc_scr[rs, :] + jnp.dot(
                pr.astype(bf16), v_of_group(g), preferred_element_type=f32)
            m_scr[rs, :] = m_new

    def mask_bias(keys):
        return jnp.where(keys >= thr, 0.0, MASK_BIAS).astype(f32)

    def cached(refs):
        def of_group(g):
            rows = pl.ds(g, PAGE_SIZE, stride=N_KV_HEADS)
            return jnp.concatenate([r[rows, :].astype(bf16) for r in refs], axis=0)
        return of_group

    attend(jnp.concatenate([mask_bias(key_ref[0, j]) for j in range(ppb)], axis=1),
           cached(k_refs), cached(v_refs))

    @pl.when(p == pl.num_programs(1) - 1)
    def _():
        def fresh(ref):
            return lambda g: ref[0, :, g * HEAD_DIM:(g + 1) * HEAD_DIM].astype(bf16)

        attend(mask_bias(keyn_ref[0, 0]), fresh(kn_ref), fresh(vn_ref))
        o_ref[0] = acc_scr[...] / l_scr[...]


def _attn_sample(page_table, q_rows, keys, thr, cache_k, cache_v, layer, k_new_pages, v_new_pages):
    nb, n_pages = page_table.shape
    tok = SAMPLE_ROWS
    rows = N_HEADS * tok
    ppb = _pages_per_step(n_pages)
    per_b = lambda b, p, pt: (b, 0, 0)
    page_block = (None, None, PAGE_SIZE * N_KV_HEADS, HEAD_DIM)
    grid_spec = pltpu.PrefetchScalarGridSpec(
        num_scalar_prefetch=1,
        grid=(nb, n_pages // ppb),
        in_specs=[pl.BlockSpec((1, rows, HEAD_DIM), per_b),
                  pl.BlockSpec((1, ppb, tok, PAGE_SIZE), lambda b, p, pt: (b, p, 0, 0)),
                  pl.BlockSpec((1, 1, tok, PAGE_SIZE), lambda b, p, pt: (b, n_pages, 0, 0)),
                  pl.BlockSpec((1, tok, PAGE_SIZE), per_b)]
                 + [_page_spec(page_block, layer, ppb, j) for j in range(ppb)]
                 + [_page_spec(page_block, layer, ppb, j) for j in range(ppb)]
                 + [pl.BlockSpec((1, PAGE_SIZE, KV_WIDTH), per_b),
                    pl.BlockSpec((1, PAGE_SIZE, KV_WIDTH), per_b)],
        out_specs=pl.BlockSpec((1, rows, HEAD_DIM), per_b),
        scratch_shapes=[pltpu.VMEM((rows, HEAD_DIM), f32),
                        pltpu.VMEM((rows, 1), f32),
                        pltpu.VMEM((rows, 1), f32)],
    )
    return pl.pallas_call(
        functools.partial(_attn_sample_body, ppb=ppb),
        grid_spec=grid_spec,
        out_shape=jax.ShapeDtypeStruct((nb, rows, HEAD_DIM), f32),
        compiler_params=_cparams(("parallel", "arbitrary")),
        name="attn_sample",
    )(page_table, q_rows, keys, keys, thr, *([cache_k] * ppb), *([cache_v] * ppb),
      k_new_pages, v_new_pages)


def _gate_mul_body(a_ref, z_ref, o_ref):
    o_ref[...] = (a_ref[...] * z_ref[...].astype(f32)).astype(o_ref.dtype)


def _gate_mul(a, zg):
    m, n = a.shape
    return pl.pallas_call(
        _gate_mul_body,
        grid=(1,),
        in_specs=[pl.BlockSpec((m, n), lambda i: (0, 0)),
                  pl.BlockSpec((m, n), lambda i: (0, 0))],
        out_specs=pl.BlockSpec((m, n), lambda i: (0, 0)),
        out_shape=jax.ShapeDtypeStruct((m, n), bf16),
        compiler_params=_cparams(("arbitrary",)),
        name="gate_mul",
    )(a, zg)


def _s5_param_body(lr_ref, li_ref, ldt_ref, bre_ref, bim_ref, ar_ref, ai_ref, bbre_ref, bbim_ref):
    lr = lr_ref[...]
    li = li_ref[...]
    dt = jnp.exp(ldt_ref[...])
    mag = jnp.exp(lr * dt)
    ar = mag * jnp.cos(li * dt)
    ai = mag * jnp.sin(li * dt)
    den = lr * lr + li * li
    nr = ar - 1.0
    cr = (nr * lr + ai * li) / den
    ci = (ai * lr - nr * li) / den
    ar_ref[...] = ar
    ai_ref[...] = ai
    for h in range(S5_GROUP):
        bbre_ref[h] = cr * bre_ref[h] - ci * bim_ref[h]
        bbim_ref[h] = cr * bim_ref[h] + ci * bre_ref[h]


def _s5_params(lam_re, lam_im, log_dt, b_re, b_im):
    depth = lam_re.shape[0]
    gs = (None, S5_GROUPS, S5_STATE)
    bs = (None, S5_GROUP, S5_GROUPS, S5_STATE)
    at3 = lambda l: (l, 0, 0)
    at4 = lambda l: (l, 0, 0, 0)
    return pl.pallas_call(
        _s5_param_body,
        grid=(depth,),
        in_specs=[pl.BlockSpec(gs, at3), pl.BlockSpec(gs, at3),
                  pl.BlockSpec((None, S5_GROUPS, 1), at3),
                  pl.BlockSpec(bs, at4), pl.BlockSpec(bs, at4)],
        out_specs=[pl.BlockSpec(gs, at3), pl.BlockSpec(gs, at3),
                   pl.BlockSpec(bs, at4), pl.BlockSpec(bs, at4)],
        out_shape=[jax.ShapeDtypeStruct((depth, S5_GROUPS, S5_STATE), f32),
                   jax.ShapeDtypeStruct((depth, S5_GROUPS, S5_STATE), f32),
                   jax.ShapeDtypeStruct((depth, S5_GROUP, S5_GROUPS, S5_STATE), f32),
                   jax.ShapeDtypeStruct((depth, S5_GROUP, S5_GROUPS, S5_STATE), f32)],
        compiler_params=_cparams(("arbitrary",)),
        name="s5_params",
    )(lam_re, lam_im, log_dt.reshape(depth, S5_GROUPS, 1), b_re, b_im)


def _gelu_tanh(x):
    return 0.5 * x * (1.0 + jnp.tanh(math.sqrt(2.0 / math.pi) * (x + 0.044715 * (x * x * x))))


def _s5_body(u_ref, zs_ref, h0r_ref, h0i_ref, ar_ref, ai_ref, wbu_ref, wcr_ref, wci_ref, d_ref, wglu_ref,
             o_ref, sr_ref, si_ref, xr_scr, xi_scr, y_scr, str_scr, sti_scr, *, tc, n_last):
    c = pl.program_id(1)

    @pl.when(c == 0)
    def _():
        str_scr[...] = h0r_ref[0]
        sti_scr[...] = h0i_ref[0]

    n_ct = S5_SG_STATE // LANES

    def gather_planes(scr, rows):
        return jnp.concatenate([scr[j, rows, :] for j in range(n_ct)], axis=1)

    def scatter_planes(scr, rows, val):
        for j in range(n_ct):
            scr[j, rows, :] = val[:, j * LANES:(j + 1) * LANES]

    for sg in range(S5_SG):
        rows = pl.ds(sg, tc, stride=S5_SG)
        u_sg = u_ref[:, sg * LANES:(sg + 1) * LANES]
        bu = jnp.dot(u_sg.astype(bf16), wbu_ref[sg], preferred_element_type=f32)
        scatter_planes(xr_scr, rows, bu[:, :S5_SG_STATE])
        scatter_planes(xi_scr, rows, bu[:, S5_SG_STATE:])

    ar = ar_ref[...]
    ai = ai_ref[...]

    def step(t, carry):
        xr, xi = carry
        r = pl.ds(pl.multiple_of(t * S5_SG, S5_SG), S5_SG)
        nr = ar * xr - ai * xi + gather_planes(xr_scr, r)
        ni = ar * xi + ai * xr + gather_planes(xi_scr, r)
        scatter_planes(xr_scr, r, nr)
        scatter_planes(xi_scr, r, ni)
        return nr, ni

    xr, xi = lax.fori_loop(0, tc, step, (str_scr[...], sti_scr[...]))
    str_scr[...] = xr
    sti_scr[...] = xi

    for sg in range(S5_SG):
        rows = pl.ds(sg, tc, stride=S5_SG)
        cs = slice(sg * LANES, (sg + 1) * LANES)
        y = (jnp.dot(gather_planes(xr_scr, rows).astype(bf16), wcr_ref[sg], preferred_element_type=f32)
             - jnp.dot(gather_planes(xi_scr, rows).astype(bf16), wci_ref[sg], preferred_element_type=f32)
             + d_ref[:, cs] * u_ref[:, cs])
        y_scr[:, cs] = _gelu_tanh(y)

    y = y_scr[...]
    gate = _sigmoid(jnp.dot(y.astype(bf16), wglu_ref[...], preferred_element_type=f32))
    o_ref[...] = (y * gate * zs_ref[...].astype(f32)).astype(o_ref.dtype)

    @pl.when(c == pl.num_programs(1) - 1)
    def _():
        last = slice((n_last - 1) * S5_SG, n_last * S5_SG)
        sr_ref[0] = gather_planes(xr_scr, last)
        si_ref[0] = gather_planes(xi_scr, last)


def _s5(u, zg, h0_re, h0_im, prm, layer, nb, seq, n_real):
    tc = min(seq, 256)
    nch = seq // tc
    n_last = n_real - (nch - 1) * tc
    zs_col = ATTN_WIDTH // S5_WIDTH
    fix2 = lambda b, c: (0, 0)
    fix3 = lambda b, c: (0, 0, 0)
    st = lambda b, c: (b, 0, 0)
    out, s_re, s_im = pl.pallas_call(
        functools.partial(_s5_body, tc=tc, n_last=n_last),
        grid=(nb, nch),
        in_specs=[pl.BlockSpec((tc, S5_WIDTH), lambda b, c: (b * nch + c, 0)),
                  pl.BlockSpec((tc, S5_WIDTH), lambda b, c: (b * nch + c, zs_col)),
                  pl.BlockSpec((1, S5_SG, S5_SG_STATE), st),
                  pl.BlockSpec((1, S5_SG, S5_SG_STATE), st),
                  _layer_spec((S5_SG, S5_SG_STATE), layer, fix2),
                  _layer_spec((S5_SG, S5_SG_STATE), layer, fix2),
                  _layer_spec((S5_SG, LANES, 2 * S5_SG_STATE), layer, fix3),
                  _layer_spec((S5_SG, S5_SG_STATE, LANES), layer, fix3),
                  _layer_spec((S5_SG, S5_SG_STATE, LANES), layer, fix3),
                  _layer_spec((1, S5_WIDTH), layer, fix2),
                  _layer_spec((S5_WIDTH, S5_WIDTH), layer, fix2)],
        out_specs=[pl.BlockSpec((tc, S5_WIDTH), lambda b, c: (b * nch + c, 0)),
                   pl.BlockSpec((1, S5_SG, S5_SG_STATE), st),
                   pl.BlockSpec((1, S5_SG, S5_SG_STATE), st)],
        out_shape=[jax.ShapeDtypeStruct((nb * seq, S5_WIDTH), bf16),
                   jax.ShapeDtypeStruct((nb, S5_SG, S5_SG_STATE), f32),
                   jax.ShapeDtypeStruct((nb, S5_SG, S5_SG_STATE), f32)],
        scratch_shapes=[pltpu.VMEM((S5_SG_STATE // LANES, tc * S5_SG, LANES), f32),
                        pltpu.VMEM((S5_SG_STATE // LANES, tc * S5_SG, LANES), f32),
                        pltpu.VMEM((tc, S5_WIDTH), f32),
                        pltpu.VMEM((S5_SG, S5_SG_STATE), f32),
                        pltpu.VMEM((S5_SG, S5_SG_STATE), f32)],
        compiler_params=_cparams(("parallel", "arbitrary")),
        name="s5_scan",
    )(u, zg, h0_re.reshape(nb, S5_SG, S5_SG_STATE), h0_im.reshape(nb, S5_SG, S5_SG_STATE),
      prm["a_re"], prm["a_im"], prm["w_bu"], prm["w_c_re"], prm["w_c_im"], prm["d"], prm["w_glu"])
    return out, s_re.reshape(nb, S5_GROUPS, S5_STATE), s_im.reshape(nb, S5_GROUPS, S5_STATE)


def _merge_body(a_ref, s_ref, wa_ref, ws_ref, ga_ref, gs_ref, o_ref):
    o_a = jnp.dot(a_ref[...], wa_ref[...], preferred_element_type=f32)
    o_s = jnp.dot(s_ref[...], ws_ref[...], preferred_element_type=f32)
    o_ref[...] = (ga_ref[...].astype(f32) * o_a + gs_ref[...].astype(f32) * o_s).astype(o_ref.dtype)


def _merge(a_in, s_in, w_a, w_s, zg, layer):
    m = a_in.shape[0]
    tm = _row_tile(m, MM_ROW_TILE)
    tn = MM_COL_TILE
    ga0 = (ATTN_WIDTH + S5_WIDTH) // tn
    gs0 = (ATTN_WIDTH + S5_WIDTH + D_MODEL) // tn
    return pl.pallas_call(
        _merge_body,
        grid=(m // tm, D_MODEL // tn),
        in_specs=[pl.BlockSpec((tm, ATTN_WIDTH), lambda i, j: (i, 0)),
                  pl.BlockSpec((tm, S5_WIDTH), lambda i, j: (i, 0)),
                  _layer_spec((ATTN_WIDTH, tn), layer, lambda i, j: (0, j)),
                  _layer_spec((S5_WIDTH, tn), layer, lambda i, j: (0, j)),
                  pl.BlockSpec((tm, tn), lambda i, j: (i, ga0 + j)),
                  pl.BlockSpec((tm, tn), lambda i, j: (i, gs0 + j))],
        out_specs=pl.BlockSpec((tm, tn), lambda i, j: (i, j)),
        out_shape=jax.ShapeDtypeStruct((m, D_MODEL), bf16),
        compiler_params=_cparams(("parallel", "parallel")),
        name="merge",
    )(a_in, s_in, w_a, w_s, zg, zg)


def _out_body(x_ref, m_ref, w_ref, o_ref):
    o_ref[...] = x_ref[...] + jnp.dot(m_ref[...], w_ref[...], preferred_element_type=f32)


def _out_proj(x, merged, w_out, layer):
    m = x.shape[0]
    tm = _row_tile(m, MM_ROW_TILE)
    tn = MM_COL_TILE
    return pl.pallas_call(
        _out_body,
        grid=(m // tm, D_MODEL // tn),
        in_specs=[pl.BlockSpec((tm, tn), lambda i, j: (i, j)),
                  pl.BlockSpec((tm, D_MODEL), lambda i, j: (i, 0)),
                  _layer_spec((D_MODEL, tn), layer, lambda i, j: (0, j))],
        out_specs=pl.BlockSpec((tm, tn), lambda i, j: (i, j)),
        out_shape=jax.ShapeDtypeStruct((m, D_MODEL), f32),
        compiler_params=_cparams(("parallel", "parallel")),
        name="out_proj",
    )(x, merged, w_out)


def _rope_tables(pos):
    posf = pos.astype(f32)[:, None]
    half = HEAD_DIM // 2
    inv = ROPE_THETA ** (-jnp.arange(half, dtype=f32) / half)
    ang = posf * inv[None, :]
    cos, sin = jnp.cos(ang), jnp.sin(ang)
    cos_h = jnp.concatenate([cos, cos], axis=1)
    sin_h = jnp.concatenate([-sin, sin], axis=1)
    half_i = IDX_DIM // 2
    inv_i = ROPE_THETA ** (-jnp.arange(half_i, dtype=f32) / half_i)
    ang_i = posf * inv_i[None, :]
    cos_i, sin_i = jnp.cos(ang_i), jnp.sin(ang_i)
    zeros = jnp.zeros_like(cos_i)
    pad = jnp.zeros((pos.shape[0], LANES - IDX_DIM), f32)
    cos_k = jnp.concatenate([cos_i, cos_i, pad], axis=1)
    sin_a = jnp.concatenate([-sin_i, zeros, pad], axis=1)
    sin_b = jnp.concatenate([zeros, sin_i, pad], axis=1)
    return dict(cos_h=cos_h, sin_h=sin_h, cos_k=cos_k, sin_a=sin_a, sin_b=sin_b,
                cos_ht=cos.T, sin_ht=sin.T, cos_t=cos_i.T, sin_t=sin_i.T)


def _pack_weights(w_in, w_glu, w_br_attn, w_br_s5, w_out, norm_gain, q_norm_gain, k_norm_gain):
    depth = w_in.shape[0]
    offs = [0]
    for s in IN_SIZES:
        offs.append(offs[-1] + s)
    w_bf = w_in.astype(bf16)
    seg = lambda k: w_bf[:, :, offs[k]:offs[k + 1]]
    t = lambda a: jnp.swapaxes(a, 1, 2)
    return dict(
        wt_q=t(seg(0)), wt_v=t(seg(2)),
        w_kv=jnp.concatenate([seg(1), seg(2)], axis=2),
        wt_qi=t(jnp.concatenate([seg(3), seg(5)], axis=2)),
        w_ki=jnp.pad(seg(4), ((0, 0), (0, 0), (0, LANES - IDX_DIM))),
        w_u=seg(7),
        w_zg=jnp.concatenate([seg(6), seg(8), seg(9), seg(10)], axis=2),
        w_glu=w_glu.astype(bf16), w_br_attn=w_br_attn.astype(bf16),
        w_br_s5=w_br_s5.astype(bf16), w_out=w_out.astype(bf16),
        norm_gain=norm_gain.reshape(depth, 1, D_MODEL),
        q_gain=q_norm_gain.reshape(depth, HEAD_DIM, 1),
        k_gain=k_norm_gain.reshape(depth, 1, HEAD_DIM))


def _block_diag(blocks):
    depth, sg, gg, r, c = blocks.shape
    eye = jnp.eye(gg, dtype=blocks.dtype)
    return jnp.einsum("lsgrc,gk->lsgrkc", blocks, eye).reshape(depth, sg, gg * r, gg * c)


def _s5_stacked_params(lam_re, lam_im, log_dt, b_re, b_im, c_re, c_im, d, w_glu_bf16):
    depth = lam_re.shape[0]
    a_re, a_im, bb_re, bb_im = _s5_params(lam_re, lam_im, log_dt,
                                          jnp.transpose(b_re, (0, 3, 1, 2)), jnp.transpose(b_im, (0, 3, 1, 2)))
    def bu_blocks(bb):
        return jnp.transpose(bb, (0, 2, 1, 3)).reshape(depth, S5_SG, S5_SG_GROUPS, S5_GROUP, S5_STATE)
    w_bu = jnp.concatenate([_block_diag(bu_blocks(bb_re)), _block_diag(bu_blocks(bb_im))], axis=3)
    def c_blocks(cm):
        return jnp.transpose(cm, (0, 1, 3, 2)).reshape(depth, S5_SG, S5_SG_GROUPS, S5_STATE, S5_GROUP)
    return dict(a_re=a_re.reshape(depth, S5_SG, S5_SG_STATE), a_im=a_im.reshape(depth, S5_SG, S5_SG_STATE),
                w_bu=w_bu.astype(bf16),
                w_c_re=_block_diag(c_blocks(c_re)).astype(bf16),
                w_c_im=_block_diag(c_blocks(c_im)).astype(bf16),
                d=d.reshape(depth, 1, S5_WIDTH), w_glu=w_glu_bf16)


def _projections(x, wts, tabs, layer):
    h = _rmsnorm(x, wts["norm_gain"], layer)
    qt = _proj_qt(h, wts["wt_q"], wts["q_gain"], tabs["cos_ht"], tabs["sin_ht"], layer)
    k, v, kb, vt = _proj_kv(h, wts["w_kv"], wts["wt_v"], wts["k_gain"], tabs["cos_h"], tabs["sin_h"], layer)
    qit, wt = _proj_qit(h, wts["wt_qi"], tabs["cos_t"], tabs["sin_t"], layer)
    ki, kib = _proj_ki(h, wts["w_ki"], tabs["cos_k"], tabs["sin_a"], tabs["sin_b"], layer)
    u = _proj_plain(h, wts["w_u"], f32, layer)
    zg = _proj_zg(h, wts["w_zg"], layer)
    return dict(qt=qt, k=k, v=v, kb=kb, vt=vt, qit=qit, wt=wt, ki=ki, kib=kib, u=u, zg=zg)


def _finish(x, a_in, s_in, zg, wts, layer):
    merged = _merge(a_in, s_in, wts["w_br_attn"], wts["w_br_s5"], zg, layer)
    return _out_proj(x, merged, wts["w_out"], layer)


def kernel(x_prompt, x_sample, cache_k, cache_v, cache_kidx, state_s5_re, state_s5_im, page_table, norm_gain, w_in, q_norm_gain, k_norm_gain, s5_lam_re, s5_lam_im, s5_log_dt, s5_b_re, s5_b_im, s5_c_re, s5_c_im, s5_d, w_glu, w_br_attn, w_br_s5, w_out):
    depth = w_in.shape[0]
    b_p, t_p = x_prompt.shape[:2]
    b_s, t_s = x_sample.shape[:2]
    tok = SAMPLE_ROWS
    assert t_s <= tok
    n_pages = page_table.shape[1]
    past = n_pages * PAGE_SIZE
    n_phys = cache_k.shape[1]

    tabs_p = _rope_tables(jnp.tile(jnp.arange(t_p, dtype=i32), b_p))
    tabs_s = _rope_tables(jnp.tile(past + jnp.arange(tok, dtype=i32), b_s))

    xp = x_prompt.reshape(b_p * t_p, D_MODEL)
    xs = jnp.pad(x_sample, ((0, 0), (0, tok - t_s), (0, 0))).reshape(b_s * tok, D_MODEL)

    cache_k4 = cache_k.reshape(depth, n_phys, PAGE_SIZE * N_KV_HEADS, HEAD_DIM)
    cache_v4 = cache_v.reshape(depth, n_phys, PAGE_SIZE * N_KV_HEADS, HEAD_DIM)
    cache_kidx_t = jnp.swapaxes(cache_kidx, 2, 3)
    zeros_state = jnp.zeros((b_p, S5_GROUPS, S5_STATE), f32)

    def new_page(a):
        w = a.shape[-1]
        return jnp.pad(a.reshape(b_s, tok, w), ((0, 0), (0, PAGE_SIZE - tok), (0, 0)))

    wts = _pack_weights(w_in, w_glu, w_br_attn, w_br_s5, w_out, norm_gain, q_norm_gain, k_norm_gain)
    s5p = _s5_stacked_params(s5_lam_re, s5_lam_im, s5_log_dt, s5_b_re, s5_b_im, s5_c_re, s5_c_im, s5_d,
                             wts["w_glu"])

    outs_p, outs_s = [], []
    for l in range(depth):
        pp = _projections(xp, wts, tabs_p, l)
        a_in = _attn_prompt(pp["qt"], pp["qit"], pp["wt"], pp["zg"], pp["kb"], pp["vt"], pp["kib"], b_p, t_p)
        s_in, sr_p, si_p = _s5(pp["u"], pp["zg"], zeros_state, zeros_state, s5p, l, b_p, t_p, t_p)
        xp = _finish(xp, a_in, s_in, pp["zg"], wts, l)
        outs_p.append((pp["k"].reshape(b_p, t_p, N_KV_HEADS, HEAD_DIM),
                       pp["v"].reshape(b_p, t_p, N_KV_HEADS, HEAD_DIM),
                       pp["ki"].reshape(b_p, t_p, IDX_DIM), sr_p, si_p))

        ps = _projections(xs, wts, tabs_s, l)
        qi_rows = jnp.transpose(ps["qit"].reshape(IDX_HEADS, IDX_DIM, b_s, tok), (2, 0, 3, 1)
                                ).reshape(b_s, IDX_HEADS * tok, IDX_DIM)
        w_rows = jnp.transpose(ps["wt"].reshape(IDX_HEADS, b_s, tok), (1, 0, 2)
                               ).reshape(b_s, IDX_HEADS * tok, 1)
        keys, thr = _idx_sample(page_table, qi_rows, w_rows, cache_kidx_t, l,
                                jnp.swapaxes(new_page(ps["ki"]), 1, 2), t_s)
        q_rows = jnp.transpose(ps["qt"].reshape(N_HEADS, HEAD_DIM, b_s, tok), (2, 0, 3, 1)
                               ).reshape(b_s, N_HEADS * tok, HEAD_DIM)
        k_s = ps["k"].reshape(b_s * tok, KV_WIDTH)
        v_s = ps["v"].reshape(b_s * tok, KV_WIDTH)
        o_rows = _attn_sample(page_table, q_rows, keys, thr, cache_k4, cache_v4, l,
                              new_page(k_s), new_page(v_s))
        attn_s = jnp.transpose(o_rows.reshape(b_s, N_HEADS, tok, HEAD_DIM), (0, 2, 1, 3)
                               ).reshape(b_s * tok, ATTN_WIDTH)
        a_in_s = _gate_mul(attn_s, ps["zg"][:, :ATTN_WIDTH])
        s_in_s, sr_s, si_s = _s5(ps["u"], ps["zg"], state_s5_re[l], state_s5_im[l], s5p, l, b_s, tok, t_s)
        xs = _finish(xs, a_in_s, s_in_s, ps["zg"], wts, l)
        real = lambda a: a.reshape((b_s, tok) + a.shape[1:])[:, :t_s]
        outs_s.append((real(k_s).reshape(b_s, t_s, N_KV_HEADS, HEAD_DIM),
                       real(v_s).reshape(b_s, t_s, N_KV_HEADS, HEAD_DIM),
                       real(ps["ki"]), sr_s, si_s))

    k_prompt, v_prompt, kidx_prompt, s5_re_prompt, s5_im_prompt = [jnp.stack(a) for a in zip(*outs_p)]
    k_sample, v_sample, kidx_sample, s5_re_sample, s5_im_sample = [jnp.stack(a) for a in zip(*outs_s)]
    y_prompt = xp.reshape(b_p, t_p, D_MODEL)
    y_sample = xs.reshape(b_s, tok, D_MODEL)[:, :t_s]
    return (y_prompt, y_sample, k_prompt, v_prompt, kidx_prompt, s5_re_prompt, s5_im_prompt,
            k_sample, v_sample, kidx_sample, s5_re_sample, s5_im_sample)
```

```python
import functools
import math

import jax
import jax.numpy as jnp
from jax import lax
from jax.experimental import pallas as pl
from jax.experimental.pallas import tpu as pltpu

D_MODEL = 2048
PAGE_SIZE = 128
N_HEADS = 16
HEAD_DIM = 128
N_KV_HEADS = 4
HEADS_PER_KV = N_HEADS // N_KV_HEADS
ATTN_WIDTH = N_HEADS * HEAD_DIM
KV_WIDTH = N_KV_HEADS * HEAD_DIM
IDX_HEADS = 16
IDX_DIM = 64
TOPK_MAX = 256
S5_WIDTH = D_MODEL // 2
S5_GROUP = 16
S5_GROUPS = S5_WIDTH // S5_GROUP
S5_STATE = 64
ROPE_THETA = 10000.0
EPS = 1e-6
IN_SIZES = (ATTN_WIDTH, KV_WIDTH, KV_WIDTH, IDX_HEADS * IDX_DIM, IDX_DIM, IDX_HEADS,
            ATTN_WIDTH, S5_WIDTH, S5_WIDTH, D_MODEL, D_MODEL)

LANES = 128
SUBLANES = 8
VMEM_LIMIT_BYTES = 56 * 1024 * 1024
MM_ROW_TILE = 1024
MM_COL_TILE = 1024

S5_SG = S5_WIDTH // LANES
S5_SG_GROUPS = S5_GROUPS // S5_SG
S5_SG_STATE = S5_SG_GROUPS * S5_STATE
S5_SCAN_UNROLL = 8

SAMPLE_ROWS = 16

ATTN_Q_TILE = 256
ATTN_ROW_BLOCK = 64
COUNT_ROWS = 32
PAGES_PER_STEP = 16
IDX_PAGES_PER_STEP = 32
LOG2_E = math.log2(math.e)

INT_MIN = -2 ** 31
MASK_BIAS = -1e30

f32 = jnp.float32
bf16 = jnp.bfloat16
i32 = jnp.int32


def _cparams(sem):
    return pltpu.CompilerParams(dimension_semantics=sem, vmem_limit_bytes=VMEM_LIMIT_BYTES)


def _row_tile(m, cap):
    return m if m <= cap else cap


def _norm_body(x_ref, g_ref, o_ref):
    x = x_ref[...]
    ms = jnp.mean(x * x, axis=-1, keepdims=True)
    o_ref[...] = (x * lax.rsqrt(ms + EPS) * g_ref[...]).astype(o_ref.dtype)


def _layer_spec(block, layer, index_map):
    return pl.BlockSpec((None,) + tuple(block), lambda *idx: (layer,) + tuple(index_map(*idx)))


def _rmsnorm(x, gains, layer):
    m, d = x.shape
    tm = _row_tile(m, 512)
    return pl.pallas_call(
        _norm_body,
        grid=(m // tm,),
        in_specs=[pl.BlockSpec((tm, d), lambda i: (i, 0)),
                  _layer_spec((1, d), layer, lambda i: (0, 0))],
        out_specs=pl.BlockSpec((tm, d), lambda i: (i, 0)),
        out_shape=jax.ShapeDtypeStruct((m, d), bf16),
        compiler_params=_cparams(("parallel",)),
        name="rmsnorm",
    )(x, gains)


def _head_norm_rope(x, gain, cos, sin):
    ms = jnp.mean(x * x, axis=-1, keepdims=True)
    y = x * lax.rsqrt(ms + EPS) * gain
    return y * cos + pltpu.roll(y, HEAD_DIM // 2, 1) * sin


def _qt_body(wt_ref, h_ref, g_ref, cos_ref, sin_ref, o_ref, *, heads):
    acc = lax.dot_general(wt_ref[...], h_ref[...], (((1,), (1,)), ((), ())),
                          preferred_element_type=f32)
    c = cos_ref[...]
    s = sin_ref[...]
    g = g_ref[...]
    half = HEAD_DIM // 2
    scale = HEAD_DIM ** -0.5 * LOG2_E
    for hh in range(heads):
        r0 = hh * HEAD_DIM
        x = acc[r0:r0 + HEAD_DIM]
        ms = jnp.mean(x * x, axis=0, keepdims=True)
        y = x * lax.rsqrt(ms + EPS) * g
        x1 = y[:half]
        x2 = y[half:]
        o_ref[r0:r0 + half, :] = ((x1 * c - x2 * s) * scale).astype(o_ref.dtype)
        o_ref[r0 + half:r0 + HEAD_DIM, :] = ((x2 * c + x1 * s) * scale).astype(o_ref.dtype)


def _proj_qt(h, wt_q, gain, cos_ht, sin_ht, layer):
    m, d = h.shape
    tm = _row_tile(m, MM_ROW_TILE)
    tn = 4 * HEAD_DIM
    return pl.pallas_call(
        functools.partial(_qt_body, heads=tn // HEAD_DIM),
        grid=(m // tm, ATTN_WIDTH // tn),
        in_specs=[_layer_spec((tn, d), layer, lambda i, j: (j, 0)),
                  pl.BlockSpec((tm, d), lambda i, j: (i, 0)),
                  _layer_spec((HEAD_DIM, 1), layer, lambda i, j: (0, 0)),
                  pl.BlockSpec((HEAD_DIM // 2, tm), lambda i, j: (0, i)),
                  pl.BlockSpec((HEAD_DIM // 2, tm), lambda i, j: (0, i))],
        out_specs=pl.BlockSpec((tn, tm), lambda i, j: (j, i)),
        out_shape=jax.ShapeDtypeStruct((ATTN_WIDTH, m), bf16),
        compiler_params=_cparams(("parallel", "parallel")),
        name="proj_q_t",
    )(wt_q, h, gain, cos_ht, sin_ht)


def _kv_body(h_ref, w_ref, wvt_ref, g_ref, cos_ref, sin_ref, k_ref, v_ref, kb_ref, vt_ref, *, chunk):
    h = h_ref[...]
    acc = jnp.dot(h, w_ref[...], preferred_element_type=f32)
    cos = cos_ref[...]
    sin = sin_ref[...]
    g = g_ref[...]
    tm = h.shape[0]
    for hh in range(N_KV_HEADS):
        sl = slice(hh * HEAD_DIM, (hh + 1) * HEAD_DIM)
        kh = _head_norm_rope(acc[:, sl], g, cos, sin)
        rows = pl.ds(hh, tm, stride=N_KV_HEADS)
        k_ref[rows, :] = kh
        v_ref[rows, :] = acc[:, KV_WIDTH + hh * HEAD_DIM:KV_WIDTH + (hh + 1) * HEAD_DIM]
        kb_ref[:, sl] = kh.astype(bf16)
    vt = lax.dot_general(wvt_ref[...], h, (((1,), (1,)), ((), ())), preferred_element_type=f32)
    for cc in range(vt_ref.shape[0]):
        vt_ref[cc] = vt[:, cc * chunk:(cc + 1) * chunk].astype(bf16)


def _proj_kv(h, w_kv, wt_v, gain, cos, sin, layer):
    m, d = h.shape
    tm = _row_tile(m, 512)
    chunk = min(tm, ATTN_Q_TILE)
    row = lambda i: (i, 0)
    fix = lambda i: (0, 0)
    return pl.pallas_call(
        functools.partial(_kv_body, chunk=chunk),
        grid=(m // tm,),
        in_specs=[pl.BlockSpec((tm, d), row),
                  _layer_spec((d, 2 * KV_WIDTH), layer, fix),
                  _layer_spec((KV_WIDTH, d), layer, fix),
                  _layer_spec((1, HEAD_DIM), layer, fix),
                  pl.BlockSpec((tm, HEAD_DIM), row),
                  pl.BlockSpec((tm, HEAD_DIM), row)],
        out_specs=[pl.BlockSpec((tm * N_KV_HEADS, HEAD_DIM), row),
                   pl.BlockSpec((tm * N_KV_HEADS, HEAD_DIM), row),
                   pl.BlockSpec((tm, KV_WIDTH), row),
                   pl.BlockSpec((tm // chunk, KV_WIDTH, chunk), lambda i: (i, 0, 0))],
        out_shape=[jax.ShapeDtypeStruct((m * N_KV_HEADS, HEAD_DIM), f32),
                   jax.ShapeDtypeStruct((m * N_KV_HEADS, HEAD_DIM), f32),
                   jax.ShapeDtypeStruct((m, KV_WIDTH), bf16),
                   jax.ShapeDtypeStruct((m // chunk, KV_WIDTH, chunk), bf16)],
        compiler_params=_cparams(("parallel",)),
        name="proj_kv",
    )(h, w_kv, wt_v, gain, cos, sin)


def _qit_body(wt_ref, h_ref, cos_ref, sin_ref, qit_ref, wt_out_ref):
    acc = lax.dot_general(wt_ref[...], h_ref[...], (((1,), (1,)), ((), ())),
                          preferred_element_type=f32)
    c = cos_ref[...]
    s = sin_ref[...]
    half = IDX_DIM // 2
    scale = IDX_DIM ** -0.5
    for hh in range(IDX_HEADS):
        r0 = hh * IDX_DIM
        x1 = acc[r0:r0 + half]
        x2 = acc[r0 + half:r0 + IDX_DIM]
        qit_ref[r0:r0 + half, :] = ((x1 * c - x2 * s) * scale).astype(qit_ref.dtype)
        qit_ref[r0 + half:r0 + IDX_DIM, :] = ((x2 * c + x1 * s) * scale).astype(qit_ref.dtype)
    n_qi = IDX_HEADS * IDX_DIM
    wt_out_ref[...] = acc[n_qi:n_qi + IDX_HEADS] * (IDX_HEADS ** -0.5)


def _proj_qit(h, wt_qi, cos_t, sin_t, layer):
    m, d = h.shape
    tm = _row_tile(m, 512)
    n_rows = wt_qi.shape[1]
    n_qi = IDX_HEADS * IDX_DIM
    return pl.pallas_call(
        _qit_body,
        grid=(m // tm,),
        in_specs=[_layer_spec((n_rows, d), layer, lambda i: (0, 0)),
                  pl.BlockSpec((tm, d), lambda i: (i, 0)),
                  pl.BlockSpec((IDX_DIM // 2, tm), lambda i: (0, i)),
                  pl.BlockSpec((IDX_DIM // 2, tm), lambda i: (0, i))],
        out_specs=[pl.BlockSpec((n_qi, tm), lambda i: (0, i)),
                   pl.BlockSpec((IDX_HEADS, tm), lambda i: (0, i))],
        out_shape=[jax.ShapeDtypeStruct((n_qi, m), bf16),
                   jax.ShapeDtypeStruct((IDX_HEADS, m), f32)],
        compiler_params=_cparams(("parallel",)),
        name="proj_qi_t",
    )(wt_qi, h, cos_t, sin_t)


def _ki_body(h_ref, w_ref, cos_ref, sina_ref, sinb_ref, ki_ref, kib_ref):
    x = jnp.dot(h_ref[...], w_ref[...], preferred_element_type=f32)
    half = IDX_DIM // 2
    r = (x * cos_ref[...] + pltpu.roll(x, LANES - half, 1) * sina_ref[...]
         + pltpu.roll(x, half, 1) * sinb_ref[...])
    ki = r[:, :IDX_DIM]
    ki_ref[...] = ki
    kib_ref[...] = ki.astype(bf16)


def _proj_ki(h, w_ki, cos_k, sin_a, sin_b, layer):
    m, d = h.shape
    tm = _row_tile(m, 512)
    row = lambda i: (i, 0)
    return pl.pallas_call(
        _ki_body,
        grid=(m // tm,),
        in_specs=[pl.BlockSpec((tm, d), row),
                  _layer_spec((d, LANES), layer, lambda i: (0, 0)),
                  pl.BlockSpec((tm, LANES), row),
                  pl.BlockSpec((tm, LANES), row),
                  pl.BlockSpec((tm, LANES), row)],
        out_specs=[pl.BlockSpec((tm, IDX_DIM), row)] * 2,
        out_shape=[jax.ShapeDtypeStruct((m, IDX_DIM), f32),
                   jax.ShapeDtypeStruct((m, IDX_DIM), bf16)],
        compiler_params=_cparams(("parallel",)),
        name="proj_ki",
    )(h, w_ki, cos_k, sin_a, sin_b)


def _plain_body(h_ref, w_ref, o_ref):
    o_ref[...] = jnp.dot(h_ref[...], w_ref[...], preferred_element_type=f32).astype(o_ref.dtype)


def _proj_plain(h, w, out_dtype, layer):
    m, d = h.shape
    n = w.shape[2]
    tm = _row_tile(m, MM_ROW_TILE)
    tn = MM_COL_TILE
    return pl.pallas_call(
        _plain_body,
        grid=(m // tm, n // tn),
        in_specs=[pl.BlockSpec((tm, d), lambda i, j: (i, 0)),
                  _layer_spec((d, tn), layer, lambda i, j: (0, j))],
        out_specs=pl.BlockSpec((tm, tn), lambda i, j: (i, j)),
        out_shape=jax.ShapeDtypeStruct((m, n), out_dtype),
        compiler_params=_cparams(("parallel", "parallel")),
        name="proj_u",
    )(h, w)


def _sigmoid(x):
    return 1.0 / (1.0 + jnp.exp(-x))


def _zg_body(h_ref, w_ref, o_ref, *, silu_tiles):
    acc = jnp.dot(h_ref[...], w_ref[...], preferred_element_type=f32)
    sg = _sigmoid(acc)
    is_silu = pl.program_id(1) < silu_tiles
    o_ref[...] = jnp.where(is_silu, acc * sg, sg).astype(o_ref.dtype)


def _proj_zg(h, w_zg, layer):
    m, d = h.shape
    n = w_zg.shape[2]
    tm = _row_tile(m, MM_ROW_TILE)
    tn = MM_COL_TILE
    return pl.pallas_call(
        functools.partial(_zg_body, silu_tiles=(ATTN_WIDTH + S5_WIDTH) // tn),
        grid=(m // tm, n // tn),
        in_specs=[pl.BlockSpec((tm, d), lambda i, j: (i, 0)),
                  _layer_spec((d, tn), layer, lambda i, j: (0, j))],
        out_specs=pl.BlockSpec((tm, tn), lambda i, j: (i, j)),
        out_shape=jax.ShapeDtypeStruct((m, n), bf16),
        compiler_params=_cparams(("parallel", "parallel")),
        name="proj_zg",
    )(h, w_zg)


def _sortable_key(score):
    b = pltpu.bitcast(score, i32)
    return b ^ ((b >> 31) & jnp.int32(0x7FFFFFFF))


def _kth_largest(count_ge, shape, k):
    zero = jnp.zeros(shape, i32)
    prefix = jnp.where(count_ge(zero) >= k, zero, jnp.full(shape, INT_MIN, i32))

    def bit_body(bi, prefix):
        cand = prefix | (jnp.int32(1) << (30 - bi))
        return jnp.where(count_ge(cand) >= k, cand, prefix)

    thr = lax.fori_loop(0, 31, bit_body, prefix)
    return jnp.maximum(thr, jnp.int32(INT_MIN + 1))


def _resolve_ties(thr, k, count_ge, count_tied_before, demote, n_pos_bits, active=None):
    excess = count_ge(thr) - k
    if active is not None:
        excess = jnp.where(active, excess, 0)

    @pl.when(jnp.max(excess) > 0)
    def _():
        need = k - count_ge(thr + 1)

        def bit_body(bi, pos):
            cand = pos | (jnp.int32(1) << (n_pos_bits - 1 - bi))
            return jnp.where(count_tied_before(thr, cand) < need, cand, pos)

        last = lax.fori_loop(0, n_pos_bits, bit_body, jnp.zeros(thr.shape, i32))
        demote(thr, last)


def _sublane_allreduce(x, op):
    for shift in (4, 2, 1):
        x = op(x, pltpu.roll(x, shift, 0))
    return x


def _attn_prompt_body(qt_ref, qit_ref, wt_ref, z_ref, k_ref, vt_ref, ki_ref, o_ref,
                      key_scr, acc_scr, m_scr, l_scr, alpha_scr, s_scr, p_scr, bias_scr,
                      *, tq, k_top, n_pos_bits):
    i = pl.program_id(1)
    n_chunks = i + 1
    half = tq // 2

    def score_keys(c, diagonal):
        for sub in range(2):
            r0 = pl.multiple_of(c * tq + sub * half, half)
            kic = ki_ref[pl.ds(r0, half), :]
            acc = jnp.zeros((half, tq), f32)
            for hh in range(IDX_HEADS):
                lg = jnp.dot(kic, qit_ref[hh * IDX_DIM:(hh + 1) * IDX_DIM, :],
                             preferred_element_type=f32)
                acc = acc + jnp.maximum(lg, 0.0) * wt_ref[hh:hh + 1, :]
            key = _sortable_key(acc)
            if diagonal:
                kpos = lax.broadcasted_iota(i32, (half, tq), 0) + sub * half
                qpos = lax.broadcasted_iota(i32, (half, tq), 1)
                key = jnp.where(kpos <= qpos, key, jnp.int32(INT_MIN))
            key_scr[pl.ds(r0, half), :] = key

    def full_chunk(c, carry):
        score_keys(c, False)
        return carry

    lax.fori_loop(0, i, full_chunk, 0)
    score_keys(i, True)

    @pl.when(n_chunks % 2 == 1)
    def _():
        key_scr[pl.ds(pl.multiple_of(n_chunks * tq, tq), tq), :] = jnp.full((tq, tq), INT_MIN, i32)

    def count_ge(cand):
        def body(c, cnt):
            kk = key_scr[pl.ds(pl.multiple_of(c * 2 * tq, 2 * tq), 2 * tq), :]
            hit = (kk >= cand).astype(i32)
            return cnt + jnp.sum(hit.reshape(2 * tq // COUNT_ROWS, COUNT_ROWS, tq), axis=0)

        cnt = lax.fori_loop(0, (n_chunks + 1) // 2, body, jnp.zeros((COUNT_ROWS, tq), i32))
        return jnp.sum(cnt, axis=0, keepdims=True)

    thr = _kth_largest(count_ge, (1, tq), k_top)

    def chunk_positions(c):
        return lax.broadcasted_iota(i32, (tq, tq), 0) + c * tq

    def count_tied_before(thr, pos_limit):
        def body(c, cnt):
            kk = key_scr[pl.ds(pl.multiple_of(c * tq, tq), tq), :]
            hit = ((kk == thr) & (chunk_positions(c) < pos_limit)).astype(i32)
            return cnt + jnp.sum(hit.reshape(tq // SUBLANES, SUBLANES, tq), axis=0)

        cnt = lax.fori_loop(0, n_chunks, body, jnp.zeros((SUBLANES, tq), i32))
        return jnp.sum(cnt, axis=0, keepdims=True)

    def demote(thr, last):
        def body(c, carry):
            rows = pl.ds(pl.multiple_of(c * tq, tq), tq)
            kk = key_scr[rows, :]
            key_scr[rows, :] = jnp.where((kk == thr) & (chunk_positions(c) > last), thr - 1, kk)
            return carry

        lax.fori_loop(0, n_chunks, body, 0)

    _resolve_ties(thr, k_top, count_ge, count_tied_before, demote, n_pos_bits)

    m_scr[...] = jnp.full(m_scr.shape, -jnp.inf, f32)
    l_scr[...] = jnp.zeros(l_scr.shape, f32)
    acc_scr[...] = jnp.zeros(acc_scr.shape, f32)
    n_sub = tq // SUBLANES
    d_sub = HEAD_DIM // SUBLANES

    def attend_chunk(c, carry):
        r0 = pl.multiple_of(c * tq, tq)
        bias_scr[...] = jnp.where(key_scr[pl.ds(r0, tq), :] >= thr, 0.0, MASK_BIAS)
        for g in range(N_KV_HEADS):
            kc = k_ref[pl.ds(r0, tq), g * HEAD_DIM:(g + 1) * HEAD_DIM]
            for hh in range(HEADS_PER_KV):
                h = g * HEADS_PER_KV + hh
                s = jnp.dot(kc, qt_ref[h * HEAD_DIM:(h + 1) * HEAD_DIM, :], preferred_element_type=f32)
                s_scr[h] = s + bias_scr[...]
        blocks = [slice(j * ATTN_ROW_BLOCK, (j + 1) * ATTN_ROW_BLOCK) for j in range(tq // ATTN_ROW_BLOCK)]
        blk_sub = ATTN_ROW_BLOCK // SUBLANES
        for h in range(N_HEADS):
            mx = m_scr[h]
            for rows in blocks:
                mx = jnp.maximum(mx, jnp.max(s_scr[h, rows, :].reshape(blk_sub, SUBLANES, tq), axis=0))
            m_new = _sublane_allreduce(mx, jnp.maximum)
            alpha_scr[h] = jnp.exp2(m_scr[h] - m_new)
            m_scr[h] = m_new
        for h in range(N_HEADS):
            m_new = m_scr[h]
            psum = jnp.zeros((SUBLANES, tq), f32)
            for rows in blocks:
                p3 = jnp.exp2(s_scr[h, rows, :].reshape(blk_sub, SUBLANES, tq) - m_new[None])
                psum = psum + jnp.sum(p3, axis=0)
                p_scr[h, rows, :] = p3.reshape(ATTN_ROW_BLOCK, tq).astype(bf16)
            l_scr[h] = alpha_scr[h] * l_scr[h] + _sublane_allreduce(psum, jnp.add)
        for g in range(N_KV_HEADS):
            vtc = vt_ref[c, g * HEAD_DIM:(g + 1) * HEAD_DIM, :]
            for hh in range(HEADS_PER_KV):
                h = g * HEADS_PER_KV + hh
                hs = slice(h * HEAD_DIM, (h + 1) * HEAD_DIM)
                pv = jnp.dot(vtc, p_scr[h], preferred_element_type=f32)
                acc = acc_scr[hs, :].reshape(d_sub, SUBLANES, tq) * alpha_scr[h][None]
                acc_scr[hs, :] = acc.reshape(HEAD_DIM, tq) + pv
        return carry

    lax.fori_loop(0, n_chunks, attend_chunk, 0)

    for h in range(N_HEADS):
        hs = slice(h * HEAD_DIM, (h + 1) * HEAD_DIM)
        out_t = (acc_scr[hs, :].reshape(d_sub, SUBLANES, tq) / l_scr[h][None]).reshape(HEAD_DIM, tq)
        o_ref[:, hs] = (out_t.T * z_ref[:, hs].astype(f32)).astype(o_ref.dtype)


def _attn_prompt(qt, qit, wt, zg, kb, vt, kib, batch, seq):
    tq = ATTN_Q_TILE
    assert seq % tq == 0
    nq = seq // tq
    k_top = min(TOPK_MAX, seq // 4)
    assert k_top <= tq
    qrow = lambda b, i: (b * nq + i, 0)
    qcol = lambda b, i: (0, b * nq + i)
    per_b = lambda b, i: (b, 0)
    return pl.pallas_call(
        functools.partial(_attn_prompt_body, tq=tq, k_top=k_top, n_pos_bits=(seq - 1).bit_length()),
        grid=(batch, nq),
        in_specs=[pl.BlockSpec((ATTN_WIDTH, tq), qcol),
                  pl.BlockSpec((IDX_HEADS * IDX_DIM, tq), qcol),
                  pl.BlockSpec((IDX_HEADS, tq), qcol),
                  pl.BlockSpec((tq, ATTN_WIDTH), qrow),
                  pl.BlockSpec((seq, KV_WIDTH), per_b),
                  pl.BlockSpec((nq, KV_WIDTH, tq), lambda b, i: (b, 0, 0)),
                  pl.BlockSpec((seq, IDX_DIM), per_b)],
        out_specs=pl.BlockSpec((tq, ATTN_WIDTH), qrow),
        out_shape=jax.ShapeDtypeStruct((batch * seq, ATTN_WIDTH), bf16),
        scratch_shapes=[pltpu.VMEM((seq + tq, tq), i32),
                        pltpu.VMEM((ATTN_WIDTH, tq), f32),
                        pltpu.VMEM((N_HEADS, SUBLANES, tq), f32),
                        pltpu.VMEM((N_HEADS, SUBLANES, tq), f32),
                        pltpu.VMEM((N_HEADS, SUBLANES, tq), f32),
                        pltpu.VMEM((N_HEADS, tq, tq), f32),
                        pltpu.VMEM((N_HEADS, tq, tq), bf16),
                        pltpu.VMEM((tq, tq), f32)],
        compiler_params=_cparams(("parallel", "arbitrary")),
        name="attn_prompt",
    )(qt, qit, wt, zg, kb, vt, kib)


def _idx_sample_body(pt_ref, qi_ref, w_ref, *rest, n_pages, n_new, k_top, ppb):
    kid_refs = rest[:ppb]
    kinew_ref, key_ref, thr_ref = rest[ppb:]
    p = pl.program_id(1)
    tok = SAMPLE_ROWS

    def page_keys(ki_page_t):
        lg = jnp.dot(qi_ref[0], ki_page_t.astype(bf16), preferred_element_type=f32)
        wgt = jnp.maximum(lg, 0.0) * w_ref[0]
        score = jnp.sum(wgt.reshape(IDX_HEADS, tok, PAGE_SIZE), axis=0)
        return _sortable_key(score)

    for j in range(ppb):
        key_ref[0, p * ppb + j] = page_keys(kid_refs[j][...])

    @pl.when(p == pl.num_programs(1) - 1)
    def _():
        key = page_keys(kinew_ref[0])
        kpos = lax.broadcasted_iota(i32, (tok, PAGE_SIZE), 1)
        qtok = lax.broadcasted_iota(i32, (tok, PAGE_SIZE), 0)
        key_ref[0, n_pages] = jnp.where((kpos <= qtok) & (kpos < n_new), key, jnp.int32(INT_MIN))

        def count_ge(cand):
            def body(c, cnt):
                hit = (key_ref[0, pl.ds(c * ppb, ppb)] >= cand).astype(i32)
                return cnt + jnp.sum(hit, axis=0)

            cnt = lax.fori_loop(0, n_pages // ppb, body, jnp.zeros((tok, PAGE_SIZE), i32))
            cnt = cnt + (key_ref[0, n_pages] >= cand).astype(i32)
            return jnp.sum(cnt, axis=-1, keepdims=True)

        thr = _kth_largest(count_ge, (tok, 1), k_top)
        thr_ref[0] = jnp.broadcast_to(thr, (tok, PAGE_SIZE))

        def page_positions(c):
            return lax.broadcasted_iota(i32, (tok, PAGE_SIZE), 1) + c * PAGE_SIZE

        def count_tied_before(thr, pos_limit):
            def body(c, cnt):
                return cnt + ((key_ref[0, c] == thr) & (page_positions(c) < pos_limit)).astype(i32)

            cnt = lax.fori_loop(0, n_pages + 1, body, jnp.zeros((tok, PAGE_SIZE), i32))
            return jnp.sum(cnt, axis=-1, keepdims=True)

        def demote(thr, last):
            def body(c, carry):
                kk = key_ref[0, c]
                key_ref[0, c] = jnp.where((kk == thr) & (page_positions(c) > last), thr - 1, kk)
                return carry

            lax.fori_loop(0, n_pages + 1, body, 0)

        real_rows = lax.broadcasted_iota(i32, (tok, 1), 0) < n_new
        _resolve_ties(thr, k_top, count_ge, count_tied_before, demote,
                      ((n_pages + 1) * PAGE_SIZE - 1).bit_length(), active=real_rows)


def _pages_per_step(n_pages, cap=PAGES_PER_STEP):
    ppb = min(cap, n_pages)
    assert n_pages % ppb == 0
    return ppb


def _page_spec(block, layer, ppb, j):
    return pl.BlockSpec(block, lambda b, p, pt: (layer, pt[b, p * ppb + j], 0, 0))


def _idx_sample(page_table, qi_rows, w_rows, cache_kidx, layer, ki_new_pages, n_new):
    nb, n_pages = page_table.shape
    tok = SAMPLE_ROWS
    ppb = _pages_per_step(n_pages, IDX_PAGES_PER_STEP)
    k_top = min(TOPK_MAX, (n_pages * PAGE_SIZE + n_new) // 4)
    per_b = lambda b, p, pt: (b, 0, 0)
    grid_spec = pltpu.PrefetchScalarGridSpec(
        num_scalar_prefetch=1,
        grid=(nb, n_pages // ppb),
        in_specs=[pl.BlockSpec((1, IDX_HEADS * tok, IDX_DIM), per_b),
                  pl.BlockSpec((1, IDX_HEADS * tok, 1), per_b)]
                 + [_page_spec((None, None, IDX_DIM, PAGE_SIZE), layer, ppb, j) for j in range(ppb)]
                 + [pl.BlockSpec((1, IDX_DIM, PAGE_SIZE), per_b)],
        out_specs=[pl.BlockSpec((1, n_pages + 1, tok, PAGE_SIZE), lambda b, p, pt: (b, 0, 0, 0)),
                   pl.BlockSpec((1, tok, PAGE_SIZE), per_b)],
    )
    return pl.pallas_call(
        functools.partial(_idx_sample_body, n_pages=n_pages, n_new=n_new, k_top=k_top, ppb=ppb),
        grid_spec=grid_spec,
        out_shape=[jax.ShapeDtypeStruct((nb, n_pages + 1, tok, PAGE_SIZE), i32),
                   jax.ShapeDtypeStruct((nb, tok, PAGE_SIZE), i32)],
        compiler_params=_cparams(("parallel", "arbitrary")),
        name="idx_sample",
    )(page_table, qi_rows, w_rows, *([cache_kidx] * ppb), ki_new_pages)


def _attn_sample_body(pt_ref, q_ref, key_ref, keyn_ref, thr_ref, *rest, ppb):
    k_refs = rest[:ppb]
    v_refs = rest[ppb:2 * ppb]
    kn_ref, vn_ref, o_ref, acc_scr, m_scr, l_scr = rest[2 * ppb:]
    p = pl.program_id(1)
    tok = SAMPLE_ROWS
    rows_g = HEADS_PER_KV * tok
    thr = thr_ref[0]

    @pl.when(p == 0)
    def _():
        m_scr[...] = jnp.full(m_scr.shape, -jnp.inf, f32)
        l_scr[...] = jnp.zeros(l_scr.shape, f32)
        acc_scr[...] = jnp.zeros(acc_scr.shape, f32)

    def attend(bias_tok, k_of_group, v_of_group):
        bias = jnp.concatenate([bias_tok] * HEADS_PER_KV, axis=0)
        for g in range(N_KV_HEADS):
            rs = slice(g * rows_g, (g + 1) * rows_g)
            s = lax.dot_general(q_ref[0, rs, :], k_of_group(g), (((1,), (1,)), ((), ())),
                                preferred_element_type=f32) + bias
            m_prev = m_scr[rs, :]
            m_new = jnp.maximum(m_prev, jnp.max(s, axis=-1, keepdims=True))
            alpha = jnp.exp2(m_prev - m_new)
            pr = jnp.exp2(s - m_new)
            l_scr[rs, :] = alpha * l_scr[rs, :] + jnp.sum(pr, axis=-1, keepdims=True)
            acc_scr[rs, :] = alpha * acc_scr[rs, :] + jnp.dot(
                pr.astype(bf16), v_of_group(g), preferred_element_type=f32)
            m_scr[rs, :] = m_new

    def mask_bias(keys):
        return jnp.where(keys >= thr, 0.0, MASK_BIAS).astype(f32)

    def cached(refs):
        def of_group(g):
            rows = pl.ds(g, PAGE_SIZE, stride=N_KV_HEADS)
            return jnp.concatenate([r[rows, :].astype(bf16) for r in refs], axis=0)
        return of_group

    attend(jnp.concatenate([mask_bias(key_ref[0, j]) for j in range(ppb)], axis=1),
           cached(k_refs), cached(v_refs))

    @pl.when(p == pl.num_programs(1) - 1)
    def _():
        def fresh(ref):
            return lambda g: ref[0, :, g * HEAD_DIM:(g + 1) * HEAD_DIM].astype(bf16)

        attend(mask_bias(keyn_ref[0, 0]), fresh(kn_ref), fresh(vn_ref))
        o_ref[0] = acc_scr[...] / l_scr[...]


def _attn_sample(page_table, q_rows, keys, thr, cache_k, cache_v, layer, k_new_pages, v_new_pages):
    nb, n_pages = page_table.shape
    tok = SAMPLE_ROWS
    rows = N_HEADS * tok
    ppb = _pages_per_step(n_pages)
    per_b = lambda b, p, pt: (b, 0, 0)
    page_block = (None, None, PAGE_SIZE * N_KV_HEADS, HEAD_DIM)
    grid_spec = pltpu.PrefetchScalarGridSpec(
        num_scalar_prefetch=1,
        grid=(nb, n_pages // ppb),
        in_specs=[pl.BlockSpec((1, rows, HEAD_DIM), per_b),
                  pl.BlockSpec((1, ppb, tok, PAGE_SIZE), lambda b, p, pt: (b, p, 0, 0)),
                  pl.BlockSpec((1, 1, tok, PAGE_SIZE), lambda b, p, pt: (b, n_pages, 0, 0)),
                  pl.BlockSpec((1, tok, PAGE_SIZE), per_b)]
                 + [_page_spec(page_block, layer, ppb, j) for j in range(ppb)]
                 + [_page_spec(page_block, layer, ppb, j) for j in range(ppb)]
                 + [pl.BlockSpec((1, PAGE_SIZE, KV_WIDTH), per_b),
                    pl.BlockSpec((1, PAGE_SIZE, KV_WIDTH), per_b)],
        out_specs=pl.BlockSpec((1, rows, HEAD_DIM), per_b),
        scratch_shapes=[pltpu.VMEM((rows, HEAD_DIM), f32),
                        pltpu.VMEM((rows, 1), f32),
                        pltpu.VMEM((rows, 1), f32)],
    )
    return pl.pallas_call(
        functools.partial(_attn_sample_body, ppb=ppb),
        grid_spec=grid_spec,
        out_shape=jax.ShapeDtypeStruct((nb, rows, HEAD_DIM), f32),
        compiler_params=_cparams(("parallel", "arbitrary")),
        name="attn_sample",
    )(page_table, q_rows, keys, keys, thr, *([cache_k] * ppb), *([cache_v] * ppb),
      k_new_pages, v_new_pages)


def _gate_mul_body(a_ref, z_ref, o_ref):
    o_ref[...] = (a_ref[...] * z_ref[...].astype(f32)).astype(o_ref.dtype)


def _gate_mul(a, zg):
    m, n = a.shape
    return pl.pallas_call(
        _gate_mul_body,
        grid=(1,),
        in_specs=[pl.BlockSpec((m, n), lambda i: (0, 0)),
                  pl.BlockSpec((m, n), lambda i: (0, 0))],
        out_specs=pl.BlockSpec((m, n), lambda i: (0, 0)),
        out_shape=jax.ShapeDtypeStruct((m, n), bf16),
        compiler_params=_cparams(("arbitrary",)),
        name="gate_mul",
    )(a, zg)


def _s5_param_body(lr_ref, li_ref, ldt_ref, bre_ref, bim_ref, ar_ref, ai_ref, bbre_ref, bbim_ref):
    lr = lr_ref[...]
    li = li_ref[...]
    dt = jnp.exp(ldt_ref[...])
    mag = jnp.exp(lr * dt)
    ar = mag * jnp.cos(li * dt)
    ai = mag * jnp.sin(li * dt)
    den = lr * lr + li * li
    nr = ar - 1.0
    cr = (nr * lr + ai * li) / den
    ci = (ai * lr - nr * li) / den
    ar_ref[...] = ar
    ai_ref[...] = ai
    for h in range(S5_GROUP):
        bbre_ref[h] = cr * bre_ref[h] - ci * bim_ref[h]
        bbim_ref[h] = cr * bim_ref[h] + ci * bre_ref[h]


def _s5_params(lam_re, lam_im, log_dt, b_re, b_im):
    depth = lam_re.shape[0]
    gs = (None, S5_GROUPS, S5_STATE)
    bs = (None, S5_GROUP, S5_GROUPS, S5_STATE)
    at3 = lambda l: (l, 0, 0)
    at4 = lambda l: (l, 0, 0, 0)
    return pl.pallas_call(
        _s5_param_body,
        grid=(depth,),
        in_specs=[pl.BlockSpec(gs, at3), pl.BlockSpec(gs, at3),
                  pl.BlockSpec((None, S5_GROUPS, 1), at3),
                  pl.BlockSpec(bs, at4), pl.BlockSpec(bs, at4)],
        out_specs=[pl.BlockSpec(gs, at3), pl.BlockSpec(gs, at3),
                   pl.BlockSpec(bs, at4), pl.BlockSpec(bs, at4)],
        out_shape=[jax.ShapeDtypeStruct((depth, S5_GROUPS, S5_STATE), f32),
                   jax.ShapeDtypeStruct((depth, S5_GROUPS, S5_STATE), f32),
                   jax.ShapeDtypeStruct((depth, S5_GROUP, S5_GROUPS, S5_STATE), f32),
                   jax.ShapeDtypeStruct((depth, S5_GROUP, S5_GROUPS, S5_STATE), f32)],
        compiler_params=_cparams(("arbitrary",)),
        name="s5_params",
    )(lam_re, lam_im, log_dt.reshape(depth, S5_GROUPS, 1), b_re, b_im)


def _gelu_tanh(x):
    return 0.5 * x * (1.0 + jnp.tanh(math.sqrt(2.0 / math.pi) * (x + 0.044715 * (x * x * x))))


def _s5_body(u_ref, zs_ref, h0r_ref, h0i_ref, ar_ref, ai_ref, wbu_ref, wcr_ref, wci_ref, d_ref, wglu_ref,
             o_ref, sr_ref, si_ref, xr_scr, xi_scr, y_scr, str_scr, sti_scr, *, tc, n_last):
    c = pl.program_id(1)

    @pl.when(c == 0)
    def _():
        str_scr[...] = h0r_ref[0]
        sti_scr[...] = h0i_ref[0]

    n_ct = S5_SG_STATE // LANES

    def gather_planes(scr, rows):
        return jnp.concatenate([scr[j, rows, :] for j in range(n_ct)], axis=1)

    def scatter_planes(scr, rows, val):
        for j in range(n_ct):
            scr[j, rows, :] = val[:, j * LANES:(j + 1) * LANES]

    for sg in range(S5_SG):
        rows = pl.ds(sg, tc, stride=S5_SG)
        u_sg = u_ref[:, sg * LANES:(sg + 1) * LANES]
        bu = jnp.dot(u_sg.astype(bf16), wbu_ref[sg], preferred_element_type=f32)
        scatter_planes(xr_scr, rows, bu[:, :S5_SG_STATE])
        scatter_planes(xi_scr, rows, bu[:, S5_SG_STATE:])

    ar = ar_ref[...]
    ai = ai_ref[...]

    def step(t, carry):
        xr, xi = carry
        r = pl.ds(pl.multiple_of(t * S5_SG, S5_SG), S5_SG)
        nr = ar * xr - ai * xi + gather_planes(xr_scr, r)
        ni = ar * xi + ai * xr + gather_planes(xi_scr, r)
        scatter_planes(xr_scr, r, nr)
        scatter_planes(xi_scr, r, ni)
        return nr, ni

    xr, xi = lax.fori_loop(0, tc, step, (str_scr[...], sti_scr[...]), unroll=S5_SCAN_UNROLL)
    str_scr[...] = xr
    sti_scr[...] = xi

    for sg in range(S5_SG):
        rows = pl.ds(sg, tc, stride=S5_SG)
        cs = slice(sg * LANES, (sg + 1) * LANES)
        y = (jnp.dot(gather_planes(xr_scr, rows).astype(bf16), wcr_ref[sg], preferred_element_type=f32)
             - jnp.dot(gather_planes(xi_scr, rows).astype(bf16), wci_ref[sg], preferred_element_type=f32)
             + d_ref[:, cs] * u_ref[:, cs])
        y_scr[:, cs] = _gelu_tanh(y)

    y = y_scr[...]
    gate = _sigmoid(jnp.dot(y.astype(bf16), wglu_ref[...], preferred_element_type=f32))
    o_ref[...] = (y * gate * zs_ref[...].astype(f32)).astype(o_ref.dtype)

    @pl.when(c == pl.num_programs(1) - 1)
    def _():
        last = slice((n_last - 1) * S5_SG, n_last * S5_SG)
        sr_ref[0] = gather_planes(xr_scr, last)
        si_ref[0] = gather_planes(xi_scr, last)


def _s5(u, zg, h0_re, h0_im, prm, layer, nb, seq, n_real):
    tc = min(seq, 256)
    nch = seq // tc
    n_last = n_real - (nch - 1) * tc
    zs_col = ATTN_WIDTH // S5_WIDTH
    fix2 = lambda b, c: (0, 0)
    fix3 = lambda b, c: (0, 0, 0)
    st = lambda b, c: (b, 0, 0)
    out, s_re, s_im = pl.pallas_call(
        functools.partial(_s5_body, tc=tc, n_last=n_last),
        grid=(nb, nch),
        in_specs=[pl.BlockSpec((tc, S5_WIDTH), lambda b, c: (b * nch + c, 0)),
                  pl.BlockSpec((tc, S5_WIDTH), lambda b, c: (b * nch + c, zs_col)),
                  pl.BlockSpec((1, S5_SG, S5_SG_STATE), st),
                  pl.BlockSpec((1, S5_SG, S5_SG_STATE), st),
                  _layer_spec((S5_SG, S5_SG_STATE), layer, fix2),
                  _layer_spec((S5_SG, S5_SG_STATE), layer, fix2),
                  _layer_spec((S5_SG, LANES, 2 * S5_SG_STATE), layer, fix3),
                  _layer_spec((S5_SG, S5_SG_STATE, LANES), layer, fix3),
                  _layer_spec((S5_SG, S5_SG_STATE, LANES), layer, fix3),
                  _layer_spec((1, S5_WIDTH), layer, fix2),
                  _layer_spec((S5_WIDTH, S5_WIDTH), layer, fix2)],
        out_specs=[pl.BlockSpec((tc, S5_WIDTH), lambda b, c: (b * nch + c, 0)),
                   pl.BlockSpec((1, S5_SG, S5_SG_STATE), st),
                   pl.BlockSpec((1, S5_SG, S5_SG_STATE), st)],
        out_shape=[jax.ShapeDtypeStruct((nb * seq, S5_WIDTH), bf16),
                   jax.ShapeDtypeStruct((nb, S5_SG, S5_SG_STATE), f32),
                   jax.ShapeDtypeStruct((nb, S5_SG, S5_SG_STATE), f32)],
        scratch_shapes=[pltpu.VMEM((S5_SG_STATE // LANES, tc * S5_SG, LANES), f32),
                        pltpu.VMEM((S5_SG_STATE // LANES, tc * S5_SG, LANES), f32),
                        pltpu.VMEM((tc, S5_WIDTH), f32),
                        pltpu.VMEM((S5_SG, S5_SG_STATE), f32),
                        pltpu.VMEM((S5_SG, S5_SG_STATE), f32)],
        compiler_params=_cparams(("parallel", "arbitrary")),
        name="s5_scan",
    )(u, zg, h0_re.reshape(nb, S5_SG, S5_SG_STATE), h0_im.reshape(nb, S5_SG, S5_SG_STATE),
      prm["a_re"], prm["a_im"], prm["w_bu"], prm["w_c_re"], prm["w_c_im"], prm["d"], prm["w_glu"])
    return out, s_re.reshape(nb, S5_GROUPS, S5_STATE), s_im.reshape(nb, S5_GROUPS, S5_STATE)


def _merge_body(a_ref, s_ref, wa_ref, ws_ref, ga_ref, gs_ref, o_ref):
    o_a = jnp.dot(a_ref[...], wa_ref[...], preferred_element_type=f32)
    o_s = jnp.dot(s_ref[...], ws_ref[...], preferred_element_type=f32)
    o_ref[...] = (ga_ref[...].astype(f32) * o_a + gs_ref[...].astype(f32) * o_s).astype(o_ref.dtype)


def _merge(a_in, s_in, w_a, w_s, zg, layer):
    m = a_in.shape[0]
    tm = _row_tile(m, MM_ROW_TILE)
    tn = MM_COL_TILE
    ga0 = (ATTN_WIDTH + S5_WIDTH) // tn
    gs0 = (ATTN_WIDTH + S5_WIDTH + D_MODEL) // tn
    return pl.pallas_call(
        _merge_body,
        grid=(m // tm, D_MODEL // tn),
        in_specs=[pl.BlockSpec((tm, ATTN_WIDTH), lambda i, j: (i, 0)),
                  pl.BlockSpec((tm, S5_WIDTH), lambda i, j: (i, 0)),
                  _layer_spec((ATTN_WIDTH, tn), layer, lambda i, j: (0, j)),
                  _layer_spec((S5_WIDTH, tn), layer, lambda i, j: (0, j)),
                  pl.BlockSpec((tm, tn), lambda i, j: (i, ga0 + j)),
                  pl.BlockSpec((tm, tn), lambda i, j: (i, gs0 + j))],
        out_specs=pl.BlockSpec((tm, tn), lambda i, j: (i, j)),
        out_shape=jax.ShapeDtypeStruct((m, D_MODEL), bf16),
        compiler_params=_cparams(("parallel", "parallel")),
        name="merge",
    )(a_in, s_in, w_a, w_s, zg, zg)


def _out_body(x_ref, m_ref, w_ref, o_ref):
    o_ref[...] = x_ref[...] + jnp.dot(m_ref[...], w_ref[...], preferred_element_type=f32)


def _out_proj(x, merged, w_out, layer):
    m = x.shape[0]
    tm = _row_tile(m, MM_ROW_TILE)
    tn = MM_COL_TILE
    return pl.pallas_call(
        _out_body,
        grid=(m // tm, D_MODEL // tn),
        in_specs=[pl.BlockSpec((tm, tn), lambda i, j: (i, j)),
                  pl.BlockSpec((tm, D_MODEL), lambda i, j: (i, 0)),
                  _layer_spec((D_MODEL, tn), layer, lambda i, j: (0, j))],
        out_specs=pl.BlockSpec((tm, tn), lambda i, j: (i, j)),
        out_shape=jax.ShapeDtypeStruct((m, D_MODEL), f32),
        compiler_params=_cparams(("parallel", "parallel")),
        name="out_proj",
    )(x, merged, w_out)


def _rope_tables(pos):
    posf = pos.astype(f32)[:, None]
    half = HEAD_DIM // 2
    inv = ROPE_THETA ** (-jnp.arange(half, dtype=f32) / half)
    ang = posf * inv[None, :]
    cos, sin = jnp.cos(ang), jnp.sin(ang)
    cos_h = jnp.concatenate([cos, cos], axis=1)
    sin_h = jnp.concatenate([-sin, sin], axis=1)
    half_i = IDX_DIM // 2
    inv_i = ROPE_THETA ** (-jnp.arange(half_i, dtype=f32) / half_i)
    ang_i = posf * inv_i[None, :]
    cos_i, sin_i = jnp.cos(ang_i), jnp.sin(ang_i)
    zeros = jnp.zeros_like(cos_i)
    pad = jnp.zeros((pos.shape[0], LANES - IDX_DIM), f32)
    cos_k = jnp.concatenate([cos_i, cos_i, pad], axis=1)
    sin_a = jnp.concatenate([-sin_i, zeros, pad], axis=1)
    sin_b = jnp.concatenate([zeros, sin_i, pad], axis=1)
    return dict(cos_h=cos_h, sin_h=sin_h, cos_k=cos_k, sin_a=sin_a, sin_b=sin_b,
                cos_ht=cos.T, sin_ht=sin.T, cos_t=cos_i.T, sin_t=sin_i.T)


def _in_proj_offsets():
    offs = [0]
    for s in IN_SIZES:
        offs.append(offs[-1] + s)
    return offs


def _repack_body(w_ref, wq_ref, wkv_ref, wqiw_ref, wki_ref, wu_ref, wzg_ref):
    offs = _in_proj_offsets()
    seg = lambda k: w_ref[:, offs[k]:offs[k + 1]].astype(bf16)
    rows = w_ref.shape[0]
    wq_ref[...] = seg(0)
    wkv_ref[...] = w_ref[:, offs[1]:offs[3]].astype(bf16)
    n_qi = IDX_HEADS * IDX_DIM
    wqiw_ref[:, :n_qi] = seg(3)
    wqiw_ref[:, n_qi:] = seg(5)
    wki_ref[:, :IDX_DIM] = seg(4)
    wki_ref[:, IDX_DIM:] = jnp.zeros((rows, LANES - IDX_DIM), bf16)
    wu_ref[...] = seg(7)
    col = 0
    for k in (6, 8, 9, 10):
        wzg_ref[:, col:col + IN_SIZES[k]] = seg(k)
        col += IN_SIZES[k]


def _repack_in_proj(w_in):
    depth, d, n_in = w_in.shape
    rows = 256
    n_qiw = IDX_HEADS * IDX_DIM + IDX_HEADS
    n_zg = ATTN_WIDTH + S5_WIDTH + 2 * D_MODEL
    widths = (ATTN_WIDTH, 2 * KV_WIDTH, n_qiw, LANES, S5_WIDTH, n_zg)
    at = lambda l, r: (l, r, 0)
    return pl.pallas_call(
        _repack_body,
        grid=(depth, d // rows),
        in_specs=[pl.BlockSpec((None, rows, n_in), at)],
        out_specs=[pl.BlockSpec((None, rows, w), at) for w in widths],
        out_shape=[jax.ShapeDtypeStruct((depth, d, w), bf16) for w in widths],
        compiler_params=_cparams(("parallel", "parallel")),
        name="repack_in_proj",
    )(w_in)


def _pack_weights(w_in, w_glu, w_br_attn, w_br_s5, w_out, norm_gain, q_norm_gain, k_norm_gain):
    depth = w_in.shape[0]
    w_q, w_kv, w_qiw, w_ki, w_u, w_zg = _repack_in_proj(w_in)
    t = lambda a: jnp.swapaxes(a, 1, 2)
    return dict(
        wt_q=t(w_q), wt_v=t(w_kv[:, :, KV_WIDTH:]), w_kv=w_kv,
        wt_qi=t(w_qiw),
        w_ki=w_ki, w_u=w_u, w_zg=w_zg,
        w_glu=w_glu.astype(bf16), w_br_attn=w_br_attn.astype(bf16),
        w_br_s5=w_br_s5.astype(bf16), w_out=w_out.astype(bf16),
        norm_gain=norm_gain.reshape(depth, 1, D_MODEL),
        q_gain=q_norm_gain.reshape(depth, HEAD_DIM, 1),
        k_gain=k_norm_gain.reshape(depth, 1, HEAD_DIM))


def _block_diag(blocks):
    depth, sg, gg, r, c = blocks.shape
    eye = jnp.eye(gg, dtype=blocks.dtype)
    return jnp.einsum("lsgrc,gk->lsgrkc", blocks, eye).reshape(depth, sg, gg * r, gg * c)


def _s5_stacked_params(lam_re, lam_im, log_dt, b_re, b_im, c_re, c_im, d, w_glu_bf16):
    depth = lam_re.shape[0]
    a_re, a_im, bb_re, bb_im = _s5_params(lam_re, lam_im, log_dt,
                                          jnp.transpose(b_re, (0, 3, 1, 2)), jnp.transpose(b_im, (0, 3, 1, 2)))
    def bu_blocks(bb):
        return jnp.transpose(bb, (0, 2, 1, 3)).reshape(depth, S5_SG, S5_SG_GROUPS, S5_GROUP, S5_STATE)
    w_bu = jnp.concatenate([_block_diag(bu_blocks(bb_re)), _block_diag(bu_blocks(bb_im))], axis=3)
    def c_blocks(cm):
        return jnp.transpose(cm, (0, 1, 3, 2)).reshape(depth, S5_SG, S5_SG_GROUPS, S5_STATE, S5_GROUP)
    return dict(a_re=a_re.reshape(depth, S5_SG, S5_SG_STATE), a_im=a_im.reshape(depth, S5_SG, S5_SG_STATE),
                w_bu=w_bu.astype(bf16),
                w_c_re=_block_diag(c_blocks(c_re)).astype(bf16),
                w_c_im=_block_diag(c_blocks(c_im)).astype(bf16),
                d=d.reshape(depth, 1, S5_WIDTH), w_glu=w_glu_bf16)


def _projections(x, wts, tabs, layer):
    h = _rmsnorm(x, wts["norm_gain"], layer)
    qt = _proj_qt(h, wts["wt_q"], wts["q_gain"], tabs["cos_ht"], tabs["sin_ht"], layer)
    k, v, kb, vt = _proj_kv(h, wts["w_kv"], wts["wt_v"], wts["k_gain"], tabs["cos_h"], tabs["sin_h"], layer)
    qit, wt = _proj_qit(h, wts["wt_qi"], tabs["cos_t"], tabs["sin_t"], layer)
    ki, kib = _proj_ki(h, wts["w_ki"], tabs["cos_k"], tabs["sin_a"], tabs["sin_b"], layer)
    u = _proj_plain(h, wts["w_u"], f32, layer)
    zg = _proj_zg(h, wts["w_zg"], layer)
    return dict(qt=qt, k=k, v=v, kb=kb, vt=vt, qit=qit, wt=wt, ki=ki, kib=kib, u=u, zg=zg)


def _finish(x, a_in, s_in, zg, wts, layer):
    merged = _merge(a_in, s_in, wts["w_br_attn"], wts["w_br_s5"], zg, layer)
    return _out_proj(x, merged, wts["w_out"], layer)


def kernel(x_prompt, x_sample, cache_k, cache_v, cache_kidx, state_s5_re, state_s5_im, page_table, norm_gain, w_in, q_norm_gain, k_norm_gain, s5_lam_re, s5_lam_im, s5_log_dt, s5_b_re, s5_b_im, s5_c_re, s5_c_im, s5_d, w_glu, w_br_attn, w_br_s5, w_out):
    depth = w_in.shape[0]
    b_p, t_p = x_prompt.shape[:2]
    b_s, t_s = x_sample.shape[:2]
    tok = SAMPLE_ROWS
    assert t_s <= tok
    n_pages = page_table.shape[1]
    past = n_pages * PAGE_SIZE
    n_phys = cache_k.shape[1]

    tabs_p = _rope_tables(jnp.tile(jnp.arange(t_p, dtype=i32), b_p))
    tabs_s = _rope_tables(jnp.tile(past + jnp.arange(tok, dtype=i32), b_s))

    xp = x_prompt.reshape(b_p * t_p, D_MODEL)
    xs = jnp.pad(x_sample, ((0, 0), (0, tok - t_s), (0, 0))).reshape(b_s * tok, D_MODEL)

    cache_k4 = cache_k.reshape(depth, n_phys, PAGE_SIZE * N_KV_HEADS, HEAD_DIM)
    cache_v4 = cache_v.reshape(depth, n_phys, PAGE_SIZE * N_KV_HEADS, HEAD_DIM)
    cache_kidx_t = jnp.swapaxes(cache_kidx, 2, 3)
    zeros_state = jnp.zeros((b_p, S5_GROUPS, S5_STATE), f32)

    def new_page(a):
        w = a.shape[-1]
        return jnp.pad(a.reshape(b_s, tok, w), ((0, 0), (0, PAGE_SIZE - tok), (0, 0)))

    wts = _pack_weights(w_in, w_glu, w_br_attn, w_br_s5, w_out, norm_gain, q_norm_gain, k_norm_gain)
    s5p = _s5_stacked_params(s5_lam_re, s5_lam_im, s5_log_dt, s5_b_re, s5_b_im, s5_c_re, s5_c_im, s5_d,
                             wts["w_glu"])

    outs_p, outs_s = [], []
    for l in range(depth):
        pp = _projections(xp, wts, tabs_p, l)
        a_in = _attn_prompt(pp["qt"], pp["qit"], pp["wt"], pp["zg"], pp["kb"], pp["vt"], pp["kib"], b_p, t_p)
        s_in, sr_p, si_p = _s5(pp["u"], pp["zg"], zeros_state, zeros_state, s5p, l, b_p, t_p, t_p)
        xp = _finish(xp, a_in, s_in, pp["zg"], wts, l)
        outs_p.append((pp["k"].reshape(b_p, t_p, N_KV_HEADS, HEAD_DIM),
                       pp["v"].reshape(b_p, t_p, N_KV_HEADS, HEAD_DIM),
                       pp["ki"].reshape(b_p, t_p, IDX_DIM), sr_p, si_p))

        ps = _projections(xs, wts, tabs_s, l)
        qi_rows = jnp.transpose(ps["qit"].reshape(IDX_HEADS, IDX_DIM, b_s, tok), (2, 0, 3, 1)
                                ).reshape(b_s, IDX_HEADS * tok, IDX_DIM)
        w_rows = jnp.transpose(ps["wt"].reshape(IDX_HEADS, b_s, tok), (1, 0, 2)
                               ).reshape(b_s, IDX_HEADS * tok, 1)
        keys, thr = _idx_sample(page_table, qi_rows, w_rows, cache_kidx_t, l,
                                jnp.swapaxes(new_page(ps["ki"]), 1, 2), t_s)
        q_rows = jnp.transpose(ps["qt"].reshape(N_HEADS, HEAD_DIM, b_s, tok), (2, 0, 3, 1)
                               ).reshape(b_s, N_HEADS * tok, HEAD_DIM)
        k_s = ps["k"].reshape(b_s * tok, KV_WIDTH)
        v_s = ps["v"].reshape(b_s * tok, KV_WIDTH)
        o_rows = _attn_sample(page_table, q_rows, keys, thr, cache_k4, cache_v4, l,
                              new_page(k_s), new_page(v_s))
        attn_s = jnp.transpose(o_rows.reshape(b_s, N_HEADS, tok, HEAD_DIM), (0, 2, 1, 3)
                               ).reshape(b_s * tok, ATTN_WIDTH)
        a_in_s = _gate_mul(attn_s, ps["zg"][:, :ATTN_WIDTH])
        s_in_s, sr_s, si_s = _s5(ps["u"], ps["zg"], state_s5_re[l], state_s5_im[l], s5p, l, b_s, tok, t_s)
        xs = _finish(xs, a_in_s, s_in_s, ps["zg"], wts, l)
        real = lambda a: a.reshape((b_s, tok) + a.shape[1:])[:, :t_s]
        outs_s.append((real(k_s).reshape(b_s, t_s, N_KV_HEADS, HEAD_DIM),
                       real(v_s).reshape(b_s, t_s, N_KV_HEADS, HEAD_DIM),
                       real(ps["ki"]), sr_s, si_s))

    k_prompt, v_prompt, kidx_prompt, s5_re_prompt, s5_im_prompt = [jnp.stack(a) for a in zip(*outs_p)]
    k_sample, v_sample, kidx_sample, s5_re_sample, s5_im_sample = [jnp.stack(a) for a in zip(*outs_s)]
    y_prompt = xp.reshape(b_p, t_p, D_MODEL)
    y_sample = xs.reshape(b_s, tok, D_MODEL)[:, :t_s]
    return (y_prompt, y_sample, k_prompt, v_prompt, kidx_prompt, s5_re_prompt, s5_im_prompt,
            k_sample, v_sample, kidx_sample, s5_re_sample, s5_im_sample)
```

```python
import functools
import math

import jax
import jax.numpy as jnp
from jax import lax
from jax.experimental import pallas as pl
from jax.experimental.pallas import tpu as pltpu

D_MODEL = 2048
PAGE_SIZE = 128
N_HEADS = 16
HEAD_DIM = 128
N_KV_HEADS = 4
HEADS_PER_KV = N_HEADS // N_KV_HEADS
ATTN_WIDTH = N_HEADS * HEAD_DIM
KV_WIDTH = N_KV_HEADS * HEAD_DIM
IDX_HEADS = 16
IDX_DIM = 64
TOPK_MAX = 256
S5_WIDTH = D_MODEL // 2
S5_GROUP = 16
S5_GROUPS = S5_WIDTH // S5_GROUP
S5_STATE = 64
ROPE_THETA = 10000.0
EPS = 1e-6
IN_SIZES = (ATTN_WIDTH, KV_WIDTH, KV_WIDTH, IDX_HEADS * IDX_DIM, IDX_DIM, IDX_HEADS,
            ATTN_WIDTH, S5_WIDTH, S5_WIDTH, D_MODEL, D_MODEL)

LANES = 128
SUBLANES = 8
VMEM_LIMIT_BYTES = 56 * 1024 * 1024
MM_ROW_TILE = 1024
MM_COL_TILE = 1024

S5_SG = S5_WIDTH // LANES
S5_SG_GROUPS = S5_GROUPS // S5_SG
S5_SG_STATE = S5_SG_GROUPS * S5_STATE
S5_SCAN_UNROLL = 8

SAMPLE_ROWS = 16

ATTN_Q_TILE = 256
ATTN_ROW_BLOCK = 64
COUNT_ROWS = 32
PAGES_PER_STEP = 16
IDX_PAGES_PER_STEP = 32
LOG2_E = math.log2(math.e)

INT_MIN = -2 ** 31
MASK_BIAS = -1e30

f32 = jnp.float32
bf16 = jnp.bfloat16
i32 = jnp.int32


def _cparams(sem):
    return pltpu.CompilerParams(dimension_semantics=sem, vmem_limit_bytes=VMEM_LIMIT_BYTES)


def _row_tile(m, cap):
    return m if m <= cap else cap


def _norm_body(x_ref, g_ref, o_ref):
    x = x_ref[...]
    ms = jnp.mean(x * x, axis=-1, keepdims=True)
    o_ref[...] = (x * lax.rsqrt(ms + EPS) * g_ref[...]).astype(o_ref.dtype)


def _layer_spec(block, layer, index_map):
    return pl.BlockSpec((None,) + tuple(block), lambda *idx: (layer,) + tuple(index_map(*idx)))


def _rmsnorm(x, gains, layer):
    m, d = x.shape
    tm = _row_tile(m, 512)
    return pl.pallas_call(
        _norm_body,
        grid=(m // tm,),
        in_specs=[pl.BlockSpec((tm, d), lambda i: (i, 0)),
                  _layer_spec((1, d), layer, lambda i: (0, 0))],
        out_specs=pl.BlockSpec((tm, d), lambda i: (i, 0)),
        out_shape=jax.ShapeDtypeStruct((m, d), bf16),
        compiler_params=_cparams(("parallel",)),
        name="rmsnorm",
    )(x, gains)


def _head_norm_rope(x, gain, cos, sin):
    ms = jnp.mean(x * x, axis=-1, keepdims=True)
    y = x * lax.rsqrt(ms + EPS) * gain
    return y * cos + pltpu.roll(y, HEAD_DIM // 2, 1) * sin


def _qt_body(wt_ref, h_ref, g_ref, cos_ref, sin_ref, o_ref, *, heads):
    acc = lax.dot_general(wt_ref[...], h_ref[...], (((1,), (1,)), ((), ())),
                          preferred_element_type=f32)
    c = cos_ref[...]
    s = sin_ref[...]
    g = g_ref[...]
    half = HEAD_DIM // 2
    scale = HEAD_DIM ** -0.5 * LOG2_E
    for hh in range(heads):
        r0 = hh * HEAD_DIM
        x = acc[r0:r0 + HEAD_DIM]
        ms = jnp.mean(x * x, axis=0, keepdims=True)
        y = x * lax.rsqrt(ms + EPS) * g
        x1 = y[:half]
        x2 = y[half:]
        o_ref[r0:r0 + half, :] = ((x1 * c - x2 * s) * scale).astype(o_ref.dtype)
        o_ref[r0 + half:r0 + HEAD_DIM, :] = ((x2 * c + x1 * s) * scale).astype(o_ref.dtype)


def _proj_qt(h, wt_q, gain, cos_ht, sin_ht, layer):
    m, d = h.shape
    tm = _row_tile(m, MM_ROW_TILE)
    tn = 4 * HEAD_DIM
    return pl.pallas_call(
        functools.partial(_qt_body, heads=tn // HEAD_DIM),
        grid=(m // tm, ATTN_WIDTH // tn),
        in_specs=[_layer_spec((tn, d), layer, lambda i, j: (j, 0)),
                  pl.BlockSpec((tm, d), lambda i, j: (i, 0)),
                  _layer_spec((HEAD_DIM, 1), layer, lambda i, j: (0, 0)),
                  pl.BlockSpec((HEAD_DIM // 2, tm), lambda i, j: (0, i)),
                  pl.BlockSpec((HEAD_DIM // 2, tm), lambda i, j: (0, i))],
        out_specs=pl.BlockSpec((tn, tm), lambda i, j: (j, i)),
        out_shape=jax.ShapeDtypeStruct((ATTN_WIDTH, m), bf16),
        compiler_params=_cparams(("parallel", "parallel")),
        name="proj_q_t",
    )(wt_q, h, gain, cos_ht, sin_ht)


def _dot_nt(a, b_t):
    return lax.dot_general(a, b_t, (((1,), (1,)), ((), ())), preferred_element_type=f32)


def _kv_body(h_ref, wt_ref, g_ref, cos_ref, sin_ref, k_ref, v_ref, kb_ref, vt_ref, *, chunk):
    h = h_ref[...]
    acc = _dot_nt(h, wt_ref[...])
    cos = cos_ref[...]
    sin = sin_ref[...]
    g = g_ref[...]
    tm = h.shape[0]
    for hh in range(N_KV_HEADS):
        sl = slice(hh * HEAD_DIM, (hh + 1) * HEAD_DIM)
        kh = _head_norm_rope(acc[:, sl], g, cos, sin)
        rows = pl.ds(hh, tm, stride=N_KV_HEADS)
        k_ref[rows, :] = kh
        v_ref[rows, :] = acc[:, KV_WIDTH + hh * HEAD_DIM:KV_WIDTH + (hh + 1) * HEAD_DIM]
        kb_ref[:, sl] = kh.astype(bf16)
    vt = _dot_nt(wt_ref[KV_WIDTH:, :], h)
    for cc in range(vt_ref.shape[0]):
        vt_ref[cc] = vt[:, cc * chunk:(cc + 1) * chunk].astype(bf16)


def _proj_kv(h, wt_kv, gain, cos, sin, layer):
    m, d = h.shape
    tm = _row_tile(m, 512)
    chunk = min(tm, ATTN_Q_TILE)
    row = lambda i: (i, 0)
    fix = lambda i: (0, 0)
    return pl.pallas_call(
        functools.partial(_kv_body, chunk=chunk),
        grid=(m // tm,),
        in_specs=[pl.BlockSpec((tm, d), row),
                  _layer_spec((2 * KV_WIDTH, d), layer, fix),
                  _layer_spec((1, HEAD_DIM), layer, fix),
                  pl.BlockSpec((tm, HEAD_DIM), row),
                  pl.BlockSpec((tm, HEAD_DIM), row)],
        out_specs=[pl.BlockSpec((tm * N_KV_HEADS, HEAD_DIM), row),
                   pl.BlockSpec((tm * N_KV_HEADS, HEAD_DIM), row),
                   pl.BlockSpec((tm, KV_WIDTH), row),
                   pl.BlockSpec((tm // chunk, KV_WIDTH, chunk), lambda i: (i, 0, 0))],
        out_shape=[jax.ShapeDtypeStruct((m * N_KV_HEADS, HEAD_DIM), f32),
                   jax.ShapeDtypeStruct((m * N_KV_HEADS, HEAD_DIM), f32),
                   jax.ShapeDtypeStruct((m, KV_WIDTH), bf16),
                   jax.ShapeDtypeStruct((m // chunk, KV_WIDTH, chunk), bf16)],
        compiler_params=_cparams(("parallel",)),
        name="proj_kv",
    )(h, wt_kv, gain, cos, sin)


def _qit_body(wt_ref, h_ref, cos_ref, sin_ref, qit_ref, wt_out_ref):
    acc = lax.dot_general(wt_ref[...], h_ref[...], (((1,), (1,)), ((), ())),
                          preferred_element_type=f32)
    c = cos_ref[...]
    s = sin_ref[...]
    half = IDX_DIM // 2
    scale = IDX_DIM ** -0.5
    for hh in range(IDX_HEADS):
        r0 = hh * IDX_DIM
        x1 = acc[r0:r0 + half]
        x2 = acc[r0 + half:r0 + IDX_DIM]
        qit_ref[r0:r0 + half, :] = ((x1 * c - x2 * s) * scale).astype(qit_ref.dtype)
        qit_ref[r0 + half:r0 + IDX_DIM, :] = ((x2 * c + x1 * s) * scale).astype(qit_ref.dtype)
    n_qi = IDX_HEADS * IDX_DIM
    wt_out_ref[...] = acc[n_qi:n_qi + IDX_HEADS] * (IDX_HEADS ** -0.5)


def _proj_qit(h, wt_qi, cos_t, sin_t, layer):
    m, d = h.shape
    tm = _row_tile(m, 512)
    n_rows = wt_qi.shape[1]
    n_qi = IDX_HEADS * IDX_DIM
    return pl.pallas_call(
        _qit_body,
        grid=(m // tm,),
        in_specs=[_layer_spec((n_rows, d), layer, lambda i: (0, 0)),
                  pl.BlockSpec((tm, d), lambda i: (i, 0)),
                  pl.BlockSpec((IDX_DIM // 2, tm), lambda i: (0, i)),
                  pl.BlockSpec((IDX_DIM // 2, tm), lambda i: (0, i))],
        out_specs=[pl.BlockSpec((n_qi, tm), lambda i: (0, i)),
                   pl.BlockSpec((IDX_HEADS, tm), lambda i: (0, i))],
        out_shape=[jax.ShapeDtypeStruct((n_qi, m), bf16),
                   jax.ShapeDtypeStruct((IDX_HEADS, m), f32)],
        compiler_params=_cparams(("parallel",)),
        name="proj_qi_t",
    )(wt_qi, h, cos_t, sin_t)


def _ki_body(h_ref, w_ref, cos_ref, sina_ref, sinb_ref, ki_ref, kib_ref):
    x = _dot_nt(h_ref[...], w_ref[...])
    half = IDX_DIM // 2
    r = (x * cos_ref[...] + pltpu.roll(x, LANES - half, 1) * sina_ref[...]
         + pltpu.roll(x, half, 1) * sinb_ref[...])
    ki = r[:, :IDX_DIM]
    ki_ref[...] = ki
    kib_ref[...] = ki.astype(bf16)


def _proj_ki(h, w_ki, cos_k, sin_a, sin_b, layer):
    m, d = h.shape
    tm = _row_tile(m, 512)
    row = lambda i: (i, 0)
    return pl.pallas_call(
        _ki_body,
        grid=(m // tm,),
        in_specs=[pl.BlockSpec((tm, d), row),
                  _layer_spec((LANES, d), layer, lambda i: (0, 0)),
                  pl.BlockSpec((tm, LANES), row),
                  pl.BlockSpec((tm, LANES), row),
                  pl.BlockSpec((tm, LANES), row)],
        out_specs=[pl.BlockSpec((tm, IDX_DIM), row)] * 2,
        out_shape=[jax.ShapeDtypeStruct((m, IDX_DIM), f32),
                   jax.ShapeDtypeStruct((m, IDX_DIM), bf16)],
        compiler_params=_cparams(("parallel",)),
        name="proj_ki",
    )(h, w_ki, cos_k, sin_a, sin_b)


def _plain_body(h_ref, w_ref, o_ref):
    o_ref[...] = _dot_nt(h_ref[...], w_ref[...]).astype(o_ref.dtype)


def _proj_plain(h, wt, out_dtype, layer):
    m, d = h.shape
    n = wt.shape[1]
    tm = _row_tile(m, MM_ROW_TILE)
    tn = MM_COL_TILE
    return pl.pallas_call(
        _plain_body,
        grid=(m // tm, n // tn),
        in_specs=[pl.BlockSpec((tm, d), lambda i, j: (i, 0)),
                  _layer_spec((tn, d), layer, lambda i, j: (j, 0))],
        out_specs=pl.BlockSpec((tm, tn), lambda i, j: (i, j)),
        out_shape=jax.ShapeDtypeStruct((m, n), out_dtype),
        compiler_params=_cparams(("parallel", "parallel")),
        name="proj_u",
    )(h, wt)


def _sigmoid(x):
    return 1.0 / (1.0 + jnp.exp(-x))


def _zg_body(h_ref, w_ref, o_ref, *, silu_tiles):
    acc = _dot_nt(h_ref[...], w_ref[...])
    sg = _sigmoid(acc)
    is_silu = pl.program_id(1) < silu_tiles
    o_ref[...] = jnp.where(is_silu, acc * sg, sg).astype(o_ref.dtype)


def _proj_zg(h, wt_zg, layer):
    m, d = h.shape
    n = wt_zg.shape[1]
    tm = _row_tile(m, MM_ROW_TILE)
    tn = MM_COL_TILE
    return pl.pallas_call(
        functools.partial(_zg_body, silu_tiles=(ATTN_WIDTH + S5_WIDTH) // tn),
        grid=(m // tm, n // tn),
        in_specs=[pl.BlockSpec((tm, d), lambda i, j: (i, 0)),
                  _layer_spec((tn, d), layer, lambda i, j: (j, 0))],
        out_specs=pl.BlockSpec((tm, tn), lambda i, j: (i, j)),
        out_shape=jax.ShapeDtypeStruct((m, n), bf16),
        compiler_params=_cparams(("parallel", "parallel")),
        name="proj_zg",
    )(h, wt_zg)


def _sortable_key(score):
    b = pltpu.bitcast(score, i32)
    return b ^ ((b >> 31) & jnp.int32(0x7FFFFFFF))


def _kth_largest(count_ge, shape, k):
    zero = jnp.zeros(shape, i32)
    prefix = jnp.where(count_ge(zero) >= k, zero, jnp.full(shape, INT_MIN, i32))

    def bit_body(bi, prefix):
        cand = prefix | (jnp.int32(1) << (30 - bi))
        return jnp.where(count_ge(cand) >= k, cand, prefix)

    thr = lax.fori_loop(0, 31, bit_body, prefix)
    return jnp.maximum(thr, jnp.int32(INT_MIN + 1))


def _resolve_ties(thr, k, count_ge, count_tied_before, demote, n_pos_bits, active=None):
    excess = count_ge(thr) - k
    if active is not None:
        excess = jnp.where(active, excess, 0)

    @pl.when(jnp.max(excess) > 0)
    def _():
        need = k - count_ge(thr + 1)

        def bit_body(bi, pos):
            cand = pos | (jnp.int32(1) << (n_pos_bits - 1 - bi))
            return jnp.where(count_tied_before(thr, cand) < need, cand, pos)

        last = lax.fori_loop(0, n_pos_bits, bit_body, jnp.zeros(thr.shape, i32))
        demote(thr, last)


def _sublane_allreduce(x, op):
    for shift in (4, 2, 1):
        x = op(x, pltpu.roll(x, shift, 0))
    return x


def _attn_prompt_body(qt_ref, qit_ref, wt_ref, z_ref, k_ref, vt_ref, ki_ref, o_ref,
                      key_scr, acc_scr, m_scr, l_scr, alpha_scr, s_scr, p_scr, bias_scr,
                      *, tq, k_top, n_pos_bits):
    i = pl.program_id(1)
    n_chunks = i + 1
    half = tq // 2

    def score_keys(c, diagonal):
        for sub in range(2):
            r0 = pl.multiple_of(c * tq + sub * half, half)
            kic = ki_ref[pl.ds(r0, half), :]
            acc = jnp.zeros((half, tq), f32)
            for hh in range(IDX_HEADS):
                lg = jnp.dot(kic, qit_ref[hh * IDX_DIM:(hh + 1) * IDX_DIM, :],
                             preferred_element_type=f32)
                acc = acc + jnp.maximum(lg, 0.0) * wt_ref[hh:hh + 1, :]
            key = _sortable_key(acc)
            if diagonal:
                kpos = lax.broadcasted_iota(i32, (half, tq), 0) + sub * half
                qpos = lax.broadcasted_iota(i32, (half, tq), 1)
                key = jnp.where(kpos <= qpos, key, jnp.int32(INT_MIN))
            key_scr[pl.ds(r0, half), :] = key

    def full_chunk(c, carry):
        score_keys(c, False)
        return carry

    lax.fori_loop(0, i, full_chunk, 0)
    score_keys(i, True)

    @pl.when(n_chunks % 2 == 1)
    def _():
        key_scr[pl.ds(pl.multiple_of(n_chunks * tq, tq), tq), :] = jnp.full((tq, tq), INT_MIN, i32)

    def count_ge(cand):
        def body(c, cnt):
            kk = key_scr[pl.ds(pl.multiple_of(c * 2 * tq, 2 * tq), 2 * tq), :]
            hit = (kk >= cand).astype(i32)
            return cnt + jnp.sum(hit.reshape(2 * tq // COUNT_ROWS, COUNT_ROWS, tq), axis=0)

        cnt = lax.fori_loop(0, (n_chunks + 1) // 2, body, jnp.zeros((COUNT_ROWS, tq), i32))
        return jnp.sum(cnt, axis=0, keepdims=True)

    thr = _kth_largest(count_ge, (1, tq), k_top)

    def chunk_positions(c):
        return lax.broadcasted_iota(i32, (tq, tq), 0) + c * tq

    def count_tied_before(thr, pos_limit):
        def body(c, cnt):
            kk = key_scr[pl.ds(pl.multiple_of(c * tq, tq), tq), :]
            hit = ((kk == thr) & (chunk_positions(c) < pos_limit)).astype(i32)
            return cnt + jnp.sum(hit.reshape(tq // SUBLANES, SUBLANES, tq), axis=0)

        cnt = lax.fori_loop(0, n_chunks, body, jnp.zeros((SUBLANES, tq), i32))
        return jnp.sum(cnt, axis=0, keepdims=True)

    def demote(thr, last):
        def body(c, carry):
            rows = pl.ds(pl.multiple_of(c * tq, tq), tq)
            kk = key_scr[rows, :]
            key_scr[rows, :] = jnp.where((kk == thr) & (chunk_positions(c) > last), thr - 1, kk)
            return carry

        lax.fori_loop(0, n_chunks, body, 0)

    _resolve_ties(thr, k_top, count_ge, count_tied_before, demote, n_pos_bits)

    m_scr[...] = jnp.full(m_scr.shape, -jnp.inf, f32)
    l_scr[...] = jnp.zeros(l_scr.shape, f32)
    acc_scr[...] = jnp.zeros(acc_scr.shape, f32)
    n_sub = tq // SUBLANES
    d_sub = HEAD_DIM // SUBLANES

    def attend_chunk(c, carry):
        r0 = pl.multiple_of(c * tq, tq)
        bias_scr[...] = jnp.where(key_scr[pl.ds(r0, tq), :] >= thr, 0.0, MASK_BIAS)
        for g in range(N_KV_HEADS):
            kc = k_ref[pl.ds(r0, tq), g * HEAD_DIM:(g + 1) * HEAD_DIM]
            for hh in range(HEADS_PER_KV):
                h = g * HEADS_PER_KV + hh
                s = jnp.dot(kc, qt_ref[h * HEAD_DIM:(h + 1) * HEAD_DIM, :], preferred_element_type=f32)
                s_scr[h] = s + bias_scr[...]
        blocks = [slice(j * ATTN_ROW_BLOCK, (j + 1) * ATTN_ROW_BLOCK) for j in range(tq // ATTN_ROW_BLOCK)]
        blk_sub = ATTN_ROW_BLOCK // SUBLANES
        for h in range(N_HEADS):
            mx = m_scr[h]
            for rows in blocks:
                mx = jnp.maximum(mx, jnp.max(s_scr[h, rows, :].reshape(blk_sub, SUBLANES, tq), axis=0))
            m_new = _sublane_allreduce(mx, jnp.maximum)
            alpha_scr[h] = jnp.exp2(m_scr[h] - m_new)
            m_scr[h] = m_new
        for h in range(N_HEADS):
            m_new = m_scr[h]
            psum = jnp.zeros((SUBLANES, tq), f32)
            for rows in blocks:
                p3 = jnp.exp2(s_scr[h, rows, :].reshape(blk_sub, SUBLANES, tq) - m_new[None])
                psum = psum + jnp.sum(p3, axis=0)
                p_scr[h, rows, :] = p3.reshape(ATTN_ROW_BLOCK, tq).astype(bf16)
            l_scr[h] = alpha_scr[h] * l_scr[h] + _sublane_allreduce(psum, jnp.add)
        for g in range(N_KV_HEADS):
            vtc = vt_ref[c, g * HEAD_DIM:(g + 1) * HEAD_DIM, :]
            for hh in range(HEADS_PER_KV):
                h = g * HEADS_PER_KV + hh
                hs = slice(h * HEAD_DIM, (h + 1) * HEAD_DIM)
                pv = jnp.dot(vtc, p_scr[h], preferred_element_type=f32)
                acc = acc_scr[hs, :].reshape(d_sub, SUBLANES, tq) * alpha_scr[h][None]
                acc_scr[hs, :] = acc.reshape(HEAD_DIM, tq) + pv
        return carry

    lax.fori_loop(0, n_chunks, attend_chunk, 0)

    for h in range(N_HEADS):
        hs = slice(h * HEAD_DIM, (h + 1) * HEAD_DIM)
        out_t = (acc_scr[hs, :].reshape(d_sub, SUBLANES, tq) / l_scr[h][None]).reshape(HEAD_DIM, tq)
        o_ref[:, hs] = (out_t.T * z_ref[:, hs].astype(f32)).astype(o_ref.dtype)


def _attn_prompt(qt, qit, wt, zg, kb, vt, kib, batch, seq):
    tq = ATTN_Q_TILE
    assert seq % tq == 0
    nq = seq // tq
    k_top = min(TOPK_MAX, seq // 4)
    assert k_top <= tq
    qrow = lambda b, i: (b * nq + i, 0)
    qcol = lambda b, i: (0, b * nq + i)
    per_b = lambda b, i: (b, 0)
    return pl.pallas_call(
        functools.partial(_attn_prompt_body, tq=tq, k_top=k_top, n_pos_bits=(seq - 1).bit_length()),
        grid=(batch, nq),
        in_specs=[pl.BlockSpec((ATTN_WIDTH, tq), qcol),
                  pl.BlockSpec((IDX_HEADS * IDX_DIM, tq), qcol),
                  pl.BlockSpec((IDX_HEADS, tq), qcol),
                  pl.BlockSpec((tq, ATTN_WIDTH), qrow),
                  pl.BlockSpec((seq, KV_WIDTH), per_b),
                  pl.BlockSpec((nq, KV_WIDTH, tq), lambda b, i: (b, 0, 0)),
                  pl.BlockSpec((seq, IDX_DIM), per_b)],
        out_specs=pl.BlockSpec((tq, ATTN_WIDTH), qrow),
        out_shape=jax.ShapeDtypeStruct((batch * seq, ATTN_WIDTH), bf16),
        scratch_shapes=[pltpu.VMEM((seq + tq, tq), i32),
                        pltpu.VMEM((ATTN_WIDTH, tq), f32),
                        pltpu.VMEM((N_HEADS, SUBLANES, tq), f32),
                        pltpu.VMEM((N_HEADS, SUBLANES, tq), f32),
                        pltpu.VMEM((N_HEADS, SUBLANES, tq), f32),
                        pltpu.VMEM((N_HEADS, tq, tq), f32),
                        pltpu.VMEM((N_HEADS, tq, tq), bf16),
                        pltpu.VMEM((tq, tq), f32)],
        compiler_params=_cparams(("parallel", "arbitrary")),
        name="attn_prompt",
    )(qt, qit, wt, zg, kb, vt, kib)


def _idx_sample_body(pt_ref, qi_ref, w_ref, *rest, n_pages, n_new, k_top, ppb):
    kid_refs = rest[:ppb]
    kinew_ref, key_ref, thr_ref = rest[ppb:]
    p = pl.program_id(1)
    tok = SAMPLE_ROWS

    def page_keys(ki_page_t):
        lg = jnp.dot(qi_ref[0], ki_page_t.astype(bf16), preferred_element_type=f32)
        wgt = jnp.maximum(lg, 0.0) * w_ref[0]
        score = jnp.sum(wgt.reshape(IDX_HEADS, tok, PAGE_SIZE), axis=0)
        return _sortable_key(score)

    for j in range(ppb):
        key_ref[0, p * ppb + j] = page_keys(kid_refs[j][...])

    @pl.when(p == pl.num_programs(1) - 1)
    def _():
        key = page_keys(kinew_ref[0])
        kpos = lax.broadcasted_iota(i32, (tok, PAGE_SIZE), 1)
        qtok = lax.broadcasted_iota(i32, (tok, PAGE_SIZE), 0)
        key_ref[0, n_pages] = jnp.where((kpos <= qtok) & (kpos < n_new), key, jnp.int32(INT_MIN))

        def count_ge(cand):
            def body(c, cnt):
                hit = (key_ref[0, pl.ds(c * ppb, ppb)] >= cand).astype(i32)
                return cnt + jnp.sum(hit, axis=0)

            cnt = lax.fori_loop(0, n_pages // ppb, body, jnp.zeros((tok, PAGE_SIZE), i32))
            cnt = cnt + (key_ref[0, n_pages] >= cand).astype(i32)
            return jnp.sum(cnt, axis=-1, keepdims=True)

        thr = _kth_largest(count_ge, (tok, 1), k_top)
        thr_ref[0] = jnp.broadcast_to(thr, (tok, PAGE_SIZE))

        def page_positions(c):
            return lax.broadcasted_iota(i32, (tok, PAGE_SIZE), 1) + c * PAGE_SIZE

        def count_tied_before(thr, pos_limit):
            def body(c, cnt):
                return cnt + ((key_ref[0, c] == thr) & (page_positions(c) < pos_limit)).astype(i32)

            cnt = lax.fori_loop(0, n_pages + 1, body, jnp.zeros((tok, PAGE_SIZE), i32))
            return jnp.sum(cnt, axis=-1, keepdims=True)

        def demote(thr, last):
            def body(c, carry):
                kk = key_ref[0, c]
                key_ref[0, c] = jnp.where((kk == thr) & (page_positions(c) > last), thr - 1, kk)
                return carry

            lax.fori_loop(0, n_pages + 1, body, 0)

        real_rows = lax.broadcasted_iota(i32, (tok, 1), 0) < n_new
        _resolve_ties(thr, k_top, count_ge, count_tied_before, demote,
                      ((n_pages + 1) * PAGE_SIZE - 1).bit_length(), active=real_rows)


def _pages_per_step(n_pages, cap=PAGES_PER_STEP):
    ppb = min(cap, n_pages)
    assert n_pages % ppb == 0
    return ppb


def _page_spec(block, layer, ppb, j):
    return pl.BlockSpec(block, lambda b, p, pt: (layer, pt[b, p * ppb + j], 0, 0))


def _idx_sample(page_table, qi_rows, w_rows, cache_kidx, layer, ki_new_pages, n_new):
    nb, n_pages = page_table.shape
    tok = SAMPLE_ROWS
    ppb = _pages_per_step(n_pages, IDX_PAGES_PER_STEP)
    k_top = min(TOPK_MAX, (n_pages * PAGE_SIZE + n_new) // 4)
    per_b = lambda b, p, pt: (b, 0, 0)
    grid_spec = pltpu.PrefetchScalarGridSpec(
        num_scalar_prefetch=1,
        grid=(nb, n_pages // ppb),
        in_specs=[pl.BlockSpec((1, IDX_HEADS * tok, IDX_DIM), per_b),
                  pl.BlockSpec((1, IDX_HEADS * tok, 1), per_b)]
                 + [_page_spec((None, None, IDX_DIM, PAGE_SIZE), layer, ppb, j) for j in range(ppb)]
                 + [pl.BlockSpec((1, IDX_DIM, PAGE_SIZE), per_b)],
        out_specs=[pl.BlockSpec((1, n_pages + 1, tok, PAGE_SIZE), lambda b, p, pt: (b, 0, 0, 0)),
                   pl.BlockSpec((1, tok, PAGE_SIZE), per_b)],
    )
    return pl.pallas_call(
        functools.partial(_idx_sample_body, n_pages=n_pages, n_new=n_new, k_top=k_top, ppb=ppb),
        grid_spec=grid_spec,
        out_shape=[jax.ShapeDtypeStruct((nb, n_pages + 1, tok, PAGE_SIZE), i32),
                   jax.ShapeDtypeStruct((nb, tok, PAGE_SIZE), i32)],
        compiler_params=_cparams(("parallel", "arbitrary")),
        name="idx_sample",
    )(page_table, qi_rows, w_rows, *([cache_kidx] * ppb), ki_new_pages)


def _attn_sample_body(pt_ref, q_ref, key_ref, keyn_ref, thr_ref, *rest, ppb):
    k_refs = rest[:ppb]
    v_refs = rest[ppb:2 * ppb]
    kn_ref, vn_ref, o_ref, acc_scr, m_scr, l_scr = rest[2 * ppb:]
    p = pl.program_id(1)
    tok = SAMPLE_ROWS
    rows_g = HEADS_PER_KV * tok
    thr = thr_ref[0]

    @pl.when(p == 0)
    def _():
        m_scr[...] = jnp.full(m_scr.shape, -jnp.inf, f32)
        l_scr[...] = jnp.zeros(l_scr.shape, f32)
        acc_scr[...] = jnp.zeros(acc_scr.shape, f32)

    def attend(bias_tok, k_of_group, v_of_group):
        bias = jnp.concatenate([bias_tok] * HEADS_PER_KV, axis=0)
        for g in range(N_KV_HEADS):
            rs = slice(g * rows_g, (g + 1) * rows_g)
            s = lax.dot_general(q_ref[0, rs, :], k_of_group(g), (((1,), (1,)), ((), ())),
                                preferred_element_type=f32) + bias
            m_prev = m_scr[rs, :]
            m_new = jnp.maximum(m_prev, jnp.max(s, axis=-1, keepdims=True))
            alpha = jnp.exp2(m_prev - m_new)
            pr = jnp.exp2(s - m_new)
            l_scr[rs, :] = alpha * l_scr[rs, :] + jnp.sum(pr, axis=-1, keepdims=True)
            acc_scr[rs, :] = alpha * acc_scr[rs, :] + jnp.dot(
                pr.astype(bf16), v_of_group(g), preferred_element_type=f32)
            m_scr[rs, :] = m_new

    def mask_bias(keys):
        return jnp.where(keys >= thr, 0.0, MASK_BIAS).astype(f32)

    def cached(refs):
        def of_group(g):
            rows = pl.ds(g, PAGE_SIZE, stride=N_KV_HEADS)
            return jnp.concatenate([r[rows, :].astype(bf16) for r in refs], axis=0)
        return of_group

    attend(jnp.concatenate([mask_bias(key_ref[0, j]) for j in range(ppb)], axis=1),
           cached(k_refs), cached(v_refs))

    @pl.when(p == pl.num_programs(1) - 1)
    def _():
        def fresh(ref):
            return lambda g: ref[0, :, g * HEAD_DIM:(g + 1) * HEAD_DIM].astype(bf16)

        attend(mask_bias(keyn_ref[0, 0]), fresh(kn_ref), fresh(vn_ref))
        o_ref[0] = acc_scr[...] / l_scr[...]


def _attn_sample(page_table, q_rows, keys, thr, cache_k, cache_v, layer, k_new_pages, v_new_pages):
    nb, n_pages = page_table.shape
    tok = SAMPLE_ROWS
    rows = N_HEADS * tok
    ppb = _pages_per_step(n_pages)
    per_b = lambda b, p, pt: (b, 0, 0)
    page_block = (None, None, PAGE_SIZE * N_KV_HEADS, HEAD_DIM)
    grid_spec = pltpu.PrefetchScalarGridSpec(
        num_scalar_prefetch=1,
        grid=(nb, n_pages // ppb),
        in_specs=[pl.BlockSpec((1, rows, HEAD_DIM), per_b),
                  pl.BlockSpec((1, ppb, tok, PAGE_SIZE), lambda b, p, pt: (b, p, 0, 0)),
                  pl.BlockSpec((1, 1, tok, PAGE_SIZE), lambda b, p, pt: (b, n_pages, 0, 0)),
                  pl.BlockSpec((1, tok, PAGE_SIZE), per_b)]
                 + [_page_spec(page_block, layer, ppb, j) for j in range(ppb)]
                 + [_page_spec(page_block, layer, ppb, j) for j in range(ppb)]
                 + [pl.BlockSpec((1, PAGE_SIZE, KV_WIDTH), per_b),
                    pl.BlockSpec((1, PAGE_SIZE, KV_WIDTH), per_b)],
        out_specs=pl.BlockSpec((1, rows, HEAD_DIM), per_b),
        scratch_shapes=[pltpu.VMEM((rows, HEAD_DIM), f32),
                        pltpu.VMEM((rows, 1), f32),
                        pltpu.VMEM((rows, 1), f32)],
    )
    return pl.pallas_call(
        functools.partial(_attn_sample_body, ppb=ppb),
        grid_spec=grid_spec,
        out_shape=jax.ShapeDtypeStruct((nb, rows, HEAD_DIM), f32),
        compiler_params=_cparams(("parallel", "arbitrary")),
        name="attn_sample",
    )(page_table, q_rows, keys, keys, thr, *([cache_k] * ppb), *([cache_v] * ppb),
      k_new_pages, v_new_pages)


def _gate_mul_body(a_ref, z_ref, o_ref):
    o_ref[...] = (a_ref[...] * z_ref[...].astype(f32)).astype(o_ref.dtype)


def _gate_mul(a, zg):
    m, n = a.shape
    return pl.pallas_call(
        _gate_mul_body,
        grid=(1,),
        in_specs=[pl.BlockSpec((m, n), lambda i: (0, 0)),
                  pl.BlockSpec((m, n), lambda i: (0, 0))],
        out_specs=pl.BlockSpec((m, n), lambda i: (0, 0)),
        out_shape=jax.ShapeDtypeStruct((m, n), bf16),
        compiler_params=_cparams(("arbitrary",)),
        name="gate_mul",
    )(a, zg)


def _s5_param_body(lr_ref, li_ref, ldt_ref, bre_ref, bim_ref, ar_ref, ai_ref, bbre_ref, bbim_ref):
    lr = lr_ref[...]
    li = li_ref[...]
    dt = jnp.exp(ldt_ref[...])
    mag = jnp.exp(lr * dt)
    ar = mag * jnp.cos(li * dt)
    ai = mag * jnp.sin(li * dt)
    den = lr * lr + li * li
    nr = ar - 1.0
    cr = (nr * lr + ai * li) / den
    ci = (ai * lr - nr * li) / den
    ar_ref[...] = ar
    ai_ref[...] = ai
    for h in range(S5_GROUP):
        bbre_ref[h] = cr * bre_ref[h] - ci * bim_ref[h]
        bbim_ref[h] = cr * bim_ref[h] + ci * bre_ref[h]


def _s5_params(lam_re, lam_im, log_dt, b_re, b_im):
    depth = lam_re.shape[0]
    gs = (None, S5_GROUPS, S5_STATE)
    bs = (None, S5_GROUP, S5_GROUPS, S5_STATE)
    at3 = lambda l: (l, 0, 0)
    at4 = lambda l: (l, 0, 0, 0)
    return pl.pallas_call(
        _s5_param_body,
        grid=(depth,),
        in_specs=[pl.BlockSpec(gs, at3), pl.BlockSpec(gs, at3),
                  pl.BlockSpec((None, S5_GROUPS, 1), at3),
                  pl.BlockSpec(bs, at4), pl.BlockSpec(bs, at4)],
        out_specs=[pl.BlockSpec(gs, at3), pl.BlockSpec(gs, at3),
                   pl.BlockSpec(bs, at4), pl.BlockSpec(bs, at4)],
        out_shape=[jax.ShapeDtypeStruct((depth, S5_GROUPS, S5_STATE), f32),
                   jax.ShapeDtypeStruct((depth, S5_GROUPS, S5_STATE), f32),
                   jax.ShapeDtypeStruct((depth, S5_GROUP, S5_GROUPS, S5_STATE), f32),
                   jax.ShapeDtypeStruct((depth, S5_GROUP, S5_GROUPS, S5_STATE), f32)],
        compiler_params=_cparams(("arbitrary",)),
        name="s5_params",
    )(lam_re, lam_im, log_dt.reshape(depth, S5_GROUPS, 1), b_re, b_im)


def _gelu_tanh(x):
    return 0.5 * x * (1.0 + jnp.tanh(math.sqrt(2.0 / math.pi) * (x + 0.044715 * (x * x * x))))


def _s5_body(u_ref, zs_ref, h0r_ref, h0i_ref, ar_ref, ai_ref, wbu_ref, wcr_ref, wci_ref, d_ref, wglu_ref,
             o_ref, sr_ref, si_ref, xr_scr, xi_scr, y_scr, str_scr, sti_scr, *, tc, n_last):
    c = pl.program_id(1)

    @pl.when(c == 0)
    def _():
        str_scr[...] = h0r_ref[0]
        sti_scr[...] = h0i_ref[0]

    n_ct = S5_SG_STATE // LANES

    def gather_planes(scr, rows):
        return jnp.concatenate([scr[j, rows, :] for j in range(n_ct)], axis=1)

    def scatter_planes(scr, rows, val):
        for j in range(n_ct):
            scr[j, rows, :] = val[:, j * LANES:(j + 1) * LANES]

    for sg in range(S5_SG):
        rows = pl.ds(sg, tc, stride=S5_SG)
        u_sg = u_ref[:, sg * LANES:(sg + 1) * LANES]
        bu = jnp.dot(u_sg.astype(bf16), wbu_ref[sg], preferred_element_type=f32)
        scatter_planes(xr_scr, rows, bu[:, :S5_SG_STATE])
        scatter_planes(xi_scr, rows, bu[:, S5_SG_STATE:])

    ar = ar_ref[...]
    ai = ai_ref[...]

    def step(t, carry):
        xr, xi = carry
        r = pl.ds(pl.multiple_of(t * S5_SG, S5_SG), S5_SG)
        nr = ar * xr - ai * xi + gather_planes(xr_scr, r)
        ni = ar * xi + ai * xr + gather_planes(xi_scr, r)
        scatter_planes(xr_scr, r, nr)
        scatter_planes(xi_scr, r, ni)
        return nr, ni

    xr, xi = lax.fori_loop(0, tc, step, (str_scr[...], sti_scr[...]), unroll=S5_SCAN_UNROLL)
    str_scr[...] = xr
    sti_scr[...] = xi

    for sg in range(S5_SG):
        rows = pl.ds(sg, tc, stride=S5_SG)
        cs = slice(sg * LANES, (sg + 1) * LANES)
        y = (jnp.dot(gather_planes(xr_scr, rows).astype(bf16), wcr_ref[sg], preferred_element_type=f32)
             - jnp.dot(gather_planes(xi_scr, rows).astype(bf16), wci_ref[sg], preferred_element_type=f32)
             + d_ref[:, cs] * u_ref[:, cs])
        y_scr[:, cs] = _gelu_tanh(y)

    y = y_scr[...]
    gate = _sigmoid(jnp.dot(y.astype(bf16), wglu_ref[...], preferred_element_type=f32))
    o_ref[...] = (y * gate * zs_ref[...].astype(f32)).astype(o_ref.dtype)

    @pl.when(c == pl.num_programs(1) - 1)
    def _():
        last = slice((n_last - 1) * S5_SG, n_last * S5_SG)
        sr_ref[0] = gather_planes(xr_scr, last)
        si_ref[0] = gather_planes(xi_scr, last)


def _s5(u, zg, h0_re, h0_im, prm, layer, nb, seq, n_real):
    tc = min(seq, 256)
    nch = seq // tc
    n_last = n_real - (nch - 1) * tc
    zs_col = ATTN_WIDTH // S5_WIDTH
    fix2 = lambda b, c: (0, 0)
    fix3 = lambda b, c: (0, 0, 0)
    st = lambda b, c: (b, 0, 0)
    out, s_re, s_im = pl.pallas_call(
        functools.partial(_s5_body, tc=tc, n_last=n_last),
        grid=(nb, nch),
        in_specs=[pl.BlockSpec((tc, S5_WIDTH), lambda b, c: (b * nch + c, 0)),
                  pl.BlockSpec((tc, S5_WIDTH), lambda b, c: (b * nch + c, zs_col)),
                  pl.BlockSpec((1, S5_SG, S5_SG_STATE), st),
                  pl.BlockSpec((1, S5_SG, S5_SG_STATE), st),
                  _layer_spec((S5_SG, S5_SG_STATE), layer, fix2),
                  _layer_spec((S5_SG, S5_SG_STATE), layer, fix2),
                  _layer_spec((S5_SG, LANES, 2 * S5_SG_STATE), layer, fix3),
                  _layer_spec((S5_SG, S5_SG_STATE, LANES), layer, fix3),
                  _layer_spec((S5_SG, S5_SG_STATE, LANES), layer, fix3),
                  _layer_spec((1, S5_WIDTH), layer, fix2),
                  _layer_spec((S5_WIDTH, S5_WIDTH), layer, fix2)],
        out_specs=[pl.BlockSpec((tc, S5_WIDTH), lambda b, c: (b * nch + c, 0)),
                   pl.BlockSpec((1, S5_SG, S5_SG_STATE), st),
                   pl.BlockSpec((1, S5_SG, S5_SG_STATE), st)],
        out_shape=[jax.ShapeDtypeStruct((nb * seq, S5_WIDTH), bf16),
                   jax.ShapeDtypeStruct((nb, S5_SG, S5_SG_STATE), f32),
                   jax.ShapeDtypeStruct((nb, S5_SG, S5_SG_STATE), f32)],
        scratch_shapes=[pltpu.VMEM((S5_SG_STATE // LANES, tc * S5_SG, LANES), f32),
                        pltpu.VMEM((S5_SG_STATE // LANES, tc * S5_SG, LANES), f32),
                        pltpu.VMEM((tc, S5_WIDTH), f32),
                        pltpu.VMEM((S5_SG, S5_SG_STATE), f32),
                        pltpu.VMEM((S5_SG, S5_SG_STATE), f32)],
        compiler_params=_cparams(("parallel", "arbitrary")),
        name="s5_scan",
    )(u, zg, h0_re.reshape(nb, S5_SG, S5_SG_STATE), h0_im.reshape(nb, S5_SG, S5_SG_STATE),
      prm["a_re"], prm["a_im"], prm["w_bu"], prm["w_c_re"], prm["w_c_im"], prm["d"], prm["w_glu"])
    return out, s_re.reshape(nb, S5_GROUPS, S5_STATE), s_im.reshape(nb, S5_GROUPS, S5_STATE)


def _merge_body(a_ref, s_ref, wa_ref, ws_ref, ga_ref, gs_ref, o_ref):
    o_a = jnp.dot(a_ref[...], wa_ref[...], preferred_element_type=f32)
    o_s = jnp.dot(s_ref[...], ws_ref[...], preferred_element_type=f32)
    o_ref[...] = (ga_ref[...].astype(f32) * o_a + gs_ref[...].astype(f32) * o_s).astype(o_ref.dtype)


def _merge(a_in, s_in, w_a, w_s, zg, layer):
    m = a_in.shape[0]
    tm = _row_tile(m, MM_ROW_TILE)
    tn = MM_COL_TILE
    ga0 = (ATTN_WIDTH + S5_WIDTH) // tn
    gs0 = (ATTN_WIDTH + S5_WIDTH + D_MODEL) // tn
    return pl.pallas_call(
        _merge_body,
        grid=(m // tm, D_MODEL // tn),
        in_specs=[pl.BlockSpec((tm, ATTN_WIDTH), lambda i, j: (i, 0)),
                  pl.BlockSpec((tm, S5_WIDTH), lambda i, j: (i, 0)),
                  _layer_spec((ATTN_WIDTH, tn), layer, lambda i, j: (0, j)),
                  _layer_spec((S5_WIDTH, tn), layer, lambda i, j: (0, j)),
                  pl.BlockSpec((tm, tn), lambda i, j: (i, ga0 + j)),
                  pl.BlockSpec((tm, tn), lambda i, j: (i, gs0 + j))],
        out_specs=pl.BlockSpec((tm, tn), lambda i, j: (i, j)),
        out_shape=jax.ShapeDtypeStruct((m, D_MODEL), bf16),
        compiler_params=_cparams(("parallel", "parallel")),
        name="merge",
    )(a_in, s_in, w_a, w_s, zg, zg)


def _out_body(x_ref, m_ref, w_ref, o_ref):
    o_ref[...] = x_ref[...] + jnp.dot(m_ref[...], w_ref[...], preferred_element_type=f32)


def _out_proj(x, merged, w_out, layer):
    m = x.shape[0]
    tm = _row_tile(m, MM_ROW_TILE)
    tn = MM_COL_TILE
    return pl.pallas_call(
        _out_body,
        grid=(m // tm, D_MODEL // tn),
        in_specs=[pl.BlockSpec((tm, tn), lambda i, j: (i, j)),
                  pl.BlockSpec((tm, D_MODEL), lambda i, j: (i, 0)),
                  _layer_spec((D_MODEL, tn), layer, lambda i, j: (0, j))],
        out_specs=pl.BlockSpec((tm, tn), lambda i, j: (i, j)),
        out_shape=jax.ShapeDtypeStruct((m, D_MODEL), f32),
        compiler_params=_cparams(("parallel", "parallel")),
        name="out_proj",
    )(x, merged, w_out)


def _rope_tables(pos):
    posf = pos.astype(f32)[:, None]
    half = HEAD_DIM // 2
    inv = ROPE_THETA ** (-jnp.arange(half, dtype=f32) / half)
    ang = posf * inv[None, :]
    cos, sin = jnp.cos(ang), jnp.sin(ang)
    cos_h = jnp.concatenate([cos, cos], axis=1)
    sin_h = jnp.concatenate([-sin, sin], axis=1)
    half_i = IDX_DIM // 2
    inv_i = ROPE_THETA ** (-jnp.arange(half_i, dtype=f32) / half_i)
    ang_i = posf * inv_i[None, :]
    cos_i, sin_i = jnp.cos(ang_i), jnp.sin(ang_i)
    zeros = jnp.zeros_like(cos_i)
    pad = jnp.zeros((pos.shape[0], LANES - IDX_DIM), f32)
    cos_k = jnp.concatenate([cos_i, cos_i, pad], axis=1)
    sin_a = jnp.concatenate([-sin_i, zeros, pad], axis=1)
    sin_b = jnp.concatenate([zeros, sin_i, pad], axis=1)
    return dict(cos_h=cos_h, sin_h=sin_h, cos_k=cos_k, sin_a=sin_a, sin_b=sin_b,
                cos_ht=cos.T, sin_ht=sin.T, cos_t=cos_i.T, sin_t=sin_i.T)


def _in_proj_offsets():
    offs = [0]
    for s in IN_SIZES:
        offs.append(offs[-1] + s)
    return offs


def _repack_body(w_ref, wq_ref, wkv_ref, wqiw_ref, wki_ref, wu_ref, wzg_ref):
    offs = _in_proj_offsets()
    seg = lambda k: w_ref[offs[k]:offs[k + 1], :].astype(bf16)
    cols = w_ref.shape[1]
    wq_ref[...] = seg(0)
    wkv_ref[...] = w_ref[offs[1]:offs[3], :].astype(bf16)
    n_qi = IDX_HEADS * IDX_DIM
    wqiw_ref[:n_qi, :] = seg(3)
    wqiw_ref[n_qi:, :] = seg(5)
    wki_ref[:IDX_DIM, :] = seg(4)
    wki_ref[IDX_DIM:, :] = jnp.zeros((LANES - IDX_DIM, cols), bf16)
    wu_ref[...] = seg(7)
    row = 0
    for k in (6, 8, 9, 10):
        wzg_ref[row:row + IN_SIZES[k], :] = seg(k)
        row += IN_SIZES[k]


def _repack_in_proj(w_in_t):
    depth, n_in, d = w_in_t.shape
    cols = 256
    n_qiw = IDX_HEADS * IDX_DIM + IDX_HEADS
    n_zg = ATTN_WIDTH + S5_WIDTH + 2 * D_MODEL
    heights = (ATTN_WIDTH, 2 * KV_WIDTH, n_qiw, LANES, S5_WIDTH, n_zg)
    at = lambda l, c: (l, 0, c)
    return pl.pallas_call(
        _repack_body,
        grid=(depth, d // cols),
        in_specs=[pl.BlockSpec((None, n_in, cols), at)],
        out_specs=[pl.BlockSpec((None, hgt, cols), at) for hgt in heights],
        out_shape=[jax.ShapeDtypeStruct((depth, hgt, d), bf16) for hgt in heights],
        compiler_params=_cparams(("parallel", "parallel")),
        name="repack_in_proj",
    )(w_in_t)


def _pack_weights(w_in, w_glu, w_br_attn, w_br_s5, w_out, norm_gain, q_norm_gain, k_norm_gain):
    depth = w_in.shape[0]
    wt_q, wt_kv, wt_qi, wt_ki, wt_u, wt_zg = _repack_in_proj(jnp.swapaxes(w_in, 1, 2))
    return dict(
        wt_q=wt_q, wt_kv=wt_kv, wt_qi=wt_qi, wt_ki=wt_ki, wt_u=wt_u, wt_zg=wt_zg,
        w_glu=w_glu.astype(bf16), w_br_attn=w_br_attn.astype(bf16),
        w_br_s5=w_br_s5.astype(bf16), w_out=w_out.astype(bf16),
        norm_gain=norm_gain.reshape(depth, 1, D_MODEL),
        q_gain=q_norm_gain.reshape(depth, HEAD_DIM, 1),
        k_gain=k_norm_gain.reshape(depth, 1, HEAD_DIM))


def _block_diag(blocks):
    depth, sg, gg, r, c = blocks.shape
    eye = jnp.eye(gg, dtype=blocks.dtype)
    return jnp.einsum("lsgrc,gk->lsgrkc", blocks, eye).reshape(depth, sg, gg * r, gg * c)


def _s5_stacked_params(lam_re, lam_im, log_dt, b_re, b_im, c_re, c_im, d, w_glu_bf16):
    depth = lam_re.shape[0]
    a_re, a_im, bb_re, bb_im = _s5_params(lam_re, lam_im, log_dt,
                                          jnp.transpose(b_re, (0, 3, 1, 2)), jnp.transpose(b_im, (0, 3, 1, 2)))
    def bu_blocks(bb):
        return jnp.transpose(bb, (0, 2, 1, 3)).reshape(depth, S5_SG, S5_SG_GROUPS, S5_GROUP, S5_STATE)
    w_bu = jnp.concatenate([_block_diag(bu_blocks(bb_re)), _block_diag(bu_blocks(bb_im))], axis=3)
    def c_blocks(cm):
        return jnp.transpose(cm, (0, 1, 3, 2)).reshape(depth, S5_SG, S5_SG_GROUPS, S5_STATE, S5_GROUP)
    return dict(a_re=a_re.reshape(depth, S5_SG, S5_SG_STATE), a_im=a_im.reshape(depth, S5_SG, S5_SG_STATE),
                w_bu=w_bu.astype(bf16),
                w_c_re=_block_diag(c_blocks(c_re)).astype(bf16),
                w_c_im=_block_diag(c_blocks(c_im)).astype(bf16),
                d=d.reshape(depth, 1, S5_WIDTH), w_glu=w_glu_bf16)


def _projections(x, wts, tabs, layer):
    h = _rmsnorm(x, wts["norm_gain"], layer)
    qt = _proj_qt(h, wts["wt_q"], wts["q_gain"], tabs["cos_ht"], tabs["sin_ht"], layer)
    k, v, kb, vt = _proj_kv(h, wts["wt_kv"], wts["k_gain"], tabs["cos_h"], tabs["sin_h"], layer)
    qit, wt = _proj_qit(h, wts["wt_qi"], tabs["cos_t"], tabs["sin_t"], layer)
    ki, kib = _proj_ki(h, wts["wt_ki"], tabs["cos_k"], tabs["sin_a"], tabs["sin_b"], layer)
    u = _proj_plain(h, wts["wt_u"], f32, layer)
    zg = _proj_zg(h, wts["wt_zg"], layer)
    return dict(qt=qt, k=k, v=v, kb=kb, vt=vt, qit=qit, wt=wt, ki=ki, kib=kib, u=u, zg=zg)


def _finish(x, a_in, s_in, zg, wts, layer):
    merged = _merge(a_in, s_in, wts["w_br_attn"], wts["w_br_s5"], zg, layer)
    return _out_proj(x, merged, wts["w_out"], layer)


def kernel(x_prompt, x_sample, cache_k, cache_v, cache_kidx, state_s5_re, state_s5_im, page_table, norm_gain, w_in, q_norm_gain, k_norm_gain, s5_lam_re, s5_lam_im, s5_log_dt, s5_b_re, s5_b_im, s5_c_re, s5_c_im, s5_d, w_glu, w_br_attn, w_br_s5, w_out):
    depth = w_in.shape[0]
    b_p, t_p = x_prompt.shape[:2]
    b_s, t_s = x_sample.shape[:2]
    tok = SAMPLE_ROWS
    assert t_s <= tok
    n_pages = page_table.shape[1]
    past = n_pages * PAGE_SIZE
    n_phys = cache_k.shape[1]

    tabs_p = _rope_tables(jnp.tile(jnp.arange(t_p, dtype=i32), b_p))
    tabs_s = _rope_tables(jnp.tile(past + jnp.arange(tok, dtype=i32), b_s))

    xp = x_prompt.reshape(b_p * t_p, D_MODEL)
    xs = jnp.pad(x_sample, ((0, 0), (0, tok - t_s), (0, 0))).reshape(b_s * tok, D_MODEL)

    cache_k4 = cache_k.reshape(depth, n_phys, PAGE_SIZE * N_KV_HEADS, HEAD_DIM)
    cache_v4 = cache_v.reshape(depth, n_phys, PAGE_SIZE * N_KV_HEADS, HEAD_DIM)
    cache_kidx_t = jnp.swapaxes(cache_kidx, 2, 3)
    zeros_state = jnp.zeros((b_p, S5_GROUPS, S5_STATE), f32)

    def new_page(a):
        w = a.shape[-1]
        return jnp.pad(a.reshape(b_s, tok, w), ((0, 0), (0, PAGE_SIZE - tok), (0, 0)))

    wts = _pack_weights(w_in, w_glu, w_br_attn, w_br_s5, w_out, norm_gain, q_norm_gain, k_norm_gain)
    s5p = _s5_stacked_params(s5_lam_re, s5_lam_im, s5_log_dt, s5_b_re, s5_b_im, s5_c_re, s5_c_im, s5_d,
                             wts["w_glu"])

    outs_p, outs_s = [], []
    for l in range(depth):
        pp = _projections(xp, wts, tabs_p, l)
        a_in = _attn_prompt(pp["qt"], pp["qit"], pp["wt"], pp["zg"], pp["kb"], pp["vt"], pp["kib"], b_p, t_p)
        s_in, sr_p, si_p = _s5(pp["u"], pp["zg"], zeros_state, zeros_state, s5p, l, b_p, t_p, t_p)
        xp = _finish(xp, a_in, s_in, pp["zg"], wts, l)
        outs_p.append((pp["k"].reshape(b_p, t_p, N_KV_HEADS, HEAD_DIM),
                       pp["v"].reshape(b_p, t_p, N_KV_HEADS, HEAD_DIM),
                       pp["ki"].reshape(b_p, t_p, IDX_DIM), sr_p, si_p))

        ps = _projections(xs, wts, tabs_s, l)
        qi_rows = jnp.transpose(ps["qit"].reshape(IDX_HEADS, IDX_DIM, b_s, tok), (2, 0, 3, 1)
                                ).reshape(b_s, IDX_HEADS * tok, IDX_DIM)
        w_rows = jnp.transpose(ps["wt"].reshape(IDX_HEADS, b_s, tok), (1, 0, 2)
                               ).reshape(b_s, IDX_HEADS * tok, 1)
        keys, thr = _idx_sample(page_table, qi_rows, w_rows, cache_kidx_t, l,
                                jnp.swapaxes(new_page(ps["ki"]), 1, 2), t_s)
        q_rows = jnp.transpose(ps["qt"].reshape(N_HEADS, HEAD_DIM, b_s, tok), (2, 0, 3, 1)
                               ).reshape(b_s, N_HEADS * tok, HEAD_DIM)
        k_s = ps["k"].reshape(b_s * tok, KV_WIDTH)
        v_s = ps["v"].reshape(b_s * tok, KV_WIDTH)
        o_rows = _attn_sample(page_table, q_rows, keys, thr, cache_k4, cache_v4, l,
                              new_page(k_s), new_page(v_s))
        attn_s = jnp.transpose(o_rows.reshape(b_s, N_HEADS, tok, HEAD_DIM), (0, 2, 1, 3)
                               ).reshape(b_s * tok, ATTN_WIDTH)
        a_in_s = _gate_mul(attn_s, ps["zg"][:, :ATTN_WIDTH])
        s_in_s, sr_s, si_s = _s5(ps["u"], ps["zg"], state_s5_re[l], state_s5_im[l], s5p, l, b_s, tok, t_s)
        xs = _finish(xs, a_in_s, s_in_s, ps["zg"], wts, l)
        real = lambda a: a.reshape((b_s, tok) + a.shape[1:])[:, :t_s]
        outs_s.append((real(k_s).reshape(b_s, t_s, N_KV_HEADS, HEAD_DIM),
                       real(v_s).reshape(b_s, t_s, N_KV_HEADS, HEAD_DIM),
                       real(ps["ki"]), sr_s, si_s))

    k_prompt, v_prompt, kidx_prompt, s5_re_prompt, s5_im_prompt = [jnp.stack(a) for a in zip(*outs_p)]
    k_sample, v_sample, kidx_sample, s5_re_sample, s5_im_sample = [jnp.stack(a) for a in zip(*outs_s)]
    y_prompt = xp.reshape(b_p, t_p, D_MODEL)
    y_sample = xs.reshape(b_s, tok, D_MODEL)[:, :t_s]
    return (y_prompt, y_sample, k_prompt, v_prompt, kidx_prompt, s5_re_prompt, s5_im_prompt,
            k_sample, v_sample, kidx_sample, s5_re_sample, s5_im_sample)
```

```python
import functools
import math

import jax
import jax.numpy as jnp
from jax import lax
from jax.experimental import pallas as pl
from jax.experimental.pallas import tpu as pltpu

D_MODEL = 2048
PAGE_SIZE = 128
N_HEADS = 16
HEAD_DIM = 128
N_KV_HEADS = 4
HEADS_PER_KV = N_HEADS // N_KV_HEADS
ATTN_WIDTH = N_HEADS * HEAD_DIM
KV_WIDTH = N_KV_HEADS * HEAD_DIM
IDX_HEADS = 16
IDX_DIM = 64
TOPK_MAX = 256
S5_WIDTH = D_MODEL // 2
S5_GROUP = 16
S5_GROUPS = S5_WIDTH // S5_GROUP
S5_STATE = 64
ROPE_THETA = 10000.0
EPS = 1e-6
IN_SIZES = (ATTN_WIDTH, KV_WIDTH, KV_WIDTH, IDX_HEADS * IDX_DIM, IDX_DIM, IDX_HEADS,
            ATTN_WIDTH, S5_WIDTH, S5_WIDTH, D_MODEL, D_MODEL)

LANES = 128
SUBLANES = 8
PACKED_SUBLANES = 16
VMEM_LIMIT_BYTES = 56 * 1024 * 1024
MM_ROW_TILE = 1024
MM_COL_TILE = 1024

S5_SG = S5_WIDTH // LANES
S5_SG_GROUPS = S5_GROUPS // S5_SG
S5_SG_STATE = S5_SG_GROUPS * S5_STATE
S5_SCAN_UNROLL = 8

SAMPLE_ROWS = 16

ATTN_Q_TILE = 256
ATTN_ROW_BLOCK = 64
COUNT_ROWS = 32
PAGES_PER_STEP = 32
IDX_PAGES_PER_STEP = 64
LOG2_E = math.log2(math.e)

INT_MIN = -2 ** 31
MASK_BIAS = -1e30

f32 = jnp.float32
bf16 = jnp.bfloat16
i32 = jnp.int32


def _cparams(sem):
    return pltpu.CompilerParams(dimension_semantics=sem, vmem_limit_bytes=VMEM_LIMIT_BYTES)


def _row_tile(m, cap):
    return m if m <= cap else cap


def _norm_body(x_ref, g_ref, o_ref):
    x = x_ref[...]
    ms = jnp.mean(x * x, axis=-1, keepdims=True)
    o_ref[...] = (x * lax.rsqrt(ms + EPS) * g_ref[...]).astype(o_ref.dtype)


def _layer_spec(block, layer, index_map):
    return pl.BlockSpec((None,) + tuple(block), lambda *idx: (layer,) + tuple(index_map(*idx)))


def _rmsnorm(x, gains, layer):
    m, d = x.shape
    tm = _row_tile(m, 512)
    return pl.pallas_call(
        _norm_body,
        grid=(m // tm,),
        in_specs=[pl.BlockSpec((tm, d), lambda i: (i, 0)),
                  _layer_spec((1, d), layer, lambda i: (0, 0))],
        out_specs=pl.BlockSpec((tm, d), lambda i: (i, 0)),
        out_shape=jax.ShapeDtypeStruct((m, d), bf16),
        compiler_params=_cparams(("parallel",)),
        name="rmsnorm",
    )(x, gains)


def _head_norm_rope(x, gain, cos, sin):
    ms = jnp.mean(x * x, axis=-1, keepdims=True)
    y = x * lax.rsqrt(ms + EPS) * gain
    return y * cos + pltpu.roll(y, HEAD_DIM // 2, 1) * sin


def _qt_body(wt_ref, h_ref, g_ref, cos_ref, sin_ref, o_ref, *, heads):
    acc = lax.dot_general(wt_ref[...], h_ref[...], (((1,), (1,)), ((), ())),
                          preferred_element_type=f32)
    c = cos_ref[...]
    s = sin_ref[...]
    g = g_ref[...]
    half = HEAD_DIM // 2
    scale = HEAD_DIM ** -0.5 * LOG2_E
    for hh in range(heads):
        r0 = hh * HEAD_DIM
        x = acc[r0:r0 + HEAD_DIM]
        ms = jnp.mean(x * x, axis=0, keepdims=True)
        y = x * lax.rsqrt(ms + EPS) * g
        x1 = y[:half]
        x2 = y[half:]
        o_ref[r0:r0 + half, :] = ((x1 * c - x2 * s) * scale).astype(o_ref.dtype)
        o_ref[r0 + half:r0 + HEAD_DIM, :] = ((x2 * c + x1 * s) * scale).astype(o_ref.dtype)


def _proj_qt(h, wt_q, gain, cos_ht, sin_ht, layer):
    m, d = h.shape
    tm = _row_tile(m, MM_ROW_TILE)
    tn = 4 * HEAD_DIM
    return pl.pallas_call(
        functools.partial(_qt_body, heads=tn // HEAD_DIM),
        grid=(m // tm, ATTN_WIDTH // tn),
        in_specs=[_layer_spec((tn, d), layer, lambda i, j: (j, 0)),
                  pl.BlockSpec((tm, d), lambda i, j: (i, 0)),
                  _layer_spec((HEAD_DIM, 1), layer, lambda i, j: (0, 0)),
                  pl.BlockSpec((HEAD_DIM // 2, tm), lambda i, j: (0, i)),
                  pl.BlockSpec((HEAD_DIM // 2, tm), lambda i, j: (0, i))],
        out_specs=pl.BlockSpec((tn, tm), lambda i, j: (j, i)),
        out_shape=jax.ShapeDtypeStruct((ATTN_WIDTH, m), bf16),
        compiler_params=_cparams(("parallel", "parallel")),
        name="proj_q_t",
    )(wt_q, h, gain, cos_ht, sin_ht)


def _dot_nt(a, b_t):
    return lax.dot_general(a, b_t, (((1,), (1,)), ((), ())), preferred_element_type=f32)


def _kv_body(h_ref, wt_ref, g_ref, cos_ref, sin_ref, k_ref, v_ref, kb_ref, vt_ref, *, chunk):
    h = h_ref[...]
    acc = _dot_nt(h, wt_ref[...])
    cos = cos_ref[...]
    sin = sin_ref[...]
    g = g_ref[...]
    tm = h.shape[0]
    for hh in range(N_KV_HEADS):
        sl = slice(hh * HEAD_DIM, (hh + 1) * HEAD_DIM)
        kh = _head_norm_rope(acc[:, sl], g, cos, sin)
        rows = pl.ds(hh, tm, stride=N_KV_HEADS)
        k_ref[rows, :] = kh
        v_ref[rows, :] = acc[:, KV_WIDTH + hh * HEAD_DIM:KV_WIDTH + (hh + 1) * HEAD_DIM]
        kb_ref[:, sl] = kh.astype(bf16)
    vt = _dot_nt(wt_ref[KV_WIDTH:, :], h)
    for cc in range(vt_ref.shape[0]):
        vt_ref[cc] = vt[:, cc * chunk:(cc + 1) * chunk].astype(bf16)


def _proj_kv(h, wt_kv, gain, cos, sin, layer):
    m, d = h.shape
    tm = _row_tile(m, 512)
    chunk = min(tm, ATTN_Q_TILE)
    row = lambda i: (i, 0)
    fix = lambda i: (0, 0)
    return pl.pallas_call(
        functools.partial(_kv_body, chunk=chunk),
        grid=(m // tm,),
        in_specs=[pl.BlockSpec((tm, d), row),
                  _layer_spec((2 * KV_WIDTH, d), layer, fix),
                  _layer_spec((1, HEAD_DIM), layer, fix),
                  pl.BlockSpec((tm, HEAD_DIM), row),
                  pl.BlockSpec((tm, HEAD_DIM), row)],
        out_specs=[pl.BlockSpec((tm * N_KV_HEADS, HEAD_DIM), row),
                   pl.BlockSpec((tm * N_KV_HEADS, HEAD_DIM), row),
                   pl.BlockSpec((tm, KV_WIDTH), row),
                   pl.BlockSpec((tm // chunk, KV_WIDTH, chunk), lambda i: (i, 0, 0))],
        out_shape=[jax.ShapeDtypeStruct((m * N_KV_HEADS, HEAD_DIM), f32),
                   jax.ShapeDtypeStruct((m * N_KV_HEADS, HEAD_DIM), f32),
                   jax.ShapeDtypeStruct((m, KV_WIDTH), bf16),
                   jax.ShapeDtypeStruct((m // chunk, KV_WIDTH, chunk), bf16)],
        compiler_params=_cparams(("parallel",)),
        name="proj_kv",
    )(h, wt_kv, gain, cos, sin)


def _qit_body(wt_ref, h_ref, cos_ref, sin_ref, qit_ref, wt_out_ref):
    acc = lax.dot_general(wt_ref[...], h_ref[...], (((1,), (1,)), ((), ())),
                          preferred_element_type=f32)
    c = cos_ref[...]
    s = sin_ref[...]
    half = IDX_DIM // 2
    scale = IDX_DIM ** -0.5
    for hh in range(IDX_HEADS):
        r0 = hh * IDX_DIM
        x1 = acc[r0:r0 + half]
        x2 = acc[r0 + half:r0 + IDX_DIM]
        qit_ref[r0:r0 + half, :] = ((x1 * c - x2 * s) * scale).astype(qit_ref.dtype)
        qit_ref[r0 + half:r0 + IDX_DIM, :] = ((x2 * c + x1 * s) * scale).astype(qit_ref.dtype)
    n_qi = IDX_HEADS * IDX_DIM
    wt_out_ref[...] = acc[n_qi:n_qi + IDX_HEADS] * (IDX_HEADS ** -0.5)


def _proj_qit(h, wt_qi, cos_t, sin_t, layer):
    m, d = h.shape
    tm = _row_tile(m, 512)
    n_rows = wt_qi.shape[1]
    n_qi = IDX_HEADS * IDX_DIM
    return pl.pallas_call(
        _qit_body,
        grid=(m // tm,),
        in_specs=[_layer_spec((n_rows, d), layer, lambda i: (0, 0)),
                  pl.BlockSpec((tm, d), lambda i: (i, 0)),
                  pl.BlockSpec((IDX_DIM // 2, tm), lambda i: (0, i)),
                  pl.BlockSpec((IDX_DIM // 2, tm), lambda i: (0, i))],
        out_specs=[pl.BlockSpec((n_qi, tm), lambda i: (0, i)),
                   pl.BlockSpec((IDX_HEADS, tm), lambda i: (0, i))],
        out_shape=[jax.ShapeDtypeStruct((n_qi, m), bf16),
                   jax.ShapeDtypeStruct((IDX_HEADS, m), f32)],
        compiler_params=_cparams(("parallel",)),
        name="proj_qi_t",
    )(wt_qi, h, cos_t, sin_t)


def _ki_body(h_ref, w_ref, cos_ref, sina_ref, sinb_ref, ki_ref, kib_ref):
    x = _dot_nt(h_ref[...], w_ref[...])
    half = IDX_DIM // 2
    r = (x * cos_ref[...] + pltpu.roll(x, LANES - half, 1) * sina_ref[...]
         + pltpu.roll(x, half, 1) * sinb_ref[...])
    ki = r[:, :IDX_DIM]
    ki_ref[...] = ki
    kib_ref[...] = ki.astype(bf16)


def _proj_ki(h, w_ki, cos_k, sin_a, sin_b, layer):
    m, d = h.shape
    tm = _row_tile(m, 512)
    row = lambda i: (i, 0)
    return pl.pallas_call(
        _ki_body,
        grid=(m // tm,),
        in_specs=[pl.BlockSpec((tm, d), row),
                  _layer_spec((LANES, d), layer, lambda i: (0, 0)),
                  pl.BlockSpec((tm, LANES), row),
                  pl.BlockSpec((tm, LANES), row),
                  pl.BlockSpec((tm, LANES), row)],
        out_specs=[pl.BlockSpec((tm, IDX_DIM), row)] * 2,
        out_shape=[jax.ShapeDtypeStruct((m, IDX_DIM), f32),
                   jax.ShapeDtypeStruct((m, IDX_DIM), bf16)],
        compiler_params=_cparams(("parallel",)),
        name="proj_ki",
    )(h, w_ki, cos_k, sin_a, sin_b)


def _plain_body(h_ref, w_ref, o_ref):
    o_ref[...] = _dot_nt(h_ref[...], w_ref[...]).astype(o_ref.dtype)


def _proj_plain(h, wt, out_dtype, layer):
    m, d = h.shape
    n = wt.shape[1]
    tm = _row_tile(m, MM_ROW_TILE)
    tn = MM_COL_TILE
    return pl.pallas_call(
        _plain_body,
        grid=(m // tm, n // tn),
        in_specs=[pl.BlockSpec((tm, d), lambda i, j: (i, 0)),
                  _layer_spec((tn, d), layer, lambda i, j: (j, 0))],
        out_specs=pl.BlockSpec((tm, tn), lambda i, j: (i, j)),
        out_shape=jax.ShapeDtypeStruct((m, n), out_dtype),
        compiler_params=_cparams(("parallel", "parallel")),
        name="proj_u",
    )(h, wt)


def _sigmoid(x):
    return 1.0 / (1.0 + jnp.exp(-x))


def _zg_body(h_ref, w_ref, o_ref, *, silu_tiles):
    acc = _dot_nt(h_ref[...], w_ref[...])
    sg = _sigmoid(acc)
    is_silu = pl.program_id(1) < silu_tiles
    o_ref[...] = jnp.where(is_silu, acc * sg, sg).astype(o_ref.dtype)


def _proj_zg(h, wt_zg, layer):
    m, d = h.shape
    n = wt_zg.shape[1]
    tm = _row_tile(m, MM_ROW_TILE)
    tn = MM_COL_TILE
    return pl.pallas_call(
        functools.partial(_zg_body, silu_tiles=(ATTN_WIDTH + S5_WIDTH) // tn),
        grid=(m // tm, n // tn),
        in_specs=[pl.BlockSpec((tm, d), lambda i, j: (i, 0)),
                  _layer_spec((tn, d), layer, lambda i, j: (j, 0))],
        out_specs=pl.BlockSpec((tm, tn), lambda i, j: (i, j)),
        out_shape=jax.ShapeDtypeStruct((m, n), bf16),
        compiler_params=_cparams(("parallel", "parallel")),
        name="proj_zg",
    )(h, wt_zg)


def _sortable_key(score):
    b = pltpu.bitcast(score, i32)
    return b ^ ((b >> 31) & jnp.int32(0x7FFFFFFF))


def _kth_largest(count_ge, shape, k):
    zero = jnp.zeros(shape, i32)
    prefix = jnp.where(count_ge(zero) >= k, zero, jnp.full(shape, INT_MIN, i32))

    def bit_body(bi, prefix):
        cand = prefix | (jnp.int32(1) << (30 - bi))
        return jnp.where(count_ge(cand) >= k, cand, prefix)

    thr = lax.fori_loop(0, 31, bit_body, prefix)
    return jnp.maximum(thr, jnp.int32(INT_MIN + 1))


def _resolve_ties(thr, k, count_ge, count_tied_before, demote, n_pos_bits, active=None):
    excess = count_ge(thr) - k
    if active is not None:
        excess = jnp.where(active, excess, 0)

    @pl.when(jnp.max(excess) > 0)
    def _():
        need = k - count_ge(thr + 1)

        def bit_body(bi, pos):
            cand = pos | (jnp.int32(1) << (n_pos_bits - 1 - bi))
            return jnp.where(count_tied_before(thr, cand) < need, cand, pos)

        last = lax.fori_loop(0, n_pos_bits, bit_body, jnp.zeros(thr.shape, i32))
        demote(thr, last)


def _sublane_allreduce(x, op):
    for shift in (4, 2, 1):
        x = op(x, pltpu.roll(x, shift, 0))
    return x


def _attn_prompt_body(qt_ref, qit_ref, wt_ref, z_ref, k_ref, vt_ref, ki_ref, o_ref,
                      key_scr, acc_scr, m_scr, l_scr, alpha_scr, s_scr, p_scr, bias_scr,
                      *, tq, k_top, n_pos_bits):
    i = pl.program_id(1)
    n_chunks = i + 1
    half = tq // 2

    def score_keys(c, diagonal):
        for sub in range(2):
            r0 = pl.multiple_of(c * tq + sub * half, half)
            kic = ki_ref[pl.ds(r0, half), :]
            acc = jnp.zeros((half, tq), f32)
            for hh in range(IDX_HEADS):
                lg = jnp.dot(kic, qit_ref[hh * IDX_DIM:(hh + 1) * IDX_DIM, :],
                             preferred_element_type=f32)
                acc = acc + jnp.maximum(lg, 0.0) * wt_ref[hh:hh + 1, :]
            key = _sortable_key(acc)
            if diagonal:
                kpos = lax.broadcasted_iota(i32, (half, tq), 0) + sub * half
                qpos = lax.broadcasted_iota(i32, (half, tq), 1)
                key = jnp.where(kpos <= qpos, key, jnp.int32(INT_MIN))
            key_scr[pl.ds(r0, half), :] = key

    def full_chunk(c, carry):
        score_keys(c, False)
        return carry

    lax.fori_loop(0, i, full_chunk, 0)
    score_keys(i, True)

    @pl.when(n_chunks % 2 == 1)
    def _():
        key_scr[pl.ds(pl.multiple_of(n_chunks * tq, tq), tq), :] = jnp.full((tq, tq), INT_MIN, i32)

    def count_ge(cand):
        def body(c, cnt):
            kk = key_scr[pl.ds(pl.multiple_of(c * 2 * tq, 2 * tq), 2 * tq), :]
            hit = (kk >= cand).astype(i32)
            return cnt + jnp.sum(hit.reshape(2 * tq // COUNT_ROWS, COUNT_ROWS, tq), axis=0)

        cnt = lax.fori_loop(0, (n_chunks + 1) // 2, body, jnp.zeros((COUNT_ROWS, tq), i32))
        return jnp.sum(cnt, axis=0, keepdims=True)

    thr = _kth_largest(count_ge, (1, tq), k_top)

    def chunk_positions(c):
        return lax.broadcasted_iota(i32, (tq, tq), 0) + c * tq

    def count_tied_before(thr, pos_limit):
        def body(c, cnt):
            kk = key_scr[pl.ds(pl.multiple_of(c * tq, tq), tq), :]
            hit = ((kk == thr) & (chunk_positions(c) < pos_limit)).astype(i32)
            return cnt + jnp.sum(hit.reshape(tq // SUBLANES, SUBLANES, tq), axis=0)

        cnt = lax.fori_loop(0, n_chunks, body, jnp.zeros((SUBLANES, tq), i32))
        return jnp.sum(cnt, axis=0, keepdims=True)

    def demote(thr, last):
        def body(c, carry):
            rows = pl.ds(pl.multiple_of(c * tq, tq), tq)
            kk = key_scr[rows, :]
            key_scr[rows, :] = jnp.where((kk == thr) & (chunk_positions(c) > last), thr - 1, kk)
            return carry

        lax.fori_loop(0, n_chunks, body, 0)

    _resolve_ties(thr, k_top, count_ge, count_tied_before, demote, n_pos_bits)

    m_scr[...] = jnp.full(m_scr.shape, -jnp.inf, f32)
    l_scr[...] = jnp.zeros(l_scr.shape, f32)
    acc_scr[...] = jnp.zeros(acc_scr.shape, f32)
    n_sub = tq // SUBLANES
    d_sub = HEAD_DIM // SUBLANES

    def attend_chunk(c, carry):
        r0 = pl.multiple_of(c * tq, tq)
        bias_scr[...] = jnp.where(key_scr[pl.ds(r0, tq), :] >= thr, 0.0, MASK_BIAS)
        for g in range(N_KV_HEADS):
            kc = k_ref[pl.ds(r0, tq), g * HEAD_DIM:(g + 1) * HEAD_DIM]
            for hh in range(HEADS_PER_KV):
                h = g * HEADS_PER_KV + hh
                s = jnp.dot(kc, qt_ref[h * HEAD_DIM:(h + 1) * HEAD_DIM, :], preferred_element_type=f32)
                s_scr[h] = s + bias_scr[...]
        blocks = [slice(j * ATTN_ROW_BLOCK, (j + 1) * ATTN_ROW_BLOCK) for j in range(tq // ATTN_ROW_BLOCK)]
        blk_sub = ATTN_ROW_BLOCK // SUBLANES
        for h in range(N_HEADS):
            mx = m_scr[h]
            for rows in blocks:
                mx = jnp.maximum(mx, jnp.max(s_scr[h, rows, :].reshape(blk_sub, SUBLANES, tq), axis=0))
            m_new = _sublane_allreduce(mx, jnp.maximum)
            alpha_scr[h] = jnp.exp2(m_scr[h] - m_new)
            m_scr[h] = m_new
        for h in range(N_HEADS):
            p3 = jnp.exp2(s_scr[h].reshape(n_sub, SUBLANES, tq) - m_scr[h][None])
            p_scr[h] = p3.reshape(tq, tq).astype(bf16)
        ones_rows = jnp.ones((PACKED_SUBLANES, tq), bf16)
        for g in range(N_KV_HEADS):
            vtc = jnp.concatenate([vt_ref[c, g * HEAD_DIM:(g + 1) * HEAD_DIM, :], ones_rows], axis=0)
            for hh in range(HEADS_PER_KV):
                h = g * HEADS_PER_KV + hh
                hs = slice(h * HEAD_DIM, (h + 1) * HEAD_DIM)
                pv = jnp.dot(vtc, p_scr[h], preferred_element_type=f32)
                alpha = alpha_scr[h]
                acc = acc_scr[hs, :].reshape(d_sub, SUBLANES, tq) * alpha[None]
                acc_scr[hs, :] = acc.reshape(HEAD_DIM, tq) + pv[:HEAD_DIM]
                l_scr[h] = alpha * l_scr[h] + pv[HEAD_DIM:HEAD_DIM + SUBLANES]
        return carry

    lax.fori_loop(0, n_chunks, attend_chunk, 0)

    for h in range(N_HEADS):
        hs = slice(h * HEAD_DIM, (h + 1) * HEAD_DIM)
        out_t = (acc_scr[hs, :].reshape(d_sub, SUBLANES, tq) / l_scr[h][None]).reshape(HEAD_DIM, tq)
        o_ref[:, hs] = (out_t.T * z_ref[:, hs].astype(f32)).astype(o_ref.dtype)


def _attn_prompt(qt, qit, wt, zg, kb, vt, kib, batch, seq):
    tq = ATTN_Q_TILE
    assert seq % tq == 0
    nq = seq // tq
    k_top = min(TOPK_MAX, seq // 4)
    assert k_top <= tq
    qrow = lambda b, i: (b * nq + i, 0)
    qcol = lambda b, i: (0, b * nq + i)
    per_b = lambda b, i: (b, 0)
    return pl.pallas_call(
        functools.partial(_attn_prompt_body, tq=tq, k_top=k_top, n_pos_bits=(seq - 1).bit_length()),
        grid=(batch, nq),
        in_specs=[pl.BlockSpec((ATTN_WIDTH, tq), qcol),
                  pl.BlockSpec((IDX_HEADS * IDX_DIM, tq), qcol),
                  pl.BlockSpec((IDX_HEADS, tq), qcol),
                  pl.BlockSpec((tq, ATTN_WIDTH), qrow),
                  pl.BlockSpec((seq, KV_WIDTH), per_b),
                  pl.BlockSpec((nq, KV_WIDTH, tq), lambda b, i: (b, 0, 0)),
                  pl.BlockSpec((seq, IDX_DIM), per_b)],
        out_specs=pl.BlockSpec((tq, ATTN_WIDTH), qrow),
        out_shape=jax.ShapeDtypeStruct((batch * seq, ATTN_WIDTH), bf16),
        scratch_shapes=[pltpu.VMEM((seq + tq, tq), i32),
                        pltpu.VMEM((ATTN_WIDTH, tq), f32),
                        pltpu.VMEM((N_HEADS, SUBLANES, tq), f32),
                        pltpu.VMEM((N_HEADS, SUBLANES, tq), f32),
                        pltpu.VMEM((N_HEADS, SUBLANES, tq), f32),
                        pltpu.VMEM((N_HEADS, tq, tq), f32),
                        pltpu.VMEM((N_HEADS, tq, tq), bf16),
                        pltpu.VMEM((tq, tq), f32)],
        compiler_params=_cparams(("parallel", "arbitrary")),
        name="attn_prompt",
    )(qt, qit, wt, zg, kb, vt, kib)


def _idx_sample_body(pt_ref, qi_ref, w_ref, *rest, n_pages, n_new, k_top, ppb):
    kid_refs = rest[:ppb]
    kinew_ref, key_ref, thr_ref = rest[ppb:]
    p = pl.program_id(1)
    tok = SAMPLE_ROWS

    def page_keys(ki_page_t):
        lg = jnp.dot(qi_ref[0], ki_page_t.astype(bf16), preferred_element_type=f32)
        wgt = jnp.maximum(lg, 0.0) * w_ref[0]
        score = jnp.sum(wgt.reshape(IDX_HEADS, tok, PAGE_SIZE), axis=0)
        return _sortable_key(score)

    for j in range(ppb):
        key_ref[0, p * ppb + j] = page_keys(kid_refs[j][...])

    @pl.when(p == pl.num_programs(1) - 1)
    def _():
        key = page_keys(kinew_ref[0])
        kpos = lax.broadcasted_iota(i32, (tok, PAGE_SIZE), 1)
        qtok = lax.broadcasted_iota(i32, (tok, PAGE_SIZE), 0)
        key_ref[0, n_pages] = jnp.where((kpos <= qtok) & (kpos < n_new), key, jnp.int32(INT_MIN))

        def count_ge(cand):
            def body(c, cnt):
                hit = (key_ref[0, pl.ds(c * ppb, ppb)] >= cand).astype(i32)
                return cnt + jnp.sum(hit, axis=0)

            cnt = lax.fori_loop(0, n_pages // ppb, body, jnp.zeros((tok, PAGE_SIZE), i32))
            cnt = cnt + (key_ref[0, n_pages] >= cand).astype(i32)
            return jnp.sum(cnt, axis=-1, keepdims=True)

        thr = _kth_largest(count_ge, (tok, 1), k_top)
        thr_ref[0] = jnp.broadcast_to(thr, (tok, PAGE_SIZE))

        def page_positions(c):
            return lax.broadcasted_iota(i32, (tok, PAGE_SIZE), 1) + c * PAGE_SIZE

        def count_tied_before(thr, pos_limit):
            def body(c, cnt):
                return cnt + ((key_ref[0, c] == thr) & (page_positions(c) < pos_limit)).astype(i32)

            cnt = lax.fori_loop(0, n_pages + 1, body, jnp.zeros((tok, PAGE_SIZE), i32))
            return jnp.sum(cnt, axis=-1, keepdims=True)

        def demote(thr, last):
            def body(c, carry):
                kk = key_ref[0, c]
                key_ref[0, c] = jnp.where((kk == thr) & (page_positions(c) > last), thr - 1, kk)
                return carry

            lax.fori_loop(0, n_pages + 1, body, 0)

        real_rows = lax.broadcasted_iota(i32, (tok, 1), 0) < n_new
        _resolve_ties(thr, k_top, count_ge, count_tied_before, demote,
                      ((n_pages + 1) * PAGE_SIZE - 1).bit_length(), active=real_rows)


def _pages_per_step(n_pages, cap=PAGES_PER_STEP):
    ppb = min(cap, n_pages)
    assert n_pages % ppb == 0
    return ppb


def _page_spec(block, layer, ppb, j):
    return pl.BlockSpec(block, lambda b, p, pt: (layer, pt[b, p * ppb + j], 0, 0))


def _idx_sample(page_table, qi_rows, w_rows, cache_kidx, layer, ki_new_pages, n_new):
    nb, n_pages = page_table.shape
    tok = SAMPLE_ROWS
    ppb = _pages_per_step(n_pages, IDX_PAGES_PER_STEP)
    k_top = min(TOPK_MAX, (n_pages * PAGE_SIZE + n_new) // 4)
    per_b = lambda b, p, pt: (b, 0, 0)
    grid_spec = pltpu.PrefetchScalarGridSpec(
        num_scalar_prefetch=1,
        grid=(nb, n_pages // ppb),
        in_specs=[pl.BlockSpec((1, IDX_HEADS * tok, IDX_DIM), per_b),
                  pl.BlockSpec((1, IDX_HEADS * tok, 1), per_b)]
                 + [_page_spec((None, None, IDX_DIM, PAGE_SIZE), layer, ppb, j) for j in range(ppb)]
                 + [pl.BlockSpec((1, IDX_DIM, PAGE_SIZE), per_b)],
        out_specs=[pl.BlockSpec((1, n_pages + 1, tok, PAGE_SIZE), lambda b, p, pt: (b, 0, 0, 0)),
                   pl.BlockSpec((1, tok, PAGE_SIZE), per_b)],
    )
    return pl.pallas_call(
        functools.partial(_idx_sample_body, n_pages=n_pages, n_new=n_new, k_top=k_top, ppb=ppb),
        grid_spec=grid_spec,
        out_shape=[jax.ShapeDtypeStruct((nb, n_pages + 1, tok, PAGE_SIZE), i32),
                   jax.ShapeDtypeStruct((nb, tok, PAGE_SIZE), i32)],
        compiler_params=_cparams(("parallel", "arbitrary")),
        name="idx_sample",
    )(page_table, qi_rows, w_rows, *([cache_kidx] * ppb), ki_new_pages)


def _attn_sample_body(pt_ref, q_ref, key_ref, keyn_ref, thr_ref, *rest, ppb):
    k_refs = rest[:ppb]
    v_refs = rest[ppb:2 * ppb]
    kn_ref, vn_ref, o_ref, acc_scr, m_scr, l_scr = rest[2 * ppb:]
    p = pl.program_id(1)
    tok = SAMPLE_ROWS
    rows_g = HEADS_PER_KV * tok
    thr = thr_ref[0]

    @pl.when(p == 0)
    def _():
        m_scr[...] = jnp.full(m_scr.shape, -jnp.inf, f32)
        l_scr[...] = jnp.zeros(l_scr.shape, f32)
        acc_scr[...] = jnp.zeros(acc_scr.shape, f32)

    def attend(bias_tok, k_of_group, v_of_group):
        bias = jnp.concatenate([bias_tok] * HEADS_PER_KV, axis=0)
        for g in range(N_KV_HEADS):
            rs = slice(g * rows_g, (g + 1) * rows_g)
            s = lax.dot_general(q_ref[0, rs, :], k_of_group(g), (((1,), (1,)), ((), ())),
                                preferred_element_type=f32) + bias
            m_prev = m_scr[rs, :]
            m_new = jnp.maximum(m_prev, jnp.max(s, axis=-1, keepdims=True))
            alpha = jnp.exp2(m_prev - m_new)
            pr = jnp.exp2(s - m_new)
            l_scr[rs, :] = alpha * l_scr[rs, :] + jnp.sum(pr, axis=-1, keepdims=True)
            acc_scr[rs, :] = alpha * acc_scr[rs, :] + jnp.dot(
                pr.astype(bf16), v_of_group(g), preferred_element_type=f32)
            m_scr[rs, :] = m_new

    def mask_bias(keys):
        return jnp.where(keys >= thr, 0.0, MASK_BIAS).astype(f32)

    def cached(refs):
        def of_group(g):
            rows = pl.ds(g, PAGE_SIZE, stride=N_KV_HEADS)
            return jnp.concatenate([r[rows, :].astype(bf16) for r in refs], axis=0)
        return of_group

    attend(jnp.concatenate([mask_bias(key_ref[0, j]) for j in range(ppb)], axis=1),
           cached(k_refs), cached(v_refs))

    @pl.when(p == pl.num_programs(1) - 1)
    def _():
        def fresh(ref):
            return lambda g: ref[0, :, g * HEAD_DIM:(g + 1) * HEAD_DIM].astype(bf16)

        attend(mask_bias(keyn_ref[0, 0]), fresh(kn_ref), fresh(vn_ref))
        o_ref[0] = acc_scr[...] / l_scr[...]


def _attn_sample(page_table, q_rows, keys, thr, cache_k, cache_v, layer, k_new_pages, v_new_pages):
    nb, n_pages = page_table.shape
    tok = SAMPLE_ROWS
    rows = N_HEADS * tok
    ppb = _pages_per_step(n_pages)
    per_b = lambda b, p, pt: (b, 0, 0)
    page_block = (None, None, PAGE_SIZE * N_KV_HEADS, HEAD_DIM)
    grid_spec = pltpu.PrefetchScalarGridSpec(
        num_scalar_prefetch=1,
        grid=(nb, n_pages // ppb),
        in_specs=[pl.BlockSpec((1, rows, HEAD_DIM), per_b),
                  pl.BlockSpec((1, ppb, tok, PAGE_SIZE), lambda b, p, pt: (b, p, 0, 0)),
                  pl.BlockSpec((1, 1, tok, PAGE_SIZE), lambda b, p, pt: (b, n_pages, 0, 0)),
                  pl.BlockSpec((1, tok, PAGE_SIZE), per_b)]
                 + [_page_spec(page_block, layer, ppb, j) for j in range(ppb)]
                 + [_page_spec(page_block, layer, ppb, j) for j in range(ppb)]
                 + [pl.BlockSpec((1, PAGE_SIZE, KV_WIDTH), per_b),
                    pl.BlockSpec((1, PAGE_SIZE, KV_WIDTH), per_b)],
        out_specs=pl.BlockSpec((1, rows, HEAD_DIM), per_b),
        scratch_shapes=[pltpu.VMEM((rows, HEAD_DIM), f32),
                        pltpu.VMEM((rows, 1), f32),
                        pltpu.VMEM((rows, 1), f32)],
    )
    return pl.pallas_call(
        functools.partial(_attn_sample_body, ppb=ppb),
        grid_spec=grid_spec,
        out_shape=jax.ShapeDtypeStruct((nb, rows, HEAD_DIM), f32),
        compiler_params=_cparams(("parallel", "arbitrary")),
        name="attn_sample",
    )(page_table, q_rows, keys, keys, thr, *([cache_k] * ppb), *([cache_v] * ppb),
      k_new_pages, v_new_pages)


def _gate_mul_body(a_ref, z_ref, o_ref):
    o_ref[...] = (a_ref[...] * z_ref[...].astype(f32)).astype(o_ref.dtype)


def _gate_mul(a, zg):
    m, n = a.shape
    return pl.pallas_call(
        _gate_mul_body,
        grid=(1,),
        in_specs=[pl.BlockSpec((m, n), lambda i: (0, 0)),
                  pl.BlockSpec((m, n), lambda i: (0, 0))],
        out_specs=pl.BlockSpec((m, n), lambda i: (0, 0)),
        out_shape=jax.ShapeDtypeStruct((m, n), bf16),
        compiler_params=_cparams(("arbitrary",)),
        name="gate_mul",
    )(a, zg)


def _s5_param_body(lr_ref, li_ref, ldt_ref, bre_ref, bim_ref, ar_ref, ai_ref, bbre_ref, bbim_ref):
    lr = lr_ref[...]
    li = li_ref[...]
    dt = jnp.exp(ldt_ref[...])
    mag = jnp.exp(lr * dt)
    ar = mag * jnp.cos(li * dt)
    ai = mag * jnp.sin(li * dt)
    den = lr * lr + li * li
    nr = ar - 1.0
    cr = (nr * lr + ai * li) / den
    ci = (ai * lr - nr * li) / den
    ar_ref[...] = ar
    ai_ref[...] = ai
    for h in range(S5_GROUP):
        bbre_ref[h] = cr * bre_ref[h] - ci * bim_ref[h]
        bbim_ref[h] = cr * bim_ref[h] + ci * bre_ref[h]


def _s5_params(lam_re, lam_im, log_dt, b_re, b_im):
    depth = lam_re.shape[0]
    gs = (None, S5_GROUPS, S5_STATE)
    bs = (None, S5_GROUP, S5_GROUPS, S5_STATE)
    at3 = lambda l: (l, 0, 0)
    at4 = lambda l: (l, 0, 0, 0)
    return pl.pallas_call(
        _s5_param_body,
        grid=(depth,),
        in_specs=[pl.BlockSpec(gs, at3), pl.BlockSpec(gs, at3),
                  pl.BlockSpec((None, S5_GROUPS, 1), at3),
                  pl.BlockSpec(bs, at4), pl.BlockSpec(bs, at4)],
        out_specs=[pl.BlockSpec(gs, at3), pl.BlockSpec(gs, at3),
                   pl.BlockSpec(bs, at4), pl.BlockSpec(bs, at4)],
        out_shape=[jax.ShapeDtypeStruct((depth, S5_GROUPS, S5_STATE), f32),
                   jax.ShapeDtypeStruct((depth, S5_GROUPS, S5_STATE), f32),
                   jax.ShapeDtypeStruct((depth, S5_GROUP, S5_GROUPS, S5_STATE), f32),
                   jax.ShapeDtypeStruct((depth, S5_GROUP, S5_GROUPS, S5_STATE), f32)],
        compiler_params=_cparams(("arbitrary",)),
        name="s5_params",
    )(lam_re, lam_im, log_dt.reshape(depth, S5_GROUPS, 1), b_re, b_im)


def _gelu_tanh(x):
    return 0.5 * x * (1.0 + jnp.tanh(math.sqrt(2.0 / math.pi) * (x + 0.044715 * (x * x * x))))


def _s5_body(u_ref, zs_ref, h0r_ref, h0i_ref, ar_ref, ai_ref, wbu_ref, wcr_ref, wci_ref, d_ref, wglu_ref,
             o_ref, sr_ref, si_ref, xr_scr, xi_scr, y_scr, str_scr, sti_scr, *, tc, n_last):
    c = pl.program_id(1)

    @pl.when(c == 0)
    def _():
        str_scr[...] = h0r_ref[0]
        sti_scr[...] = h0i_ref[0]

    n_ct = S5_SG_STATE // LANES

    def gather_planes(scr, rows):
        return jnp.concatenate([scr[j, rows, :] for j in range(n_ct)], axis=1)

    def scatter_planes(scr, rows, val):
        for j in range(n_ct):
            scr[j, rows, :] = val[:, j * LANES:(j + 1) * LANES]

    for sg in range(S5_SG):
        rows = pl.ds(sg, tc, stride=S5_SG)
        u_sg = u_ref[:, sg * LANES:(sg + 1) * LANES]
        bu = jnp.dot(u_sg.astype(bf16), wbu_ref[sg], preferred_element_type=f32)
        scatter_planes(xr_scr, rows, bu[:, :S5_SG_STATE])
        scatter_planes(xi_scr, rows, bu[:, S5_SG_STATE:])

    ar = ar_ref[...]
    ai = ai_ref[...]

    def step(t, carry):
        xr, xi = carry
        r = pl.ds(pl.multiple_of(t * S5_SG, S5_SG), S5_SG)
        nr = ar * xr - ai * xi + gather_planes(xr_scr, r)
        ni = ar * xi + ai * xr + gather_planes(xi_scr, r)
        scatter_planes(xr_scr, r, nr)
        scatter_planes(xi_scr, r, ni)
        return nr, ni

    xr, xi = lax.fori_loop(0, tc, step, (str_scr[...], sti_scr[...]), unroll=S5_SCAN_UNROLL)
    str_scr[...] = xr
    sti_scr[...] = xi

    for sg in range(S5_SG):
        rows = pl.ds(sg, tc, stride=S5_SG)
        cs = slice(sg * LANES, (sg + 1) * LANES)
        y = (jnp.dot(gather_planes(xr_scr, rows).astype(bf16), wcr_ref[sg], preferred_element_type=f32)
             - jnp.dot(gather_planes(xi_scr, rows).astype(bf16), wci_ref[sg], preferred_element_type=f32)
             + d_ref[:, cs] * u_ref[:, cs])
        y_scr[:, cs] = _gelu_tanh(y)

    y = y_scr[...]
    gate = _sigmoid(jnp.dot(y.astype(bf16), wglu_ref[...], preferred_element_type=f32))
    o_ref[...] = (y * gate * zs_ref[...].astype(f32)).astype(o_ref.dtype)

    @pl.when(c == pl.num_programs(1) - 1)
    def _():
        last = slice((n_last - 1) * S5_SG, n_last * S5_SG)
        sr_ref[0] = gather_planes(xr_scr, last)
        si_ref[0] = gather_planes(xi_scr, last)


def _s5(u, zg, h0_re, h0_im, prm, layer, nb, seq, n_real):
    tc = min(seq, 256)
    nch = seq // tc
    n_last = n_real - (nch - 1) * tc
    zs_col = ATTN_WIDTH // S5_WIDTH
    fix2 = lambda b, c: (0, 0)
    fix3 = lambda b, c: (0, 0, 0)
    st = lambda b, c: (b, 0, 0)
    out, s_re, s_im = pl.pallas_call(
        functools.partial(_s5_body, tc=tc, n_last=n_last),
        grid=(nb, nch),
        in_specs=[pl.BlockSpec((tc, S5_WIDTH), lambda b, c: (b * nch + c, 0)),
                  pl.BlockSpec((tc, S5_WIDTH), lambda b, c: (b * nch + c, zs_col)),
                  pl.BlockSpec((1, S5_SG, S5_SG_STATE), st),
                  pl.BlockSpec((1, S5_SG, S5_SG_STATE), st),
                  _layer_spec((S5_SG, S5_SG_STATE), layer, fix2),
                  _layer_spec((S5_SG, S5_SG_STATE), layer, fix2),
                  _layer_spec((S5_SG, LANES, 2 * S5_SG_STATE), layer, fix3),
                  _layer_spec((S5_SG, S5_SG_STATE, LANES), layer, fix3),
                  _layer_spec((S5_SG, S5_SG_STATE, LANES), layer, fix3),
                  _layer_spec((1, S5_WIDTH), layer, fix2),
                  _layer_spec((S5_WIDTH, S5_WIDTH), layer, fix2)],
        out_specs=[pl.BlockSpec((tc, S5_WIDTH), lambda b, c: (b * nch + c, 0)),
                   pl.BlockSpec((1, S5_SG, S5_SG_STATE), st),
                   pl.BlockSpec((1, S5_SG, S5_SG_STATE), st)],
        out_shape=[jax.ShapeDtypeStruct((nb * seq, S5_WIDTH), bf16),
                   jax.ShapeDtypeStruct((nb, S5_SG, S5_SG_STATE), f32),
                   jax.ShapeDtypeStruct((nb, S5_SG, S5_SG_STATE), f32)],
        scratch_shapes=[pltpu.VMEM((S5_SG_STATE // LANES, tc * S5_SG, LANES), f32),
                        pltpu.VMEM((S5_SG_STATE // LANES, tc * S5_SG, LANES), f32),
                        pltpu.VMEM((tc, S5_WIDTH), f32),
                        pltpu.VMEM((S5_SG, S5_SG_STATE), f32),
                        pltpu.VMEM((S5_SG, S5_SG_STATE), f32)],
        compiler_params=_cparams(("parallel", "arbitrary")),
        name="s5_scan",
    )(u, zg, h0_re.reshape(nb, S5_SG, S5_SG_STATE), h0_im.reshape(nb, S5_SG, S5_SG_STATE),
      prm["a_re"], prm["a_im"], prm["w_bu"], prm["w_c_re"], prm["w_c_im"], prm["d"], prm["w_glu"])
    return out, s_re.reshape(nb, S5_GROUPS, S5_STATE), s_im.reshape(nb, S5_GROUPS, S5_STATE)


def _merge_body(a_ref, s_ref, wa_ref, ws_ref, ga_ref, gs_ref, o_ref):
    o_a = jnp.dot(a_ref[...], wa_ref[...], preferred_element_type=f32)
    o_s = jnp.dot(s_ref[...], ws_ref[...], preferred_element_type=f32)
    o_ref[...] = (ga_ref[...].astype(f32) * o_a + gs_ref[...].astype(f32) * o_s).astype(o_ref.dtype)


def _merge(a_in, s_in, w_a, w_s, zg, layer):
    m = a_in.shape[0]
    tm = _row_tile(m, MM_ROW_TILE)
    tn = MM_COL_TILE
    ga0 = (ATTN_WIDTH + S5_WIDTH) // tn
    gs0 = (ATTN_WIDTH + S5_WIDTH + D_MODEL) // tn
    return pl.pallas_call(
        _merge_body,
        grid=(m // tm, D_MODEL // tn),
        in_specs=[pl.BlockSpec((tm, ATTN_WIDTH), lambda i, j: (i, 0)),
                  pl.BlockSpec((tm, S5_WIDTH), lambda i, j: (i, 0)),
                  _layer_spec((ATTN_WIDTH, tn), layer, lambda i, j: (0, j)),
                  _layer_spec((S5_WIDTH, tn), layer, lambda i, j: (0, j)),
                  pl.BlockSpec((tm, tn), lambda i, j: (i, ga0 + j)),
                  pl.BlockSpec((tm, tn), lambda i, j: (i, gs0 + j))],
        out_specs=pl.BlockSpec((tm, tn), lambda i, j: (i, j)),
        out_shape=jax.ShapeDtypeStruct((m, D_MODEL), bf16),
        compiler_params=_cparams(("parallel", "parallel")),
        name="merge",
    )(a_in, s_in, w_a, w_s, zg, zg)


def _out_body(x_ref, m_ref, w_ref, o_ref):
    o_ref[...] = x_ref[...] + jnp.dot(m_ref[...], w_ref[...], preferred_element_type=f32)


def _out_proj(x, merged, w_out, layer):
    m = x.shape[0]
    tm = _row_tile(m, MM_ROW_TILE)
    tn = MM_COL_TILE
    return pl.pallas_call(
        _out_body,
        grid=(m // tm, D_MODEL // tn),
        in_specs=[pl.BlockSpec((tm, tn), lambda i, j: (i, j)),
                  pl.BlockSpec((tm, D_MODEL), lambda i, j: (i, 0)),
                  _layer_spec((D_MODEL, tn), layer, lambda i, j: (0, j))],
        out_specs=pl.BlockSpec((tm, tn), lambda i, j: (i, j)),
        out_shape=jax.ShapeDtypeStruct((m, D_MODEL), f32),
        compiler_params=_cparams(("parallel", "parallel")),
        name="out_proj",
    )(x, merged, w_out)


def _rope_tables(pos):
    posf = pos.astype(f32)[:, None]
    half = HEAD_DIM // 2
    inv = ROPE_THETA ** (-jnp.arange(half, dtype=f32) / half)
    ang = posf * inv[None, :]
    cos, sin = jnp.cos(ang), jnp.sin(ang)
    cos_h = jnp.concatenate([cos, cos], axis=1)
    sin_h = jnp.concatenate([-sin, sin], axis=1)
    half_i = IDX_DIM // 2
    inv_i = ROPE_THETA ** (-jnp.arange(half_i, dtype=f32) / half_i)
    ang_i = posf * inv_i[None, :]
    cos_i, sin_i = jnp.cos(ang_i), jnp.sin(ang_i)
    zeros = jnp.zeros_like(cos_i)
    pad = jnp.zeros((pos.shape[0], LANES - IDX_DIM), f32)
    cos_k = jnp.concatenate([cos_i, cos_i, pad], axis=1)
    sin_a = jnp.concatenate([-sin_i, zeros, pad], axis=1)
    sin_b = jnp.concatenate([zeros, sin_i, pad], axis=1)
    return dict(cos_h=cos_h, sin_h=sin_h, cos_k=cos_k, sin_a=sin_a, sin_b=sin_b,
                cos_ht=cos.T, sin_ht=sin.T, cos_t=cos_i.T, sin_t=sin_i.T)


def _in_proj_offsets():
    offs = [0]
    for s in IN_SIZES:
        offs.append(offs[-1] + s)
    return offs


def _repack_body(w_ref, wq_ref, wkv_ref, wqiw_ref, wki_ref, wu_ref, wzg_ref):
    offs = _in_proj_offsets()
    seg = lambda k: w_ref[offs[k]:offs[k + 1], :].astype(bf16)
    cols = w_ref.shape[1]
    wq_ref[...] = seg(0)
    wkv_ref[...] = w_ref[offs[1]:offs[3], :].astype(bf16)
    n_qi = IDX_HEADS * IDX_DIM
    wqiw_ref[:n_qi, :] = seg(3)
    wqiw_ref[n_qi:, :] = seg(5)
    wki_ref[:IDX_DIM, :] = seg(4)
    wki_ref[IDX_DIM:, :] = jnp.zeros((LANES - IDX_DIM, cols), bf16)
    wu_ref[...] = seg(7)
    row = 0
    for k in (6, 8, 9, 10):
        wzg_ref[row:row + IN_SIZES[k], :] = seg(k)
        row += IN_SIZES[k]


def _repack_in_proj(w_in_t):
    depth, n_in, d = w_in_t.shape
    cols = 256
    n_qiw = IDX_HEADS * IDX_DIM + IDX_HEADS
    n_zg = ATTN_WIDTH + S5_WIDTH + 2 * D_MODEL
    heights = (ATTN_WIDTH, 2 * KV_WIDTH, n_qiw, LANES, S5_WIDTH, n_zg)
    at = lambda l, c: (l, 0, c)
    return pl.pallas_call(
        _repack_body,
        grid=(depth, d // cols),
        in_specs=[pl.BlockSpec((None, n_in, cols), at)],
        out_specs=[pl.BlockSpec((None, hgt, cols), at) for hgt in heights],
        out_shape=[jax.ShapeDtypeStruct((depth, hgt, d), bf16) for hgt in heights],
        compiler_params=_cparams(("parallel", "parallel")),
        name="repack_in_proj",
    )(w_in_t)


def _pack_weights(w_in, w_glu, w_br_attn, w_br_s5, w_out, norm_gain, q_norm_gain, k_norm_gain):
    depth = w_in.shape[0]
    wt_q, wt_kv, wt_qi, wt_ki, wt_u, wt_zg = _repack_in_proj(jnp.swapaxes(w_in, 1, 2))
    return dict(
        wt_q=wt_q, wt_kv=wt_kv, wt_qi=wt_qi, wt_ki=wt_ki, wt_u=wt_u, wt_zg=wt_zg,
        w_glu=w_glu.astype(bf16), w_br_attn=w_br_attn.astype(bf16),
        w_br_s5=w_br_s5.astype(bf16), w_out=w_out.astype(bf16),
        norm_gain=norm_gain.reshape(depth, 1, D_MODEL),
        q_gain=q_norm_gain.reshape(depth, HEAD_DIM, 1),
        k_gain=k_norm_gain.reshape(depth, 1, HEAD_DIM))


def _block_diag(blocks):
    depth, sg, gg, r, c = blocks.shape
    eye = jnp.eye(gg, dtype=blocks.dtype)
    return jnp.einsum("lsgrc,gk->lsgrkc", blocks, eye).reshape(depth, sg, gg * r, gg * c)


def _s5_stacked_params(lam_re, lam_im, log_dt, b_re, b_im, c_re, c_im, d, w_glu_bf16):
    depth = lam_re.shape[0]
    a_re, a_im, bb_re, bb_im = _s5_params(lam_re, lam_im, log_dt,
                                          jnp.transpose(b_re, (0, 3, 1, 2)), jnp.transpose(b_im, (0, 3, 1, 2)))
    def bu_blocks(bb):
        return jnp.transpose(bb, (0, 2, 1, 3)).reshape(depth, S5_SG, S5_SG_GROUPS, S5_GROUP, S5_STATE)
    w_bu = jnp.concatenate([_block_diag(bu_blocks(bb_re)), _block_diag(bu_blocks(bb_im))], axis=3)
    def c_blocks(cm):
        return jnp.transpose(cm, (0, 1, 3, 2)).reshape(depth, S5_SG, S5_SG_GROUPS, S5_STATE, S5_GROUP)
    return dict(a_re=a_re.reshape(depth, S5_SG, S5_SG_STATE), a_im=a_im.reshape(depth, S5_SG, S5_SG_STATE),
                w_bu=w_bu.astype(bf16),
                w_c_re=_block_diag(c_blocks(c_re)).astype(bf16),
                w_c_im=_block_diag(c_blocks(c_im)).astype(bf16),
                d=d.reshape(depth, 1, S5_WIDTH), w_glu=w_glu_bf16)


def _projections(x, wts, tabs, layer):
    h = _rmsnorm(x, wts["norm_gain"], layer)
    qt = _proj_qt(h, wts["wt_q"], wts["q_gain"], tabs["cos_ht"], tabs["sin_ht"], layer)
    k, v, kb, vt = _proj_kv(h, wts["wt_kv"], wts["k_gain"], tabs["cos_h"], tabs["sin_h"], layer)
    qit, wt = _proj_qit(h, wts["wt_qi"], tabs["cos_t"], tabs["sin_t"], layer)
    ki, kib = _proj_ki(h, wts["wt_ki"], tabs["cos_k"], tabs["sin_a"], tabs["sin_b"], layer)
    u = _proj_plain(h, wts["wt_u"], f32, layer)
    zg = _proj_zg(h, wts["wt_zg"], layer)
    return dict(qt=qt, k=k, v=v, kb=kb, vt=vt, qit=qit, wt=wt, ki=ki, kib=kib, u=u, zg=zg)


def _finish(x, a_in, s_in, zg, wts, layer):
    merged = _merge(a_in, s_in, wts["w_br_attn"], wts["w_br_s5"], zg, layer)
    return _out_proj(x, merged, wts["w_out"], layer)


def kernel(x_prompt, x_sample, cache_k, cache_v, cache_kidx, state_s5_re, state_s5_im, page_table, norm_gain, w_in, q_norm_gain, k_norm_gain, s5_lam_re, s5_lam_im, s5_log_dt, s5_b_re, s5_b_im, s5_c_re, s5_c_im, s5_d, w_glu, w_br_attn, w_br_s5, w_out):
    depth = w_in.shape[0]
    b_p, t_p = x_prompt.shape[:2]
    b_s, t_s = x_sample.shape[:2]
    tok = SAMPLE_ROWS
    assert t_s <= tok
    n_pages = page_table.shape[1]
    past = n_pages * PAGE_SIZE
    n_phys = cache_k.shape[1]

    tabs_p = _rope_tables(jnp.tile(jnp.arange(t_p, dtype=i32), b_p))
    tabs_s = _rope_tables(jnp.tile(past + jnp.arange(tok, dtype=i32), b_s))

    xp = x_prompt.reshape(b_p * t_p, D_MODEL)
    xs = jnp.pad(x_sample, ((0, 0), (0, tok - t_s), (0, 0))).reshape(b_s * tok, D_MODEL)

    cache_k4 = cache_k.reshape(depth, n_phys, PAGE_SIZE * N_KV_HEADS, HEAD_DIM)
    cache_v4 = cache_v.reshape(depth, n_phys, PAGE_SIZE * N_KV_HEADS, HEAD_DIM)
    cache_kidx_t = jnp.swapaxes(cache_kidx, 2, 3)
    zeros_state = jnp.zeros((b_p, S5_GROUPS, S5_STATE), f32)

    def new_page(a):
        w = a.shape[-1]
        return jnp.pad(a.reshape(b_s, tok, w), ((0, 0), (0, PAGE_SIZE - tok), (0, 0)))

    wts = _pack_weights(w_in, w_glu, w_br_attn, w_br_s5, w_out, norm_gain, q_norm_gain, k_norm_gain)
    s5p = _s5_stacked_params(s5_lam_re, s5_lam_im, s5_log_dt, s5_b_re, s5_b_im, s5_c_re, s5_c_im, s5_d,
                             wts["w_glu"])

    outs_p, outs_s = [], []
    for l in range(depth):
        pp = _projections(xp, wts, tabs_p, l)
        a_in = _attn_prompt(pp["qt"], pp["qit"], pp["wt"], pp["zg"], pp["kb"], pp["vt"], pp["kib"], b_p, t_p)
        s_in, sr_p, si_p = _s5(pp["u"], pp["zg"], zeros_state, zeros_state, s5p, l, b_p, t_p, t_p)
        xp = _finish(xp, a_in, s_in, pp["zg"], wts, l)
        outs_p.append((pp["k"].reshape(b_p, t_p, N_KV_HEADS, HEAD_DIM),
                       pp["v"].reshape(b_p, t_p, N_KV_HEADS, HEAD_DIM),
                       pp["ki"].reshape(b_p, t_p, IDX_DIM), sr_p, si_p))

        ps = _projections(xs, wts, tabs_s, l)
        qi_rows = jnp.transpose(ps["qit"].reshape(IDX_HEADS, IDX_DIM, b_s, tok), (2, 0, 3, 1)
                                ).reshape(b_s, IDX_HEADS * tok, IDX_DIM)
        w_rows = jnp.transpose(ps["wt"].reshape(IDX_HEADS, b_s, tok), (1, 0, 2)
                               ).reshape(b_s, IDX_HEADS * tok, 1)
        keys, thr = _idx_sample(page_table, qi_rows, w_rows, cache_kidx_t, l,
                                jnp.swapaxes(new_page(ps["ki"]), 1, 2), t_s)
        q_rows = jnp.transpose(ps["qt"].reshape(N_HEADS, HEAD_DIM, b_s, tok), (2, 0, 3, 1)
                               ).reshape(b_s, N_HEADS * tok, HEAD_DIM)
        k_s = ps["k"].reshape(b_s * tok, KV_WIDTH)
        v_s = ps["v"].reshape(b_s * tok, KV_WIDTH)
        o_rows = _attn_sample(page_table, q_rows, keys, thr, cache_k4, cache_v4, l,
                              new_page(k_s), new_page(v_s))
        attn_s = jnp.transpose(o_rows.reshape(b_s, N_HEADS, tok, HEAD_DIM), (0, 2, 1, 3)
                               ).reshape(b_s * tok, ATTN_WIDTH)
        a_in_s = _gate_mul(attn_s, ps["zg"][:, :ATTN_WIDTH])
        s_in_s, sr_s, si_s = _s5(ps["u"], ps["zg"], state_s5_re[l], state_s5_im[l], s5p, l, b_s, tok, t_s)
        xs = _finish(xs, a_in_s, s_in_s, ps["zg"], wts, l)
        real = lambda a: a.reshape((b_s, tok) + a.shape[1:])[:, :t_s]
        outs_s.append((real(k_s).reshape(b_s, t_s, N_KV_HEADS, HEAD_DIM),
                       real(v_s).reshape(b_s, t_s, N_KV_HEADS, HEAD_DIM),
                       real(ps["ki"]), sr_s, si_s))

    k_prompt, v_prompt, kidx_prompt, s5_re_prompt, s5_im_prompt = [jnp.stack(a) for a in zip(*outs_p)]
    k_sample, v_sample, kidx_sample, s5_re_sample, s5_im_sample = [jnp.stack(a) for a in zip(*outs_s)]
    y_prompt = xp.reshape(b_p, t_p, D_MODEL)
    y_sample = xs.reshape(b_s, tok, D_MODEL)[:, :t_s]
    return (y_prompt, y_sample, k_prompt, v_prompt, kidx_prompt, s5_re_prompt, s5_im_prompt,
            k_sample, v_sample, kidx_sample, s5_re_sample, s5_im_sample)
```

```python
import functools
import math

import jax
import jax.numpy as jnp
from jax import lax
from jax.experimental import pallas as pl
from jax.experimental.pallas import tpu as pltpu

D_MODEL = 2048
PAGE_SIZE = 128
N_HEADS = 16
HEAD_DIM = 128
N_KV_HEADS = 4
HEADS_PER_KV = N_HEADS // N_KV_HEADS
ATTN_WIDTH = N_HEADS * HEAD_DIM
KV_WIDTH = N_KV_HEADS * HEAD_DIM
IDX_HEADS = 16
IDX_DIM = 64
TOPK_MAX = 256
S5_WIDTH = D_MODEL // 2
S5_GROUP = 16
S5_GROUPS = S5_WIDTH // S5_GROUP
S5_STATE = 64
ROPE_THETA = 10000.0
EPS = 1e-6
IN_SIZES = (ATTN_WIDTH, KV_WIDTH, KV_WIDTH, IDX_HEADS * IDX_DIM, IDX_DIM, IDX_HEADS,
            ATTN_WIDTH, S5_WIDTH, S5_WIDTH, D_MODEL, D_MODEL)

LANES = 128
SUBLANES = 8
PACKED_SUBLANES = 16
VMEM_LIMIT_BYTES = 56 * 1024 * 1024
MM_ROW_TILE = 1024
MM_COL_TILE = 1024

S5_SG = S5_WIDTH // LANES
S5_SG_GROUPS = S5_GROUPS // S5_SG
S5_SG_STATE = S5_SG_GROUPS * S5_STATE
S5_SCAN_UNROLL = 8

SAMPLE_ROWS = 16

ATTN_Q_TILE = 256
ATTN_ROW_BLOCK = 64
COUNT_ROWS = 32
PAGES_PER_STEP = 32
IDX_PAGES_PER_STEP = 64
LOG2_E = math.log2(math.e)

INT_MIN = -2 ** 31
MASK_BIAS = -1e30

f32 = jnp.float32
bf16 = jnp.bfloat16
i32 = jnp.int32


def _cparams(sem):
    return pltpu.CompilerParams(dimension_semantics=sem, vmem_limit_bytes=VMEM_LIMIT_BYTES)


def _row_tile(m, cap):
    return m if m <= cap else cap


def _norm_body(x_ref, g_ref, o_ref):
    x = x_ref[...]
    ms = jnp.mean(x * x, axis=-1, keepdims=True)
    o_ref[...] = (x * lax.rsqrt(ms + EPS) * g_ref[...]).astype(o_ref.dtype)


def _layer_spec(block, layer, index_map):
    return pl.BlockSpec((None,) + tuple(block), lambda *idx: (layer,) + tuple(index_map(*idx)))


def _rmsnorm(x, gains, layer):
    m, d = x.shape
    tm = _row_tile(m, 512)
    return pl.pallas_call(
        _norm_body,
        grid=(m // tm,),
        in_specs=[pl.BlockSpec((tm, d), lambda i: (i, 0)),
                  _layer_spec((1, d), layer, lambda i: (0, 0))],
        out_specs=pl.BlockSpec((tm, d), lambda i: (i, 0)),
        out_shape=jax.ShapeDtypeStruct((m, d), bf16),
        compiler_params=_cparams(("parallel",)),
        name="rmsnorm",
    )(x, gains)


def _head_norm_rope(x, gain, cos, sin):
    ms = jnp.mean(x * x, axis=-1, keepdims=True)
    y = x * lax.rsqrt(ms + EPS) * gain
    return y * cos + pltpu.roll(y, HEAD_DIM // 2, 1) * sin


def _qt_body(wt_ref, h_ref, g_ref, cos_ref, sin_ref, o_ref, *, heads):
    acc = lax.dot_general(wt_ref[...], h_ref[...], (((1,), (1,)), ((), ())),
                          preferred_element_type=f32)
    c = cos_ref[...]
    s = sin_ref[...]
    g = g_ref[...]
    half = HEAD_DIM // 2
    scale = HEAD_DIM ** -0.5 * LOG2_E
    for hh in range(heads):
        r0 = hh * HEAD_DIM
        x = acc[r0:r0 + HEAD_DIM]
        ms = jnp.mean(x * x, axis=0, keepdims=True)
        y = x * lax.rsqrt(ms + EPS) * g
        x1 = y[:half]
        x2 = y[half:]
        o_ref[r0:r0 + half, :] = ((x1 * c - x2 * s) * scale).astype(o_ref.dtype)
        o_ref[r0 + half:r0 + HEAD_DIM, :] = ((x2 * c + x1 * s) * scale).astype(o_ref.dtype)


def _proj_qt(h, wt_q, gain, cos_ht, sin_ht, layer):
    m, d = h.shape
    tm = _row_tile(m, MM_ROW_TILE)
    tn = 4 * HEAD_DIM
    return pl.pallas_call(
        functools.partial(_qt_body, heads=tn // HEAD_DIM),
        grid=(m // tm, ATTN_WIDTH // tn),
        in_specs=[_layer_spec((tn, d), layer, lambda i, j: (j, 0)),
                  pl.BlockSpec((tm, d), lambda i, j: (i, 0)),
                  _layer_spec((HEAD_DIM, 1), layer, lambda i, j: (0, 0)),
                  pl.BlockSpec((HEAD_DIM // 2, tm), lambda i, j: (0, i)),
                  pl.BlockSpec((HEAD_DIM // 2, tm), lambda i, j: (0, i))],
        out_specs=pl.BlockSpec((tn, tm), lambda i, j: (j, i)),
        out_shape=jax.ShapeDtypeStruct((ATTN_WIDTH, m), bf16),
        compiler_params=_cparams(("parallel", "parallel")),
        name="proj_q_t",
    )(wt_q, h, gain, cos_ht, sin_ht)


def _dot_nt(a, b_t):
    return lax.dot_general(a, b_t, (((1,), (1,)), ((), ())), preferred_element_type=f32)


def _kv_body(h_ref, wt_ref, g_ref, cos_ref, sin_ref, k_ref, v_ref, kb_ref, vt_ref, *, chunk):
    h = h_ref[...]
    acc = _dot_nt(h, wt_ref[...])
    cos = cos_ref[...]
    sin = sin_ref[...]
    g = g_ref[...]
    tm = h.shape[0]
    for hh in range(N_KV_HEADS):
        sl = slice(hh * HEAD_DIM, (hh + 1) * HEAD_DIM)
        kh = _head_norm_rope(acc[:, sl], g, cos, sin)
        rows = pl.ds(hh, tm, stride=N_KV_HEADS)
        k_ref[rows, :] = kh
        v_ref[rows, :] = acc[:, KV_WIDTH + hh * HEAD_DIM:KV_WIDTH + (hh + 1) * HEAD_DIM]
        kb_ref[:, sl] = kh.astype(bf16)
    vt = _dot_nt(wt_ref[KV_WIDTH:, :], h)
    for cc in range(vt_ref.shape[0]):
        vt_ref[cc] = vt[:, cc * chunk:(cc + 1) * chunk].astype(bf16)


def _proj_kv(h, wt_kv, gain, cos, sin, layer):
    m, d = h.shape
    tm = _row_tile(m, 512)
    chunk = min(tm, ATTN_Q_TILE)
    row = lambda i: (i, 0)
    fix = lambda i: (0, 0)
    return pl.pallas_call(
        functools.partial(_kv_body, chunk=chunk),
        grid=(m // tm,),
        in_specs=[pl.BlockSpec((tm, d), row),
                  _layer_spec((2 * KV_WIDTH, d), layer, fix),
                  _layer_spec((1, HEAD_DIM), layer, fix),
                  pl.BlockSpec((tm, HEAD_DIM), row),
                  pl.BlockSpec((tm, HEAD_DIM), row)],
        out_specs=[pl.BlockSpec((tm * N_KV_HEADS, HEAD_DIM), row),
                   pl.BlockSpec((tm * N_KV_HEADS, HEAD_DIM), row),
                   pl.BlockSpec((tm, KV_WIDTH), row),
                   pl.BlockSpec((tm // chunk, KV_WIDTH, chunk), lambda i: (i, 0, 0))],
        out_shape=[jax.ShapeDtypeStruct((m * N_KV_HEADS, HEAD_DIM), f32),
                   jax.ShapeDtypeStruct((m * N_KV_HEADS, HEAD_DIM), f32),
                   jax.ShapeDtypeStruct((m, KV_WIDTH), bf16),
                   jax.ShapeDtypeStruct((m // chunk, KV_WIDTH, chunk), bf16)],
        compiler_params=_cparams(("parallel",)),
        name="proj_kv",
    )(h, wt_kv, gain, cos, sin)


def _qit_body(wt_ref, h_ref, cos_ref, sin_ref, qit_ref, wt_out_ref):
    acc = lax.dot_general(wt_ref[...], h_ref[...], (((1,), (1,)), ((), ())),
                          preferred_element_type=f32)
    c = cos_ref[...]
    s = sin_ref[...]
    half = IDX_DIM // 2
    scale = IDX_DIM ** -0.5
    for hh in range(IDX_HEADS):
        r0 = hh * IDX_DIM
        x1 = acc[r0:r0 + half]
        x2 = acc[r0 + half:r0 + IDX_DIM]
        qit_ref[r0:r0 + half, :] = ((x1 * c - x2 * s) * scale).astype(qit_ref.dtype)
        qit_ref[r0 + half:r0 + IDX_DIM, :] = ((x2 * c + x1 * s) * scale).astype(qit_ref.dtype)
    n_qi = IDX_HEADS * IDX_DIM
    wt_out_ref[...] = acc[n_qi:n_qi + IDX_HEADS] * (IDX_HEADS ** -0.5)


def _proj_qit(h, wt_qi, cos_t, sin_t, layer):
    m, d = h.shape
    tm = _row_tile(m, 512)
    n_rows = wt_qi.shape[1]
    n_qi = IDX_HEADS * IDX_DIM
    return pl.pallas_call(
        _qit_body,
        grid=(m // tm,),
        in_specs=[_layer_spec((n_rows, d), layer, lambda i: (0, 0)),
                  pl.BlockSpec((tm, d), lambda i: (i, 0)),
                  pl.BlockSpec((IDX_DIM // 2, tm), lambda i: (0, i)),
                  pl.BlockSpec((IDX_DIM // 2, tm), lambda i: (0, i))],
        out_specs=[pl.BlockSpec((n_qi, tm), lambda i: (0, i)),
                   pl.BlockSpec((IDX_HEADS, tm), lambda i: (0, i))],
        out_shape=[jax.ShapeDtypeStruct((n_qi, m), bf16),
                   jax.ShapeDtypeStruct((IDX_HEADS, m), f32)],
        compiler_params=_cparams(("parallel",)),
        name="proj_qi_t",
    )(wt_qi, h, cos_t, sin_t)


def _ki_body(h_ref, w_ref, cos_ref, sina_ref, sinb_ref, ki_ref, kib_ref):
    x = _dot_nt(h_ref[...], w_ref[...])
    half = IDX_DIM // 2
    r = (x * cos_ref[...] + pltpu.roll(x, LANES - half, 1) * sina_ref[...]
         + pltpu.roll(x, half, 1) * sinb_ref[...])
    ki = r[:, :IDX_DIM]
    ki_ref[...] = ki
    kib_ref[...] = ki.astype(bf16)


def _proj_ki(h, w_ki, cos_k, sin_a, sin_b, layer):
    m, d = h.shape
    tm = _row_tile(m, 512)
    row = lambda i: (i, 0)
    return pl.pallas_call(
        _ki_body,
        grid=(m // tm,),
        in_specs=[pl.BlockSpec((tm, d), row),
                  _layer_spec((LANES, d), layer, lambda i: (0, 0)),
                  pl.BlockSpec((tm, LANES), row),
                  pl.BlockSpec((tm, LANES), row),
                  pl.BlockSpec((tm, LANES), row)],
        out_specs=[pl.BlockSpec((tm, IDX_DIM), row)] * 2,
        out_shape=[jax.ShapeDtypeStruct((m, IDX_DIM), f32),
                   jax.ShapeDtypeStruct((m, IDX_DIM), bf16)],
        compiler_params=_cparams(("parallel",)),
        name="proj_ki",
    )(h, w_ki, cos_k, sin_a, sin_b)


def _plain_body(h_ref, w_ref, o_ref):
    o_ref[...] = _dot_nt(h_ref[...], w_ref[...]).astype(o_ref.dtype)


def _proj_plain(h, wt, out_dtype, layer):
    m, d = h.shape
    n = wt.shape[1]
    tm = _row_tile(m, MM_ROW_TILE)
    tn = MM_COL_TILE
    return pl.pallas_call(
        _plain_body,
        grid=(m // tm, n // tn),
        in_specs=[pl.BlockSpec((tm, d), lambda i, j: (i, 0)),
                  _layer_spec((tn, d), layer, lambda i, j: (j, 0))],
        out_specs=pl.BlockSpec((tm, tn), lambda i, j: (i, j)),
        out_shape=jax.ShapeDtypeStruct((m, n), out_dtype),
        compiler_params=_cparams(("parallel", "parallel")),
        name="proj_u",
    )(h, wt)


def _sigmoid(x):
    return 1.0 / (1.0 + jnp.exp(-x))


def _zg_body(h_ref, w_ref, o_ref, *, silu_tiles):
    acc = _dot_nt(h_ref[...], w_ref[...])
    sg = _sigmoid(acc)
    is_silu = pl.program_id(1) < silu_tiles
    o_ref[...] = jnp.where(is_silu, acc * sg, sg).astype(o_ref.dtype)


def _proj_zg(h, wt_zg, layer):
    m, d = h.shape
    n = wt_zg.shape[1]
    tm = _row_tile(m, MM_ROW_TILE)
    tn = MM_COL_TILE
    return pl.pallas_call(
        functools.partial(_zg_body, silu_tiles=(ATTN_WIDTH + S5_WIDTH) // tn),
        grid=(m // tm, n // tn),
        in_specs=[pl.BlockSpec((tm, d), lambda i, j: (i, 0)),
                  _layer_spec((tn, d), layer, lambda i, j: (j, 0))],
        out_specs=pl.BlockSpec((tm, tn), lambda i, j: (i, j)),
        out_shape=jax.ShapeDtypeStruct((m, n), bf16),
        compiler_params=_cparams(("parallel", "parallel")),
        name="proj_zg",
    )(h, wt_zg)


def _sortable_key(score):
    b = pltpu.bitcast(score, i32)
    return b ^ ((b >> 31) & jnp.int32(0x7FFFFFFF))


def _kth_largest(count_ge, shape, k):
    zero = jnp.zeros(shape, i32)
    prefix = jnp.where(count_ge(zero) >= k, zero, jnp.full(shape, INT_MIN, i32))

    def bit_body(bi, prefix):
        cand = prefix | (jnp.int32(1) << (30 - bi))
        return jnp.where(count_ge(cand) >= k, cand, prefix)

    thr = lax.fori_loop(0, 31, bit_body, prefix)
    return jnp.maximum(thr, jnp.int32(INT_MIN + 1))


def _resolve_ties(thr, k, count_ge, count_tied_before, demote, n_pos_bits, active=None):
    excess = count_ge(thr) - k
    if active is not None:
        excess = jnp.where(active, excess, 0)

    @pl.when(jnp.max(excess) > 0)
    def _():
        need = k - count_ge(thr + 1)

        def bit_body(bi, pos):
            cand = pos | (jnp.int32(1) << (n_pos_bits - 1 - bi))
            return jnp.where(count_tied_before(thr, cand) < need, cand, pos)

        last = lax.fori_loop(0, n_pos_bits, bit_body, jnp.zeros(thr.shape, i32))
        demote(thr, last)


def _sublane_allreduce(x, op):
    for shift in (4, 2, 1):
        x = op(x, pltpu.roll(x, shift, 0))
    return x


def _attn_prompt_body(qt_ref, qit_ref, wt_ref, z_ref, k_ref, vt_ref, ki_ref, o_ref,
                      key_scr, acc_scr, m_scr, l_scr, alpha_scr, s_scr, p_scr, bias_scr,
                      *, tq, k_top, n_pos_bits):
    i = pl.program_id(1)
    n_chunks = i + 1
    half = tq // 2

    def score_keys(c, diagonal):
        for sub in range(2):
            r0 = pl.multiple_of(c * tq + sub * half, half)
            kic = ki_ref[pl.ds(r0, half), :]
            acc = jnp.zeros((half, tq), f32)
            for hh in range(IDX_HEADS):
                lg = jnp.dot(kic, qit_ref[hh * IDX_DIM:(hh + 1) * IDX_DIM, :],
                             preferred_element_type=f32)
                acc = acc + jnp.maximum(lg, 0.0) * wt_ref[hh:hh + 1, :]
            key = _sortable_key(acc)
            if diagonal:
                kpos = lax.broadcasted_iota(i32, (half, tq), 0) + sub * half
                qpos = lax.broadcasted_iota(i32, (half, tq), 1)
                key = jnp.where(kpos <= qpos, key, jnp.int32(INT_MIN))
            key_scr[pl.ds(r0, half), :] = key

    def full_chunk(c, carry):
        score_keys(c, False)
        return carry

    lax.fori_loop(0, i, full_chunk, 0)
    score_keys(i, True)

    @pl.when(n_chunks % 2 == 1)
    def _():
        key_scr[pl.ds(pl.multiple_of(n_chunks * tq, tq), tq), :] = jnp.full((tq, tq), INT_MIN, i32)

    def count_ge(cand):
        def body(c, cnt):
            kk = key_scr[pl.ds(pl.multiple_of(c * 2 * tq, 2 * tq), 2 * tq), :]
            hit = (kk >= cand).astype(i32)
            return cnt + jnp.sum(hit.reshape(2 * tq // COUNT_ROWS, COUNT_ROWS, tq), axis=0)

        cnt = lax.fori_loop(0, (n_chunks + 1) // 2, body, jnp.zeros((COUNT_ROWS, tq), i32))
        return jnp.sum(cnt, axis=0, keepdims=True)

    thr = _kth_largest(count_ge, (1, tq), k_top)

    def chunk_positions(c):
        return lax.broadcasted_iota(i32, (tq, tq), 0) + c * tq

    def count_tied_before(thr, pos_limit):
        def body(c, cnt):
            kk = key_scr[pl.ds(pl.multiple_of(c * tq, tq), tq), :]
            hit = ((kk == thr) & (chunk_positions(c) < pos_limit)).astype(i32)
            return cnt + jnp.sum(hit.reshape(tq // SUBLANES, SUBLANES, tq), axis=0)

        cnt = lax.fori_loop(0, n_chunks, body, jnp.zeros((SUBLANES, tq), i32))
        return jnp.sum(cnt, axis=0, keepdims=True)

    def demote(thr, last):
        def body(c, carry):
            rows = pl.ds(pl.multiple_of(c * tq, tq), tq)
            kk = key_scr[rows, :]
            key_scr[rows, :] = jnp.where((kk == thr) & (chunk_positions(c) > last), thr - 1, kk)
            return carry

        lax.fori_loop(0, n_chunks, body, 0)

    _resolve_ties(thr, k_top, count_ge, count_tied_before, demote, n_pos_bits)

    m_scr[...] = jnp.full(m_scr.shape, -jnp.inf, f32)
    l_scr[...] = jnp.zeros(l_scr.shape, f32)
    acc_scr[...] = jnp.zeros(acc_scr.shape, f32)
    n_sub = tq // SUBLANES
    d_sub = HEAD_DIM // SUBLANES

    def attend_chunk(c, carry):
        r0 = pl.multiple_of(c * tq, tq)
        bias_scr[...] = jnp.where(key_scr[pl.ds(r0, tq), :] >= thr, 0.0, MASK_BIAS)
        for g in range(N_KV_HEADS):
            kc = k_ref[pl.ds(r0, tq), g * HEAD_DIM:(g + 1) * HEAD_DIM]
            for hh in range(HEADS_PER_KV):
                h = g * HEADS_PER_KV + hh
                s = jnp.dot(kc, qt_ref[h * HEAD_DIM:(h + 1) * HEAD_DIM, :], preferred_element_type=f32)
                s_scr[h] = s + bias_scr[...]
        blocks = [slice(j * ATTN_ROW_BLOCK, (j + 1) * ATTN_ROW_BLOCK) for j in range(tq // ATTN_ROW_BLOCK)]
        blk_sub = ATTN_ROW_BLOCK // SUBLANES
        for h in range(N_HEADS):
            mx = m_scr[h]
            for rows in blocks:
                mx = jnp.maximum(mx, jnp.max(s_scr[h, rows, :].reshape(blk_sub, SUBLANES, tq), axis=0))
            m_new = _sublane_allreduce(mx, jnp.maximum)
            alpha_scr[h] = jnp.exp2(m_scr[h] - m_new)
            m_scr[h] = m_new
        for h in range(N_HEADS):
            p3 = jnp.exp2(s_scr[h].reshape(n_sub, SUBLANES, tq) - m_scr[h][None])
            p_scr[h] = p3.reshape(tq, tq).astype(bf16)
        ones_rows = jnp.ones((PACKED_SUBLANES, tq), bf16)
        for g in range(N_KV_HEADS):
            vtc = jnp.concatenate([vt_ref[c, g * HEAD_DIM:(g + 1) * HEAD_DIM, :], ones_rows], axis=0)
            for hh in range(HEADS_PER_KV):
                h = g * HEADS_PER_KV + hh
                hs = slice(h * HEAD_DIM, (h + 1) * HEAD_DIM)
                pv = jnp.dot(vtc, p_scr[h], preferred_element_type=f32)
                alpha = alpha_scr[h]
                acc = acc_scr[hs, :].reshape(d_sub, SUBLANES, tq) * alpha[None]
                acc_scr[hs, :] = acc.reshape(HEAD_DIM, tq) + pv[:HEAD_DIM]
                l_scr[h] = alpha * l_scr[h] + pv[HEAD_DIM:HEAD_DIM + SUBLANES]
        return carry

    lax.fori_loop(0, n_chunks, attend_chunk, 0)

    for h in range(N_HEADS):
        hs = slice(h * HEAD_DIM, (h + 1) * HEAD_DIM)
        out_t = (acc_scr[hs, :].reshape(d_sub, SUBLANES, tq) / l_scr[h][None]).reshape(HEAD_DIM, tq)
        o_ref[:, hs] = (out_t.T * z_ref[:, hs].astype(f32)).astype(o_ref.dtype)


def _attn_prompt(qt, qit, wt, zg, kb, vt, kib, batch, seq):
    tq = ATTN_Q_TILE
    assert seq % tq == 0
    nq = seq // tq
    k_top = min(TOPK_MAX, seq // 4)
    assert k_top <= tq
    qrow = lambda b, i: (b * nq + i, 0)
    qcol = lambda b, i: (0, b * nq + i)
    per_b = lambda b, i: (b, 0)
    return pl.pallas_call(
        functools.partial(_attn_prompt_body, tq=tq, k_top=k_top, n_pos_bits=(seq - 1).bit_length()),
        grid=(batch, nq),
        in_specs=[pl.BlockSpec((ATTN_WIDTH, tq), qcol),
                  pl.BlockSpec((IDX_HEADS * IDX_DIM, tq), qcol),
                  pl.BlockSpec((IDX_HEADS, tq), qcol),
                  pl.BlockSpec((tq, ATTN_WIDTH), qrow),
                  pl.BlockSpec((seq, KV_WIDTH), per_b),
                  pl.BlockSpec((nq, KV_WIDTH, tq), lambda b, i: (b, 0, 0)),
                  pl.BlockSpec((seq, IDX_DIM), per_b)],
        out_specs=pl.BlockSpec((tq, ATTN_WIDTH), qrow),
        out_shape=jax.ShapeDtypeStruct((batch * seq, ATTN_WIDTH), bf16),
        scratch_shapes=[pltpu.VMEM((seq + tq, tq), i32),
                        pltpu.VMEM((ATTN_WIDTH, tq), f32),
                        pltpu.VMEM((N_HEADS, SUBLANES, tq), f32),
                        pltpu.VMEM((N_HEADS, SUBLANES, tq), f32),
                        pltpu.VMEM((N_HEADS, SUBLANES, tq), f32),
                        pltpu.VMEM((N_HEADS, tq, tq), f32),
                        pltpu.VMEM((N_HEADS, tq, tq), bf16),
                        pltpu.VMEM((tq, tq), f32)],
        compiler_params=_cparams(("parallel", "arbitrary")),
        name="attn_prompt",
    )(qt, qit, wt, zg, kb, vt, kib)


def _idx_sample_body(pt_ref, qi_ref, w_ref, *rest, n_pages, n_new, k_top, ppb):
    kid_refs = rest[:ppb]
    kinew_ref, key_ref, thr_ref = rest[ppb:]
    p = pl.program_id(1)
    tok = SAMPLE_ROWS

    def page_keys(ki_page_t):
        lg = jnp.dot(qi_ref[0], ki_page_t.astype(bf16), preferred_element_type=f32)
        wgt = jnp.maximum(lg, 0.0) * w_ref[0]
        score = jnp.sum(wgt.reshape(IDX_HEADS, tok, PAGE_SIZE), axis=0)
        return _sortable_key(score)

    for j in range(ppb):
        key_ref[0, p * ppb + j] = page_keys(kid_refs[j][...])

    @pl.when(p == pl.num_programs(1) - 1)
    def _():
        key = page_keys(kinew_ref[0])
        kpos = lax.broadcasted_iota(i32, (tok, PAGE_SIZE), 1)
        qtok = lax.broadcasted_iota(i32, (tok, PAGE_SIZE), 0)
        key_ref[0, n_pages] = jnp.where((kpos <= qtok) & (kpos < n_new), key, jnp.int32(INT_MIN))

        def count_ge(cand):
            def body(c, cnt):
                hit = (key_ref[0, pl.ds(c * ppb, ppb)] >= cand).astype(i32)
                return cnt + jnp.sum(hit, axis=0)

            cnt = lax.fori_loop(0, n_pages // ppb, body, jnp.zeros((tok, PAGE_SIZE), i32))
            cnt = cnt + (key_ref[0, n_pages] >= cand).astype(i32)
            return jnp.sum(cnt, axis=-1, keepdims=True)

        thr = _kth_largest(count_ge, (tok, 1), k_top)
        thr_ref[0] = jnp.broadcast_to(thr, (tok, PAGE_SIZE))

        def page_positions(c):
            return lax.broadcasted_iota(i32, (tok, PAGE_SIZE), 1) + c * PAGE_SIZE

        def count_tied_before(thr, pos_limit):
            def body(c, cnt):
                return cnt + ((key_ref[0, c] == thr) & (page_positions(c) < pos_limit)).astype(i32)

            cnt = lax.fori_loop(0, n_pages + 1, body, jnp.zeros((tok, PAGE_SIZE), i32))
            return jnp.sum(cnt, axis=-1, keepdims=True)

        def demote(thr, last):
            def body(c, carry):
                kk = key_ref[0, c]
                key_ref[0, c] = jnp.where((kk == thr) & (page_positions(c) > last), thr - 1, kk)
                return carry

            lax.fori_loop(0, n_pages + 1, body, 0)

        real_rows = lax.broadcasted_iota(i32, (tok, 1), 0) < n_new
        _resolve_ties(thr, k_top, count_ge, count_tied_before, demote,
                      ((n_pages + 1) * PAGE_SIZE - 1).bit_length(), active=real_rows)


def _pages_per_step(n_pages, cap=PAGES_PER_STEP):
    ppb = min(cap, n_pages)
    assert n_pages % ppb == 0
    return ppb


def _page_spec(block, layer, ppb, j):
    return pl.BlockSpec(block, lambda b, p, pt: (layer, pt[b, p * ppb + j], 0, 0))


def _idx_sample(page_table, qi_rows, w_rows, cache_kidx, layer, ki_new_pages, n_new):
    nb, n_pages = page_table.shape
    tok = SAMPLE_ROWS
    ppb = _pages_per_step(n_pages, IDX_PAGES_PER_STEP)
    k_top = min(TOPK_MAX, (n_pages * PAGE_SIZE + n_new) // 4)
    per_b = lambda b, p, pt: (b, 0, 0)
    grid_spec = pltpu.PrefetchScalarGridSpec(
        num_scalar_prefetch=1,
        grid=(nb, n_pages // ppb),
        in_specs=[pl.BlockSpec((1, IDX_HEADS * tok, IDX_DIM), per_b),
                  pl.BlockSpec((1, IDX_HEADS * tok, PAGE_SIZE), per_b)]
                 + [_page_spec((None, None, IDX_DIM, PAGE_SIZE), layer, ppb, j) for j in range(ppb)]
                 + [pl.BlockSpec((1, IDX_DIM, PAGE_SIZE), per_b)],
        out_specs=[pl.BlockSpec((1, n_pages + 1, tok, PAGE_SIZE), lambda b, p, pt: (b, 0, 0, 0)),
                   pl.BlockSpec((1, tok, PAGE_SIZE), per_b)],
    )
    return pl.pallas_call(
        functools.partial(_idx_sample_body, n_pages=n_pages, n_new=n_new, k_top=k_top, ppb=ppb),
        grid_spec=grid_spec,
        out_shape=[jax.ShapeDtypeStruct((nb, n_pages + 1, tok, PAGE_SIZE), i32),
                   jax.ShapeDtypeStruct((nb, tok, PAGE_SIZE), i32)],
        compiler_params=_cparams(("parallel", "arbitrary")),
        name="idx_sample",
    )(page_table, qi_rows, w_rows, *([cache_kidx] * ppb), ki_new_pages)


def _attn_sample_body(pt_ref, q_ref, key_ref, keyn_ref, thr_ref, *rest, ppb):
    k_refs = rest[:ppb]
    v_refs = rest[ppb:2 * ppb]
    kn_ref, vn_ref, o_ref, acc_scr, m_scr, l_scr = rest[2 * ppb:]
    p = pl.program_id(1)
    tok = SAMPLE_ROWS
    rows_g = HEADS_PER_KV * tok
    thr = thr_ref[0]

    @pl.when(p == 0)
    def _():
        m_scr[...] = jnp.full(m_scr.shape, -jnp.inf, f32)
        l_scr[...] = jnp.zeros(l_scr.shape, f32)
        acc_scr[...] = jnp.zeros(acc_scr.shape, f32)

    def attend(bias_tok, k_of_group, v_of_group):
        bias = jnp.concatenate([bias_tok] * HEADS_PER_KV, axis=0)
        for g in range(N_KV_HEADS):
            rs = slice(g * rows_g, (g + 1) * rows_g)
            s = lax.dot_general(q_ref[0, rs, :], k_of_group(g), (((1,), (1,)), ((), ())),
                                preferred_element_type=f32) + bias
            m_prev = m_scr[rs, :]
            m_new = jnp.maximum(m_prev, jnp.max(s, axis=-1, keepdims=True))
            alpha = jnp.exp2(m_prev - m_new)
            pr = jnp.exp2(s - m_new)
            l_scr[rs, :] = alpha * l_scr[rs, :] + jnp.sum(pr, axis=-1, keepdims=True)
            acc_scr[rs, :] = alpha * acc_scr[rs, :] + jnp.dot(
                pr.astype(bf16), v_of_group(g), preferred_element_type=f32)
            m_scr[rs, :] = m_new

    def mask_bias(keys):
        return jnp.where(keys >= thr, 0.0, MASK_BIAS).astype(f32)

    def cached(refs):
        def of_group(g):
            rows = pl.ds(g, PAGE_SIZE, stride=N_KV_HEADS)
            return jnp.concatenate([r[rows, :].astype(bf16) for r in refs], axis=0)
        return of_group

    attend(jnp.concatenate([mask_bias(key_ref[0, j]) for j in range(ppb)], axis=1),
           cached(k_refs), cached(v_refs))

    @pl.when(p == pl.num_programs(1) - 1)
    def _():
        def fresh(ref):
            return lambda g: ref[0, :, g * HEAD_DIM:(g + 1) * HEAD_DIM].astype(bf16)

        attend(mask_bias(keyn_ref[0, 0]), fresh(kn_ref), fresh(vn_ref))
        o_ref[0] = acc_scr[...] / l_scr[...]


def _attn_sample(page_table, q_rows, keys, thr, cache_k, cache_v, layer, k_new_pages, v_new_pages):
    nb, n_pages = page_table.shape
    tok = SAMPLE_ROWS
    rows = N_HEADS * tok
    ppb = _pages_per_step(n_pages)
    per_b = lambda b, p, pt: (b, 0, 0)
    page_block = (None, None, PAGE_SIZE * N_KV_HEADS, HEAD_DIM)
    grid_spec = pltpu.PrefetchScalarGridSpec(
        num_scalar_prefetch=1,
        grid=(nb, n_pages // ppb),
        in_specs=[pl.BlockSpec((1, rows, HEAD_DIM), per_b),
                  pl.BlockSpec((1, ppb, tok, PAGE_SIZE), lambda b, p, pt: (b, p, 0, 0)),
                  pl.BlockSpec((1, 1, tok, PAGE_SIZE), lambda b, p, pt: (b, n_pages, 0, 0)),
                  pl.BlockSpec((1, tok, PAGE_SIZE), per_b)]
                 + [_page_spec(page_block, layer, ppb, j) for j in range(ppb)]
                 + [_page_spec(page_block, layer, ppb, j) for j in range(ppb)]
                 + [pl.BlockSpec((1, PAGE_SIZE, KV_WIDTH), per_b),
                    pl.BlockSpec((1, PAGE_SIZE, KV_WIDTH), per_b)],
        out_specs=pl.BlockSpec((1, rows, HEAD_DIM), per_b),
        scratch_shapes=[pltpu.VMEM((rows, HEAD_DIM), f32),
                        pltpu.VMEM((rows, 1), f32),
                        pltpu.VMEM((rows, 1), f32)],
    )
    return pl.pallas_call(
        functools.partial(_attn_sample_body, ppb=ppb),
        grid_spec=grid_spec,
        out_shape=jax.ShapeDtypeStruct((nb, rows, HEAD_DIM), f32),
        compiler_params=_cparams(("parallel", "arbitrary")),
        name="attn_sample",
    )(page_table, q_rows, keys, keys, thr, *([cache_k] * ppb), *([cache_v] * ppb),
      k_new_pages, v_new_pages)


def _gate_mul_body(a_ref, z_ref, o_ref):
    o_ref[...] = (a_ref[...] * z_ref[...].astype(f32)).astype(o_ref.dtype)


def _gate_mul(a, zg):
    m, n = a.shape
    return pl.pallas_call(
        _gate_mul_body,
        grid=(1,),
        in_specs=[pl.BlockSpec((m, n), lambda i: (0, 0)),
                  pl.BlockSpec((m, n), lambda i: (0, 0))],
        out_specs=pl.BlockSpec((m, n), lambda i: (0, 0)),
        out_shape=jax.ShapeDtypeStruct((m, n), bf16),
        compiler_params=_cparams(("arbitrary",)),
        name="gate_mul",
    )(a, zg)


def _s5_param_body(lr_ref, li_ref, ldt_ref, bre_ref, bim_ref, ar_ref, ai_ref, bbre_ref, bbim_ref):
    lr = lr_ref[...]
    li = li_ref[...]
    dt = jnp.exp(ldt_ref[...])
    mag = jnp.exp(lr * dt)
    ar = mag * jnp.cos(li * dt)
    ai = mag * jnp.sin(li * dt)
    den = lr * lr + li * li
    nr = ar - 1.0
    cr = (nr * lr + ai * li) / den
    ci = (ai * lr - nr * li) / den
    ar_ref[...] = ar
    ai_ref[...] = ai
    for h in range(S5_GROUP):
        bbre_ref[h] = cr * bre_ref[h] - ci * bim_ref[h]
        bbim_ref[h] = cr * bim_ref[h] + ci * bre_ref[h]


def _s5_params(lam_re, lam_im, log_dt, b_re, b_im):
    depth = lam_re.shape[0]
    gs = (None, S5_GROUPS, S5_STATE)
    bs = (None, S5_GROUP, S5_GROUPS, S5_STATE)
    at3 = lambda l: (l, 0, 0)
    at4 = lambda l: (l, 0, 0, 0)
    return pl.pallas_call(
        _s5_param_body,
        grid=(depth,),
        in_specs=[pl.BlockSpec(gs, at3), pl.BlockSpec(gs, at3),
                  pl.BlockSpec((None, S5_GROUPS, 1), at3),
                  pl.BlockSpec(bs, at4), pl.BlockSpec(bs, at4)],
        out_specs=[pl.BlockSpec(gs, at3), pl.BlockSpec(gs, at3),
                   pl.BlockSpec(bs, at4), pl.BlockSpec(bs, at4)],
        out_shape=[jax.ShapeDtypeStruct((depth, S5_GROUPS, S5_STATE), f32),
                   jax.ShapeDtypeStruct((depth, S5_GROUPS, S5_STATE), f32),
                   jax.ShapeDtypeStruct((depth, S5_GROUP, S5_GROUPS, S5_STATE), f32),
                   jax.ShapeDtypeStruct((depth, S5_GROUP, S5_GROUPS, S5_STATE), f32)],
        compiler_params=_cparams(("arbitrary",)),
        name="s5_params",
    )(lam_re, lam_im, log_dt.reshape(depth, S5_GROUPS, 1), b_re, b_im)


def _gelu_tanh(x):
    return 0.5 * x * (1.0 + jnp.tanh(math.sqrt(2.0 / math.pi) * (x + 0.044715 * (x * x * x))))


def _s5_body(u_ref, zs_ref, h0r_ref, h0i_ref, ar_ref, ai_ref, wbu_ref, wcr_ref, wci_ref, d_ref, wglu_ref,
             o_ref, sr_ref, si_ref, xr_scr, xi_scr, y_scr, str_scr, sti_scr, *, tc, n_last):
    c = pl.program_id(1)

    @pl.when(c == 0)
    def _():
        str_scr[...] = h0r_ref[0]
        sti_scr[...] = h0i_ref[0]

    n_ct = S5_SG_STATE // LANES

    def gather_planes(scr, rows):
        return jnp.concatenate([scr[j, rows, :] for j in range(n_ct)], axis=1)

    def scatter_planes(scr, rows, val):
        for j in range(n_ct):
            scr[j, rows, :] = val[:, j * LANES:(j + 1) * LANES]

    for sg in range(S5_SG):
        rows = pl.ds(sg, tc, stride=S5_SG)
        u_sg = u_ref[:, sg * LANES:(sg + 1) * LANES]
        bu = jnp.dot(u_sg.astype(bf16), wbu_ref[sg], preferred_element_type=f32)
        scatter_planes(xr_scr, rows, bu[:, :S5_SG_STATE])
        scatter_planes(xi_scr, rows, bu[:, S5_SG_STATE:])

    ar = ar_ref[...]
    ai = ai_ref[...]

    def step(t, carry):
        xr, xi = carry
        r = pl.ds(pl.multiple_of(t * S5_SG, S5_SG), S5_SG)
        nr = ar * xr - ai * xi + gather_planes(xr_scr, r)
        ni = ar * xi + ai * xr + gather_planes(xi_scr, r)
        scatter_planes(xr_scr, r, nr)
        scatter_planes(xi_scr, r, ni)
        return nr, ni

    xr, xi = lax.fori_loop(0, tc, step, (str_scr[...], sti_scr[...]), unroll=S5_SCAN_UNROLL)
    str_scr[...] = xr
    sti_scr[...] = xi

    for sg in range(S5_SG):
        rows = pl.ds(sg, tc, stride=S5_SG)
        cs = slice(sg * LANES, (sg + 1) * LANES)
        y = (jnp.dot(gather_planes(xr_scr, rows).astype(bf16), wcr_ref[sg], preferred_element_type=f32)
             - jnp.dot(gather_planes(xi_scr, rows).astype(bf16), wci_ref[sg], preferred_element_type=f32)
             + d_ref[:, cs] * u_ref[:, cs])
        y_scr[:, cs] = _gelu_tanh(y)

    y = y_scr[...]
    gate = _sigmoid(jnp.dot(y.astype(bf16), wglu_ref[...], preferred_element_type=f32))
    o_ref[...] = (y * gate * zs_ref[...].astype(f32)).astype(o_ref.dtype)

    @pl.when(c == pl.num_programs(1) - 1)
    def _():
        last = slice((n_last - 1) * S5_SG, n_last * S5_SG)
        sr_ref[0] = gather_planes(xr_scr, last)
        si_ref[0] = gather_planes(xi_scr, last)


def _s5(u, zg, h0_re, h0_im, prm, layer, nb, seq, n_real):
    tc = min(seq, 256)
    nch = seq // tc
    n_last = n_real - (nch - 1) * tc
    zs_col = ATTN_WIDTH // S5_WIDTH
    fix2 = lambda b, c: (0, 0)
    fix3 = lambda b, c: (0, 0, 0)
    st = lambda b, c: (b, 0, 0)
    out, s_re, s_im = pl.pallas_call(
        functools.partial(_s5_body, tc=tc, n_last=n_last),
        grid=(nb, nch),
        in_specs=[pl.BlockSpec((tc, S5_WIDTH), lambda b, c: (b * nch + c, 0)),
                  pl.BlockSpec((tc, S5_WIDTH), lambda b, c: (b * nch + c, zs_col)),
                  pl.BlockSpec((1, S5_SG, S5_SG_STATE), st),
                  pl.BlockSpec((1, S5_SG, S5_SG_STATE), st),
                  _layer_spec((S5_SG, S5_SG_STATE), layer, fix2),
                  _layer_spec((S5_SG, S5_SG_STATE), layer, fix2),
                  _layer_spec((S5_SG, LANES, 2 * S5_SG_STATE), layer, fix3),
                  _layer_spec((S5_SG, S5_SG_STATE, LANES), layer, fix3),
                  _layer_spec((S5_SG, S5_SG_STATE, LANES), layer, fix3),
                  _layer_spec((1, S5_WIDTH), layer, fix2),
                  _layer_spec((S5_WIDTH, S5_WIDTH), layer, fix2)],
        out_specs=[pl.BlockSpec((tc, S5_WIDTH), lambda b, c: (b * nch + c, 0)),
                   pl.BlockSpec((1, S5_SG, S5_SG_STATE), st),
                   pl.BlockSpec((1, S5_SG, S5_SG_STATE), st)],
        out_shape=[jax.ShapeDtypeStruct((nb * seq, S5_WIDTH), bf16),
                   jax.ShapeDtypeStruct((nb, S5_SG, S5_SG_STATE), f32),
                   jax.ShapeDtypeStruct((nb, S5_SG, S5_SG_STATE), f32)],
        scratch_shapes=[pltpu.VMEM((S5_SG_STATE // LANES, tc * S5_SG, LANES), f32),
                        pltpu.VMEM((S5_SG_STATE // LANES, tc * S5_SG, LANES), f32),
                        pltpu.VMEM((tc, S5_WIDTH), f32),
                        pltpu.VMEM((S5_SG, S5_SG_STATE), f32),
                        pltpu.VMEM((S5_SG, S5_SG_STATE), f32)],
        compiler_params=_cparams(("parallel", "arbitrary")),
        name="s5_scan",
    )(u, zg, h0_re.reshape(nb, S5_SG, S5_SG_STATE), h0_im.reshape(nb, S5_SG, S5_SG_STATE),
      prm["a_re"], prm["a_im"], prm["w_bu"], prm["w_c_re"], prm["w_c_im"], prm["d"], prm["w_glu"])
    return out, s_re.reshape(nb, S5_GROUPS, S5_STATE), s_im.reshape(nb, S5_GROUPS, S5_STATE)


def _merge_body(a_ref, s_ref, wa_ref, ws_ref, ga_ref, gs_ref, o_ref):
    o_a = jnp.dot(a_ref[...], wa_ref[...], preferred_element_type=f32)
    o_s = jnp.dot(s_ref[...], ws_ref[...], preferred_element_type=f32)
    o_ref[...] = (ga_ref[...].astype(f32) * o_a + gs_ref[...].astype(f32) * o_s).astype(o_ref.dtype)


def _merge(a_in, s_in, w_a, w_s, zg, layer):
    m = a_in.shape[0]
    tm = _row_tile(m, MM_ROW_TILE)
    tn = MM_COL_TILE
    ga0 = (ATTN_WIDTH + S5_WIDTH) // tn
    gs0 = (ATTN_WIDTH + S5_WIDTH + D_MODEL) // tn
    return pl.pallas_call(
        _merge_body,
        grid=(m // tm, D_MODEL // tn),
        in_specs=[pl.BlockSpec((tm, ATTN_WIDTH), lambda i, j: (i, 0)),
                  pl.BlockSpec((tm, S5_WIDTH), lambda i, j: (i, 0)),
                  _layer_spec((ATTN_WIDTH, tn), layer, lambda i, j: (0, j)),
                  _layer_spec((S5_WIDTH, tn), layer, lambda i, j: (0, j)),
                  pl.BlockSpec((tm, tn), lambda i, j: (i, ga0 + j)),
                  pl.BlockSpec((tm, tn), lambda i, j: (i, gs0 + j))],
        out_specs=pl.BlockSpec((tm, tn), lambda i, j: (i, j)),
        out_shape=jax.ShapeDtypeStruct((m, D_MODEL), bf16),
        compiler_params=_cparams(("parallel", "parallel")),
        name="merge",
    )(a_in, s_in, w_a, w_s, zg, zg)


def _out_body(x_ref, m_ref, w_ref, o_ref):
    o_ref[...] = x_ref[...] + jnp.dot(m_ref[...], w_ref[...], preferred_element_type=f32)


def _out_proj(x, merged, w_out, layer):
    m = x.shape[0]
    tm = _row_tile(m, MM_ROW_TILE)
    tn = MM_COL_TILE
    return pl.pallas_call(
        _out_body,
        grid=(m // tm, D_MODEL // tn),
        in_specs=[pl.BlockSpec((tm, tn), lambda i, j: (i, j)),
                  pl.BlockSpec((tm, D_MODEL), lambda i, j: (i, 0)),
                  _layer_spec((D_MODEL, tn), layer, lambda i, j: (0, j))],
        out_specs=pl.BlockSpec((tm, tn), lambda i, j: (i, j)),
        out_shape=jax.ShapeDtypeStruct((m, D_MODEL), f32),
        compiler_params=_cparams(("parallel", "parallel")),
        name="out_proj",
    )(x, merged, w_out)


def _out_norm_body(x_ref, m_ref, w_ref, g_ref, o_ref, h_ref):
    y = x_ref[...] + jnp.dot(m_ref[...], w_ref[...], preferred_element_type=f32)
    o_ref[...] = y
    ms = jnp.mean(y * y, axis=-1, keepdims=True)
    h_ref[...] = (y * lax.rsqrt(ms + EPS) * g_ref[...]).astype(h_ref.dtype)


def _out_proj_norm(x, merged, w_out, gains, layer):
    m = x.shape[0]
    tm = _row_tile(m, 512)
    row = lambda i: (i, 0)
    return pl.pallas_call(
        _out_norm_body,
        grid=(m // tm,),
        in_specs=[pl.BlockSpec((tm, D_MODEL), row),
                  pl.BlockSpec((tm, D_MODEL), row),
                  _layer_spec((D_MODEL, D_MODEL), layer, lambda i: (0, 0)),
                  _layer_spec((1, D_MODEL), layer + 1, lambda i: (0, 0))],
        out_specs=[pl.BlockSpec((tm, D_MODEL), row), pl.BlockSpec((tm, D_MODEL), row)],
        out_shape=[jax.ShapeDtypeStruct((m, D_MODEL), f32),
                   jax.ShapeDtypeStruct((m, D_MODEL), bf16)],
        compiler_params=_cparams(("parallel",)),
        name="out_proj_norm",
    )(x, merged, w_out, gains)


def _rope_tables(pos):
    posf = pos.astype(f32)[:, None]
    half = HEAD_DIM // 2
    inv = ROPE_THETA ** (-jnp.arange(half, dtype=f32) / half)
    ang = posf * inv[None, :]
    cos, sin = jnp.cos(ang), jnp.sin(ang)
    cos_h = jnp.concatenate([cos, cos], axis=1)
    sin_h = jnp.concatenate([-sin, sin], axis=1)
    half_i = IDX_DIM // 2
    inv_i = ROPE_THETA ** (-jnp.arange(half_i, dtype=f32) / half_i)
    ang_i = posf * inv_i[None, :]
    cos_i, sin_i = jnp.cos(ang_i), jnp.sin(ang_i)
    zeros = jnp.zeros_like(cos_i)
    pad = jnp.zeros((pos.shape[0], LANES - IDX_DIM), f32)
    cos_k = jnp.concatenate([cos_i, cos_i, pad], axis=1)
    sin_a = jnp.concatenate([-sin_i, zeros, pad], axis=1)
    sin_b = jnp.concatenate([zeros, sin_i, pad], axis=1)
    return dict(cos_h=cos_h, sin_h=sin_h, cos_k=cos_k, sin_a=sin_a, sin_b=sin_b,
                cos_ht=cos.T, sin_ht=sin.T, cos_t=cos_i.T, sin_t=sin_i.T)


def _in_proj_offsets():
    offs = [0]
    for s in IN_SIZES:
        offs.append(offs[-1] + s)
    return offs


def _repack_body(w_ref, wq_ref, wkv_ref, wqiw_ref, wki_ref, wu_ref, wzg_ref):
    offs = _in_proj_offsets()
    seg = lambda k: w_ref[offs[k]:offs[k + 1], :].astype(bf16)
    cols = w_ref.shape[1]
    wq_ref[...] = seg(0)
    wkv_ref[...] = w_ref[offs[1]:offs[3], :].astype(bf16)
    n_qi = IDX_HEADS * IDX_DIM
    wqiw_ref[:n_qi, :] = seg(3)
    wqiw_ref[n_qi:, :] = seg(5)
    wki_ref[:IDX_DIM, :] = seg(4)
    wki_ref[IDX_DIM:, :] = jnp.zeros((LANES - IDX_DIM, cols), bf16)
    wu_ref[...] = seg(7)
    row = 0
    for k in (6, 8, 9, 10):
        wzg_ref[row:row + IN_SIZES[k], :] = seg(k)
        row += IN_SIZES[k]


def _repack_in_proj(w_in_t):
    depth, n_in, d = w_in_t.shape
    cols = 256
    n_qiw = IDX_HEADS * IDX_DIM + IDX_HEADS
    n_zg = ATTN_WIDTH + S5_WIDTH + 2 * D_MODEL
    heights = (ATTN_WIDTH, 2 * KV_WIDTH, n_qiw, LANES, S5_WIDTH, n_zg)
    at = lambda l, c: (l, 0, c)
    return pl.pallas_call(
        _repack_body,
        grid=(depth, d // cols),
        in_specs=[pl.BlockSpec((None, n_in, cols), at)],
        out_specs=[pl.BlockSpec((None, hgt, cols), at) for hgt in heights],
        out_shape=[jax.ShapeDtypeStruct((depth, hgt, d), bf16) for hgt in heights],
        compiler_params=_cparams(("parallel", "parallel")),
        name="repack_in_proj",
    )(w_in_t)


def _pack_weights(w_in, w_glu, w_br_attn, w_br_s5, w_out, norm_gain, q_norm_gain, k_norm_gain):
    depth = w_in.shape[0]
    wt_q, wt_kv, wt_qi, wt_ki, wt_u, wt_zg = _repack_in_proj(jnp.swapaxes(w_in, 1, 2))
    return dict(
        wt_q=wt_q, wt_kv=wt_kv, wt_qi=wt_qi, wt_ki=wt_ki, wt_u=wt_u, wt_zg=wt_zg,
        w_glu=w_glu.astype(bf16), w_br_attn=w_br_attn.astype(bf16),
        w_br_s5=w_br_s5.astype(bf16), w_out=w_out.astype(bf16),
        norm_gain=norm_gain.reshape(depth, 1, D_MODEL),
        q_gain=q_norm_gain.reshape(depth, HEAD_DIM, 1),
        k_gain=k_norm_gain.reshape(depth, 1, HEAD_DIM))


def _block_diag(blocks):
    depth, sg, gg, r, c = blocks.shape
    eye = jnp.eye(gg, dtype=blocks.dtype)
    return jnp.einsum("lsgrc,gk->lsgrkc", blocks, eye).reshape(depth, sg, gg * r, gg * c)


def _s5_stacked_params(lam_re, lam_im, log_dt, b_re, b_im, c_re, c_im, d, w_glu_bf16):
    depth = lam_re.shape[0]
    a_re, a_im, bb_re, bb_im = _s5_params(lam_re, lam_im, log_dt,
                                          jnp.transpose(b_re, (0, 3, 1, 2)), jnp.transpose(b_im, (0, 3, 1, 2)))
    def bu_blocks(bb):
        return jnp.transpose(bb, (0, 2, 1, 3)).reshape(depth, S5_SG, S5_SG_GROUPS, S5_GROUP, S5_STATE)
    w_bu = jnp.concatenate([_block_diag(bu_blocks(bb_re)), _block_diag(bu_blocks(bb_im))], axis=3)
    def c_blocks(cm):
        return jnp.transpose(cm, (0, 1, 3, 2)).reshape(depth, S5_SG, S5_SG_GROUPS, S5_STATE, S5_GROUP)
    return dict(a_re=a_re.reshape(depth, S5_SG, S5_SG_STATE), a_im=a_im.reshape(depth, S5_SG, S5_SG_STATE),
                w_bu=w_bu.astype(bf16),
                w_c_re=_block_diag(c_blocks(c_re)).astype(bf16),
                w_c_im=_block_diag(c_blocks(c_im)).astype(bf16),
                d=d.reshape(depth, 1, S5_WIDTH), w_glu=w_glu_bf16)


def _projections(h, wts, tabs, layer):
    qt = _proj_qt(h, wts["wt_q"], wts["q_gain"], tabs["cos_ht"], tabs["sin_ht"], layer)
    k, v, kb, vt = _proj_kv(h, wts["wt_kv"], wts["k_gain"], tabs["cos_h"], tabs["sin_h"], layer)
    qit, wt = _proj_qit(h, wts["wt_qi"], tabs["cos_t"], tabs["sin_t"], layer)
    ki, kib = _proj_ki(h, wts["wt_ki"], tabs["cos_k"], tabs["sin_a"], tabs["sin_b"], layer)
    u = _proj_plain(h, wts["wt_u"], f32, layer)
    zg = _proj_zg(h, wts["wt_zg"], layer)
    return dict(qt=qt, k=k, v=v, kb=kb, vt=vt, qit=qit, wt=wt, ki=ki, kib=kib, u=u, zg=zg)


def _finish(x, a_in, s_in, zg, wts, layer, last):
    merged = _merge(a_in, s_in, wts["w_br_attn"], wts["w_br_s5"], zg, layer)
    if last:
        return _out_proj(x, merged, wts["w_out"], layer), None
    return _out_proj_norm(x, merged, wts["w_out"], wts["norm_gain"], layer)


def kernel(x_prompt, x_sample, cache_k, cache_v, cache_kidx, state_s5_re, state_s5_im, page_table, norm_gain, w_in, q_norm_gain, k_norm_gain, s5_lam_re, s5_lam_im, s5_log_dt, s5_b_re, s5_b_im, s5_c_re, s5_c_im, s5_d, w_glu, w_br_attn, w_br_s5, w_out):
    depth = w_in.shape[0]
    b_p, t_p = x_prompt.shape[:2]
    b_s, t_s = x_sample.shape[:2]
    tok = SAMPLE_ROWS
    assert t_s <= tok
    n_pages = page_table.shape[1]
    past = n_pages * PAGE_SIZE
    n_phys = cache_k.shape[1]

    tabs_p = _rope_tables(jnp.tile(jnp.arange(t_p, dtype=i32), b_p))
    tabs_s = _rope_tables(jnp.tile(past + jnp.arange(tok, dtype=i32), b_s))

    xp = x_prompt.reshape(b_p * t_p, D_MODEL)
    xs = jnp.pad(x_sample, ((0, 0), (0, tok - t_s), (0, 0))).reshape(b_s * tok, D_MODEL)

    cache_k4 = cache_k.reshape(depth, n_phys, PAGE_SIZE * N_KV_HEADS, HEAD_DIM)
    cache_v4 = cache_v.reshape(depth, n_phys, PAGE_SIZE * N_KV_HEADS, HEAD_DIM)
    cache_kidx_t = jnp.swapaxes(cache_kidx, 2, 3)
    zeros_state = jnp.zeros((b_p, S5_GROUPS, S5_STATE), f32)

    def new_page(a):
        w = a.shape[-1]
        return jnp.pad(a.reshape(b_s, tok, w), ((0, 0), (0, PAGE_SIZE - tok), (0, 0)))

    wts = _pack_weights(w_in, w_glu, w_br_attn, w_br_s5, w_out, norm_gain, q_norm_gain, k_norm_gain)
    s5p = _s5_stacked_params(s5_lam_re, s5_lam_im, s5_log_dt, s5_b_re, s5_b_im, s5_c_re, s5_c_im, s5_d,
                             wts["w_glu"])

    outs_p, outs_s = [], []
    hp = _rmsnorm(xp, wts["norm_gain"], 0)
    hs = _rmsnorm(xs, wts["norm_gain"], 0)
    for l in range(depth):
        last = l == depth - 1
        pp = _projections(hp, wts, tabs_p, l)
        a_in = _attn_prompt(pp["qt"], pp["qit"], pp["wt"], pp["zg"], pp["kb"], pp["vt"], pp["kib"], b_p, t_p)
        s_in, sr_p, si_p = _s5(pp["u"], pp["zg"], zeros_state, zeros_state, s5p, l, b_p, t_p, t_p)
        xp, hp = _finish(xp, a_in, s_in, pp["zg"], wts, l, last)
        outs_p.append((pp["k"].reshape(b_p, t_p, N_KV_HEADS, HEAD_DIM),
                       pp["v"].reshape(b_p, t_p, N_KV_HEADS, HEAD_DIM),
                       pp["ki"].reshape(b_p, t_p, IDX_DIM), sr_p, si_p))

        ps = _projections(hs, wts, tabs_s, l)
        qi_rows = jnp.transpose(ps["qit"].reshape(IDX_HEADS, IDX_DIM, b_s, tok), (2, 0, 3, 1)
                                ).reshape(b_s, IDX_HEADS * tok, IDX_DIM)
        w_rows = jnp.broadcast_to(
            jnp.transpose(ps["wt"].reshape(IDX_HEADS, b_s, tok), (1, 0, 2)).reshape(b_s, IDX_HEADS * tok, 1),
            (b_s, IDX_HEADS * tok, PAGE_SIZE))
        keys, thr = _idx_sample(page_table, qi_rows, w_rows, cache_kidx_t, l,
                                jnp.swapaxes(new_page(ps["ki"]), 1, 2), t_s)
        q_rows = jnp.transpose(ps["qt"].reshape(N_HEADS, HEAD_DIM, b_s, tok), (2, 0, 3, 1)
                               ).reshape(b_s, N_HEADS * tok, HEAD_DIM)
        k_s = ps["k"].reshape(b_s * tok, KV_WIDTH)
        v_s = ps["v"].reshape(b_s * tok, KV_WIDTH)
        o_rows = _attn_sample(page_table, q_rows, keys, thr, cache_k4, cache_v4, l,
                              new_page(k_s), new_page(v_s))
        attn_s = jnp.transpose(o_rows.reshape(b_s, N_HEADS, tok, HEAD_DIM), (0, 2, 1, 3)
                               ).reshape(b_s * tok, ATTN_WIDTH)
        a_in_s = _gate_mul(attn_s, ps["zg"][:, :ATTN_WIDTH])
        s_in_s, sr_s, si_s = _s5(ps["u"], ps["zg"], state_s5_re[l], state_s5_im[l], s5p, l, b_s, tok, t_s)
        xs, hs = _finish(xs, a_in_s, s_in_s, ps["zg"], wts, l, last)
        real = lambda a: a.reshape((b_s, tok) + a.shape[1:])[:, :t_s]
        outs_s.append((real(k_s).reshape(b_s, t_s, N_KV_HEADS, HEAD_DIM),
                       real(v_s).reshape(b_s, t_s, N_KV_HEADS, HEAD_DIM),
                       real(ps["ki"]), sr_s, si_s))

    k_prompt, v_prompt, kidx_prompt, s5_re_prompt, s5_im_prompt = [jnp.stack(a) for a in zip(*outs_p)]
    k_sample, v_sample, kidx_sample, s5_re_sample, s5_im_sample = [jnp.stack(a) for a in zip(*outs_s)]
    y_prompt = xp.reshape(b_p, t_p, D_MODEL)
    y_sample = xs.reshape(b_s, tok, D_MODEL)[:, :t_s]
    return (y_prompt, y_sample, k_prompt, v_prompt, kidx_prompt, s5_re_prompt, s5_im_prompt,
            k_sample, v_sample, kidx_sample, s5_re_sample, s5_im_sample)
```

```python
import functools
import math

import jax
import jax.numpy as jnp
from jax import lax
from jax.experimental import pallas as pl
from jax.experimental.pallas import tpu as pltpu

D_MODEL = 2048
PAGE_SIZE = 128
N_HEADS = 16
HEAD_DIM = 128
N_KV_HEADS = 4
HEADS_PER_KV = N_HEADS // N_KV_HEADS
ATTN_WIDTH = N_HEADS * HEAD_DIM
KV_WIDTH = N_KV_HEADS * HEAD_DIM
IDX_HEADS = 16
IDX_DIM = 64
TOPK_MAX = 256
S5_WIDTH = D_MODEL // 2
S5_GROUP = 16
S5_GROUPS = S5_WIDTH // S5_GROUP
S5_STATE = 64
ROPE_THETA = 10000.0
EPS = 1e-6
IN_SIZES = (ATTN_WIDTH, KV_WIDTH, KV_WIDTH, IDX_HEADS * IDX_DIM, IDX_DIM, IDX_HEADS,
            ATTN_WIDTH, S5_WIDTH, S5_WIDTH, D_MODEL, D_MODEL)

LANES = 128
SUBLANES = 8
PACKED_SUBLANES = 16
VMEM_LIMIT_BYTES = 56 * 1024 * 1024
MM_ROW_TILE = 1024
MM_COL_TILE = 1024
ROW_TILE = 512
REPACK_COLS = 256
S5_CHUNK = 256

S5_SG = S5_WIDTH // LANES
S5_SG_GROUPS = S5_GROUPS // S5_SG
S5_SG_STATE = S5_SG_GROUPS * S5_STATE
S5_SCAN_UNROLL = 8

SAMPLE_ROWS = 16

ATTN_Q_TILE = 256
ATTN_ROW_BLOCK = 64
COUNT_ROWS = 32
PAGES_PER_STEP = 32
IDX_PAGES_PER_STEP = 64
LOG2_E = math.log2(math.e)

INT_MIN = -2 ** 31
MASK_BIAS = -1e30

f32 = jnp.float32
bf16 = jnp.bfloat16
i32 = jnp.int32


def _cparams(sem):
    return pltpu.CompilerParams(dimension_semantics=sem, vmem_limit_bytes=VMEM_LIMIT_BYTES)


def _row_tile(m, cap):
    return m if m <= cap else cap


def _norm_body(x_ref, g_ref, o_ref):
    x = x_ref[...]
    ms = jnp.mean(x * x, axis=-1, keepdims=True)
    o_ref[...] = (x * lax.rsqrt(ms + EPS) * g_ref[...]).astype(o_ref.dtype)


def _layer_spec(block, layer, index_map):
    return pl.BlockSpec((None,) + tuple(block), lambda *idx: (layer,) + tuple(index_map(*idx)))


def _rmsnorm(x, gains, layer):
    m, d = x.shape
    tm = _row_tile(m, ROW_TILE)
    return pl.pallas_call(
        _norm_body,
        grid=(m // tm,),
        in_specs=[pl.BlockSpec((tm, d), lambda i: (i, 0)),
                  _layer_spec((1, d), layer, lambda i: (0, 0))],
        out_specs=pl.BlockSpec((tm, d), lambda i: (i, 0)),
        out_shape=jax.ShapeDtypeStruct((m, d), bf16),
        compiler_params=_cparams(("parallel",)),
        name="rmsnorm",
    )(x, gains)


def _head_norm_rope(x, gain, cos, sin):
    ms = jnp.mean(x * x, axis=-1, keepdims=True)
    y = x * lax.rsqrt(ms + EPS) * gain
    return y * cos + pltpu.roll(y, HEAD_DIM // 2, 1) * sin


def _qt_body(wt_ref, h_ref, g_ref, cos_ref, sin_ref, o_ref, *, heads):
    acc = lax.dot_general(wt_ref[...], h_ref[...], (((1,), (1,)), ((), ())),
                          preferred_element_type=f32)
    c = cos_ref[...]
    s = sin_ref[...]
    g = g_ref[...]
    half = HEAD_DIM // 2
    scale = HEAD_DIM ** -0.5 * LOG2_E
    for hh in range(heads):
        r0 = hh * HEAD_DIM
        x = acc[r0:r0 + HEAD_DIM]
        ms = jnp.mean(x * x, axis=0, keepdims=True)
        y = x * lax.rsqrt(ms + EPS) * g
        x1 = y[:half]
        x2 = y[half:]
        o_ref[r0:r0 + half, :] = ((x1 * c - x2 * s) * scale).astype(o_ref.dtype)
        o_ref[r0 + half:r0 + HEAD_DIM, :] = ((x2 * c + x1 * s) * scale).astype(o_ref.dtype)


def _proj_qt(h, wt_q, gain, cos_ht, sin_ht, layer):
    m, d = h.shape
    tm = _row_tile(m, MM_ROW_TILE)
    tn = MM_COL_TILE
    return pl.pallas_call(
        functools.partial(_qt_body, heads=tn // HEAD_DIM),
        grid=(m // tm, ATTN_WIDTH // tn),
        in_specs=[_layer_spec((tn, d), layer, lambda i, j: (j, 0)),
                  pl.BlockSpec((tm, d), lambda i, j: (i, 0)),
                  _layer_spec((HEAD_DIM, 1), layer, lambda i, j: (0, 0)),
                  pl.BlockSpec((HEAD_DIM // 2, tm), lambda i, j: (0, i)),
                  pl.BlockSpec((HEAD_DIM // 2, tm), lambda i, j: (0, i))],
        out_specs=pl.BlockSpec((tn, tm), lambda i, j: (j, i)),
        out_shape=jax.ShapeDtypeStruct((ATTN_WIDTH, m), bf16),
        compiler_params=_cparams(("parallel", "parallel")),
        name="proj_q_t",
    )(wt_q, h, gain, cos_ht, sin_ht)


def _dot_nt(a, b_t):
    return lax.dot_general(a, b_t, (((1,), (1,)), ((), ())), preferred_element_type=f32)


def _kv_body(h_ref, wt_ref, g_ref, cos_ref, sin_ref, k_ref, v_ref, kb_ref, vt_ref, *, chunk):
    h = h_ref[...]
    acc = _dot_nt(h, wt_ref[...])
    cos = cos_ref[...]
    sin = sin_ref[...]
    g = g_ref[...]
    tm = h.shape[0]
    for hh in range(N_KV_HEADS):
        sl = slice(hh * HEAD_DIM, (hh + 1) * HEAD_DIM)
        kh = _head_norm_rope(acc[:, sl], g, cos, sin)
        rows = pl.ds(hh, tm, stride=N_KV_HEADS)
        k_ref[rows, :] = kh
        v_ref[rows, :] = acc[:, KV_WIDTH + hh * HEAD_DIM:KV_WIDTH + (hh + 1) * HEAD_DIM]
        kb_ref[:, sl] = kh.astype(bf16)
    vt = _dot_nt(wt_ref[KV_WIDTH:, :], h)
    for cc in range(vt_ref.shape[0]):
        vt_ref[cc] = vt[:, cc * chunk:(cc + 1) * chunk].astype(bf16)


def _proj_kv(h, wt_kv, gain, cos, sin, layer):
    m, d = h.shape
    tm = _row_tile(m, ROW_TILE)
    chunk = min(tm, ATTN_Q_TILE)
    row = lambda i: (i, 0)
    fix = lambda i: (0, 0)
    return pl.pallas_call(
        functools.partial(_kv_body, chunk=chunk),
        grid=(m // tm,),
        in_specs=[pl.BlockSpec((tm, d), row),
                  _layer_spec((2 * KV_WIDTH, d), layer, fix),
                  _layer_spec((1, HEAD_DIM), layer, fix),
                  pl.BlockSpec((tm, HEAD_DIM), row),
                  pl.BlockSpec((tm, HEAD_DIM), row)],
        out_specs=[pl.BlockSpec((tm * N_KV_HEADS, HEAD_DIM), row),
                   pl.BlockSpec((tm * N_KV_HEADS, HEAD_DIM), row),
                   pl.BlockSpec((tm, KV_WIDTH), row),
                   pl.BlockSpec((tm // chunk, KV_WIDTH, chunk), lambda i: (i, 0, 0))],
        out_shape=[jax.ShapeDtypeStruct((m * N_KV_HEADS, HEAD_DIM), f32),
                   jax.ShapeDtypeStruct((m * N_KV_HEADS, HEAD_DIM), f32),
                   jax.ShapeDtypeStruct((m, KV_WIDTH), bf16),
                   jax.ShapeDtypeStruct((m // chunk, KV_WIDTH, chunk), bf16)],
        compiler_params=_cparams(("parallel",)),
        name="proj_kv",
    )(h, wt_kv, gain, cos, sin)


def _qit_body(wt_ref, h_ref, cos_ref, sin_ref, qit_ref, wt_out_ref):
    acc = lax.dot_general(wt_ref[...], h_ref[...], (((1,), (1,)), ((), ())),
                          preferred_element_type=f32)
    c = cos_ref[...]
    s = sin_ref[...]
    half = IDX_DIM // 2
    scale = IDX_DIM ** -0.5
    for hh in range(IDX_HEADS):
        r0 = hh * IDX_DIM
        x1 = acc[r0:r0 + half]
        x2 = acc[r0 + half:r0 + IDX_DIM]
        qit_ref[r0:r0 + half, :] = ((x1 * c - x2 * s) * scale).astype(qit_ref.dtype)
        qit_ref[r0 + half:r0 + IDX_DIM, :] = ((x2 * c + x1 * s) * scale).astype(qit_ref.dtype)
    n_qi = IDX_HEADS * IDX_DIM
    wt_out_ref[...] = acc[n_qi:n_qi + IDX_HEADS] * (IDX_HEADS ** -0.5)


def _proj_qit(h, wt_qi, cos_t, sin_t, layer):
    m, d = h.shape
    tm = _row_tile(m, ROW_TILE)
    n_rows = wt_qi.shape[1]
    n_qi = IDX_HEADS * IDX_DIM
    return pl.pallas_call(
        _qit_body,
        grid=(m // tm,),
        in_specs=[_layer_spec((n_rows, d), layer, lambda i: (0, 0)),
                  pl.BlockSpec((tm, d), lambda i: (i, 0)),
                  pl.BlockSpec((IDX_DIM // 2, tm), lambda i: (0, i)),
                  pl.BlockSpec((IDX_DIM // 2, tm), lambda i: (0, i))],
        out_specs=[pl.BlockSpec((n_qi, tm), lambda i: (0, i)),
                   pl.BlockSpec((IDX_HEADS, tm), lambda i: (0, i))],
        out_shape=[jax.ShapeDtypeStruct((n_qi, m), bf16),
                   jax.ShapeDtypeStruct((IDX_HEADS, m), f32)],
        compiler_params=_cparams(("parallel",)),
        name="proj_qi_t",
    )(wt_qi, h, cos_t, sin_t)


def _ki_body(h_ref, w_ref, cos_ref, sina_ref, sinb_ref, ki_ref, kib_ref):
    x = _dot_nt(h_ref[...], w_ref[...])
    half = IDX_DIM // 2
    r = (x * cos_ref[...] + pltpu.roll(x, LANES - half, 1) * sina_ref[...]
         + pltpu.roll(x, half, 1) * sinb_ref[...])
    ki = r[:, :IDX_DIM]
    ki_ref[...] = ki
    kib_ref[...] = ki.astype(bf16)


def _proj_ki(h, w_ki, cos_k, sin_a, sin_b, layer):
    m, d = h.shape
    tm = _row_tile(m, ROW_TILE)
    row = lambda i: (i, 0)
    return pl.pallas_call(
        _ki_body,
        grid=(m // tm,),
        in_specs=[pl.BlockSpec((tm, d), row),
                  _layer_spec((LANES, d), layer, lambda i: (0, 0)),
                  pl.BlockSpec((tm, LANES), row),
                  pl.BlockSpec((tm, LANES), row),
                  pl.BlockSpec((tm, LANES), row)],
        out_specs=[pl.BlockSpec((tm, IDX_DIM), row)] * 2,
        out_shape=[jax.ShapeDtypeStruct((m, IDX_DIM), f32),
                   jax.ShapeDtypeStruct((m, IDX_DIM), bf16)],
        compiler_params=_cparams(("parallel",)),
        name="proj_ki",
    )(h, w_ki, cos_k, sin_a, sin_b)


def _plain_body(h_ref, w_ref, o_ref):
    o_ref[...] = _dot_nt(h_ref[...], w_ref[...]).astype(o_ref.dtype)


def _proj_plain(h, wt, out_dtype, layer):
    m, d = h.shape
    n = wt.shape[1]
    tm = _row_tile(m, MM_ROW_TILE)
    tn = MM_COL_TILE
    return pl.pallas_call(
        _plain_body,
        grid=(m // tm, n // tn),
        in_specs=[pl.BlockSpec((tm, d), lambda i, j: (i, 0)),
                  _layer_spec((tn, d), layer, lambda i, j: (j, 0))],
        out_specs=pl.BlockSpec((tm, tn), lambda i, j: (i, j)),
        out_shape=jax.ShapeDtypeStruct((m, n), out_dtype),
        compiler_params=_cparams(("parallel", "parallel")),
        name="proj_u",
    )(h, wt)


def _sigmoid(x):
    return 1.0 / (1.0 + jnp.exp(-x))


def _zg_body(h_ref, w_ref, o_ref, *, silu_tiles):
    acc = _dot_nt(h_ref[...], w_ref[...])
    sg = _sigmoid(acc)
    is_silu = pl.program_id(1) < silu_tiles
    o_ref[...] = jnp.where(is_silu, acc * sg, sg).astype(o_ref.dtype)


def _proj_zg(h, wt_zg, layer):
    m, d = h.shape
    n = wt_zg.shape[1]
    tm = _row_tile(m, MM_ROW_TILE)
    tn = MM_COL_TILE
    return pl.pallas_call(
        functools.partial(_zg_body, silu_tiles=(ATTN_WIDTH + S5_WIDTH) // tn),
        grid=(m // tm, n // tn),
        in_specs=[pl.BlockSpec((tm, d), lambda i, j: (i, 0)),
                  _layer_spec((tn, d), layer, lambda i, j: (j, 0))],
        out_specs=pl.BlockSpec((tm, tn), lambda i, j: (i, j)),
        out_shape=jax.ShapeDtypeStruct((m, n), bf16),
        compiler_params=_cparams(("parallel", "parallel")),
        name="proj_zg",
    )(h, wt_zg)


def _sortable_key(score):
    b = pltpu.bitcast(score + 0.0, i32)
    return b ^ ((b >> 31) & jnp.int32(0x7FFFFFFF))


def _kth_largest(count_ge, shape, k):
    zero = jnp.zeros(shape, i32)
    prefix = jnp.where(count_ge(zero) >= k, zero, jnp.full(shape, INT_MIN, i32))

    def bit_body(bi, prefix):
        cand = prefix | (jnp.int32(1) << (30 - bi))
        return jnp.where(count_ge(cand) >= k, cand, prefix)

    thr = lax.fori_loop(0, 31, bit_body, prefix)
    return jnp.maximum(thr, jnp.int32(INT_MIN + 1))


def _resolve_ties(thr, k, count_ge, count_tied_before, demote, n_pos_bits, active=None):
    excess = count_ge(thr) - k
    if active is not None:
        excess = jnp.where(active, excess, 0)

    @pl.when(jnp.max(excess) > 0)
    def _():
        need = k - count_ge(thr + 1)

        def bit_body(bi, pos):
            cand = pos | (jnp.int32(1) << (n_pos_bits - 1 - bi))
            return jnp.where(count_tied_before(thr, cand) < need, cand, pos)

        last = lax.fori_loop(0, n_pos_bits, bit_body, jnp.zeros(thr.shape, i32))
        demote(thr, last)


def _sublane_allreduce(x, op):
    for shift in (4, 2, 1):
        x = op(x, pltpu.roll(x, shift, 0))
    return x


def _attn_prompt_body(qt_ref, qit_ref, wt_ref, z_ref, k_ref, vt_ref, ki_ref, o_ref,
                      key_scr, acc_scr, m_scr, l_scr, alpha_scr, s_scr, p_scr, bias_scr,
                      *, tq, k_top, n_pos_bits):
    i = pl.program_id(1)
    n_chunks = i + 1
    half = tq // 2

    def score_keys(c, diagonal):
        for sub in range(2):
            r0 = pl.multiple_of(c * tq + sub * half, half)
            kic = ki_ref[pl.ds(r0, half), :]
            acc = jnp.zeros((half, tq), f32)
            for hh in range(IDX_HEADS):
                lg = jnp.dot(kic, qit_ref[hh * IDX_DIM:(hh + 1) * IDX_DIM, :],
                             preferred_element_type=f32)
                acc = acc + jnp.maximum(lg, 0.0) * wt_ref[hh:hh + 1, :]
            key = _sortable_key(acc)
            if diagonal:
                kpos = lax.broadcasted_iota(i32, (half, tq), 0) + sub * half
                qpos = lax.broadcasted_iota(i32, (half, tq), 1)
                key = jnp.where(kpos <= qpos, key, jnp.int32(INT_MIN))
            key_scr[pl.ds(r0, half), :] = key

    def full_chunk(c, carry):
        score_keys(c, False)
        return carry

    lax.fori_loop(0, i, full_chunk, 0)
    score_keys(i, True)

    @pl.when(n_chunks % 2 == 1)
    def _():
        key_scr[pl.ds(pl.multiple_of(n_chunks * tq, tq), tq), :] = jnp.full((tq, tq), INT_MIN, i32)

    def count_ge(cand):
        def body(c, cnt):
            kk = key_scr[pl.ds(pl.multiple_of(c * 2 * tq, 2 * tq), 2 * tq), :]
            hit = (kk >= cand).astype(i32)
            return cnt + jnp.sum(hit.reshape(2 * tq // COUNT_ROWS, COUNT_ROWS, tq), axis=0)

        cnt = lax.fori_loop(0, (n_chunks + 1) // 2, body, jnp.zeros((COUNT_ROWS, tq), i32))
        return jnp.sum(cnt, axis=0, keepdims=True)

    thr = _kth_largest(count_ge, (1, tq), k_top)

    def chunk_positions(c):
        return lax.broadcasted_iota(i32, (tq, tq), 0) + c * tq

    def count_tied_before(thr, pos_limit):
        def body(c, cnt):
            kk = key_scr[pl.ds(pl.multiple_of(c * tq, tq), tq), :]
            hit = ((kk == thr) & (chunk_positions(c) < pos_limit)).astype(i32)
            return cnt + jnp.sum(hit.reshape(tq // SUBLANES, SUBLANES, tq), axis=0)

        cnt = lax.fori_loop(0, n_chunks, body, jnp.zeros((SUBLANES, tq), i32))
        return jnp.sum(cnt, axis=0, keepdims=True)

    def demote(thr, last):
        def body(c, carry):
            rows = pl.ds(pl.multiple_of(c * tq, tq), tq)
            kk = key_scr[rows, :]
            key_scr[rows, :] = jnp.where((kk == thr) & (chunk_positions(c) > last), thr - 1, kk)
            return carry

        lax.fori_loop(0, n_chunks, body, 0)

    _resolve_ties(thr, k_top, count_ge, count_tied_before, demote, n_pos_bits)

    m_scr[...] = jnp.full(m_scr.shape, -jnp.inf, f32)
    l_scr[...] = jnp.zeros(l_scr.shape, f32)
    acc_scr[...] = jnp.zeros(acc_scr.shape, f32)
    n_sub = tq // SUBLANES
    d_sub = HEAD_DIM // SUBLANES

    def attend_chunk(c, carry):
        r0 = pl.multiple_of(c * tq, tq)
        bias_scr[...] = jnp.where(key_scr[pl.ds(r0, tq), :] >= thr, 0.0, MASK_BIAS)
        for g in range(N_KV_HEADS):
            kc = k_ref[pl.ds(r0, tq), g * HEAD_DIM:(g + 1) * HEAD_DIM]
            for hh in range(HEADS_PER_KV):
                h = g * HEADS_PER_KV + hh
                s = jnp.dot(kc, qt_ref[h * HEAD_DIM:(h + 1) * HEAD_DIM, :], preferred_element_type=f32)
                s_scr[h] = s + bias_scr[...]
        blocks = [slice(j * ATTN_ROW_BLOCK, (j + 1) * ATTN_ROW_BLOCK) for j in range(tq // ATTN_ROW_BLOCK)]
        blk_sub = ATTN_ROW_BLOCK // SUBLANES
        for h in range(N_HEADS):
            mx = m_scr[h]
            for rows in blocks:
                mx = jnp.maximum(mx, jnp.max(s_scr[h, rows, :].reshape(blk_sub, SUBLANES, tq), axis=0))
            m_new = _sublane_allreduce(mx, jnp.maximum)
            alpha_scr[h] = jnp.exp2(m_scr[h] - m_new)
            m_scr[h] = m_new
        for h in range(N_HEADS):
            p3 = jnp.exp2(s_scr[h].reshape(n_sub, SUBLANES, tq) - m_scr[h][None])
            p_scr[h] = p3.reshape(tq, tq).astype(bf16)
        ones_rows = jnp.ones((PACKED_SUBLANES, tq), bf16)
        for g in range(N_KV_HEADS):
            vtc = jnp.concatenate([vt_ref[c, g * HEAD_DIM:(g + 1) * HEAD_DIM, :], ones_rows], axis=0)
            for hh in range(HEADS_PER_KV):
                h = g * HEADS_PER_KV + hh
                hs = slice(h * HEAD_DIM, (h + 1) * HEAD_DIM)
                pv = jnp.dot(vtc, p_scr[h], preferred_element_type=f32)
                alpha = alpha_scr[h]
                acc = acc_scr[hs, :].reshape(d_sub, SUBLANES, tq) * alpha[None]
                acc_scr[hs, :] = acc.reshape(HEAD_DIM, tq) + pv[:HEAD_DIM]
                l_scr[h] = alpha * l_scr[h] + pv[HEAD_DIM:HEAD_DIM + SUBLANES]
        return carry

    lax.fori_loop(0, n_chunks, attend_chunk, 0)

    for h in range(N_HEADS):
        hs = slice(h * HEAD_DIM, (h + 1) * HEAD_DIM)
        out_t = (acc_scr[hs, :].reshape(d_sub, SUBLANES, tq) / l_scr[h][None]).reshape(HEAD_DIM, tq)
        o_ref[:, hs] = (out_t.T * z_ref[:, hs].astype(f32)).astype(o_ref.dtype)


def _attn_prompt(qt, qit, wt, zg, kb, vt, kib, batch, seq):
    tq = ATTN_Q_TILE
    assert seq % tq == 0
    nq = seq // tq
    k_top = min(TOPK_MAX, seq // 4)
    assert k_top <= tq
    qrow = lambda b, i: (b * nq + i, 0)
    qcol = lambda b, i: (0, b * nq + i)
    per_b = lambda b, i: (b, 0)
    return pl.pallas_call(
        functools.partial(_attn_prompt_body, tq=tq, k_top=k_top, n_pos_bits=(seq - 1).bit_length()),
        grid=(batch, nq),
        in_specs=[pl.BlockSpec((ATTN_WIDTH, tq), qcol),
                  pl.BlockSpec((IDX_HEADS * IDX_DIM, tq), qcol),
                  pl.BlockSpec((IDX_HEADS, tq), qcol),
                  pl.BlockSpec((tq, ATTN_WIDTH), qrow),
                  pl.BlockSpec((seq, KV_WIDTH), per_b),
                  pl.BlockSpec((nq, KV_WIDTH, tq), lambda b, i: (b, 0, 0)),
                  pl.BlockSpec((seq, IDX_DIM), per_b)],
        out_specs=pl.BlockSpec((tq, ATTN_WIDTH), qrow),
        out_shape=jax.ShapeDtypeStruct((batch * seq, ATTN_WIDTH), bf16),
        scratch_shapes=[pltpu.VMEM((seq + tq, tq), i32),
                        pltpu.VMEM((ATTN_WIDTH, tq), f32),
                        pltpu.VMEM((N_HEADS, SUBLANES, tq), f32),
                        pltpu.VMEM((N_HEADS, SUBLANES, tq), f32),
                        pltpu.VMEM((N_HEADS, SUBLANES, tq), f32),
                        pltpu.VMEM((N_HEADS, tq, tq), f32),
                        pltpu.VMEM((N_HEADS, tq, tq), bf16),
                        pltpu.VMEM((tq, tq), f32)],
        compiler_params=_cparams(("parallel", "arbitrary")),
        name="attn_prompt",
    )(qt, qit, wt, zg, kb, vt, kib)


def _idx_sample_body(pt_ref, qi_ref, w_ref, *rest, n_pages, n_new, k_top, ppb):
    kid_refs = rest[:ppb]
    kinew_ref, key_ref, thr_ref = rest[ppb:]
    p = pl.program_id(1)
    tok = SAMPLE_ROWS

    def page_keys(ki_page_t):
        lg = jnp.dot(qi_ref[0], ki_page_t.astype(bf16), preferred_element_type=f32)
        wgt = jnp.maximum(lg, 0.0) * w_ref[0]
        score = jnp.sum(wgt.reshape(IDX_HEADS, tok, PAGE_SIZE), axis=0)
        return _sortable_key(score)

    for j in range(ppb):
        key_ref[0, p * ppb + j] = page_keys(kid_refs[j][...])

    @pl.when(p == pl.num_programs(1) - 1)
    def _():
        key = page_keys(kinew_ref[0])
        kpos = lax.broadcasted_iota(i32, (tok, PAGE_SIZE), 1)
        qtok = lax.broadcasted_iota(i32, (tok, PAGE_SIZE), 0)
        key_ref[0, n_pages] = jnp.where((kpos <= qtok) & (kpos < n_new), key, jnp.int32(INT_MIN))

        def count_ge(cand):
            def body(c, cnt):
                hit = (key_ref[0, pl.ds(c * ppb, ppb)] >= cand).astype(i32)
                return cnt + jnp.sum(hit, axis=0)

            cnt = lax.fori_loop(0, n_pages // ppb, body, jnp.zeros((tok, PAGE_SIZE), i32))
            cnt = cnt + (key_ref[0, n_pages] >= cand).astype(i32)
            return jnp.sum(cnt, axis=-1, keepdims=True)

        thr = _kth_largest(count_ge, (tok, 1), k_top)
        thr_ref[0] = jnp.broadcast_to(thr, (tok, PAGE_SIZE))

        def page_positions(c):
            return lax.broadcasted_iota(i32, (tok, PAGE_SIZE), 1) + c * PAGE_SIZE

        def count_tied_before(thr, pos_limit):
            def body(c, cnt):
                return cnt + ((key_ref[0, c] == thr) & (page_positions(c) < pos_limit)).astype(i32)

            cnt = lax.fori_loop(0, n_pages + 1, body, jnp.zeros((tok, PAGE_SIZE), i32))
            return jnp.sum(cnt, axis=-1, keepdims=True)

        def demote(thr, last):
            def body(c, carry):
                kk = key_ref[0, c]
                key_ref[0, c] = jnp.where((kk == thr) & (page_positions(c) > last), thr - 1, kk)
                return carry

            lax.fori_loop(0, n_pages + 1, body, 0)

        real_rows = lax.broadcasted_iota(i32, (tok, 1), 0) < n_new
        _resolve_ties(thr, k_top, count_ge, count_tied_before, demote,
                      ((n_pages + 1) * PAGE_SIZE - 1).bit_length(), active=real_rows)


def _pages_per_step(n_pages, cap=PAGES_PER_STEP):
    ppb = min(cap, n_pages)
    assert n_pages % ppb == 0
    return ppb


def _page_spec(block, layer, ppb, j):
    return pl.BlockSpec(block, lambda b, p, pt: (layer, pt[b, p * ppb + j], 0, 0))


def _idx_sample(page_table, qi_rows, w_rows, cache_kidx, layer, ki_new_pages, n_new):
    nb, n_pages = page_table.shape
    tok = SAMPLE_ROWS
    ppb = _pages_per_step(n_pages, IDX_PAGES_PER_STEP)
    k_top = min(TOPK_MAX, (n_pages * PAGE_SIZE + n_new) // 4)
    per_b = lambda b, p, pt: (b, 0, 0)
    grid_spec = pltpu.PrefetchScalarGridSpec(
        num_scalar_prefetch=1,
        grid=(nb, n_pages // ppb),
        in_specs=[pl.BlockSpec((1, IDX_HEADS * tok, IDX_DIM), per_b),
                  pl.BlockSpec((1, IDX_HEADS * tok, PAGE_SIZE), per_b)]
                 + [_page_spec((None, None, IDX_DIM, PAGE_SIZE), layer, ppb, j) for j in range(ppb)]
                 + [pl.BlockSpec((1, IDX_DIM, PAGE_SIZE), per_b)],
        out_specs=[pl.BlockSpec((1, n_pages + 1, tok, PAGE_SIZE), lambda b, p, pt: (b, 0, 0, 0)),
                   pl.BlockSpec((1, tok, PAGE_SIZE), per_b)],
    )
    return pl.pallas_call(
        functools.partial(_idx_sample_body, n_pages=n_pages, n_new=n_new, k_top=k_top, ppb=ppb),
        grid_spec=grid_spec,
        out_shape=[jax.ShapeDtypeStruct((nb, n_pages + 1, tok, PAGE_SIZE), i32),
                   jax.ShapeDtypeStruct((nb, tok, PAGE_SIZE), i32)],
        compiler_params=_cparams(("parallel", "arbitrary")),
        name="idx_sample",
    )(page_table, qi_rows, w_rows, *([cache_kidx] * ppb), ki_new_pages)


def _attn_sample_body(pt_ref, q_ref, key_ref, keyn_ref, thr_ref, *rest, ppb):
    k_refs = rest[:ppb]
    v_refs = rest[ppb:2 * ppb]
    kn_ref, vn_ref, o_ref, acc_scr, m_scr, l_scr = rest[2 * ppb:]
    p = pl.program_id(1)
    tok = SAMPLE_ROWS
    rows_g = HEADS_PER_KV * tok
    thr = thr_ref[0]

    @pl.when(p == 0)
    def _():
        m_scr[...] = jnp.full(m_scr.shape, -jnp.inf, f32)
        l_scr[...] = jnp.zeros(l_scr.shape, f32)
        acc_scr[...] = jnp.zeros(acc_scr.shape, f32)

    def attend(bias_tok, k_of_group, v_of_group):
        bias = jnp.concatenate([bias_tok] * HEADS_PER_KV, axis=0)
        for g in range(N_KV_HEADS):
            rs = slice(g * rows_g, (g + 1) * rows_g)
            s = lax.dot_general(q_ref[0, rs, :], k_of_group(g), (((1,), (1,)), ((), ())),
                                preferred_element_type=f32) + bias
            m_prev = m_scr[rs, :]
            m_new = jnp.maximum(m_prev, jnp.max(s, axis=-1, keepdims=True))
            alpha = jnp.exp2(m_prev - m_new)
            pr = jnp.exp2(s - m_new)
            l_scr[rs, :] = alpha * l_scr[rs, :] + jnp.sum(pr, axis=-1, keepdims=True)
            acc_scr[rs, :] = alpha * acc_scr[rs, :] + jnp.dot(
                pr.astype(bf16), v_of_group(g), preferred_element_type=f32)
            m_scr[rs, :] = m_new

    def mask_bias(keys):
        return jnp.where(keys >= thr, 0.0, MASK_BIAS).astype(f32)

    def cached(refs):
        def of_group(g):
            rows = pl.ds(g, PAGE_SIZE, stride=N_KV_HEADS)
            return jnp.concatenate([r[rows, :].astype(bf16) for r in refs], axis=0)
        return of_group

    attend(jnp.concatenate([mask_bias(key_ref[0, j]) for j in range(ppb)], axis=1),
           cached(k_refs), cached(v_refs))

    @pl.when(p == pl.num_programs(1) - 1)
    def _():
        def fresh(ref):
            return lambda g: ref[0, :, g * HEAD_DIM:(g + 1) * HEAD_DIM].astype(bf16)

        attend(mask_bias(keyn_ref[0, 0]), fresh(kn_ref), fresh(vn_ref))
        o_ref[0] = acc_scr[...] / l_scr[...]


def _attn_sample(page_table, q_rows, keys, thr, cache_k, cache_v, layer, k_new_pages, v_new_pages):
    nb, n_pages = page_table.shape
    tok = SAMPLE_ROWS
    rows = N_HEADS * tok
    ppb = _pages_per_step(n_pages)
    per_b = lambda b, p, pt: (b, 0, 0)
    page_block = (None, None, PAGE_SIZE * N_KV_HEADS, HEAD_DIM)
    grid_spec = pltpu.PrefetchScalarGridSpec(
        num_scalar_prefetch=1,
        grid=(nb, n_pages // ppb),
        in_specs=[pl.BlockSpec((1, rows, HEAD_DIM), per_b),
                  pl.BlockSpec((1, ppb, tok, PAGE_SIZE), lambda b, p, pt: (b, p, 0, 0)),
                  pl.BlockSpec((1, 1, tok, PAGE_SIZE), lambda b, p, pt: (b, n_pages, 0, 0)),
                  pl.BlockSpec((1, tok, PAGE_SIZE), per_b)]
                 + [_page_spec(page_block, layer, ppb, j) for j in range(ppb)]
                 + [_page_spec(page_block, layer, ppb, j) for j in range(ppb)]
                 + [pl.BlockSpec((1, PAGE_SIZE, KV_WIDTH), per_b),
                    pl.BlockSpec((1, PAGE_SIZE, KV_WIDTH), per_b)],
        out_specs=pl.BlockSpec((1, rows, HEAD_DIM), per_b),
        scratch_shapes=[pltpu.VMEM((rows, HEAD_DIM), f32),
                        pltpu.VMEM((rows, 1), f32),
                        pltpu.VMEM((rows, 1), f32)],
    )
    return pl.pallas_call(
        functools.partial(_attn_sample_body, ppb=ppb),
        grid_spec=grid_spec,
        out_shape=jax.ShapeDtypeStruct((nb, rows, HEAD_DIM), f32),
        compiler_params=_cparams(("parallel", "arbitrary")),
        name="attn_sample",
    )(page_table, q_rows, keys, keys, thr, *([cache_k] * ppb), *([cache_v] * ppb),
      k_new_pages, v_new_pages)


def _gate_mul_body(a_ref, z_ref, o_ref):
    o_ref[...] = (a_ref[...] * z_ref[...].astype(f32)).astype(o_ref.dtype)


def _gate_mul(a, zg):
    m, n = a.shape
    return pl.pallas_call(
        _gate_mul_body,
        grid=(1,),
        in_specs=[pl.BlockSpec((m, n), lambda i: (0, 0)),
                  pl.BlockSpec((m, n), lambda i: (0, 0))],
        out_specs=pl.BlockSpec((m, n), lambda i: (0, 0)),
        out_shape=jax.ShapeDtypeStruct((m, n), bf16),
        compiler_params=_cparams(("arbitrary",)),
        name="gate_mul",
    )(a, zg)


def _s5_param_body(lr_ref, li_ref, ldt_ref, bre_ref, bim_ref, ar_ref, ai_ref, bbre_ref, bbim_ref):
    lr = lr_ref[...]
    li = li_ref[...]
    dt = jnp.exp(ldt_ref[...])
    mag = jnp.exp(lr * dt)
    ar = mag * jnp.cos(li * dt)
    ai = mag * jnp.sin(li * dt)
    den = lr * lr + li * li
    nr = ar - 1.0
    cr = (nr * lr + ai * li) / den
    ci = (ai * lr - nr * li) / den
    ar_ref[...] = ar
    ai_ref[...] = ai
    for h in range(S5_GROUP):
        bbre_ref[h] = cr * bre_ref[h] - ci * bim_ref[h]
        bbim_ref[h] = cr * bim_ref[h] + ci * bre_ref[h]


def _s5_params(lam_re, lam_im, log_dt, b_re, b_im):
    depth = lam_re.shape[0]
    gs = (None, S5_GROUPS, S5_STATE)
    bs = (None, S5_GROUP, S5_GROUPS, S5_STATE)
    at3 = lambda l: (l, 0, 0)
    at4 = lambda l: (l, 0, 0, 0)
    return pl.pallas_call(
        _s5_param_body,
        grid=(depth,),
        in_specs=[pl.BlockSpec(gs, at3), pl.BlockSpec(gs, at3),
                  pl.BlockSpec((None, S5_GROUPS, 1), at3),
                  pl.BlockSpec(bs, at4), pl.BlockSpec(bs, at4)],
        out_specs=[pl.BlockSpec(gs, at3), pl.BlockSpec(gs, at3),
                   pl.BlockSpec(bs, at4), pl.BlockSpec(bs, at4)],
        out_shape=[jax.ShapeDtypeStruct((depth, S5_GROUPS, S5_STATE), f32),
                   jax.ShapeDtypeStruct((depth, S5_GROUPS, S5_STATE), f32),
                   jax.ShapeDtypeStruct((depth, S5_GROUP, S5_GROUPS, S5_STATE), f32),
                   jax.ShapeDtypeStruct((depth, S5_GROUP, S5_GROUPS, S5_STATE), f32)],
        compiler_params=_cparams(("arbitrary",)),
        name="s5_params",
    )(lam_re, lam_im, log_dt.reshape(depth, S5_GROUPS, 1), b_re, b_im)


def _gelu_tanh(x):
    return 0.5 * x * (1.0 + jnp.tanh(math.sqrt(2.0 / math.pi) * (x + 0.044715 * (x * x * x))))


def _s5_body(u_ref, zs_ref, h0r_ref, h0i_ref, ar_ref, ai_ref, wbu_ref, wcr_ref, wci_ref, d_ref, wglu_ref,
             o_ref, sr_ref, si_ref, xr_scr, xi_scr, y_scr, str_scr, sti_scr, *, tc, n_last):
    c = pl.program_id(1)

    @pl.when(c == 0)
    def _():
        str_scr[...] = h0r_ref[0]
        sti_scr[...] = h0i_ref[0]

    n_ct = S5_SG_STATE // LANES

    def gather_planes(scr, rows):
        return jnp.concatenate([scr[j, rows, :] for j in range(n_ct)], axis=1)

    def scatter_planes(scr, rows, val):
        for j in range(n_ct):
            scr[j, rows, :] = val[:, j * LANES:(j + 1) * LANES]

    for sg in range(S5_SG):
        rows = pl.ds(sg, tc, stride=S5_SG)
        u_sg = u_ref[:, sg * LANES:(sg + 1) * LANES]
        bu = jnp.dot(u_sg.astype(bf16), wbu_ref[sg], preferred_element_type=f32)
        scatter_planes(xr_scr, rows, bu[:, :S5_SG_STATE])
        scatter_planes(xi_scr, rows, bu[:, S5_SG_STATE:])

    ar = ar_ref[...]
    ai = ai_ref[...]

    def step(t, carry):
        xr, xi = carry
        r = pl.ds(pl.multiple_of(t * S5_SG, S5_SG), S5_SG)
        nr = ar * xr - ai * xi + gather_planes(xr_scr, r)
        ni = ar * xi + ai * xr + gather_planes(xi_scr, r)
        scatter_planes(xr_scr, r, nr)
        scatter_planes(xi_scr, r, ni)
        return nr, ni

    xr, xi = lax.fori_loop(0, tc, step, (str_scr[...], sti_scr[...]), unroll=S5_SCAN_UNROLL)
    str_scr[...] = xr
    sti_scr[...] = xi

    for sg in range(S5_SG):
        rows = pl.ds(sg, tc, stride=S5_SG)
        cs = slice(sg * LANES, (sg + 1) * LANES)
        y = (jnp.dot(gather_planes(xr_scr, rows).astype(bf16), wcr_ref[sg], preferred_element_type=f32)
             - jnp.dot(gather_planes(xi_scr, rows).astype(bf16), wci_ref[sg], preferred_element_type=f32)
             + d_ref[:, cs] * u_ref[:, cs])
        y_scr[:, cs] = _gelu_tanh(y)

    y = y_scr[...]
    gate = _sigmoid(jnp.dot(y.astype(bf16), wglu_ref[...], preferred_element_type=f32))
    o_ref[...] = (y * gate * zs_ref[...].astype(f32)).astype(o_ref.dtype)

    @pl.when(c == pl.num_programs(1) - 1)
    def _():
        last = slice((n_last - 1) * S5_SG, n_last * S5_SG)
        sr_ref[0] = gather_planes(xr_scr, last)
        si_ref[0] = gather_planes(xi_scr, last)


def _s5(u, zg, h0_re, h0_im, prm, layer, nb, seq, n_real):
    tc = min(seq, S5_CHUNK)
    nch = seq // tc
    n_last = n_real - (nch - 1) * tc
    zs_col = ATTN_WIDTH // S5_WIDTH
    fix2 = lambda b, c: (0, 0)
    fix3 = lambda b, c: (0, 0, 0)
    st = lambda b, c: (b, 0, 0)
    out, s_re, s_im = pl.pallas_call(
        functools.partial(_s5_body, tc=tc, n_last=n_last),
        grid=(nb, nch),
        in_specs=[pl.BlockSpec((tc, S5_WIDTH), lambda b, c: (b * nch + c, 0)),
                  pl.BlockSpec((tc, S5_WIDTH), lambda b, c: (b * nch + c, zs_col)),
                  pl.BlockSpec((1, S5_SG, S5_SG_STATE), st),
                  pl.BlockSpec((1, S5_SG, S5_SG_STATE), st),
                  _layer_spec((S5_SG, S5_SG_STATE), layer, fix2),
                  _layer_spec((S5_SG, S5_SG_STATE), layer, fix2),
                  _layer_spec((S5_SG, LANES, 2 * S5_SG_STATE), layer, fix3),
                  _layer_spec((S5_SG, S5_SG_STATE, LANES), layer, fix3),
                  _layer_spec((S5_SG, S5_SG_STATE, LANES), layer, fix3),
                  _layer_spec((1, S5_WIDTH), layer, fix2),
                  _layer_spec((S5_WIDTH, S5_WIDTH), layer, fix2)],
        out_specs=[pl.BlockSpec((tc, S5_WIDTH), lambda b, c: (b * nch + c, 0)),
                   pl.BlockSpec((1, S5_SG, S5_SG_STATE), st),
                   pl.BlockSpec((1, S5_SG, S5_SG_STATE), st)],
        out_shape=[jax.ShapeDtypeStruct((nb * seq, S5_WIDTH), bf16),
                   jax.ShapeDtypeStruct((nb, S5_SG, S5_SG_STATE), f32),
                   jax.ShapeDtypeStruct((nb, S5_SG, S5_SG_STATE), f32)],
        scratch_shapes=[pltpu.VMEM((S5_SG_STATE // LANES, tc * S5_SG, LANES), f32),
                        pltpu.VMEM((S5_SG_STATE // LANES, tc * S5_SG, LANES), f32),
                        pltpu.VMEM((tc, S5_WIDTH), f32),
                        pltpu.VMEM((S5_SG, S5_SG_STATE), f32),
                        pltpu.VMEM((S5_SG, S5_SG_STATE), f32)],
        compiler_params=_cparams(("parallel", "arbitrary")),
        name="s5_scan",
    )(u, zg, h0_re.reshape(nb, S5_SG, S5_SG_STATE), h0_im.reshape(nb, S5_SG, S5_SG_STATE),
      prm["a_re"], prm["a_im"], prm["w_bu"], prm["w_c_re"], prm["w_c_im"], prm["d"], prm["w_glu"])
    return out, s_re.reshape(nb, S5_GROUPS, S5_STATE), s_im.reshape(nb, S5_GROUPS, S5_STATE)


def _merge_body(a_ref, s_ref, wa_ref, ws_ref, ga_ref, gs_ref, o_ref):
    o_a = jnp.dot(a_ref[...], wa_ref[...], preferred_element_type=f32)
    o_s = jnp.dot(s_ref[...], ws_ref[...], preferred_element_type=f32)
    o_ref[...] = (ga_ref[...].astype(f32) * o_a + gs_ref[...].astype(f32) * o_s).astype(o_ref.dtype)


def _merge(a_in, s_in, w_a, w_s, zg, layer):
    m = a_in.shape[0]
    tm = _row_tile(m, MM_ROW_TILE)
    tn = MM_COL_TILE
    ga0 = (ATTN_WIDTH + S5_WIDTH) // tn
    gs0 = (ATTN_WIDTH + S5_WIDTH + D_MODEL) // tn
    return pl.pallas_call(
        _merge_body,
        grid=(m // tm, D_MODEL // tn),
        in_specs=[pl.BlockSpec((tm, ATTN_WIDTH), lambda i, j: (i, 0)),
                  pl.BlockSpec((tm, S5_WIDTH), lambda i, j: (i, 0)),
                  _layer_spec((ATTN_WIDTH, tn), layer, lambda i, j: (0, j)),
                  _layer_spec((S5_WIDTH, tn), layer, lambda i, j: (0, j)),
                  pl.BlockSpec((tm, tn), lambda i, j: (i, ga0 + j)),
                  pl.BlockSpec((tm, tn), lambda i, j: (i, gs0 + j))],
        out_specs=pl.BlockSpec((tm, tn), lambda i, j: (i, j)),
        out_shape=jax.ShapeDtypeStruct((m, D_MODEL), bf16),
        compiler_params=_cparams(("parallel", "parallel")),
        name="merge",
    )(a_in, s_in, w_a, w_s, zg, zg)


def _out_body(x_ref, m_ref, w_ref, o_ref):
    o_ref[...] = x_ref[...] + jnp.dot(m_ref[...], w_ref[...], preferred_element_type=f32)


def _out_proj(x, merged, w_out, layer):
    m = x.shape[0]
    tm = _row_tile(m, MM_ROW_TILE)
    tn = MM_COL_TILE
    return pl.pallas_call(
        _out_body,
        grid=(m // tm, D_MODEL // tn),
        in_specs=[pl.BlockSpec((tm, tn), lambda i, j: (i, j)),
                  pl.BlockSpec((tm, D_MODEL), lambda i, j: (i, 0)),
                  _layer_spec((D_MODEL, tn), layer, lambda i, j: (0, j))],
        out_specs=pl.BlockSpec((tm, tn), lambda i, j: (i, j)),
        out_shape=jax.ShapeDtypeStruct((m, D_MODEL), f32),
        compiler_params=_cparams(("parallel", "parallel")),
        name="out_proj",
    )(x, merged, w_out)


def _out_norm_body(x_ref, m_ref, w_ref, g_ref, o_ref, h_ref):
    y = x_ref[...] + jnp.dot(m_ref[...], w_ref[...], preferred_element_type=f32)
    o_ref[...] = y
    ms = jnp.mean(y * y, axis=-1, keepdims=True)
    h_ref[...] = (y * lax.rsqrt(ms + EPS) * g_ref[...]).astype(h_ref.dtype)


def _out_proj_norm(x, merged, w_out, gains, layer):
    m = x.shape[0]
    tm = _row_tile(m, ROW_TILE)
    row = lambda i: (i, 0)
    return pl.pallas_call(
        _out_norm_body,
        grid=(m // tm,),
        in_specs=[pl.BlockSpec((tm, D_MODEL), row),
                  pl.BlockSpec((tm, D_MODEL), row),
                  _layer_spec((D_MODEL, D_MODEL), layer, lambda i: (0, 0)),
                  _layer_spec((1, D_MODEL), layer + 1, lambda i: (0, 0))],
        out_specs=[pl.BlockSpec((tm, D_MODEL), row), pl.BlockSpec((tm, D_MODEL), row)],
        out_shape=[jax.ShapeDtypeStruct((m, D_MODEL), f32),
                   jax.ShapeDtypeStruct((m, D_MODEL), bf16)],
        compiler_params=_cparams(("parallel",)),
        name="out_proj_norm",
    )(x, merged, w_out, gains)


def _rope_tables(pos):
    posf = pos.astype(f32)[:, None]
    half = HEAD_DIM // 2
    inv = ROPE_THETA ** (-jnp.arange(half, dtype=f32) / half)
    ang = posf * inv[None, :]
    cos, sin = jnp.cos(ang), jnp.sin(ang)
    cos_h = jnp.concatenate([cos, cos], axis=1)
    sin_h = jnp.concatenate([-sin, sin], axis=1)
    half_i = IDX_DIM // 2
    inv_i = ROPE_THETA ** (-jnp.arange(half_i, dtype=f32) / half_i)
    ang_i = posf * inv_i[None, :]
    cos_i, sin_i = jnp.cos(ang_i), jnp.sin(ang_i)
    zeros = jnp.zeros_like(cos_i)
    pad = jnp.zeros((pos.shape[0], LANES - IDX_DIM), f32)
    cos_k = jnp.concatenate([cos_i, cos_i, pad], axis=1)
    sin_a = jnp.concatenate([-sin_i, zeros, pad], axis=1)
    sin_b = jnp.concatenate([zeros, sin_i, pad], axis=1)
    return dict(cos_h=cos_h, sin_h=sin_h, cos_k=cos_k, sin_a=sin_a, sin_b=sin_b,
                cos_ht=cos.T, sin_ht=sin.T, cos_t=cos_i.T, sin_t=sin_i.T)


def _in_proj_offsets():
    offs = [0]
    for s in IN_SIZES:
        offs.append(offs[-1] + s)
    return offs


def _repack_body(w_ref, wq_ref, wkv_ref, wqiw_ref, wki_ref, wu_ref, wzg_ref):
    offs = _in_proj_offsets()
    seg = lambda k: w_ref[offs[k]:offs[k + 1], :].astype(bf16)
    cols = w_ref.shape[1]
    wq_ref[...] = seg(0)
    wkv_ref[...] = w_ref[offs[1]:offs[3], :].astype(bf16)
    n_qi = IDX_HEADS * IDX_DIM
    wqiw_ref[:n_qi, :] = seg(3)
    wqiw_ref[n_qi:, :] = seg(5)
    wki_ref[:IDX_DIM, :] = seg(4)
    wki_ref[IDX_DIM:, :] = jnp.zeros((LANES - IDX_DIM, cols), bf16)
    wu_ref[...] = seg(7)
    row = 0
    for k in (6, 8, 9, 10):
        wzg_ref[row:row + IN_SIZES[k], :] = seg(k)
        row += IN_SIZES[k]


def _repack_in_proj(w_in_t):
    depth, n_in, d = w_in_t.shape
    cols = REPACK_COLS
    n_qiw = IDX_HEADS * IDX_DIM + IDX_HEADS
    n_zg = ATTN_WIDTH + S5_WIDTH + 2 * D_MODEL
    heights = (ATTN_WIDTH, 2 * KV_WIDTH, n_qiw, LANES, S5_WIDTH, n_zg)
    at = lambda l, c: (l, 0, c)
    return pl.pallas_call(
        _repack_body,
        grid=(depth, d // cols),
        in_specs=[pl.BlockSpec((None, n_in, cols), at)],
        out_specs=[pl.BlockSpec((None, hgt, cols), at) for hgt in heights],
        out_shape=[jax.ShapeDtypeStruct((depth, hgt, d), bf16) for hgt in heights],
        compiler_params=_cparams(("parallel", "parallel")),
        name="repack_in_proj",
    )(w_in_t)


def _pack_weights(w_in, w_glu, w_br_attn, w_br_s5, w_out, norm_gain, q_norm_gain, k_norm_gain):
    depth = w_in.shape[0]
    wt_q, wt_kv, wt_qi, wt_ki, wt_u, wt_zg = _repack_in_proj(jnp.swapaxes(w_in, 1, 2))
    return dict(
        wt_q=wt_q, wt_kv=wt_kv, wt_qi=wt_qi, wt_ki=wt_ki, wt_u=wt_u, wt_zg=wt_zg,
        w_glu=w_glu.astype(bf16), w_br_attn=w_br_attn.astype(bf16),
        w_br_s5=w_br_s5.astype(bf16), w_out=w_out.astype(bf16),
        norm_gain=norm_gain.reshape(depth, 1, D_MODEL),
        q_gain=q_norm_gain.reshape(depth, HEAD_DIM, 1),
        k_gain=k_norm_gain.reshape(depth, 1, HEAD_DIM))


def _block_diag(blocks):
    depth, sg, gg, r, c = blocks.shape
    eye = jnp.eye(gg, dtype=blocks.dtype)
    return jnp.einsum("lsgrc,gk->lsgrkc", blocks, eye).reshape(depth, sg, gg * r, gg * c)


def _s5_stacked_params(lam_re, lam_im, log_dt, b_re, b_im, c_re, c_im, d, w_glu_bf16):
    depth = lam_re.shape[0]
    a_re, a_im, bb_re, bb_im = _s5_params(lam_re, lam_im, log_dt,
                                          jnp.transpose(b_re, (0, 3, 1, 2)), jnp.transpose(b_im, (0, 3, 1, 2)))
    def bu_blocks(bb):
        return jnp.transpose(bb, (0, 2, 1, 3)).reshape(depth, S5_SG, S5_SG_GROUPS, S5_GROUP, S5_STATE)
    w_bu = jnp.concatenate([_block_diag(bu_blocks(bb_re)), _block_diag(bu_blocks(bb_im))], axis=3)
    def c_blocks(cm):
        return jnp.transpose(cm, (0, 1, 3, 2)).reshape(depth, S5_SG, S5_SG_GROUPS, S5_STATE, S5_GROUP)
    return dict(a_re=a_re.reshape(depth, S5_SG, S5_SG_STATE), a_im=a_im.reshape(depth, S5_SG, S5_SG_STATE),
                w_bu=w_bu.astype(bf16),
                w_c_re=_block_diag(c_blocks(c_re)).astype(bf16),
                w_c_im=_block_diag(c_blocks(c_im)).astype(bf16),
                d=d.reshape(depth, 1, S5_WIDTH), w_glu=w_glu_bf16)


def _projections(h, wts, tabs, layer):
    qt = _proj_qt(h, wts["wt_q"], wts["q_gain"], tabs["cos_ht"], tabs["sin_ht"], layer)
    k, v, kb, vt = _proj_kv(h, wts["wt_kv"], wts["k_gain"], tabs["cos_h"], tabs["sin_h"], layer)
    qit, wt = _proj_qit(h, wts["wt_qi"], tabs["cos_t"], tabs["sin_t"], layer)
    ki, kib = _proj_ki(h, wts["wt_ki"], tabs["cos_k"], tabs["sin_a"], tabs["sin_b"], layer)
    u = _proj_plain(h, wts["wt_u"], f32, layer)
    zg = _proj_zg(h, wts["wt_zg"], layer)
    return dict(qt=qt, k=k, v=v, kb=kb, vt=vt, qit=qit, wt=wt, ki=ki, kib=kib, u=u, zg=zg)


def _finish(x, a_in, s_in, zg, wts, layer, last):
    merged = _merge(a_in, s_in, wts["w_br_attn"], wts["w_br_s5"], zg, layer)
    if last:
        return _out_proj(x, merged, wts["w_out"], layer), None
    return _out_proj_norm(x, merged, wts["w_out"], wts["norm_gain"], layer)


def kernel(x_prompt, x_sample, cache_k, cache_v, cache_kidx, state_s5_re, state_s5_im, page_table, norm_gain, w_in, q_norm_gain, k_norm_gain, s5_lam_re, s5_lam_im, s5_log_dt, s5_b_re, s5_b_im, s5_c_re, s5_c_im, s5_d, w_glu, w_br_attn, w_br_s5, w_out):
    depth = w_in.shape[0]
    b_p, t_p = x_prompt.shape[:2]
    b_s, t_s = x_sample.shape[:2]
    tok = SAMPLE_ROWS
    assert t_s <= tok
    n_pages = page_table.shape[1]
    past = n_pages * PAGE_SIZE
    n_phys = cache_k.shape[1]

    tabs_p = _rope_tables(jnp.tile(jnp.arange(t_p, dtype=i32), b_p))
    tabs_s = _rope_tables(jnp.tile(past + jnp.arange(tok, dtype=i32), b_s))

    xp = x_prompt.reshape(b_p * t_p, D_MODEL)
    xs = jnp.pad(x_sample, ((0, 0), (0, tok - t_s), (0, 0))).reshape(b_s * tok, D_MODEL)

    cache_k4 = cache_k.reshape(depth, n_phys, PAGE_SIZE * N_KV_HEADS, HEAD_DIM)
    cache_v4 = cache_v.reshape(depth, n_phys, PAGE_SIZE * N_KV_HEADS, HEAD_DIM)
    cache_kidx_t = jnp.swapaxes(cache_kidx, 2, 3)
    zeros_state = jnp.zeros((b_p, S5_GROUPS, S5_STATE), f32)

    def new_page(a):
        w = a.shape[-1]
        return jnp.pad(a.reshape(b_s, tok, w), ((0, 0), (0, PAGE_SIZE - tok), (0, 0)))

    wts = _pack_weights(w_in, w_glu, w_br_attn, w_br_s5, w_out, norm_gain, q_norm_gain, k_norm_gain)
    s5p = _s5_stacked_params(s5_lam_re, s5_lam_im, s5_log_dt, s5_b_re, s5_b_im, s5_c_re, s5_c_im, s5_d,
                             wts["w_glu"])

    outs_p, outs_s = [], []
    hp = _rmsnorm(xp, wts["norm_gain"], 0)
    hs = _rmsnorm(xs, wts["norm_gain"], 0)
    for l in range(depth):
        last = l == depth - 1
        pp = _projections(hp, wts, tabs_p, l)
        a_in = _attn_prompt(pp["qt"], pp["qit"], pp["wt"], pp["zg"], pp["kb"], pp["vt"], pp["kib"], b_p, t_p)
        s_in, sr_p, si_p = _s5(pp["u"], pp["zg"], zeros_state, zeros_state, s5p, l, b_p, t_p, t_p)
        xp, hp = _finish(xp, a_in, s_in, pp["zg"], wts, l, last)
        outs_p.append((pp["k"].reshape(b_p, t_p, N_KV_HEADS, HEAD_DIM),
                       pp["v"].reshape(b_p, t_p, N_KV_HEADS, HEAD_DIM),
                       pp["ki"].reshape(b_p, t_p, IDX_DIM), sr_p, si_p))

        ps = _projections(hs, wts, tabs_s, l)
        qi_rows = jnp.transpose(ps["qit"].reshape(IDX_HEADS, IDX_DIM, b_s, tok), (2, 0, 3, 1)
                                ).reshape(b_s, IDX_HEADS * tok, IDX_DIM)
        w_rows = jnp.broadcast_to(
            jnp.transpose(ps["wt"].reshape(IDX_HEADS, b_s, tok), (1, 0, 2)).reshape(b_s, IDX_HEADS * tok, 1),
            (b_s, IDX_HEADS * tok, PAGE_SIZE))
        keys, thr = _idx_sample(page_table, qi_rows, w_rows, cache_kidx_t, l,
                                jnp.swapaxes(new_page(ps["ki"]), 1, 2), t_s)
        q_rows = jnp.transpose(ps["qt"].reshape(N_HEADS, HEAD_DIM, b_s, tok), (2, 0, 3, 1)
                               ).reshape(b_s, N_HEADS * tok, HEAD_DIM)
        k_s = ps["k"].reshape(b_s * tok, KV_WIDTH)
        v_s = ps["v"].reshape(b_s * tok, KV_WIDTH)
        o_rows = _attn_sample(page_table, q_rows, keys, thr, cache_k4, cache_v4, l,
                              new_page(k_s), new_page(v_s))
        attn_s = jnp.transpose(o_rows.reshape(b_s, N_HEADS, tok, HEAD_DIM), (0, 2, 1, 3)
                               ).reshape(b_s * tok, ATTN_WIDTH)
        a_in_s = _gate_mul(attn_s, ps["zg"][:, :ATTN_WIDTH])
        s_in_s, sr_s, si_s = _s5(ps["u"], ps["zg"], state_s5_re[l], state_s5_im[l], s5p, l, b_s, tok, t_s)
        xs, hs = _finish(xs, a_in_s, s_in_s, ps["zg"], wts, l, last)
        real = lambda a: a.reshape((b_s, tok) + a.shape[1:])[:, :t_s]
        outs_s.append((real(k_s).reshape(b_s, t_s, N_KV_HEADS, HEAD_DIM),
                       real(v_s).reshape(b_s, t_s, N_KV_HEADS, HEAD_DIM),
                       real(ps["ki"]), sr_s, si_s))

    k_prompt, v_prompt, kidx_prompt, s5_re_prompt, s5_im_prompt = [jnp.stack(a) for a in zip(*outs_p)]
    k_sample, v_sample, kidx_sample, s5_re_sample, s5_im_sample = [jnp.stack(a) for a in zip(*outs_s)]
    y_prompt = xp.reshape(b_p, t_p, D_MODEL)
    y_sample = xs.reshape(b_s, tok, D_MODEL)[:, :t_s]
    return (y_prompt, y_sample, k_prompt, v_prompt, kidx_prompt, s5_re_prompt, s5_im_prompt,
            k_sample, v_sample, kidx_sample, s5_re_sample, s5_im_sample)
```

```python
import functools
import math

import jax
import jax.numpy as jnp
from jax import lax
from jax.experimental import pallas as pl
from jax.experimental.pallas import tpu as pltpu

D_MODEL = 2048
PAGE_SIZE = 128
N_HEADS = 16
HEAD_DIM = 128
N_KV_HEADS = 4
HEADS_PER_KV = N_HEADS // N_KV_HEADS
ATTN_WIDTH = N_HEADS * HEAD_DIM
KV_WIDTH = N_KV_HEADS * HEAD_DIM
IDX_HEADS = 16
IDX_DIM = 64
TOPK_MAX = 256
S5_WIDTH = D_MODEL // 2
S5_GROUP = 16
S5_GROUPS = S5_WIDTH // S5_GROUP
S5_STATE = 64
ROPE_THETA = 10000.0
EPS = 1e-6
IN_SIZES = (ATTN_WIDTH, KV_WIDTH, KV_WIDTH, IDX_HEADS * IDX_DIM, IDX_DIM, IDX_HEADS,
            ATTN_WIDTH, S5_WIDTH, S5_WIDTH, D_MODEL, D_MODEL)

LANES = 128
SUBLANES = 8
PACKED_SUBLANES = 16
VMEM_LIMIT_BYTES = 56 * 1024 * 1024
MM_ROW_TILE = 1024
MM_COL_TILE = 1024
ROW_TILE = 512
REPACK_COLS = 256
S5_CHUNK = 256

S5_SG = S5_WIDTH // LANES
S5_SG_GROUPS = S5_GROUPS // S5_SG
S5_SG_STATE = S5_SG_GROUPS * S5_STATE
S5_SCAN_UNROLL = 8

SAMPLE_ROWS = 16
IDX_ROWS = SUBLANES

ATTN_Q_TILE = 256
ATTN_ROW_BLOCK = 64
COUNT_ROWS = 32
PAGES_PER_STEP = 32
IDX_PAGES_PER_STEP = 64
LOG2_E = math.log2(math.e)

INT_MIN = -2 ** 31
MASK_BIAS = -1e30

f32 = jnp.float32
bf16 = jnp.bfloat16
i32 = jnp.int32


def _cparams(sem):
    return pltpu.CompilerParams(dimension_semantics=sem, vmem_limit_bytes=VMEM_LIMIT_BYTES)


def _row_tile(m, cap):
    return m if m <= cap else cap


def _norm_body(x_ref, g_ref, o_ref):
    x = x_ref[...]
    ms = jnp.mean(x * x, axis=-1, keepdims=True)
    o_ref[...] = (x * lax.rsqrt(ms + EPS) * g_ref[...]).astype(o_ref.dtype)


def _layer_spec(block, layer, index_map):
    return pl.BlockSpec((None,) + tuple(block), lambda *idx: (layer,) + tuple(index_map(*idx)))


def _rmsnorm(x, gains, layer):
    m, d = x.shape
    tm = _row_tile(m, ROW_TILE)
    return pl.pallas_call(
        _norm_body,
        grid=(m // tm,),
        in_specs=[pl.BlockSpec((tm, d), lambda i: (i, 0)),
                  _layer_spec((1, d), layer, lambda i: (0, 0))],
        out_specs=pl.BlockSpec((tm, d), lambda i: (i, 0)),
        out_shape=jax.ShapeDtypeStruct((m, d), bf16),
        compiler_params=_cparams(("parallel",)),
        name="rmsnorm",
    )(x, gains)


def _head_norm_rope(x, gain, cos, sin):
    ms = jnp.mean(x * x, axis=-1, keepdims=True)
    y = x * lax.rsqrt(ms + EPS) * gain
    return y * cos + pltpu.roll(y, HEAD_DIM // 2, 1) * sin


def _qt_body(wt_ref, h_ref, g_ref, cos_ref, sin_ref, o_ref, *, heads):
    acc = lax.dot_general(wt_ref[...], h_ref[...], (((1,), (1,)), ((), ())),
                          preferred_element_type=f32)
    c = cos_ref[...]
    s = sin_ref[...]
    g = g_ref[...]
    half = HEAD_DIM // 2
    scale = HEAD_DIM ** -0.5 * LOG2_E
    for hh in range(heads):
        r0 = hh * HEAD_DIM
        x = acc[r0:r0 + HEAD_DIM]
        ms = jnp.mean(x * x, axis=0, keepdims=True)
        y = x * lax.rsqrt(ms + EPS) * g
        x1 = y[:half]
        x2 = y[half:]
        o_ref[r0:r0 + half, :] = ((x1 * c - x2 * s) * scale).astype(o_ref.dtype)
        o_ref[r0 + half:r0 + HEAD_DIM, :] = ((x2 * c + x1 * s) * scale).astype(o_ref.dtype)


def _proj_qt(h, wt_q, gain, cos_ht, sin_ht, layer):
    m, d = h.shape
    tm = _row_tile(m, MM_ROW_TILE)
    tn = MM_COL_TILE
    return pl.pallas_call(
        functools.partial(_qt_body, heads=tn // HEAD_DIM),
        grid=(m // tm, ATTN_WIDTH // tn),
        in_specs=[_layer_spec((tn, d), layer, lambda i, j: (j, 0)),
                  pl.BlockSpec((tm, d), lambda i, j: (i, 0)),
                  _layer_spec((HEAD_DIM, 1), layer, lambda i, j: (0, 0)),
                  pl.BlockSpec((HEAD_DIM // 2, tm), lambda i, j: (0, i)),
                  pl.BlockSpec((HEAD_DIM // 2, tm), lambda i, j: (0, i))],
        out_specs=pl.BlockSpec((tn, tm), lambda i, j: (j, i)),
        out_shape=jax.ShapeDtypeStruct((ATTN_WIDTH, m), bf16),
        compiler_params=_cparams(("parallel", "parallel")),
        name="proj_q_t",
    )(wt_q, h, gain, cos_ht, sin_ht)


def _dot_nt(a, b_t):
    return lax.dot_general(a, b_t, (((1,), (1,)), ((), ())), preferred_element_type=f32)


def _kv_body(h_ref, wt_ref, g_ref, cos_ref, sin_ref, k_ref, v_ref, kb_ref, vt_ref, *, chunk):
    h = h_ref[...]
    acc = _dot_nt(h, wt_ref[...])
    cos = cos_ref[...]
    sin = sin_ref[...]
    g = g_ref[...]
    tm = h.shape[0]
    for hh in range(N_KV_HEADS):
        sl = slice(hh * HEAD_DIM, (hh + 1) * HEAD_DIM)
        kh = _head_norm_rope(acc[:, sl], g, cos, sin)
        rows = pl.ds(hh, tm, stride=N_KV_HEADS)
        k_ref[rows, :] = kh
        v_ref[rows, :] = acc[:, KV_WIDTH + hh * HEAD_DIM:KV_WIDTH + (hh + 1) * HEAD_DIM]
        kb_ref[:, sl] = kh.astype(bf16)
    vt = _dot_nt(wt_ref[KV_WIDTH:, :], h)
    for cc in range(vt_ref.shape[0]):
        vt_ref[cc] = vt[:, cc * chunk:(cc + 1) * chunk].astype(bf16)


def _proj_kv(h, wt_kv, gain, cos, sin, layer):
    m, d = h.shape
    tm = _row_tile(m, ROW_TILE)
    chunk = min(tm, ATTN_Q_TILE)
    row = lambda i: (i, 0)
    fix = lambda i: (0, 0)
    return pl.pallas_call(
        functools.partial(_kv_body, chunk=chunk),
        grid=(m // tm,),
        in_specs=[pl.BlockSpec((tm, d), row),
                  _layer_spec((2 * KV_WIDTH, d), layer, fix),
                  _layer_spec((1, HEAD_DIM), layer, fix),
                  pl.BlockSpec((tm, HEAD_DIM), row),
                  pl.BlockSpec((tm, HEAD_DIM), row)],
        out_specs=[pl.BlockSpec((tm * N_KV_HEADS, HEAD_DIM), row),
                   pl.BlockSpec((tm * N_KV_HEADS, HEAD_DIM), row),
                   pl.BlockSpec((tm, KV_WIDTH), row),
                   pl.BlockSpec((tm // chunk, KV_WIDTH, chunk), lambda i: (i, 0, 0))],
        out_shape=[jax.ShapeDtypeStruct((m * N_KV_HEADS, HEAD_DIM), f32),
                   jax.ShapeDtypeStruct((m * N_KV_HEADS, HEAD_DIM), f32),
                   jax.ShapeDtypeStruct((m, KV_WIDTH), bf16),
                   jax.ShapeDtypeStruct((m // chunk, KV_WIDTH, chunk), bf16)],
        compiler_params=_cparams(("parallel",)),
        name="proj_kv",
    )(h, wt_kv, gain, cos, sin)


def _qit_body(wt_ref, h_ref, cos_ref, sin_ref, qit_ref, wt_out_ref):
    acc = lax.dot_general(wt_ref[...], h_ref[...], (((1,), (1,)), ((), ())),
                          preferred_element_type=f32)
    c = cos_ref[...]
    s = sin_ref[...]
    half = IDX_DIM // 2
    scale = IDX_DIM ** -0.5
    for hh in range(IDX_HEADS):
        r0 = hh * IDX_DIM
        x1 = acc[r0:r0 + half]
        x2 = acc[r0 + half:r0 + IDX_DIM]
        qit_ref[r0:r0 + half, :] = ((x1 * c - x2 * s) * scale).astype(qit_ref.dtype)
        qit_ref[r0 + half:r0 + IDX_DIM, :] = ((x2 * c + x1 * s) * scale).astype(qit_ref.dtype)
    n_qi = IDX_HEADS * IDX_DIM
    wt_out_ref[...] = acc[n_qi:n_qi + IDX_HEADS] * (IDX_HEADS ** -0.5)


def _proj_qit(h, wt_qi, cos_t, sin_t, layer):
    m, d = h.shape
    tm = _row_tile(m, ROW_TILE)
    n_rows = wt_qi.shape[1]
    n_qi = IDX_HEADS * IDX_DIM
    return pl.pallas_call(
        _qit_body,
        grid=(m // tm,),
        in_specs=[_layer_spec((n_rows, d), layer, lambda i: (0, 0)),
                  pl.BlockSpec((tm, d), lambda i: (i, 0)),
                  pl.BlockSpec((IDX_DIM // 2, tm), lambda i: (0, i)),
                  pl.BlockSpec((IDX_DIM // 2, tm), lambda i: (0, i))],
        out_specs=[pl.BlockSpec((n_qi, tm), lambda i: (0, i)),
                   pl.BlockSpec((IDX_HEADS, tm), lambda i: (0, i))],
        out_shape=[jax.ShapeDtypeStruct((n_qi, m), bf16),
                   jax.ShapeDtypeStruct((IDX_HEADS, m), f32)],
        compiler_params=_cparams(("parallel",)),
        name="proj_qi_t",
    )(wt_qi, h, cos_t, sin_t)


def _ki_body(h_ref, w_ref, cos_ref, sina_ref, sinb_ref, ki_ref, kib_ref):
    x = _dot_nt(h_ref[...], w_ref[...])
    half = IDX_DIM // 2
    r = (x * cos_ref[...] + pltpu.roll(x, LANES - half, 1) * sina_ref[...]
         + pltpu.roll(x, half, 1) * sinb_ref[...])
    ki = r[:, :IDX_DIM]
    ki_ref[...] = ki
    kib_ref[...] = ki.astype(bf16)


def _proj_ki(h, w_ki, cos_k, sin_a, sin_b, layer):
    m, d = h.shape
    tm = _row_tile(m, ROW_TILE)
    row = lambda i: (i, 0)
    return pl.pallas_call(
        _ki_body,
        grid=(m // tm,),
        in_specs=[pl.BlockSpec((tm, d), row),
                  _layer_spec((LANES, d), layer, lambda i: (0, 0)),
                  pl.BlockSpec((tm, LANES), row),
                  pl.BlockSpec((tm, LANES), row),
                  pl.BlockSpec((tm, LANES), row)],
        out_specs=[pl.BlockSpec((tm, IDX_DIM), row)] * 2,
        out_shape=[jax.ShapeDtypeStruct((m, IDX_DIM), f32),
                   jax.ShapeDtypeStruct((m, IDX_DIM), bf16)],
        compiler_params=_cparams(("parallel",)),
        name="proj_ki",
    )(h, w_ki, cos_k, sin_a, sin_b)


def _plain_body(h_ref, w_ref, o_ref):
    o_ref[...] = _dot_nt(h_ref[...], w_ref[...]).astype(o_ref.dtype)


def _proj_plain(h, wt, out_dtype, layer):
    m, d = h.shape
    n = wt.shape[1]
    tm = _row_tile(m, MM_ROW_TILE)
    tn = MM_COL_TILE
    return pl.pallas_call(
        _plain_body,
        grid=(m // tm, n // tn),
        in_specs=[pl.BlockSpec((tm, d), lambda i, j: (i, 0)),
                  _layer_spec((tn, d), layer, lambda i, j: (j, 0))],
        out_specs=pl.BlockSpec((tm, tn), lambda i, j: (i, j)),
        out_shape=jax.ShapeDtypeStruct((m, n), out_dtype),
        compiler_params=_cparams(("parallel", "parallel")),
        name="proj_u",
    )(h, wt)


def _sigmoid(x):
    return 1.0 / (1.0 + jnp.exp(-x))


def _zg_body(h_ref, w_ref, o_ref, *, silu_tiles):
    acc = _dot_nt(h_ref[...], w_ref[...])
    sg = _sigmoid(acc)
    is_silu = pl.program_id(1) < silu_tiles
    o_ref[...] = jnp.where(is_silu, acc * sg, sg).astype(o_ref.dtype)


def _proj_zg(h, wt_zg, layer):
    m, d = h.shape
    n = wt_zg.shape[1]
    tm = _row_tile(m, MM_ROW_TILE)
    tn = MM_COL_TILE
    return pl.pallas_call(
        functools.partial(_zg_body, silu_tiles=(ATTN_WIDTH + S5_WIDTH) // tn),
        grid=(m // tm, n // tn),
        in_specs=[pl.BlockSpec((tm, d), lambda i, j: (i, 0)),
                  _layer_spec((tn, d), layer, lambda i, j: (j, 0))],
        out_specs=pl.BlockSpec((tm, tn), lambda i, j: (i, j)),
        out_shape=jax.ShapeDtypeStruct((m, n), bf16),
        compiler_params=_cparams(("parallel", "parallel")),
        name="proj_zg",
    )(h, wt_zg)


def _sortable_key(score):
    b = pltpu.bitcast(score + 0.0, i32)
    return b ^ ((b >> 31) & jnp.int32(0x7FFFFFFF))


def _kth_largest(count_ge, shape, k):
    zero = jnp.zeros(shape, i32)
    prefix = jnp.where(count_ge(zero) >= k, zero, jnp.full(shape, INT_MIN, i32))

    def bit_body(bi, prefix):
        cand = prefix | (jnp.int32(1) << (30 - bi))
        return jnp.where(count_ge(cand) >= k, cand, prefix)

    thr = lax.fori_loop(0, 31, bit_body, prefix)
    return jnp.maximum(thr, jnp.int32(INT_MIN + 1))


def _resolve_ties(thr, k, count_ge, count_tied_before, demote, n_pos_bits, active=None):
    excess = count_ge(thr) - k
    if active is not None:
        excess = jnp.where(active, excess, 0)

    @pl.when(jnp.max(excess) > 0)
    def _():
        need = k - count_ge(thr + 1)

        def bit_body(bi, pos):
            cand = pos | (jnp.int32(1) << (n_pos_bits - 1 - bi))
            return jnp.where(count_tied_before(thr, cand) < need, cand, pos)

        last = lax.fori_loop(0, n_pos_bits, bit_body, jnp.zeros(thr.shape, i32))
        demote(thr, last)


def _sublane_allreduce(x, op):
    for shift in (4, 2, 1):
        x = op(x, pltpu.roll(x, shift, 0))
    return x


def _attn_prompt_body(qt_ref, qit_ref, wt_ref, z_ref, k_ref, vt_ref, ki_ref, o_ref,
                      key_scr, acc_scr, m_scr, l_scr, alpha_scr, s_scr, p_scr, bias_scr,
                      *, tq, k_top, n_pos_bits):
    i = pl.program_id(1)
    n_chunks = i + 1
    half = tq // 2

    def score_keys(c, diagonal):
        for sub in range(2):
            r0 = pl.multiple_of(c * tq + sub * half, half)
            kic = ki_ref[pl.ds(r0, half), :]
            acc = jnp.zeros((half, tq), f32)
            for hh in range(IDX_HEADS):
                lg = jnp.dot(kic, qit_ref[hh * IDX_DIM:(hh + 1) * IDX_DIM, :],
                             preferred_element_type=f32)
                acc = acc + jnp.maximum(lg, 0.0) * wt_ref[hh:hh + 1, :]
            key = _sortable_key(acc)
            if diagonal:
                kpos = lax.broadcasted_iota(i32, (half, tq), 0) + sub * half
                qpos = lax.broadcasted_iota(i32, (half, tq), 1)
                key = jnp.where(kpos <= qpos, key, jnp.int32(INT_MIN))
            key_scr[pl.ds(r0, half), :] = key

    def full_chunk(c, carry):
        score_keys(c, False)
        return carry

    lax.fori_loop(0, i, full_chunk, 0)
    score_keys(i, True)

    @pl.when(n_chunks % 2 == 1)
    def _():
        key_scr[pl.ds(pl.multiple_of(n_chunks * tq, tq), tq), :] = jnp.full((tq, tq), INT_MIN, i32)

    def count_ge(cand):
        def body(c, cnt):
            kk = key_scr[pl.ds(pl.multiple_of(c * 2 * tq, 2 * tq), 2 * tq), :]
            hit = (kk >= cand).astype(i32)
            return cnt + jnp.sum(hit.reshape(2 * tq // COUNT_ROWS, COUNT_ROWS, tq), axis=0)

        cnt = lax.fori_loop(0, (n_chunks + 1) // 2, body, jnp.zeros((COUNT_ROWS, tq), i32))
        return jnp.sum(cnt, axis=0, keepdims=True)

    thr = _kth_largest(count_ge, (1, tq), k_top)

    def chunk_positions(c):
        return lax.broadcasted_iota(i32, (tq, tq), 0) + c * tq

    def count_tied_before(thr, pos_limit):
        def body(c, cnt):
            kk = key_scr[pl.ds(pl.multiple_of(c * tq, tq), tq), :]
            hit = ((kk == thr) & (chunk_positions(c) < pos_limit)).astype(i32)
            return cnt + jnp.sum(hit.reshape(tq // SUBLANES, SUBLANES, tq), axis=0)

        cnt = lax.fori_loop(0, n_chunks, body, jnp.zeros((SUBLANES, tq), i32))
        return jnp.sum(cnt, axis=0, keepdims=True)

    def demote(thr, last):
        def body(c, carry):
            rows = pl.ds(pl.multiple_of(c * tq, tq), tq)
            kk = key_scr[rows, :]
            key_scr[rows, :] = jnp.where((kk == thr) & (chunk_positions(c) > last), thr - 1, kk)
            return carry

        lax.fori_loop(0, n_chunks, body, 0)

    _resolve_ties(thr, k_top, count_ge, count_tied_before, demote, n_pos_bits)

    m_scr[...] = jnp.full(m_scr.shape, -jnp.inf, f32)
    l_scr[...] = jnp.zeros(l_scr.shape, f32)
    acc_scr[...] = jnp.zeros(acc_scr.shape, f32)
    n_sub = tq // SUBLANES
    d_sub = HEAD_DIM // SUBLANES

    def attend_chunk(c, carry):
        r0 = pl.multiple_of(c * tq, tq)
        bias_scr[...] = jnp.where(key_scr[pl.ds(r0, tq), :] >= thr, 0.0, MASK_BIAS)
        for g in range(N_KV_HEADS):
            kc = k_ref[pl.ds(r0, tq), g * HEAD_DIM:(g + 1) * HEAD_DIM]
            for hh in range(HEADS_PER_KV):
                h = g * HEADS_PER_KV + hh
                s = jnp.dot(kc, qt_ref[h * HEAD_DIM:(h + 1) * HEAD_DIM, :], preferred_element_type=f32)
                s_scr[h] = s + bias_scr[...]
        blocks = [slice(j * ATTN_ROW_BLOCK, (j + 1) * ATTN_ROW_BLOCK) for j in range(tq // ATTN_ROW_BLOCK)]
        blk_sub = ATTN_ROW_BLOCK // SUBLANES
        for h in range(N_HEADS):
            mx = m_scr[h]
            for rows in blocks:
                mx = jnp.maximum(mx, jnp.max(s_scr[h, rows, :].reshape(blk_sub, SUBLANES, tq), axis=0))
            m_new = _sublane_allreduce(mx, jnp.maximum)
            alpha_scr[h] = jnp.exp2(m_scr[h] - m_new)
            m_scr[h] = m_new
        for h in range(N_HEADS):
            p3 = jnp.exp2(s_scr[h].reshape(n_sub, SUBLANES, tq) - m_scr[h][None])
            p_scr[h] = p3.reshape(tq, tq).astype(bf16)
        ones_rows = jnp.ones((PACKED_SUBLANES, tq), bf16)
        for g in range(N_KV_HEADS):
            vtc = jnp.concatenate([vt_ref[c, g * HEAD_DIM:(g + 1) * HEAD_DIM, :], ones_rows], axis=0)
            for hh in range(HEADS_PER_KV):
                h = g * HEADS_PER_KV + hh
                hs = slice(h * HEAD_DIM, (h + 1) * HEAD_DIM)
                pv = jnp.dot(vtc, p_scr[h], preferred_element_type=f32)
                alpha = alpha_scr[h]
                acc = acc_scr[hs, :].reshape(d_sub, SUBLANES, tq) * alpha[None]
                acc_scr[hs, :] = acc.reshape(HEAD_DIM, tq) + pv[:HEAD_DIM]
                l_scr[h] = alpha * l_scr[h] + pv[HEAD_DIM:HEAD_DIM + SUBLANES]
        return carry

    lax.fori_loop(0, n_chunks, attend_chunk, 0)

    for h in range(N_HEADS):
        hs = slice(h * HEAD_DIM, (h + 1) * HEAD_DIM)
        out_t = (acc_scr[hs, :].reshape(d_sub, SUBLANES, tq) / l_scr[h][None]).reshape(HEAD_DIM, tq)
        o_ref[:, hs] = (out_t.T * z_ref[:, hs].astype(f32)).astype(o_ref.dtype)


def _attn_prompt(qt, qit, wt, zg, kb, vt, kib, batch, seq):
    tq = ATTN_Q_TILE
    assert seq % tq == 0
    nq = seq // tq
    k_top = min(TOPK_MAX, seq // 4)
    assert k_top <= tq
    qrow = lambda b, i: (b * nq + i, 0)
    qcol = lambda b, i: (0, b * nq + i)
    per_b = lambda b, i: (b, 0)
    return pl.pallas_call(
        functools.partial(_attn_prompt_body, tq=tq, k_top=k_top, n_pos_bits=(seq - 1).bit_length()),
        grid=(batch, nq),
        in_specs=[pl.BlockSpec((ATTN_WIDTH, tq), qcol),
                  pl.BlockSpec((IDX_HEADS * IDX_DIM, tq), qcol),
                  pl.BlockSpec((IDX_HEADS, tq), qcol),
                  pl.BlockSpec((tq, ATTN_WIDTH), qrow),
                  pl.BlockSpec((seq, KV_WIDTH), per_b),
                  pl.BlockSpec((nq, KV_WIDTH, tq), lambda b, i: (b, 0, 0)),
                  pl.BlockSpec((seq, IDX_DIM), per_b)],
        out_specs=pl.BlockSpec((tq, ATTN_WIDTH), qrow),
        out_shape=jax.ShapeDtypeStruct((batch * seq, ATTN_WIDTH), bf16),
        scratch_shapes=[pltpu.VMEM((seq + tq, tq), i32),
                        pltpu.VMEM((ATTN_WIDTH, tq), f32),
                        pltpu.VMEM((N_HEADS, SUBLANES, tq), f32),
                        pltpu.VMEM((N_HEADS, SUBLANES, tq), f32),
                        pltpu.VMEM((N_HEADS, SUBLANES, tq), f32),
                        pltpu.VMEM((N_HEADS, tq, tq), f32),
                        pltpu.VMEM((N_HEADS, tq, tq), bf16),
                        pltpu.VMEM((tq, tq), f32)],
        compiler_params=_cparams(("parallel", "arbitrary")),
        name="attn_prompt",
    )(qt, qit, wt, zg, kb, vt, kib)


def _idx_sample_body(pt_ref, qi_ref, w_ref, *rest, n_pages, n_new, k_top, ppb):
    kid_refs = rest[:ppb]
    kinew_ref, key_ref, thr_ref = rest[ppb:]
    p = pl.program_id(1)
    tok = IDX_ROWS

    def page_keys(ki_page_t):
        lg = jnp.dot(qi_ref[0], ki_page_t.astype(bf16), preferred_element_type=f32)
        wgt = jnp.maximum(lg, 0.0) * w_ref[0]
        score = jnp.sum(wgt.reshape(IDX_HEADS, tok, PAGE_SIZE), axis=0)
        return _sortable_key(score)

    for j in range(ppb):
        key_ref[0, p * ppb + j] = page_keys(kid_refs[j][...])

    @pl.when(p == pl.num_programs(1) - 1)
    def _():
        key = page_keys(kinew_ref[0])
        kpos = lax.broadcasted_iota(i32, (tok, PAGE_SIZE), 1)
        qtok = lax.broadcasted_iota(i32, (tok, PAGE_SIZE), 0)
        key_ref[0, n_pages] = jnp.where((kpos <= qtok) & (kpos < n_new), key, jnp.int32(INT_MIN))

        def count_ge(cand):
            def body(c, cnt):
                hit = (key_ref[0, pl.ds(c * ppb, ppb)] >= cand).astype(i32)
                return cnt + jnp.sum(hit, axis=0)

            cnt = lax.fori_loop(0, n_pages // ppb, body, jnp.zeros((tok, PAGE_SIZE), i32))
            cnt = cnt + (key_ref[0, n_pages] >= cand).astype(i32)
            return jnp.sum(cnt, axis=-1, keepdims=True)

        thr = _kth_largest(count_ge, (tok, 1), k_top)
        thr_ref[0] = jnp.broadcast_to(thr, (tok, PAGE_SIZE))

        def page_positions(c):
            return lax.broadcasted_iota(i32, (tok, PAGE_SIZE), 1) + c * PAGE_SIZE

        def count_tied_before(thr, pos_limit):
            def body(c, cnt):
                return cnt + ((key_ref[0, c] == thr) & (page_positions(c) < pos_limit)).astype(i32)

            cnt = lax.fori_loop(0, n_pages + 1, body, jnp.zeros((tok, PAGE_SIZE), i32))
            return jnp.sum(cnt, axis=-1, keepdims=True)

        def demote(thr, last):
            def body(c, carry):
                kk = key_ref[0, c]
                key_ref[0, c] = jnp.where((kk == thr) & (page_positions(c) > last), thr - 1, kk)
                return carry

            lax.fori_loop(0, n_pages + 1, body, 0)

        real_rows = lax.broadcasted_iota(i32, (tok, 1), 0) < n_new
        _resolve_ties(thr, k_top, count_ge, count_tied_before, demote,
                      ((n_pages + 1) * PAGE_SIZE - 1).bit_length(), active=real_rows)


def _pages_per_step(n_pages, cap=PAGES_PER_STEP):
    ppb = min(cap, n_pages)
    assert n_pages % ppb == 0
    return ppb


def _page_spec(block, layer, ppb, j):
    return pl.BlockSpec(block, lambda b, p, pt: (layer, pt[b, p * ppb + j], 0, 0))


def _idx_sample(page_table, qi_rows, w_rows, cache_kidx, layer, ki_new_pages, n_new):
    nb, n_pages = page_table.shape
    tok = IDX_ROWS
    assert n_new <= tok
    ppb = _pages_per_step(n_pages, IDX_PAGES_PER_STEP)
    k_top = min(TOPK_MAX, (n_pages * PAGE_SIZE + n_new) // 4)
    per_b = lambda b, p, pt: (b, 0, 0)
    grid_spec = pltpu.PrefetchScalarGridSpec(
        num_scalar_prefetch=1,
        grid=(nb, n_pages // ppb),
        in_specs=[pl.BlockSpec((1, IDX_HEADS * tok, IDX_DIM), per_b),
                  pl.BlockSpec((1, IDX_HEADS * tok, PAGE_SIZE), per_b)]
                 + [_page_spec((None, None, IDX_DIM, PAGE_SIZE), layer, ppb, j) for j in range(ppb)]
                 + [pl.BlockSpec((1, IDX_DIM, PAGE_SIZE), per_b)],
        out_specs=[pl.BlockSpec((1, n_pages + 1, tok, PAGE_SIZE), lambda b, p, pt: (b, 0, 0, 0)),
                   pl.BlockSpec((1, tok, PAGE_SIZE), per_b)],
    )
    return pl.pallas_call(
        functools.partial(_idx_sample_body, n_pages=n_pages, n_new=n_new, k_top=k_top, ppb=ppb),
        grid_spec=grid_spec,
        out_shape=[jax.ShapeDtypeStruct((nb, n_pages + 1, tok, PAGE_SIZE), i32),
                   jax.ShapeDtypeStruct((nb, tok, PAGE_SIZE), i32)],
        compiler_params=_cparams(("parallel", "arbitrary")),
        name="idx_sample",
    )(page_table, qi_rows, w_rows, *([cache_kidx] * ppb), ki_new_pages)


def _attn_sample_body(pt_ref, q_ref, key_ref, keyn_ref, thr_ref, *rest, ppb):
    k_refs = rest[:ppb]
    v_refs = rest[ppb:2 * ppb]
    kn_ref, vn_ref, o_ref, acc_scr, m_scr, l_scr = rest[2 * ppb:]
    p = pl.program_id(1)
    tok = SAMPLE_ROWS
    rows_g = HEADS_PER_KV * tok
    thr = thr_ref[0]

    @pl.when(p == 0)
    def _():
        m_scr[...] = jnp.full(m_scr.shape, -jnp.inf, f32)
        l_scr[...] = jnp.zeros(l_scr.shape, f32)
        acc_scr[...] = jnp.zeros(acc_scr.shape, f32)

    def attend(bias_tok, k_of_group, v_of_group):
        bias = jnp.concatenate([bias_tok] * (rows_g // IDX_ROWS), axis=0)
        for g in range(N_KV_HEADS):
            rs = slice(g * rows_g, (g + 1) * rows_g)
            s = lax.dot_general(q_ref[0, rs, :], k_of_group(g), (((1,), (1,)), ((), ())),
                                preferred_element_type=f32) + bias
            m_prev = m_scr[rs, :]
            m_new = jnp.maximum(m_prev, jnp.max(s, axis=-1, keepdims=True))
            alpha = jnp.exp2(m_prev - m_new)
            pr = jnp.exp2(s - m_new)
            l_scr[rs, :] = alpha * l_scr[rs, :] + jnp.sum(pr, axis=-1, keepdims=True)
            acc_scr[rs, :] = alpha * acc_scr[rs, :] + jnp.dot(
                pr.astype(bf16), v_of_group(g), preferred_element_type=f32)
            m_scr[rs, :] = m_new

    def mask_bias(keys):
        return jnp.where(keys >= thr, 0.0, MASK_BIAS).astype(f32)

    def cached(refs):
        def of_group(g):
            rows = pl.ds(g, PAGE_SIZE, stride=N_KV_HEADS)
            return jnp.concatenate([r[rows, :].astype(bf16) for r in refs], axis=0)
        return of_group

    attend(jnp.concatenate([mask_bias(key_ref[0, j]) for j in range(ppb)], axis=1),
           cached(k_refs), cached(v_refs))

    @pl.when(p == pl.num_programs(1) - 1)
    def _():
        def fresh(ref):
            return lambda g: ref[0, :, g * HEAD_DIM:(g + 1) * HEAD_DIM].astype(bf16)

        attend(mask_bias(keyn_ref[0, 0]), fresh(kn_ref), fresh(vn_ref))
        o_ref[0] = acc_scr[...] / l_scr[...]


def _attn_sample(page_table, q_rows, keys, thr, cache_k, cache_v, layer, k_new_pages, v_new_pages):
    nb, n_pages = page_table.shape
    tok = SAMPLE_ROWS
    rows = N_HEADS * tok
    ppb = _pages_per_step(n_pages)
    per_b = lambda b, p, pt: (b, 0, 0)
    page_block = (None, None, PAGE_SIZE * N_KV_HEADS, HEAD_DIM)
    grid_spec = pltpu.PrefetchScalarGridSpec(
        num_scalar_prefetch=1,
        grid=(nb, n_pages // ppb),
        in_specs=[pl.BlockSpec((1, rows, HEAD_DIM), per_b),
                  pl.BlockSpec((1, ppb, IDX_ROWS, PAGE_SIZE), lambda b, p, pt: (b, p, 0, 0)),
                  pl.BlockSpec((1, 1, IDX_ROWS, PAGE_SIZE), lambda b, p, pt: (b, n_pages, 0, 0)),
                  pl.BlockSpec((1, IDX_ROWS, PAGE_SIZE), per_b)]
                 + [_page_spec(page_block, layer, ppb, j) for j in range(ppb)]
                 + [_page_spec(page_block, layer, ppb, j) for j in range(ppb)]
                 + [pl.BlockSpec((1, PAGE_SIZE, KV_WIDTH), per_b),
                    pl.BlockSpec((1, PAGE_SIZE, KV_WIDTH), per_b)],
        out_specs=pl.BlockSpec((1, rows, HEAD_DIM), per_b),
        scratch_shapes=[pltpu.VMEM((rows, HEAD_DIM), f32),
                        pltpu.VMEM((rows, 1), f32),
                        pltpu.VMEM((rows, 1), f32)],
    )
    return pl.pallas_call(
        functools.partial(_attn_sample_body, ppb=ppb),
        grid_spec=grid_spec,
        out_shape=jax.ShapeDtypeStruct((nb, rows, HEAD_DIM), f32),
        compiler_params=_cparams(("parallel", "arbitrary")),
        name="attn_sample",
    )(page_table, q_rows, keys, keys, thr, *([cache_k] * ppb), *([cache_v] * ppb),
      k_new_pages, v_new_pages)


def _gate_mul_body(a_ref, z_ref, o_ref):
    o_ref[...] = (a_ref[...] * z_ref[...].astype(f32)).astype(o_ref.dtype)


def _gate_mul(a, zg):
    m, n = a.shape
    return pl.pallas_call(
        _gate_mul_body,
        grid=(1,),
        in_specs=[pl.BlockSpec((m, n), lambda i: (0, 0)),
                  pl.BlockSpec((m, n), lambda i: (0, 0))],
        out_specs=pl.BlockSpec((m, n), lambda i: (0, 0)),
        out_shape=jax.ShapeDtypeStruct((m, n), bf16),
        compiler_params=_cparams(("arbitrary",)),
        name="gate_mul",
    )(a, zg)


def _s5_param_body(lr_ref, li_ref, ldt_ref, bre_ref, bim_ref, ar_ref, ai_ref, bbre_ref, bbim_ref):
    lr = lr_ref[...]
    li = li_ref[...]
    dt = jnp.exp(ldt_ref[...])
    mag = jnp.exp(lr * dt)
    ar = mag * jnp.cos(li * dt)
    ai = mag * jnp.sin(li * dt)
    den = lr * lr + li * li
    nr = ar - 1.0
    cr = (nr * lr + ai * li) / den
    ci = (ai * lr - nr * li) / den
    ar_ref[...] = ar
    ai_ref[...] = ai
    for h in range(S5_GROUP):
        bbre_ref[h] = cr * bre_ref[h] - ci * bim_ref[h]
        bbim_ref[h] = cr * bim_ref[h] + ci * bre_ref[h]


def _s5_params(lam_re, lam_im, log_dt, b_re, b_im):
    depth = lam_re.shape[0]
    gs = (None, S5_GROUPS, S5_STATE)
    bs = (None, S5_GROUP, S5_GROUPS, S5_STATE)
    at3 = lambda l: (l, 0, 0)
    at4 = lambda l: (l, 0, 0, 0)
    return pl.pallas_call(
        _s5_param_body,
        grid=(depth,),
        in_specs=[pl.BlockSpec(gs, at3), pl.BlockSpec(gs, at3),
                  pl.BlockSpec((None, S5_GROUPS, 1), at3),
                  pl.BlockSpec(bs, at4), pl.BlockSpec(bs, at4)],
        out_specs=[pl.BlockSpec(gs, at3), pl.BlockSpec(gs, at3),
                   pl.BlockSpec(bs, at4), pl.BlockSpec(bs, at4)],
        out_shape=[jax.ShapeDtypeStruct((depth, S5_GROUPS, S5_STATE), f32),
                   jax.ShapeDtypeStruct((depth, S5_GROUPS, S5_STATE), f32),
                   jax.ShapeDtypeStruct((depth, S5_GROUP, S5_GROUPS, S5_STATE), f32),
                   jax.ShapeDtypeStruct((depth, S5_GROUP, S5_GROUPS, S5_STATE), f32)],
        compiler_params=_cparams(("arbitrary",)),
        name="s5_params",
    )(lam_re, lam_im, log_dt.reshape(depth, S5_GROUPS, 1), b_re, b_im)


def _gelu_tanh(x):
    return 0.5 * x * (1.0 + jnp.tanh(math.sqrt(2.0 / math.pi) * (x + 0.044715 * (x * x * x))))


def _s5_body(u_ref, zs_ref, h0r_ref, h0i_ref, ar_ref, ai_ref, wbu_ref, wcr_ref, wci_ref, d_ref, wglu_ref,
             o_ref, sr_ref, si_ref, xr_scr, xi_scr, y_scr, str_scr, sti_scr, *, tc, n_last):
    c = pl.program_id(1)

    @pl.when(c == 0)
    def _():
        str_scr[...] = h0r_ref[0]
        sti_scr[...] = h0i_ref[0]

    n_ct = S5_SG_STATE // LANES

    def gather_planes(scr, rows):
        return jnp.concatenate([scr[j, rows, :] for j in range(n_ct)], axis=1)

    def scatter_planes(scr, rows, val):
        for j in range(n_ct):
            scr[j, rows, :] = val[:, j * LANES:(j + 1) * LANES]

    for sg in range(S5_SG):
        rows = pl.ds(sg, tc, stride=S5_SG)
        u_sg = u_ref[:, sg * LANES:(sg + 1) * LANES]
        bu = jnp.dot(u_sg.astype(bf16), wbu_ref[sg], preferred_element_type=f32)
        scatter_planes(xr_scr, rows, bu[:, :S5_SG_STATE])
        scatter_planes(xi_scr, rows, bu[:, S5_SG_STATE:])

    ar = ar_ref[...]
    ai = ai_ref[...]

    def step(t, carry):
        xr, xi = carry
        r = pl.ds(pl.multiple_of(t * S5_SG, S5_SG), S5_SG)
        nr = ar * xr - ai * xi + gather_planes(xr_scr, r)
        ni = ar * xi + ai * xr + gather_planes(xi_scr, r)
        scatter_planes(xr_scr, r, nr)
        scatter_planes(xi_scr, r, ni)
        return nr, ni

    xr, xi = lax.fori_loop(0, tc, step, (str_scr[...], sti_scr[...]), unroll=S5_SCAN_UNROLL)
    str_scr[...] = xr
    sti_scr[...] = xi

    for sg in range(S5_SG):
        rows = pl.ds(sg, tc, stride=S5_SG)
        cs = slice(sg * LANES, (sg + 1) * LANES)
        y = (jnp.dot(gather_planes(xr_scr, rows).astype(bf16), wcr_ref[sg], preferred_element_type=f32)
             - jnp.dot(gather_planes(xi_scr, rows).astype(bf16), wci_ref[sg], preferred_element_type=f32)
             + d_ref[:, cs] * u_ref[:, cs])
        y_scr[:, cs] = _gelu_tanh(y)

    y = y_scr[...]
    gate = _sigmoid(jnp.dot(y.astype(bf16), wglu_ref[...], preferred_element_type=f32))
    o_ref[...] = (y * gate * zs_ref[...].astype(f32)).astype(o_ref.dtype)

    @pl.when(c == pl.num_programs(1) - 1)
    def _():
        last = slice((n_last - 1) * S5_SG, n_last * S5_SG)
        sr_ref[0] = gather_planes(xr_scr, last)
        si_ref[0] = gather_planes(xi_scr, last)


def _s5(u, zg, h0_re, h0_im, prm, layer, nb, seq, n_real):
    tc = min(seq, S5_CHUNK)
    nch = seq // tc
    n_last = n_real - (nch - 1) * tc
    zs_col = ATTN_WIDTH // S5_WIDTH
    fix2 = lambda b, c: (0, 0)
    fix3 = lambda b, c: (0, 0, 0)
    st = lambda b, c: (b, 0, 0)
    out, s_re, s_im = pl.pallas_call(
        functools.partial(_s5_body, tc=tc, n_last=n_last),
        grid=(nb, nch),
        in_specs=[pl.BlockSpec((tc, S5_WIDTH), lambda b, c: (b * nch + c, 0)),
                  pl.BlockSpec((tc, S5_WIDTH), lambda b, c: (b * nch + c, zs_col)),
                  pl.BlockSpec((1, S5_SG, S5_SG_STATE), st),
                  pl.BlockSpec((1, S5_SG, S5_SG_STATE), st),
                  _layer_spec((S5_SG, S5_SG_STATE), layer, fix2),
                  _layer_spec((S5_SG, S5_SG_STATE), layer, fix2),
                  _layer_spec((S5_SG, LANES, 2 * S5_SG_STATE), layer, fix3),
                  _layer_spec((S5_SG, S5_SG_STATE, LANES), layer, fix3),
                  _layer_spec((S5_SG, S5_SG_STATE, LANES), layer, fix3),
                  _layer_spec((1, S5_WIDTH), layer, fix2),
                  _layer_spec((S5_WIDTH, S5_WIDTH), layer, fix2)],
        out_specs=[pl.BlockSpec((tc, S5_WIDTH), lambda b, c: (b * nch + c, 0)),
                   pl.BlockSpec((1, S5_SG, S5_SG_STATE), st),
                   pl.BlockSpec((1, S5_SG, S5_SG_STATE), st)],
        out_shape=[jax.ShapeDtypeStruct((nb * seq, S5_WIDTH), bf16),
                   jax.ShapeDtypeStruct((nb, S5_SG, S5_SG_STATE), f32),
                   jax.ShapeDtypeStruct((nb, S5_SG, S5_SG_STATE), f32)],
        scratch_shapes=[pltpu.VMEM((S5_SG_STATE // LANES, tc * S5_SG, LANES), f32),
                        pltpu.VMEM((S5_SG_STATE // LANES, tc * S5_SG, LANES), f32),
                        pltpu.VMEM((tc, S5_WIDTH), f32),
                        pltpu.VMEM((S5_SG, S5_SG_STATE), f32),
                        pltpu.VMEM((S5_SG, S5_SG_STATE), f32)],
        compiler_params=_cparams(("parallel", "arbitrary")),
        name="s5_scan",
    )(u, zg, h0_re.reshape(nb, S5_SG, S5_SG_STATE), h0_im.reshape(nb, S5_SG, S5_SG_STATE),
      prm["a_re"], prm["a_im"], prm["w_bu"], prm["w_c_re"], prm["w_c_im"], prm["d"], prm["w_glu"])
    return out, s_re.reshape(nb, S5_GROUPS, S5_STATE), s_im.reshape(nb, S5_GROUPS, S5_STATE)


def _merge_body(a_ref, s_ref, wa_ref, ws_ref, ga_ref, gs_ref, o_ref):
    o_a = jnp.dot(a_ref[...], wa_ref[...], preferred_element_type=f32)
    o_s = jnp.dot(s_ref[...], ws_ref[...], preferred_element_type=f32)
    o_ref[...] = (ga_ref[...].astype(f32) * o_a + gs_ref[...].astype(f32) * o_s).astype(o_ref.dtype)


def _merge(a_in, s_in, w_a, w_s, zg, layer):
    m = a_in.shape[0]
    tm = _row_tile(m, MM_ROW_TILE)
    tn = MM_COL_TILE
    ga0 = (ATTN_WIDTH + S5_WIDTH) // tn
    gs0 = (ATTN_WIDTH + S5_WIDTH + D_MODEL) // tn
    return pl.pallas_call(
        _merge_body,
        grid=(m // tm, D_MODEL // tn),
        in_specs=[pl.BlockSpec((tm, ATTN_WIDTH), lambda i, j: (i, 0)),
                  pl.BlockSpec((tm, S5_WIDTH), lambda i, j: (i, 0)),
                  _layer_spec((ATTN_WIDTH, tn), layer, lambda i, j: (0, j)),
                  _layer_spec((S5_WIDTH, tn), layer, lambda i, j: (0, j)),
                  pl.BlockSpec((tm, tn), lambda i, j: (i, ga0 + j)),
                  pl.BlockSpec((tm, tn), lambda i, j: (i, gs0 + j))],
        out_specs=pl.BlockSpec((tm, tn), lambda i, j: (i, j)),
        out_shape=jax.ShapeDtypeStruct((m, D_MODEL), bf16),
        compiler_params=_cparams(("parallel", "parallel")),
        name="merge",
    )(a_in, s_in, w_a, w_s, zg, zg)


def _out_body(x_ref, m_ref, w_ref, o_ref):
    o_ref[...] = x_ref[...] + jnp.dot(m_ref[...], w_ref[...], preferred_element_type=f32)


def _out_proj(x, merged, w_out, layer):
    m = x.shape[0]
    tm = _row_tile(m, MM_ROW_TILE)
    tn = MM_COL_TILE
    return pl.pallas_call(
        _out_body,
        grid=(m // tm, D_MODEL // tn),
        in_specs=[pl.BlockSpec((tm, tn), lambda i, j: (i, j)),
                  pl.BlockSpec((tm, D_MODEL), lambda i, j: (i, 0)),
                  _layer_spec((D_MODEL, tn), layer, lambda i, j: (0, j))],
        out_specs=pl.BlockSpec((tm, tn), lambda i, j: (i, j)),
        out_shape=jax.ShapeDtypeStruct((m, D_MODEL), f32),
        compiler_params=_cparams(("parallel", "parallel")),
        name="out_proj",
    )(x, merged, w_out)


def _out_norm_body(x_ref, m_ref, w_ref, g_ref, o_ref, h_ref):
    y = x_ref[...] + jnp.dot(m_ref[...], w_ref[...], preferred_element_type=f32)
    o_ref[...] = y
    ms = jnp.mean(y * y, axis=-1, keepdims=True)
    h_ref[...] = (y * lax.rsqrt(ms + EPS) * g_ref[...]).astype(h_ref.dtype)


def _out_proj_norm(x, merged, w_out, gains, layer):
    m = x.shape[0]
    tm = _row_tile(m, ROW_TILE)
    row = lambda i: (i, 0)
    return pl.pallas_call(
        _out_norm_body,
        grid=(m // tm,),
        in_specs=[pl.BlockSpec((tm, D_MODEL), row),
                  pl.BlockSpec((tm, D_MODEL), row),
                  _layer_spec((D_MODEL, D_MODEL), layer, lambda i: (0, 0)),
                  _layer_spec((1, D_MODEL), layer + 1, lambda i: (0, 0))],
        out_specs=[pl.BlockSpec((tm, D_MODEL), row), pl.BlockSpec((tm, D_MODEL), row)],
        out_shape=[jax.ShapeDtypeStruct((m, D_MODEL), f32),
                   jax.ShapeDtypeStruct((m, D_MODEL), bf16)],
        compiler_params=_cparams(("parallel",)),
        name="out_proj_norm",
    )(x, merged, w_out, gains)


def _rope_tables(pos):
    posf = pos.astype(f32)[:, None]
    half = HEAD_DIM // 2
    inv = ROPE_THETA ** (-jnp.arange(half, dtype=f32) / half)
    ang = posf * inv[None, :]
    cos, sin = jnp.cos(ang), jnp.sin(ang)
    cos_h = jnp.concatenate([cos, cos], axis=1)
    sin_h = jnp.concatenate([-sin, sin], axis=1)
    half_i = IDX_DIM // 2
    inv_i = ROPE_THETA ** (-jnp.arange(half_i, dtype=f32) / half_i)
    ang_i = posf * inv_i[None, :]
    cos_i, sin_i = jnp.cos(ang_i), jnp.sin(ang_i)
    zeros = jnp.zeros_like(cos_i)
    pad = jnp.zeros((pos.shape[0], LANES - IDX_DIM), f32)
    cos_k = jnp.concatenate([cos_i, cos_i, pad], axis=1)
    sin_a = jnp.concatenate([-sin_i, zeros, pad], axis=1)
    sin_b = jnp.concatenate([zeros, sin_i, pad], axis=1)
    return dict(cos_h=cos_h, sin_h=sin_h, cos_k=cos_k, sin_a=sin_a, sin_b=sin_b,
                cos_ht=cos.T, sin_ht=sin.T, cos_t=cos_i.T, sin_t=sin_i.T)


def _in_proj_offsets():
    offs = [0]
    for s in IN_SIZES:
        offs.append(offs[-1] + s)
    return offs


def _repack_body(w_ref, wq_ref, wkv_ref, wqiw_ref, wki_ref, wu_ref, wzg_ref):
    offs = _in_proj_offsets()
    seg = lambda k: w_ref[offs[k]:offs[k + 1], :].astype(bf16)
    cols = w_ref.shape[1]
    wq_ref[...] = seg(0)
    wkv_ref[...] = w_ref[offs[1]:offs[3], :].astype(bf16)
    n_qi = IDX_HEADS * IDX_DIM
    wqiw_ref[:n_qi, :] = seg(3)
    wqiw_ref[n_qi:, :] = seg(5)
    wki_ref[:IDX_DIM, :] = seg(4)
    wki_ref[IDX_DIM:, :] = jnp.zeros((LANES - IDX_DIM, cols), bf16)
    wu_ref[...] = seg(7)
    row = 0
    for k in (6, 8, 9, 10):
        wzg_ref[row:row + IN_SIZES[k], :] = seg(k)
        row += IN_SIZES[k]


def _repack_in_proj(w_in_t):
    depth, n_in, d = w_in_t.shape
    cols = REPACK_COLS
    n_qiw = IDX_HEADS * IDX_DIM + IDX_HEADS
    n_zg = ATTN_WIDTH + S5_WIDTH + 2 * D_MODEL
    heights = (ATTN_WIDTH, 2 * KV_WIDTH, n_qiw, LANES, S5_WIDTH, n_zg)
    at = lambda l, c: (l, 0, c)
    return pl.pallas_call(
        _repack_body,
        grid=(depth, d // cols),
        in_specs=[pl.BlockSpec((None, n_in, cols), at)],
        out_specs=[pl.BlockSpec((None, hgt, cols), at) for hgt in heights],
        out_shape=[jax.ShapeDtypeStruct((depth, hgt, d), bf16) for hgt in heights],
        compiler_params=_cparams(("parallel", "parallel")),
        name="repack_in_proj",
    )(w_in_t)


def _pack_weights(w_in, w_glu, w_br_attn, w_br_s5, w_out, norm_gain, q_norm_gain, k_norm_gain):
    depth = w_in.shape[0]
    wt_q, wt_kv, wt_qi, wt_ki, wt_u, wt_zg = _repack_in_proj(jnp.swapaxes(w_in, 1, 2))
    return dict(
        wt_q=wt_q, wt_kv=wt_kv, wt_qi=wt_qi, wt_ki=wt_ki, wt_u=wt_u, wt_zg=wt_zg,
        w_glu=w_glu.astype(bf16), w_br_attn=w_br_attn.astype(bf16),
        w_br_s5=w_br_s5.astype(bf16), w_out=w_out.astype(bf16),
        norm_gain=norm_gain.reshape(depth, 1, D_MODEL),
        q_gain=q_norm_gain.reshape(depth, HEAD_DIM, 1),
        k_gain=k_norm_gain.reshape(depth, 1, HEAD_DIM))


def _block_diag(blocks):
    depth, sg, gg, r, c = blocks.shape
    eye = jnp.eye(gg, dtype=blocks.dtype)
    return jnp.einsum("lsgrc,gk->lsgrkc", blocks, eye).reshape(depth, sg, gg * r, gg * c)


def _s5_stacked_params(lam_re, lam_im, log_dt, b_re, b_im, c_re, c_im, d, w_glu_bf16):
    depth = lam_re.shape[0]
    a_re, a_im, bb_re, bb_im = _s5_params(lam_re, lam_im, log_dt,
                                          jnp.transpose(b_re, (0, 3, 1, 2)), jnp.transpose(b_im, (0, 3, 1, 2)))
    def bu_blocks(bb):
        return jnp.transpose(bb, (0, 2, 1, 3)).reshape(depth, S5_SG, S5_SG_GROUPS, S5_GROUP, S5_STATE)
    w_bu = jnp.concatenate([_block_diag(bu_blocks(bb_re)), _block_diag(bu_blocks(bb_im))], axis=3)
    def c_blocks(cm):
        return jnp.transpose(cm, (0, 1, 3, 2)).reshape(depth, S5_SG, S5_SG_GROUPS, S5_STATE, S5_GROUP)
    return dict(a_re=a_re.reshape(depth, S5_SG, S5_SG_STATE), a_im=a_im.reshape(depth, S5_SG, S5_SG_STATE),
                w_bu=w_bu.astype(bf16),
                w_c_re=_block_diag(c_blocks(c_re)).astype(bf16),
                w_c_im=_block_diag(c_blocks(c_im)).astype(bf16),
                d=d.reshape(depth, 1, S5_WIDTH), w_glu=w_glu_bf16)


def _projections(h, wts, tabs, layer):
    qt = _proj_qt(h, wts["wt_q"], wts["q_gain"], tabs["cos_ht"], tabs["sin_ht"], layer)
    k, v, kb, vt = _proj_kv(h, wts["wt_kv"], wts["k_gain"], tabs["cos_h"], tabs["sin_h"], layer)
    qit, wt = _proj_qit(h, wts["wt_qi"], tabs["cos_t"], tabs["sin_t"], layer)
    ki, kib = _proj_ki(h, wts["wt_ki"], tabs["cos_k"], tabs["sin_a"], tabs["sin_b"], layer)
    u = _proj_plain(h, wts["wt_u"], f32, layer)
    zg = _proj_zg(h, wts["wt_zg"], layer)
    return dict(qt=qt, k=k, v=v, kb=kb, vt=vt, qit=qit, wt=wt, ki=ki, kib=kib, u=u, zg=zg)


def _finish(x, a_in, s_in, zg, wts, layer, last):
    merged = _merge(a_in, s_in, wts["w_br_attn"], wts["w_br_s5"], zg, layer)
    if last:
        return _out_proj(x, merged, wts["w_out"], layer), None
    return _out_proj_norm(x, merged, wts["w_out"], wts["norm_gain"], layer)


def kernel(x_prompt, x_sample, cache_k, cache_v, cache_kidx, state_s5_re, state_s5_im, page_table, norm_gain, w_in, q_norm_gain, k_norm_gain, s5_lam_re, s5_lam_im, s5_log_dt, s5_b_re, s5_b_im, s5_c_re, s5_c_im, s5_d, w_glu, w_br_attn, w_br_s5, w_out):
    depth = w_in.shape[0]
    b_p, t_p = x_prompt.shape[:2]
    b_s, t_s = x_sample.shape[:2]
    tok = SAMPLE_ROWS
    assert t_s <= tok
    n_pages = page_table.shape[1]
    past = n_pages * PAGE_SIZE
    n_phys = cache_k.shape[1]

    tabs_p = _rope_tables(jnp.tile(jnp.arange(t_p, dtype=i32), b_p))
    tabs_s = _rope_tables(jnp.tile(past + jnp.arange(tok, dtype=i32), b_s))

    xp = x_prompt.reshape(b_p * t_p, D_MODEL)
    xs = jnp.pad(x_sample, ((0, 0), (0, tok - t_s), (0, 0))).reshape(b_s * tok, D_MODEL)

    cache_k4 = cache_k.reshape(depth, n_phys, PAGE_SIZE * N_KV_HEADS, HEAD_DIM)
    cache_v4 = cache_v.reshape(depth, n_phys, PAGE_SIZE * N_KV_HEADS, HEAD_DIM)
    cache_kidx_t = jnp.swapaxes(cache_kidx, 2, 3)
    zeros_state = jnp.zeros((b_p, S5_GROUPS, S5_STATE), f32)

    def new_page(a):
        w = a.shape[-1]
        return jnp.pad(a.reshape(b_s, tok, w), ((0, 0), (0, PAGE_SIZE - tok), (0, 0)))

    wts = _pack_weights(w_in, w_glu, w_br_attn, w_br_s5, w_out, norm_gain, q_norm_gain, k_norm_gain)
    s5p = _s5_stacked_params(s5_lam_re, s5_lam_im, s5_log_dt, s5_b_re, s5_b_im, s5_c_re, s5_c_im, s5_d,
                             wts["w_glu"])

    outs_p, outs_s = [], []
    hp = _rmsnorm(xp, wts["norm_gain"], 0)
    hs = _rmsnorm(xs, wts["norm_gain"], 0)
    for l in range(depth):
        last = l == depth - 1
        pp = _projections(hp, wts, tabs_p, l)
        a_in = _attn_prompt(pp["qt"], pp["qit"], pp["wt"], pp["zg"], pp["kb"], pp["vt"], pp["kib"], b_p, t_p)
        s_in, sr_p, si_p = _s5(pp["u"], pp["zg"], zeros_state, zeros_state, s5p, l, b_p, t_p, t_p)
        xp, hp = _finish(xp, a_in, s_in, pp["zg"], wts, l, last)
        outs_p.append((pp["k"].reshape(b_p, t_p, N_KV_HEADS, HEAD_DIM),
                       pp["v"].reshape(b_p, t_p, N_KV_HEADS, HEAD_DIM),
                       pp["ki"].reshape(b_p, t_p, IDX_DIM), sr_p, si_p))

        ps = _projections(hs, wts, tabs_s, l)
        qi_rows = jnp.transpose(ps["qit"].reshape(IDX_HEADS, IDX_DIM, b_s, tok)[..., :IDX_ROWS], (2, 0, 3, 1)
                                ).reshape(b_s, IDX_HEADS * IDX_ROWS, IDX_DIM)
        w_rows = jnp.broadcast_to(
            jnp.transpose(ps["wt"].reshape(IDX_HEADS, b_s, tok)[..., :IDX_ROWS], (1, 0, 2)
                          ).reshape(b_s, IDX_HEADS * IDX_ROWS, 1),
            (b_s, IDX_HEADS * IDX_ROWS, PAGE_SIZE))
        keys, thr = _idx_sample(page_table, qi_rows, w_rows, cache_kidx_t, l,
                                jnp.swapaxes(new_page(ps["ki"]), 1, 2), t_s)
        q_rows = jnp.transpose(ps["qt"].reshape(N_HEADS, HEAD_DIM, b_s, tok), (2, 0, 3, 1)
                               ).reshape(b_s, N_HEADS * tok, HEAD_DIM)
        k_s = ps["k"].reshape(b_s * tok, KV_WIDTH)
        v_s = ps["v"].reshape(b_s * tok, KV_WIDTH)
        o_rows = _attn_sample(page_table, q_rows, keys, thr, cache_k4, cache_v4, l,
                              new_page(k_s), new_page(v_s))
        attn_s = jnp.transpose(o_rows.reshape(b_s, N_HEADS, tok, HEAD_DIM), (0, 2, 1, 3)
                               ).reshape(b_s * tok, ATTN_WIDTH)
        a_in_s = _gate_mul(attn_s, ps["zg"][:, :ATTN_WIDTH])
        s_in_s, sr_s, si_s = _s5(ps["u"], ps["zg"], state_s5_re[l], state_s5_im[l], s5p, l, b_s, tok, t_s)
        xs, hs = _finish(xs, a_in_s, s_in_s, ps["zg"], wts, l, last)
        real = lambda a: a.reshape((b_s, tok) + a.shape[1:])[:, :t_s]
        outs_s.append((real(k_s).reshape(b_s, t_s, N_KV_HEADS, HEAD_DIM),
                       real(v_s).reshape(b_s, t_s, N_KV_HEADS, HEAD_DIM),
                       real(ps["ki"]), sr_s, si_s))

    k_prompt, v_prompt, kidx_prompt, s5_re_prompt, s5_im_prompt = [jnp.stack(a) for a in zip(*outs_p)]
    k_sample, v_sample, kidx_sample, s5_re_sample, s5_im_sample = [jnp.stack(a) for a in zip(*outs_s)]
    y_prompt = xp.reshape(b_p, t_p, D_MODEL)
    y_sample = xs.reshape(b_s, tok, D_MODEL)[:, :t_s]
    return (y_prompt, y_sample, k_prompt, v_prompt, kidx_prompt, s5_re_prompt, s5_im_prompt,
            k_sample, v_sample, kidx_sample, s5_re_sample, s5_im_sample)
```

```python
import functools
import math

import jax
import jax.numpy as jnp
from jax import lax
from jax.experimental import pallas as pl
from jax.experimental.pallas import tpu as pltpu

D_MODEL = 2048
PAGE_SIZE = 128
N_HEADS = 16
HEAD_DIM = 128
N_KV_HEADS = 4
HEADS_PER_KV = N_HEADS // N_KV_HEADS
ATTN_WIDTH = N_HEADS * HEAD_DIM
KV_WIDTH = N_KV_HEADS * HEAD_DIM
IDX_HEADS = 16
IDX_DIM = 64
TOPK_MAX = 256
S5_WIDTH = D_MODEL // 2
S5_GROUP = 16
S5_GROUPS = S5_WIDTH // S5_GROUP
S5_STATE = 64
ROPE_THETA = 10000.0
EPS = 1e-6
IN_SIZES = (ATTN_WIDTH, KV_WIDTH, KV_WIDTH, IDX_HEADS * IDX_DIM, IDX_DIM, IDX_HEADS,
            ATTN_WIDTH, S5_WIDTH, S5_WIDTH, D_MODEL, D_MODEL)

LANES = 128
SUBLANES = 8
PACKED_SUBLANES = 16
VMEM_LIMIT_BYTES = 56 * 1024 * 1024
MM_ROW_TILE = 1024
MM_COL_TILE = 1024
ROW_TILE = 512
REPACK_COLS = 256
S5_CHUNK = 512

S5_SG = S5_WIDTH // LANES
S5_SG_GROUPS = S5_GROUPS // S5_SG
S5_SG_STATE = S5_SG_GROUPS * S5_STATE
S5_SCAN_UNROLL = 8

SAMPLE_ROWS = 16
IDX_ROWS = SUBLANES

ATTN_Q_TILE = 256
ATTN_ROW_BLOCK = 64
COUNT_ROWS = 32
PAGES_PER_STEP = 32
IDX_PAGES_PER_STEP = 64
LOG2_E = math.log2(math.e)

INT_MIN = -2 ** 31
MASK_BIAS = -1e30

f32 = jnp.float32
bf16 = jnp.bfloat16
i32 = jnp.int32


def _cparams(sem):
    return pltpu.CompilerParams(dimension_semantics=sem, vmem_limit_bytes=VMEM_LIMIT_BYTES)


def _row_tile(m, cap):
    return m if m <= cap else cap


def _norm_body(x_ref, g_ref, o_ref):
    x = x_ref[...]
    ms = jnp.mean(x * x, axis=-1, keepdims=True)
    o_ref[...] = (x * lax.rsqrt(ms + EPS) * g_ref[...]).astype(o_ref.dtype)


def _layer_spec(block, layer, index_map):
    return pl.BlockSpec((None,) + tuple(block), lambda *idx: (layer,) + tuple(index_map(*idx)))


def _rmsnorm(x, gains, layer):
    m, d = x.shape
    tm = _row_tile(m, ROW_TILE)
    return pl.pallas_call(
        _norm_body,
        grid=(m // tm,),
        in_specs=[pl.BlockSpec((tm, d), lambda i: (i, 0)),
                  _layer_spec((1, d), layer, lambda i: (0, 0))],
        out_specs=pl.BlockSpec((tm, d), lambda i: (i, 0)),
        out_shape=jax.ShapeDtypeStruct((m, d), bf16),
        compiler_params=_cparams(("parallel",)),
        name="rmsnorm",
    )(x, gains)


def _head_norm_rope(x, gain, cos, sin):
    ms = jnp.mean(x * x, axis=-1, keepdims=True)
    y = x * lax.rsqrt(ms + EPS) * gain
    return y * cos + pltpu.roll(y, HEAD_DIM // 2, 1) * sin


def _qt_body(wt_ref, h_ref, g_ref, cos_ref, sin_ref, o_ref, *, heads):
    acc = lax.dot_general(wt_ref[...], h_ref[...], (((1,), (1,)), ((), ())),
                          preferred_element_type=f32)
    c = cos_ref[...]
    s = sin_ref[...]
    g = g_ref[...]
    half = HEAD_DIM // 2
    scale = HEAD_DIM ** -0.5 * LOG2_E
    for hh in range(heads):
        r0 = hh * HEAD_DIM
        x = acc[r0:r0 + HEAD_DIM]
        ms = jnp.mean(x * x, axis=0, keepdims=True)
        y = x * lax.rsqrt(ms + EPS) * g
        x1 = y[:half]
        x2 = y[half:]
        o_ref[r0:r0 + half, :] = ((x1 * c - x2 * s) * scale).astype(o_ref.dtype)
        o_ref[r0 + half:r0 + HEAD_DIM, :] = ((x2 * c + x1 * s) * scale).astype(o_ref.dtype)


def _proj_qt(h, wt_q, gain, cos_ht, sin_ht, layer):
    m, d = h.shape
    tm = _row_tile(m, MM_ROW_TILE)
    tn = MM_COL_TILE
    return pl.pallas_call(
        functools.partial(_qt_body, heads=tn // HEAD_DIM),
        grid=(m // tm, ATTN_WIDTH // tn),
        in_specs=[_layer_spec((tn, d), layer, lambda i, j: (j, 0)),
                  pl.BlockSpec((tm, d), lambda i, j: (i, 0)),
                  _layer_spec((HEAD_DIM, 1), layer, lambda i, j: (0, 0)),
                  pl.BlockSpec((HEAD_DIM // 2, tm), lambda i, j: (0, i)),
                  pl.BlockSpec((HEAD_DIM // 2, tm), lambda i, j: (0, i))],
        out_specs=pl.BlockSpec((tn, tm), lambda i, j: (j, i)),
        out_shape=jax.ShapeDtypeStruct((ATTN_WIDTH, m), bf16),
        compiler_params=_cparams(("parallel", "parallel")),
        name="proj_q_t",
    )(wt_q, h, gain, cos_ht, sin_ht)


def _dot_nt(a, b_t):
    return lax.dot_general(a, b_t, (((1,), (1,)), ((), ())), preferred_element_type=f32)


def _kv_body(h_ref, wt_ref, g_ref, cos_ref, sin_ref, k_ref, v_ref, kb_ref, vt_ref, *, chunk):
    h = h_ref[...]
    acc = _dot_nt(h, wt_ref[...])
    cos = cos_ref[...]
    sin = sin_ref[...]
    g = g_ref[...]
    tm = h.shape[0]
    for hh in range(N_KV_HEADS):
        sl = slice(hh * HEAD_DIM, (hh + 1) * HEAD_DIM)
        kh = _head_norm_rope(acc[:, sl], g, cos, sin)
        rows = pl.ds(hh, tm, stride=N_KV_HEADS)
        k_ref[rows, :] = kh
        v_ref[rows, :] = acc[:, KV_WIDTH + hh * HEAD_DIM:KV_WIDTH + (hh + 1) * HEAD_DIM]
        kb_ref[:, sl] = kh.astype(bf16)
    vt = _dot_nt(wt_ref[KV_WIDTH:, :], h)
    for cc in range(vt_ref.shape[0]):
        vt_ref[cc] = vt[:, cc * chunk:(cc + 1) * chunk].astype(bf16)


def _proj_kv(h, wt_kv, gain, cos, sin, layer):
    m, d = h.shape
    tm = _row_tile(m, MM_ROW_TILE)
    chunk = min(tm, ATTN_Q_TILE)
    row = lambda i: (i, 0)
    fix = lambda i: (0, 0)
    return pl.pallas_call(
        functools.partial(_kv_body, chunk=chunk),
        grid=(m // tm,),
        in_specs=[pl.BlockSpec((tm, d), row),
                  _layer_spec((2 * KV_WIDTH, d), layer, fix),
                  _layer_spec((1, HEAD_DIM), layer, fix),
                  pl.BlockSpec((tm, HEAD_DIM), row),
                  pl.BlockSpec((tm, HEAD_DIM), row)],
        out_specs=[pl.BlockSpec((tm * N_KV_HEADS, HEAD_DIM), row),
                   pl.BlockSpec((tm * N_KV_HEADS, HEAD_DIM), row),
                   pl.BlockSpec((tm, KV_WIDTH), row),
                   pl.BlockSpec((tm // chunk, KV_WIDTH, chunk), lambda i: (i, 0, 0))],
        out_shape=[jax.ShapeDtypeStruct((m * N_KV_HEADS, HEAD_DIM), f32),
                   jax.ShapeDtypeStruct((m * N_KV_HEADS, HEAD_DIM), f32),
                   jax.ShapeDtypeStruct((m, KV_WIDTH), bf16),
                   jax.ShapeDtypeStruct((m // chunk, KV_WIDTH, chunk), bf16)],
        compiler_params=_cparams(("parallel",)),
        name="proj_kv",
    )(h, wt_kv, gain, cos, sin)


def _qit_body(wt_ref, h_ref, cos_ref, sin_ref, qit_ref, wt_out_ref):
    acc = lax.dot_general(wt_ref[...], h_ref[...], (((1,), (1,)), ((), ())),
                          preferred_element_type=f32)
    c = cos_ref[...]
    s = sin_ref[...]
    half = IDX_DIM // 2
    scale = IDX_DIM ** -0.5
    for hh in range(IDX_HEADS):
        r0 = hh * IDX_DIM
        x1 = acc[r0:r0 + half]
        x2 = acc[r0 + half:r0 + IDX_DIM]
        qit_ref[r0:r0 + half, :] = ((x1 * c - x2 * s) * scale).astype(qit_ref.dtype)
        qit_ref[r0 + half:r0 + IDX_DIM, :] = ((x2 * c + x1 * s) * scale).astype(qit_ref.dtype)
    n_qi = IDX_HEADS * IDX_DIM
    wt_out_ref[...] = acc[n_qi:n_qi + IDX_HEADS] * (IDX_HEADS ** -0.5)


def _proj_qit(h, wt_qi, cos_t, sin_t, layer):
    m, d = h.shape
    tm = _row_tile(m, ROW_TILE)
    n_rows = wt_qi.shape[1]
    n_qi = IDX_HEADS * IDX_DIM
    return pl.pallas_call(
        _qit_body,
        grid=(m // tm,),
        in_specs=[_layer_spec((n_rows, d), layer, lambda i: (0, 0)),
                  pl.BlockSpec((tm, d), lambda i: (i, 0)),
                  pl.BlockSpec((IDX_DIM // 2, tm), lambda i: (0, i)),
                  pl.BlockSpec((IDX_DIM // 2, tm), lambda i: (0, i))],
        out_specs=[pl.BlockSpec((n_qi, tm), lambda i: (0, i)),
                   pl.BlockSpec((IDX_HEADS, tm), lambda i: (0, i))],
        out_shape=[jax.ShapeDtypeStruct((n_qi, m), bf16),
                   jax.ShapeDtypeStruct((IDX_HEADS, m), f32)],
        compiler_params=_cparams(("parallel",)),
        name="proj_qi_t",
    )(wt_qi, h, cos_t, sin_t)


def _ki_body(h_ref, w_ref, cos_ref, sina_ref, sinb_ref, ki_ref, kib_ref):
    x = _dot_nt(h_ref[...], w_ref[...])
    half = IDX_DIM // 2
    r = (x * cos_ref[...] + pltpu.roll(x, LANES - half, 1) * sina_ref[...]
         + pltpu.roll(x, half, 1) * sinb_ref[...])
    ki = r[:, :IDX_DIM]
    ki_ref[...] = ki
    kib_ref[...] = ki.astype(bf16)


def _proj_ki(h, w_ki, cos_k, sin_a, sin_b, layer):
    m, d = h.shape
    tm = _row_tile(m, ROW_TILE)
    row = lambda i: (i, 0)
    return pl.pallas_call(
        _ki_body,
        grid=(m // tm,),
        in_specs=[pl.BlockSpec((tm, d), row),
                  _layer_spec((LANES, d), layer, lambda i: (0, 0)),
                  pl.BlockSpec((tm, LANES), row),
                  pl.BlockSpec((tm, LANES), row),
                  pl.BlockSpec((tm, LANES), row)],
        out_specs=[pl.BlockSpec((tm, IDX_DIM), row)] * 2,
        out_shape=[jax.ShapeDtypeStruct((m, IDX_DIM), f32),
                   jax.ShapeDtypeStruct((m, IDX_DIM), bf16)],
        compiler_params=_cparams(("parallel",)),
        name="proj_ki",
    )(h, w_ki, cos_k, sin_a, sin_b)


def _plain_body(h_ref, w_ref, o_ref):
    o_ref[...] = _dot_nt(h_ref[...], w_ref[...]).astype(o_ref.dtype)


def _proj_plain(h, wt, out_dtype, layer):
    m, d = h.shape
    n = wt.shape[1]
    tm = _row_tile(m, MM_ROW_TILE)
    tn = MM_COL_TILE
    return pl.pallas_call(
        _plain_body,
        grid=(m // tm, n // tn),
        in_specs=[pl.BlockSpec((tm, d), lambda i, j: (i, 0)),
                  _layer_spec((tn, d), layer, lambda i, j: (j, 0))],
        out_specs=pl.BlockSpec((tm, tn), lambda i, j: (i, j)),
        out_shape=jax.ShapeDtypeStruct((m, n), out_dtype),
        compiler_params=_cparams(("parallel", "parallel")),
        name="proj_u",
    )(h, wt)


def _sigmoid(x):
    return 1.0 / (1.0 + jnp.exp(-x))


def _zg_body(h_ref, w_ref, o_ref, *, silu_tiles):
    acc = _dot_nt(h_ref[...], w_ref[...])
    sg = _sigmoid(acc)
    is_silu = pl.program_id(1) < silu_tiles
    o_ref[...] = jnp.where(is_silu, acc * sg, sg).astype(o_ref.dtype)


def _proj_zg(h, wt_zg, layer):
    m, d = h.shape
    n = wt_zg.shape[1]
    tm = _row_tile(m, MM_ROW_TILE)
    tn = MM_COL_TILE
    return pl.pallas_call(
        functools.partial(_zg_body, silu_tiles=(ATTN_WIDTH + S5_WIDTH) // tn),
        grid=(m // tm, n // tn),
        in_specs=[pl.BlockSpec((tm, d), lambda i, j: (i, 0)),
                  _layer_spec((tn, d), layer, lambda i, j: (j, 0))],
        out_specs=pl.BlockSpec((tm, tn), lambda i, j: (i, j)),
        out_shape=jax.ShapeDtypeStruct((m, n), bf16),
        compiler_params=_cparams(("parallel", "parallel")),
        name="proj_zg",
    )(h, wt_zg)


def _sortable_key(score):
    b = pltpu.bitcast(score + 0.0, i32)
    return b ^ ((b >> 31) & jnp.int32(0x7FFFFFFF))


def _kth_largest(count_ge, shape, k):
    zero = jnp.zeros(shape, i32)
    prefix = jnp.where(count_ge(zero) >= k, zero, jnp.full(shape, INT_MIN, i32))

    def bit_body(bi, prefix):
        cand = prefix | (jnp.int32(1) << (30 - bi))
        return jnp.where(count_ge(cand) >= k, cand, prefix)

    thr = lax.fori_loop(0, 31, bit_body, prefix)
    return jnp.maximum(thr, jnp.int32(INT_MIN + 1))


def _resolve_ties(thr, k, count_ge, count_tied_before, demote, n_pos_bits, active=None):
    excess = count_ge(thr) - k
    if active is not None:
        excess = jnp.where(active, excess, 0)

    @pl.when(jnp.max(excess) > 0)
    def _():
        need = k - count_ge(thr + 1)

        def bit_body(bi, pos):
            cand = pos | (jnp.int32(1) << (n_pos_bits - 1 - bi))
            return jnp.where(count_tied_before(thr, cand) < need, cand, pos)

        last = lax.fori_loop(0, n_pos_bits, bit_body, jnp.zeros(thr.shape, i32))
        demote(thr, last)


def _sublane_allreduce(x, op):
    for shift in (4, 2, 1):
        x = op(x, pltpu.roll(x, shift, 0))
    return x


def _attn_prompt_body(qt_ref, qit_ref, wt_ref, z_ref, k_ref, vt_ref, ki_ref, o_ref,
                      key_scr, acc_scr, m_scr, l_scr, alpha_scr, s_scr, p_scr, bias_scr,
                      *, tq, k_top, n_pos_bits):
    i = pl.program_id(1)
    n_chunks = i + 1
    half = tq // 2

    def score_keys(c, diagonal):
        for sub in range(2):
            r0 = pl.multiple_of(c * tq + sub * half, half)
            kic = ki_ref[pl.ds(r0, half), :]
            acc = jnp.zeros((half, tq), f32)
            for hh in range(IDX_HEADS):
                lg = jnp.dot(kic, qit_ref[hh * IDX_DIM:(hh + 1) * IDX_DIM, :],
                             preferred_element_type=f32)
                acc = acc + jnp.maximum(lg, 0.0) * wt_ref[hh:hh + 1, :]
            key = _sortable_key(acc)
            if diagonal:
                kpos = lax.broadcasted_iota(i32, (half, tq), 0) + sub * half
                qpos = lax.broadcasted_iota(i32, (half, tq), 1)
                key = jnp.where(kpos <= qpos, key, jnp.int32(INT_MIN))
            key_scr[pl.ds(r0, half), :] = key

    def full_chunk(c, carry):
        score_keys(c, False)
        return carry

    lax.fori_loop(0, i, full_chunk, 0)
    score_keys(i, True)

    @pl.when(n_chunks % 2 == 1)
    def _():
        key_scr[pl.ds(pl.multiple_of(n_chunks * tq, tq), tq), :] = jnp.full((tq, tq), INT_MIN, i32)

    def count_ge(cand):
        def body(c, cnt):
            kk = key_scr[pl.ds(pl.multiple_of(c * 2 * tq, 2 * tq), 2 * tq), :]
            hit = (kk >= cand).astype(i32)
            return cnt + jnp.sum(hit.reshape(2 * tq // COUNT_ROWS, COUNT_ROWS, tq), axis=0)

        cnt = lax.fori_loop(0, (n_chunks + 1) // 2, body, jnp.zeros((COUNT_ROWS, tq), i32))
        return jnp.sum(cnt, axis=0, keepdims=True)

    thr = _kth_largest(count_ge, (1, tq), k_top)

    def chunk_positions(c):
        return lax.broadcasted_iota(i32, (tq, tq), 0) + c * tq

    def count_tied_before(thr, pos_limit):
        def body(c, cnt):
            kk = key_scr[pl.ds(pl.multiple_of(c * tq, tq), tq), :]
            hit = ((kk == thr) & (chunk_positions(c) < pos_limit)).astype(i32)
            return cnt + jnp.sum(hit.reshape(tq // SUBLANES, SUBLANES, tq), axis=0)

        cnt = lax.fori_loop(0, n_chunks, body, jnp.zeros((SUBLANES, tq), i32))
        return jnp.sum(cnt, axis=0, keepdims=True)

    def demote(thr, last):
        def body(c, carry):
            rows = pl.ds(pl.multiple_of(c * tq, tq), tq)
            kk = key_scr[rows, :]
            key_scr[rows, :] = jnp.where((kk == thr) & (chunk_positions(c) > last), thr - 1, kk)
            return carry

        lax.fori_loop(0, n_chunks, body, 0)

    _resolve_ties(thr, k_top, count_ge, count_tied_before, demote, n_pos_bits)

    m_scr[...] = jnp.full(m_scr.shape, -jnp.inf, f32)
    l_scr[...] = jnp.zeros(l_scr.shape, f32)
    acc_scr[...] = jnp.zeros(acc_scr.shape, f32)
    n_sub = tq // SUBLANES
    d_sub = HEAD_DIM // SUBLANES

    def attend_chunk(c, carry):
        r0 = pl.multiple_of(c * tq, tq)
        bias_scr[...] = jnp.where(key_scr[pl.ds(r0, tq), :] >= thr, 0.0, MASK_BIAS)
        for g in range(N_KV_HEADS):
            kc = k_ref[pl.ds(r0, tq), g * HEAD_DIM:(g + 1) * HEAD_DIM]
            for hh in range(HEADS_PER_KV):
                h = g * HEADS_PER_KV + hh
                s = jnp.dot(kc, qt_ref[h * HEAD_DIM:(h + 1) * HEAD_DIM, :], preferred_element_type=f32)
                s_scr[h] = s + bias_scr[...]
        blocks = [slice(j * ATTN_ROW_BLOCK, (j + 1) * ATTN_ROW_BLOCK) for j in range(tq // ATTN_ROW_BLOCK)]
        blk_sub = ATTN_ROW_BLOCK // SUBLANES
        for h in range(N_HEADS):
            mx = m_scr[h]
            for rows in blocks:
                mx = jnp.maximum(mx, jnp.max(s_scr[h, rows, :].reshape(blk_sub, SUBLANES, tq), axis=0))
            m_new = _sublane_allreduce(mx, jnp.maximum)
            alpha_scr[h] = jnp.exp2(m_scr[h] - m_new)
            m_scr[h] = m_new
        for h in range(N_HEADS):
            p3 = jnp.exp2(s_scr[h].reshape(n_sub, SUBLANES, tq) - m_scr[h][None])
            p_scr[h] = p3.reshape(tq, tq).astype(bf16)
        ones_rows = jnp.ones((PACKED_SUBLANES, tq), bf16)
        for g in range(N_KV_HEADS):
            vtc = jnp.concatenate([vt_ref[c, g * HEAD_DIM:(g + 1) * HEAD_DIM, :], ones_rows], axis=0)
            for hh in range(HEADS_PER_KV):
                h = g * HEADS_PER_KV + hh
                hs = slice(h * HEAD_DIM, (h + 1) * HEAD_DIM)
                pv = jnp.dot(vtc, p_scr[h], preferred_element_type=f32)
                alpha = alpha_scr[h]
                acc = acc_scr[hs, :].reshape(d_sub, SUBLANES, tq) * alpha[None]
                acc_scr[hs, :] = acc.reshape(HEAD_DIM, tq) + pv[:HEAD_DIM]
                l_scr[h] = alpha * l_scr[h] + pv[HEAD_DIM:HEAD_DIM + SUBLANES]
        return carry

    lax.fori_loop(0, n_chunks, attend_chunk, 0)

    for h in range(N_HEADS):
        hs = slice(h * HEAD_DIM, (h + 1) * HEAD_DIM)
        out_t = (acc_scr[hs, :].reshape(d_sub, SUBLANES, tq) / l_scr[h][None]).reshape(HEAD_DIM, tq)
        o_ref[:, hs] = (out_t.T * z_ref[:, hs].astype(f32)).astype(o_ref.dtype)


def _attn_prompt(qt, qit, wt, zg, kb, vt, kib, batch, seq):
    tq = ATTN_Q_TILE
    assert seq % tq == 0
    nq = seq // tq
    k_top = min(TOPK_MAX, seq // 4)
    assert k_top <= tq
    qrow = lambda b, i: (b * nq + i, 0)
    qcol = lambda b, i: (0, b * nq + i)
    per_b = lambda b, i: (b, 0)
    return pl.pallas_call(
        functools.partial(_attn_prompt_body, tq=tq, k_top=k_top, n_pos_bits=(seq - 1).bit_length()),
        grid=(batch, nq),
        in_specs=[pl.BlockSpec((ATTN_WIDTH, tq), qcol),
                  pl.BlockSpec((IDX_HEADS * IDX_DIM, tq), qcol),
                  pl.BlockSpec((IDX_HEADS, tq), qcol),
                  pl.BlockSpec((tq, ATTN_WIDTH), qrow),
                  pl.BlockSpec((seq, KV_WIDTH), per_b),
                  pl.BlockSpec((nq, KV_WIDTH, tq), lambda b, i: (b, 0, 0)),
                  pl.BlockSpec((seq, IDX_DIM), per_b)],
        out_specs=pl.BlockSpec((tq, ATTN_WIDTH), qrow),
        out_shape=jax.ShapeDtypeStruct((batch * seq, ATTN_WIDTH), bf16),
        scratch_shapes=[pltpu.VMEM((seq + tq, tq), i32),
                        pltpu.VMEM((ATTN_WIDTH, tq), f32),
                        pltpu.VMEM((N_HEADS, SUBLANES, tq), f32),
                        pltpu.VMEM((N_HEADS, SUBLANES, tq), f32),
                        pltpu.VMEM((N_HEADS, SUBLANES, tq), f32),
                        pltpu.VMEM((N_HEADS, tq, tq), f32),
                        pltpu.VMEM((N_HEADS, tq, tq), bf16),
                        pltpu.VMEM((tq, tq), f32)],
        compiler_params=_cparams(("parallel", "arbitrary")),
        name="attn_prompt",
    )(qt, qit, wt, zg, kb, vt, kib)


def _idx_sample_body(pt_ref, qi_ref, w_ref, *rest, n_pages, n_new, k_top, ppb):
    kid_refs = rest[:ppb]
    kinew_ref, key_ref, thr_ref = rest[ppb:]
    p = pl.program_id(1)
    tok = IDX_ROWS

    def page_keys(ki_page_t):
        lg = jnp.dot(qi_ref[0], ki_page_t.astype(bf16), preferred_element_type=f32)
        wgt = jnp.maximum(lg, 0.0) * w_ref[0]
        score = jnp.sum(wgt.reshape(IDX_HEADS, tok, PAGE_SIZE), axis=0)
        return _sortable_key(score)

    for j in range(ppb):
        key_ref[0, p * ppb + j] = page_keys(kid_refs[j][...])

    @pl.when(p == pl.num_programs(1) - 1)
    def _():
        key = page_keys(kinew_ref[0])
        kpos = lax.broadcasted_iota(i32, (tok, PAGE_SIZE), 1)
        qtok = lax.broadcasted_iota(i32, (tok, PAGE_SIZE), 0)
        key_ref[0, n_pages] = jnp.where((kpos <= qtok) & (kpos < n_new), key, jnp.int32(INT_MIN))

        def count_ge(cand):
            def body(c, cnt):
                hit = (key_ref[0, pl.ds(c * ppb, ppb)] >= cand).astype(i32)
                return cnt + jnp.sum(hit, axis=0)

            cnt = lax.fori_loop(0, n_pages // ppb, body, jnp.zeros((tok, PAGE_SIZE), i32))
            cnt = cnt + (key_ref[0, n_pages] >= cand).astype(i32)
            return jnp.sum(cnt, axis=-1, keepdims=True)

        thr = _kth_largest(count_ge, (tok, 1), k_top)
        thr_ref[0] = jnp.broadcast_to(thr, (tok, PAGE_SIZE))

        def page_positions(c):
            return lax.broadcasted_iota(i32, (tok, PAGE_SIZE), 1) + c * PAGE_SIZE

        def count_tied_before(thr, pos_limit):
            def body(c, cnt):
                return cnt + ((key_ref[0, c] == thr) & (page_positions(c) < pos_limit)).astype(i32)

            cnt = lax.fori_loop(0, n_pages + 1, body, jnp.zeros((tok, PAGE_SIZE), i32))
            return jnp.sum(cnt, axis=-1, keepdims=True)

        def demote(thr, last):
            def body(c, carry):
                kk = key_ref[0, c]
                key_ref[0, c] = jnp.where((kk == thr) & (page_positions(c) > last), thr - 1, kk)
                return carry

            lax.fori_loop(0, n_pages + 1, body, 0)

        real_rows = lax.broadcasted_iota(i32, (tok, 1), 0) < n_new
        _resolve_ties(thr, k_top, count_ge, count_tied_before, demote,
                      ((n_pages + 1) * PAGE_SIZE - 1).bit_length(), active=real_rows)


def _pages_per_step(n_pages, cap=PAGES_PER_STEP):
    ppb = min(cap, n_pages)
    assert n_pages % ppb == 0
    return ppb


def _page_spec(block, layer, ppb, j):
    return pl.BlockSpec(block, lambda b, p, pt: (layer, pt[b, p * ppb + j], 0, 0))


def _idx_sample(page_table, qi_rows, w_rows, cache_kidx, layer, ki_new_pages, n_new):
    nb, n_pages = page_table.shape
    tok = IDX_ROWS
    assert n_new <= tok
    ppb = _pages_per_step(n_pages, IDX_PAGES_PER_STEP)
    k_top = min(TOPK_MAX, (n_pages * PAGE_SIZE + n_new) // 4)
    per_b = lambda b, p, pt: (b, 0, 0)
    grid_spec = pltpu.PrefetchScalarGridSpec(
        num_scalar_prefetch=1,
        grid=(nb, n_pages // ppb),
        in_specs=[pl.BlockSpec((1, IDX_HEADS * tok, IDX_DIM), per_b),
                  pl.BlockSpec((1, IDX_HEADS * tok, PAGE_SIZE), per_b)]
                 + [_page_spec((None, None, IDX_DIM, PAGE_SIZE), layer, ppb, j) for j in range(ppb)]
                 + [pl.BlockSpec((1, IDX_DIM, PAGE_SIZE), per_b)],
        out_specs=[pl.BlockSpec((1, n_pages + 1, tok, PAGE_SIZE), lambda b, p, pt: (b, 0, 0, 0)),
                   pl.BlockSpec((1, tok, PAGE_SIZE), per_b)],
    )
    return pl.pallas_call(
        functools.partial(_idx_sample_body, n_pages=n_pages, n_new=n_new, k_top=k_top, ppb=ppb),
        grid_spec=grid_spec,
        out_shape=[jax.ShapeDtypeStruct((nb, n_pages + 1, tok, PAGE_SIZE), i32),
                   jax.ShapeDtypeStruct((nb, tok, PAGE_SIZE), i32)],
        compiler_params=_cparams(("parallel", "arbitrary")),
        name="idx_sample",
    )(page_table, qi_rows, w_rows, *([cache_kidx] * ppb), ki_new_pages)


def _attn_sample_body(pt_ref, q_ref, key_ref, keyn_ref, thr_ref, *rest, ppb):
    k_refs = rest[:ppb]
    v_refs = rest[ppb:2 * ppb]
    kn_ref, vn_ref, o_ref, acc_scr, m_scr, l_scr = rest[2 * ppb:]
    p = pl.program_id(1)
    tok = SAMPLE_ROWS
    rows_g = HEADS_PER_KV * tok
    thr = thr_ref[0]

    @pl.when(p == 0)
    def _():
        m_scr[...] = jnp.full(m_scr.shape, -jnp.inf, f32)
        l_scr[...] = jnp.zeros(l_scr.shape, f32)
        acc_scr[...] = jnp.zeros(acc_scr.shape, f32)

    def attend(bias_tok, k_of_group, v_of_group):
        bias = jnp.concatenate([bias_tok] * (rows_g // IDX_ROWS), axis=0)
        for g in range(N_KV_HEADS):
            rs = slice(g * rows_g, (g + 1) * rows_g)
            s = lax.dot_general(q_ref[0, rs, :], k_of_group(g), (((1,), (1,)), ((), ())),
                                preferred_element_type=f32) + bias
            m_prev = m_scr[rs, :]
            m_new = jnp.maximum(m_prev, jnp.max(s, axis=-1, keepdims=True))
            alpha = jnp.exp2(m_prev - m_new)
            pr = jnp.exp2(s - m_new)
            l_scr[rs, :] = alpha * l_scr[rs, :] + jnp.sum(pr, axis=-1, keepdims=True)
            acc_scr[rs, :] = alpha * acc_scr[rs, :] + jnp.dot(
                pr.astype(bf16), v_of_group(g), preferred_element_type=f32)
            m_scr[rs, :] = m_new

    def mask_bias(keys):
        return jnp.where(keys >= thr, 0.0, MASK_BIAS).astype(f32)

    def cached(refs):
        def of_group(g):
            rows = pl.ds(g, PAGE_SIZE, stride=N_KV_HEADS)
            return jnp.concatenate([r[rows, :].astype(bf16) for r in refs], axis=0)
        return of_group

    attend(jnp.concatenate([mask_bias(key_ref[0, j]) for j in range(ppb)], axis=1),
           cached(k_refs), cached(v_refs))

    @pl.when(p == pl.num_programs(1) - 1)
    def _():
        def fresh(ref):
            return lambda g: ref[0, :, g * HEAD_DIM:(g + 1) * HEAD_DIM].astype(bf16)

        attend(mask_bias(keyn_ref[0, 0]), fresh(kn_ref), fresh(vn_ref))
        o_ref[0] = acc_scr[...] / l_scr[...]


def _attn_sample(page_table, q_rows, keys, thr, cache_k, cache_v, layer, k_new_pages, v_new_pages):
    nb, n_pages = page_table.shape
    tok = SAMPLE_ROWS
    rows = N_HEADS * tok
    ppb = _pages_per_step(n_pages)
    per_b = lambda b, p, pt: (b, 0, 0)
    page_block = (None, None, PAGE_SIZE * N_KV_HEADS, HEAD_DIM)
    grid_spec = pltpu.PrefetchScalarGridSpec(
        num_scalar_prefetch=1,
        grid=(nb, n_pages // ppb),
        in_specs=[pl.BlockSpec((1, rows, HEAD_DIM), per_b),
                  pl.BlockSpec((1, ppb, IDX_ROWS, PAGE_SIZE), lambda b, p, pt: (b, p, 0, 0)),
                  pl.BlockSpec((1, 1, IDX_ROWS, PAGE_SIZE), lambda b, p, pt: (b, n_pages, 0, 0)),
                  pl.BlockSpec((1, IDX_ROWS, PAGE_SIZE), per_b)]
                 + [_page_spec(page_block, layer, ppb, j) for j in range(ppb)]
                 + [_page_spec(page_block, layer, ppb, j) for j in range(ppb)]
                 + [pl.BlockSpec((1, PAGE_SIZE, KV_WIDTH), per_b),
                    pl.BlockSpec((1, PAGE_SIZE, KV_WIDTH), per_b)],
        out_specs=pl.BlockSpec((1, rows, HEAD_DIM), per_b),
        scratch_shapes=[pltpu.VMEM((rows, HEAD_DIM), f32),
                        pltpu.VMEM((rows, 1), f32),
                        pltpu.VMEM((rows, 1), f32)],
    )
    return pl.pallas_call(
        functools.partial(_attn_sample_body, ppb=ppb),
        grid_spec=grid_spec,
        out_shape=jax.ShapeDtypeStruct((nb, rows, HEAD_DIM), f32),
        compiler_params=_cparams(("parallel", "arbitrary")),
        name="attn_sample",
    )(page_table, q_rows, keys, keys, thr, *([cache_k] * ppb), *([cache_v] * ppb),
      k_new_pages, v_new_pages)


def _gate_mul_body(a_ref, z_ref, o_ref):
    o_ref[...] = (a_ref[...] * z_ref[...].astype(f32)).astype(o_ref.dtype)


def _gate_mul(a, zg):
    m, n = a.shape
    return pl.pallas_call(
        _gate_mul_body,
        grid=(1,),
        in_specs=[pl.BlockSpec((m, n), lambda i: (0, 0)),
                  pl.BlockSpec((m, n), lambda i: (0, 0))],
        out_specs=pl.BlockSpec((m, n), lambda i: (0, 0)),
        out_shape=jax.ShapeDtypeStruct((m, n), bf16),
        compiler_params=_cparams(("arbitrary",)),
        name="gate_mul",
    )(a, zg)


def _s5_param_body(lr_ref, li_ref, ldt_ref, bre_ref, bim_ref, ar_ref, ai_ref, bbre_ref, bbim_ref):
    lr = lr_ref[...]
    li = li_ref[...]
    dt = jnp.exp(ldt_ref[...])
    mag = jnp.exp(lr * dt)
    ar = mag * jnp.cos(li * dt)
    ai = mag * jnp.sin(li * dt)
    den = lr * lr + li * li
    nr = ar - 1.0
    cr = (nr * lr + ai * li) / den
    ci = (ai * lr - nr * li) / den
    ar_ref[...] = ar
    ai_ref[...] = ai
    for h in range(S5_GROUP):
        bbre_ref[h] = cr * bre_ref[h] - ci * bim_ref[h]
        bbim_ref[h] = cr * bim_ref[h] + ci * bre_ref[h]


def _s5_params(lam_re, lam_im, log_dt, b_re, b_im):
    depth = lam_re.shape[0]
    gs = (None, S5_GROUPS, S5_STATE)
    bs = (None, S5_GROUP, S5_GROUPS, S5_STATE)
    at3 = lambda l: (l, 0, 0)
    at4 = lambda l: (l, 0, 0, 0)
    return pl.pallas_call(
        _s5_param_body,
        grid=(depth,),
        in_specs=[pl.BlockSpec(gs, at3), pl.BlockSpec(gs, at3),
                  pl.BlockSpec((None, S5_GROUPS, 1), at3),
                  pl.BlockSpec(bs, at4), pl.BlockSpec(bs, at4)],
        out_specs=[pl.BlockSpec(gs, at3), pl.BlockSpec(gs, at3),
                   pl.BlockSpec(bs, at4), pl.BlockSpec(bs, at4)],
        out_shape=[jax.ShapeDtypeStruct((depth, S5_GROUPS, S5_STATE), f32),
                   jax.ShapeDtypeStruct((depth, S5_GROUPS, S5_STATE), f32),
                   jax.ShapeDtypeStruct((depth, S5_GROUP, S5_GROUPS, S5_STATE), f32),
                   jax.ShapeDtypeStruct((depth, S5_GROUP, S5_GROUPS, S5_STATE), f32)],
        compiler_params=_cparams(("arbitrary",)),
        name="s5_params",
    )(lam_re, lam_im, log_dt.reshape(depth, S5_GROUPS, 1), b_re, b_im)


def _gelu_tanh(x):
    return 0.5 * x * (1.0 + jnp.tanh(math.sqrt(2.0 / math.pi) * (x + 0.044715 * (x * x * x))))


def _s5_body(u_ref, zs_ref, h0r_ref, h0i_ref, ar_ref, ai_ref, wbu_ref, wcr_ref, wci_ref, d_ref, wglu_ref,
             o_ref, sr_ref, si_ref, xr_scr, xi_scr, y_scr, str_scr, sti_scr, *, tc, n_last):
    c = pl.program_id(1)

    @pl.when(c == 0)
    def _():
        str_scr[...] = h0r_ref[0]
        sti_scr[...] = h0i_ref[0]

    n_ct = S5_SG_STATE // LANES

    def gather_planes(scr, rows):
        return jnp.concatenate([scr[j, rows, :] for j in range(n_ct)], axis=1)

    def scatter_planes(scr, rows, val):
        for j in range(n_ct):
            scr[j, rows, :] = val[:, j * LANES:(j + 1) * LANES]

    for sg in range(S5_SG):
        rows = pl.ds(sg, tc, stride=S5_SG)
        u_sg = u_ref[:, sg * LANES:(sg + 1) * LANES]
        bu = jnp.dot(u_sg.astype(bf16), wbu_ref[sg], preferred_element_type=f32)
        scatter_planes(xr_scr, rows, bu[:, :S5_SG_STATE])
        scatter_planes(xi_scr, rows, bu[:, S5_SG_STATE:])

    ar = ar_ref[...]
    ai = ai_ref[...]

    def step(t, carry):
        xr, xi = carry
        r = pl.ds(pl.multiple_of(t * S5_SG, S5_SG), S5_SG)
        nr = ar * xr - ai * xi + gather_planes(xr_scr, r)
        ni = ar * xi + ai * xr + gather_planes(xi_scr, r)
        scatter_planes(xr_scr, r, nr)
        scatter_planes(xi_scr, r, ni)
        return nr, ni

    xr, xi = lax.fori_loop(0, tc, step, (str_scr[...], sti_scr[...]), unroll=S5_SCAN_UNROLL)
    str_scr[...] = xr
    sti_scr[...] = xi

    for sg in range(S5_SG):
        rows = pl.ds(sg, tc, stride=S5_SG)
        cs = slice(sg * LANES, (sg + 1) * LANES)
        y = (jnp.dot(gather_planes(xr_scr, rows).astype(bf16), wcr_ref[sg], preferred_element_type=f32)
             - jnp.dot(gather_planes(xi_scr, rows).astype(bf16), wci_ref[sg], preferred_element_type=f32)
             + d_ref[:, cs] * u_ref[:, cs])
        y_scr[:, cs] = _gelu_tanh(y)

    y = y_scr[...]
    gate = _sigmoid(jnp.dot(y.astype(bf16), wglu_ref[...], preferred_element_type=f32))
    o_ref[...] = (y * gate * zs_ref[...].astype(f32)).astype(o_ref.dtype)

    @pl.when(c == pl.num_programs(1) - 1)
    def _():
        last = slice((n_last - 1) * S5_SG, n_last * S5_SG)
        sr_ref[0] = gather_planes(xr_scr, last)
        si_ref[0] = gather_planes(xi_scr, last)


def _s5(u, zg, h0_re, h0_im, prm, layer, nb, seq, n_real):
    tc = min(seq, S5_CHUNK)
    nch = seq // tc
    n_last = n_real - (nch - 1) * tc
    zs_col = ATTN_WIDTH // S5_WIDTH
    fix2 = lambda b, c: (0, 0)
    fix3 = lambda b, c: (0, 0, 0)
    st = lambda b, c: (b, 0, 0)
    out, s_re, s_im = pl.pallas_call(
        functools.partial(_s5_body, tc=tc, n_last=n_last),
        grid=(nb, nch),
        in_specs=[pl.BlockSpec((tc, S5_WIDTH), lambda b, c: (b * nch + c, 0)),
                  pl.BlockSpec((tc, S5_WIDTH), lambda b, c: (b * nch + c, zs_col)),
                  pl.BlockSpec((1, S5_SG, S5_SG_STATE), st),
                  pl.BlockSpec((1, S5_SG, S5_SG_STATE), st),
                  _layer_spec((S5_SG, S5_SG_STATE), layer, fix2),
                  _layer_spec((S5_SG, S5_SG_STATE), layer, fix2),
                  _layer_spec((S5_SG, LANES, 2 * S5_SG_STATE), layer, fix3),
                  _layer_spec((S5_SG, S5_SG_STATE, LANES), layer, fix3),
                  _layer_spec((S5_SG, S5_SG_STATE, LANES), layer, fix3),
                  _layer_spec((1, S5_WIDTH), layer, fix2),
                  _layer_spec((S5_WIDTH, S5_WIDTH), layer, fix2)],
        out_specs=[pl.BlockSpec((tc, S5_WIDTH), lambda b, c: (b * nch + c, 0)),
                   pl.BlockSpec((1, S5_SG, S5_SG_STATE), st),
                   pl.BlockSpec((1, S5_SG, S5_SG_STATE), st)],
        out_shape=[jax.ShapeDtypeStruct((nb * seq, S5_WIDTH), bf16),
                   jax.ShapeDtypeStruct((nb, S5_SG, S5_SG_STATE), f32),
                   jax.ShapeDtypeStruct((nb, S5_SG, S5_SG_STATE), f32)],
        scratch_shapes=[pltpu.VMEM((S5_SG_STATE // LANES, tc * S5_SG, LANES), f32),
                        pltpu.VMEM((S5_SG_STATE // LANES, tc * S5_SG, LANES), f32),
                        pltpu.VMEM((tc, S5_WIDTH), f32),
                        pltpu.VMEM((S5_SG, S5_SG_STATE), f32),
                        pltpu.VMEM((S5_SG, S5_SG_STATE), f32)],
        compiler_params=_cparams(("parallel", "arbitrary")),
        name="s5_scan",
    )(u, zg, h0_re.reshape(nb, S5_SG, S5_SG_STATE), h0_im.reshape(nb, S5_SG, S5_SG_STATE),
      prm["a_re"], prm["a_im"], prm["w_bu"], prm["w_c_re"], prm["w_c_im"], prm["d"], prm["w_glu"])
    return out, s_re.reshape(nb, S5_GROUPS, S5_STATE), s_im.reshape(nb, S5_GROUPS, S5_STATE)


def _merge_body(a_ref, s_ref, wa_ref, ws_ref, ga_ref, gs_ref, o_ref):
    o_a = jnp.dot(a_ref[...], wa_ref[...], preferred_element_type=f32)
    o_s = jnp.dot(s_ref[...], ws_ref[...], preferred_element_type=f32)
    o_ref[...] = (ga_ref[...].astype(f32) * o_a + gs_ref[...].astype(f32) * o_s).astype(o_ref.dtype)


def _merge(a_in, s_in, w_a, w_s, zg, layer):
    m = a_in.shape[0]
    tm = _row_tile(m, MM_ROW_TILE)
    tn = MM_COL_TILE
    ga0 = (ATTN_WIDTH + S5_WIDTH) // tn
    gs0 = (ATTN_WIDTH + S5_WIDTH + D_MODEL) // tn
    return pl.pallas_call(
        _merge_body,
        grid=(m // tm, D_MODEL // tn),
        in_specs=[pl.BlockSpec((tm, ATTN_WIDTH), lambda i, j: (i, 0)),
                  pl.BlockSpec((tm, S5_WIDTH), lambda i, j: (i, 0)),
                  _layer_spec((ATTN_WIDTH, tn), layer, lambda i, j: (0, j)),
                  _layer_spec((S5_WIDTH, tn), layer, lambda i, j: (0, j)),
                  pl.BlockSpec((tm, tn), lambda i, j: (i, ga0 + j)),
                  pl.BlockSpec((tm, tn), lambda i, j: (i, gs0 + j))],
        out_specs=pl.BlockSpec((tm, tn), lambda i, j: (i, j)),
        out_shape=jax.ShapeDtypeStruct((m, D_MODEL), bf16),
        compiler_params=_cparams(("parallel", "parallel")),
        name="merge",
    )(a_in, s_in, w_a, w_s, zg, zg)


def _out_body(x_ref, m_ref, w_ref, o_ref):
    o_ref[...] = x_ref[...] + jnp.dot(m_ref[...], w_ref[...], preferred_element_type=f32)


def _out_proj(x, merged, w_out, layer):
    m = x.shape[0]
    tm = _row_tile(m, MM_ROW_TILE)
    tn = MM_COL_TILE
    return pl.pallas_call(
        _out_body,
        grid=(m // tm, D_MODEL // tn),
        in_specs=[pl.BlockSpec((tm, tn), lambda i, j: (i, j)),
                  pl.BlockSpec((tm, D_MODEL), lambda i, j: (i, 0)),
                  _layer_spec((D_MODEL, tn), layer, lambda i, j: (0, j))],
        out_specs=pl.BlockSpec((tm, tn), lambda i, j: (i, j)),
        out_shape=jax.ShapeDtypeStruct((m, D_MODEL), f32),
        compiler_params=_cparams(("parallel", "parallel")),
        name="out_proj",
    )(x, merged, w_out)


def _out_norm_body(x_ref, m_ref, w_ref, g_ref, o_ref, h_ref):
    y = x_ref[...] + jnp.dot(m_ref[...], w_ref[...], preferred_element_type=f32)
    o_ref[...] = y
    ms = jnp.mean(y * y, axis=-1, keepdims=True)
    h_ref[...] = (y * lax.rsqrt(ms + EPS) * g_ref[...]).astype(h_ref.dtype)


def _out_proj_norm(x, merged, w_out, gains, layer):
    m = x.shape[0]
    tm = _row_tile(m, ROW_TILE)
    row = lambda i: (i, 0)
    return pl.pallas_call(
        _out_norm_body,
        grid=(m // tm,),
        in_specs=[pl.BlockSpec((tm, D_MODEL), row),
                  pl.BlockSpec((tm, D_MODEL), row),
                  _layer_spec((D_MODEL, D_MODEL), layer, lambda i: (0, 0)),
                  _layer_spec((1, D_MODEL), layer + 1, lambda i: (0, 0))],
        out_specs=[pl.BlockSpec((tm, D_MODEL), row), pl.BlockSpec((tm, D_MODEL), row)],
        out_shape=[jax.ShapeDtypeStruct((m, D_MODEL), f32),
                   jax.ShapeDtypeStruct((m, D_MODEL), bf16)],
        compiler_params=_cparams(("parallel",)),
        name="out_proj_norm",
    )(x, merged, w_out, gains)


def _rope_tables(pos):
    posf = pos.astype(f32)[:, None]
    half = HEAD_DIM // 2
    inv = ROPE_THETA ** (-jnp.arange(half, dtype=f32) / half)
    ang = posf * inv[None, :]
    cos, sin = jnp.cos(ang), jnp.sin(ang)
    cos_h = jnp.concatenate([cos, cos], axis=1)
    sin_h = jnp.concatenate([-sin, sin], axis=1)
    half_i = IDX_DIM // 2
    inv_i = ROPE_THETA ** (-jnp.arange(half_i, dtype=f32) / half_i)
    ang_i = posf * inv_i[None, :]
    cos_i, sin_i = jnp.cos(ang_i), jnp.sin(ang_i)
    zeros = jnp.zeros_like(cos_i)
    pad = jnp.zeros((pos.shape[0], LANES - IDX_DIM), f32)
    cos_k = jnp.concatenate([cos_i, cos_i, pad], axis=1)
    sin_a = jnp.concatenate([-sin_i, zeros, pad], axis=1)
    sin_b = jnp.concatenate([zeros, sin_i, pad], axis=1)
    return dict(cos_h=cos_h, sin_h=sin_h, cos_k=cos_k, sin_a=sin_a, sin_b=sin_b,
                cos_ht=cos.T, sin_ht=sin.T, cos_t=cos_i.T, sin_t=sin_i.T)


def _in_proj_offsets():
    offs = [0]
    for s in IN_SIZES:
        offs.append(offs[-1] + s)
    return offs


def _repack_body(w_ref, wq_ref, wkv_ref, wqiw_ref, wki_ref, wu_ref, wzg_ref):
    offs = _in_proj_offsets()
    seg = lambda k: w_ref[offs[k]:offs[k + 1], :].astype(bf16)
    cols = w_ref.shape[1]
    wq_ref[...] = seg(0)
    wkv_ref[...] = w_ref[offs[1]:offs[3], :].astype(bf16)
    n_qi = IDX_HEADS * IDX_DIM
    wqiw_ref[:n_qi, :] = seg(3)
    wqiw_ref[n_qi:, :] = seg(5)
    wki_ref[:IDX_DIM, :] = seg(4)
    wki_ref[IDX_DIM:, :] = jnp.zeros((LANES - IDX_DIM, cols), bf16)
    wu_ref[...] = seg(7)
    row = 0
    for k in (6, 8, 9, 10):
        wzg_ref[row:row + IN_SIZES[k], :] = seg(k)
        row += IN_SIZES[k]


def _repack_in_proj(w_in_t):
    depth, n_in, d = w_in_t.shape
    cols = REPACK_COLS
    n_qiw = IDX_HEADS * IDX_DIM + IDX_HEADS
    n_zg = ATTN_WIDTH + S5_WIDTH + 2 * D_MODEL
    heights = (ATTN_WIDTH, 2 * KV_WIDTH, n_qiw, LANES, S5_WIDTH, n_zg)
    at = lambda l, c: (l, 0, c)
    return pl.pallas_call(
        _repack_body,
        grid=(depth, d // cols),
        in_specs=[pl.BlockSpec((None, n_in, cols), at)],
        out_specs=[pl.BlockSpec((None, hgt, cols), at) for hgt in heights],
        out_shape=[jax.ShapeDtypeStruct((depth, hgt, d), bf16) for hgt in heights],
        compiler_params=_cparams(("parallel", "parallel")),
        name="repack_in_proj",
    )(w_in_t)


def _pack_weights(w_in, w_glu, w_br_attn, w_br_s5, w_out, norm_gain, q_norm_gain, k_norm_gain):
    depth = w_in.shape[0]
    wt_q, wt_kv, wt_qi, wt_ki, wt_u, wt_zg = _repack_in_proj(jnp.swapaxes(w_in, 1, 2))
    return dict(
        wt_q=wt_q, wt_kv=wt_kv, wt_qi=wt_qi, wt_ki=wt_ki, wt_u=wt_u, wt_zg=wt_zg,
        w_glu=w_glu.astype(bf16), w_br_attn=w_br_attn.astype(bf16),
        w_br_s5=w_br_s5.astype(bf16), w_out=w_out.astype(bf16),
        norm_gain=norm_gain.reshape(depth, 1, D_MODEL),
        q_gain=q_norm_gain.reshape(depth, HEAD_DIM, 1),
        k_gain=k_norm_gain.reshape(depth, 1, HEAD_DIM))


def _block_diag(blocks):
    depth, sg, gg, r, c = blocks.shape
    eye = jnp.eye(gg, dtype=blocks.dtype)
    return jnp.einsum("lsgrc,gk->lsgrkc", blocks, eye).reshape(depth, sg, gg * r, gg * c)


def _s5_stacked_params(lam_re, lam_im, log_dt, b_re, b_im, c_re, c_im, d, w_glu_bf16):
    depth = lam_re.shape[0]
    a_re, a_im, bb_re, bb_im = _s5_params(lam_re, lam_im, log_dt,
                                          jnp.transpose(b_re, (0, 3, 1, 2)), jnp.transpose(b_im, (0, 3, 1, 2)))
    def bu_blocks(bb):
        return jnp.transpose(bb, (0, 2, 1, 3)).reshape(depth, S5_SG, S5_SG_GROUPS, S5_GROUP, S5_STATE)
    w_bu = jnp.concatenate([_block_diag(bu_blocks(bb_re)), _block_diag(bu_blocks(bb_im))], axis=3)
    def c_blocks(cm):
        return jnp.transpose(cm, (0, 1, 3, 2)).reshape(depth, S5_SG, S5_SG_GROUPS, S5_STATE, S5_GROUP)
    return dict(a_re=a_re.reshape(depth, S5_SG, S5_SG_STATE), a_im=a_im.reshape(depth, S5_SG, S5_SG_STATE),
                w_bu=w_bu.astype(bf16),
                w_c_re=_block_diag(c_blocks(c_re)).astype(bf16),
                w_c_im=_block_diag(c_blocks(c_im)).astype(bf16),
                d=d.reshape(depth, 1, S5_WIDTH), w_glu=w_glu_bf16)


def _projections(h, wts, tabs, layer):
    qt = _proj_qt(h, wts["wt_q"], wts["q_gain"], tabs["cos_ht"], tabs["sin_ht"], layer)
    k, v, kb, vt = _proj_kv(h, wts["wt_kv"], wts["k_gain"], tabs["cos_h"], tabs["sin_h"], layer)
    qit, wt = _proj_qit(h, wts["wt_qi"], tabs["cos_t"], tabs["sin_t"], layer)
    ki, kib = _proj_ki(h, wts["wt_ki"], tabs["cos_k"], tabs["sin_a"], tabs["sin_b"], layer)
    u = _proj_plain(h, wts["wt_u"], f32, layer)
    zg = _proj_zg(h, wts["wt_zg"], layer)
    return dict(qt=qt, k=k, v=v, kb=kb, vt=vt, qit=qit, wt=wt, ki=ki, kib=kib, u=u, zg=zg)


def _finish(x, a_in, s_in, zg, wts, layer, last):
    merged = _merge(a_in, s_in, wts["w_br_attn"], wts["w_br_s5"], zg, layer)
    if last:
        return _out_proj(x, merged, wts["w_out"], layer), None
    return _out_proj_norm(x, merged, wts["w_out"], wts["norm_gain"], layer)


def kernel(x_prompt, x_sample, cache_k, cache_v, cache_kidx, state_s5_re, state_s5_im, page_table, norm_gain, w_in, q_norm_gain, k_norm_gain, s5_lam_re, s5_lam_im, s5_log_dt, s5_b_re, s5_b_im, s5_c_re, s5_c_im, s5_d, w_glu, w_br_attn, w_br_s5, w_out):
    depth = w_in.shape[0]
    b_p, t_p = x_prompt.shape[:2]
    b_s, t_s = x_sample.shape[:2]
    tok = SAMPLE_ROWS
    assert t_s <= tok
    n_pages = page_table.shape[1]
    past = n_pages * PAGE_SIZE
    n_phys = cache_k.shape[1]

    tabs_p = _rope_tables(jnp.tile(jnp.arange(t_p, dtype=i32), b_p))
    tabs_s = _rope_tables(jnp.tile(past + jnp.arange(tok, dtype=i32), b_s))

    xp = x_prompt.reshape(b_p * t_p, D_MODEL)
    xs = jnp.pad(x_sample, ((0, 0), (0, tok - t_s), (0, 0))).reshape(b_s * tok, D_MODEL)

    cache_k4 = cache_k.reshape(depth, n_phys, PAGE_SIZE * N_KV_HEADS, HEAD_DIM)
    cache_v4 = cache_v.reshape(depth, n_phys, PAGE_SIZE * N_KV_HEADS, HEAD_DIM)
    cache_kidx_t = jnp.swapaxes(cache_kidx, 2, 3)
    zeros_state = jnp.zeros((b_p, S5_GROUPS, S5_STATE), f32)

    def new_page(a):
        w = a.shape[-1]
        return jnp.pad(a.reshape(b_s, tok, w), ((0, 0), (0, PAGE_SIZE - tok), (0, 0)))

    wts = _pack_weights(w_in, w_glu, w_br_attn, w_br_s5, w_out, norm_gain, q_norm_gain, k_norm_gain)
    s5p = _s5_stacked_params(s5_lam_re, s5_lam_im, s5_log_dt, s5_b_re, s5_b_im, s5_c_re, s5_c_im, s5_d,
                             wts["w_glu"])

    outs_p, outs_s = [], []
    hp = _rmsnorm(xp, wts["norm_gain"], 0)
    hs = _rmsnorm(xs, wts["norm_gain"], 0)
    for l in range(depth):
        last = l == depth - 1
        pp = _projections(hp, wts, tabs_p, l)
        a_in = _attn_prompt(pp["qt"], pp["qit"], pp["wt"], pp["zg"], pp["kb"], pp["vt"], pp["kib"], b_p, t_p)
        s_in, sr_p, si_p = _s5(pp["u"], pp["zg"], zeros_state, zeros_state, s5p, l, b_p, t_p, t_p)
        xp, hp = _finish(xp, a_in, s_in, pp["zg"], wts, l, last)
        outs_p.append((pp["k"].reshape(b_p, t_p, N_KV_HEADS, HEAD_DIM),
                       pp["v"].reshape(b_p, t_p, N_KV_HEADS, HEAD_DIM),
                       pp["ki"].reshape(b_p, t_p, IDX_DIM), sr_p, si_p))

        ps = _projections(hs, wts, tabs_s, l)
        qi_rows = jnp.transpose(ps["qit"].reshape(IDX_HEADS, IDX_DIM, b_s, tok)[..., :IDX_ROWS], (2, 0, 3, 1)
                                ).reshape(b_s, IDX_HEADS * IDX_ROWS, IDX_DIM)
        w_rows = jnp.broadcast_to(
            jnp.transpose(ps["wt"].reshape(IDX_HEADS, b_s, tok)[..., :IDX_ROWS], (1, 0, 2)
                          ).reshape(b_s, IDX_HEADS * IDX_ROWS, 1),
            (b_s, IDX_HEADS * IDX_ROWS, PAGE_SIZE))
        keys, thr = _idx_sample(page_table, qi_rows, w_rows, cache_kidx_t, l,
                                jnp.swapaxes(new_page(ps["ki"]), 1, 2), t_s)
        q_rows = jnp.transpose(ps["qt"].reshape(N_HEADS, HEAD_DIM, b_s, tok), (2, 0, 3, 1)
                               ).reshape(b_s, N_HEADS * tok, HEAD_DIM)
        k_s = ps["k"].reshape(b_s * tok, KV_WIDTH)
        v_s = ps["v"].reshape(b_s * tok, KV_WIDTH)
        o_rows = _attn_sample(page_table, q_rows, keys, thr, cache_k4, cache_v4, l,
                              new_page(k_s), new_page(v_s))
        attn_s = jnp.transpose(o_rows.reshape(b_s, N_HEADS, tok, HEAD_DIM), (0, 2, 1, 3)
                               ).reshape(b_s * tok, ATTN_WIDTH)
        a_in_s = _gate_mul(attn_s, ps["zg"][:, :ATTN_WIDTH])
        s_in_s, sr_s, si_s = _s5(ps["u"], ps["zg"], state_s5_re[l], state_s5_im[l], s5p, l, b_s, tok, t_s)
        xs, hs = _finish(xs, a_in_s, s_in_s, ps["zg"], wts, l, last)
        real = lambda a: a.reshape((b_s, tok) + a.shape[1:])[:, :t_s]
        outs_s.append((real(k_s).reshape(b_s, t_s, N_KV_HEADS, HEAD_DIM),
                       real(v_s).reshape(b_s, t_s, N_KV_HEADS, HEAD_DIM),
                       real(ps["ki"]), sr_s, si_s))

    k_prompt, v_prompt, kidx_prompt, s5_re_prompt, s5_im_prompt = [jnp.stack(a) for a in zip(*outs_p)]
    k_sample, v_sample, kidx_sample, s5_re_sample, s5_im_sample = [jnp.stack(a) for a in zip(*outs_s)]
    y_prompt = xp.reshape(b_p, t_p, D_MODEL)
    y_sample = xs.reshape(b_s, tok, D_MODEL)[:, :t_s]
    return (y_prompt, y_sample, k_prompt, v_prompt, kidx_prompt, s5_re_prompt, s5_im_prompt,
            k_sample, v_sample, kidx_sample, s5_re_sample, s5_im_sample)
```
